```python
import jax, jax.numpy as jnp
from jax import lax
import numpy as np

D_MODEL = 1024
BATCH = 2
SEQ = 8192
DEPTH = 1

HG_WIDTH = D_MODEL // 2
HG_DK = 128
HG_DV = 128
HG_HEADS = HG_WIDTH // HG_DK
HG_CHUNK = 64
POOL_WIDTH = D_MODEL // 2
POOL_WINDOWS = (2, 4, 8, 16)
N_POOL = len(POOL_WINDOWS)
POOL_GROUP = POOL_WIDTH // N_POOL
IN_COLS = 4 * HG_WIDTH + POOL_WIDTH + 2 * D_MODEL
N_EXPERTS = 256
TOP_K = 8
N_GROUPS = 8
TOPK_GROUPS = 4
D_EXPERT = D_MODEL // 4
ROUTED_SCALE = 2.5
MOE_BLOCK = 128
EPS = 1e-6

kernel_name = 'hybrid_hgrn2_pool_moe_adaln'


def rmsnorm(x, g):
    xf = x.astype(jnp.float32)
    y = xf * lax.rsqrt(jnp.mean(xf * xf, axis=-1, keepdims=True) + EPS)
    return (y * g.astype(jnp.float32)).astype(x.dtype)


def swiglu(h, wg, wu, wd):
    return (jax.nn.silu(h @ wg) * (h @ wu)) @ wd


def hgrn2_chunked(q, k, log_f, v):
    B, S, H, K = q.shape
    V = v.shape[-1]
    C = HG_CHUNK
    nc = S // C

    def to_chunks(t):
        return t.reshape(B, nc, C, H, t.shape[-1]).transpose(1, 0, 3, 2, 4)

    causal = jnp.tril(jnp.ones((C, C), dtype=bool))[:, :, None]

    def step(state, inp):
        qc, kc, lfc, vc = inp
        b = jnp.cumsum(lfc, axis=2)
        rel = jnp.where(causal, b[:, :, :, None, :] - b[:, :, None, :, :], -jnp.inf)
        scores = jnp.sum(qc[:, :, :, None, :] * jnp.exp(rel) * kc[:, :, None, :, :], axis=-1)
        o = (jnp.einsum('bhts,bhsv->bhtv', scores, vc)
             + jnp.einsum('bhtk,bhkv->bhtv', qc * jnp.exp(b), state))
        b_end = b[:, :, -1:, :]
        state = (jnp.exp(b_end[:, :, 0, :])[..., None] * state
                 + jnp.einsum('bhsk,bhsv->bhkv', kc * jnp.exp(b_end - b), vc))
        return state, o

    s0 = jnp.zeros((B, H, K, V), jnp.float32)
    _, o = lax.scan(step, s0, (to_chunks(q), to_chunks(k), to_chunks(log_f), to_chunks(v)))
    return o.transpose(1, 0, 3, 2, 4).reshape(B, S, H, V)


def pool_mixer(u, pool_w, pool_b, pool_scale):
    B, S, _ = u.shape
    uf = u.astype(jnp.float32)
    csum = jnp.pad(jnp.cumsum(uf, axis=1), ((0, 0), (1, 0), (0, 0)))
    pos1 = jnp.arange(1, S + 1)
    outs = []
    for g, w in enumerate(POOL_WINDOWS):
        sl = slice(g * POOL_GROUP, (g + 1) * POOL_GROUP)
        cg = csum[:, :, sl]
        lag = jnp.pad(cg[:, :S + 1 - w], ((0, 0), (w - 1, 0), (0, 0)))
        count = jnp.minimum(pos1, w).astype(jnp.float32)[None, :, None]
        outs.append((cg[:, 1:] - lag) / count - uf[:, :, sl])
    m = jnp.stack(outs, axis=2)
    y = jnp.einsum('bsgc,gcd->bsgd', m, pool_w.astype(jnp.float32)) + pool_b.astype(jnp.float32)
    return (y.reshape(B, S, POOL_WIDTH) * pool_scale.astype(jnp.float32)).astype(u.dtype)


def moe_ffn(h, router_w, router_bias, wg, wu, wd, swg, swu, swd):
    B, S, D = h.shape
    T = B * S
    hf = h.reshape(T, D)
    scores = jax.nn.sigmoid(hf.astype(jnp.float32) @ router_w.astype(jnp.float32))
    biased = scores + router_bias.astype(jnp.float32)
    per_group = N_EXPERTS // N_GROUPS
    group_score = jnp.sum(lax.top_k(biased.reshape(T, N_GROUPS, per_group), 2)[0], axis=-1)
    _, group_idx = lax.top_k(group_score, TOPK_GROUPS)
    group_keep = jnp.any(group_idx[:, :, None] == jnp.arange(N_GROUPS)[None, None, :], axis=1)
    biased = jnp.where(jnp.repeat(group_keep, per_group, axis=1), biased, -jnp.inf)
    _, expert_idx = lax.top_k(biased, TOP_K)
    gate = jnp.take_along_axis(scores, expert_idx, axis=1)
    gate = gate / jnp.sum(gate, axis=1, keepdims=True) * ROUTED_SCALE

    n_assign = T * TOP_K
    M = MOE_BLOCK
    n_blocks = (n_assign + N_EXPERTS * (M - 1) + M - 1) // M
    n_rows = n_blocks * M
    e_flat = expert_idx.reshape(-1)
    tok_flat = jnp.repeat(jnp.arange(T, dtype=jnp.int32), TOP_K)
    g_flat = gate.reshape(-1)
    order = jnp.argsort(e_flat)
    e_sorted = e_flat[order]
    counts = jnp.bincount(e_flat, length=N_EXPERTS)
    starts = jnp.cumsum(counts) - counts
    padded = (counts + M - 1) // M * M
    pad_ends = jnp.cumsum(padded)
    pad_starts = pad_ends - padded
    dest = pad_starts[e_sorted] + (jnp.arange(n_assign) - starts[e_sorted])
    row_tok = jnp.full((n_rows,), T, jnp.int32).at[dest].set(tok_flat[order])
    row_gate = jnp.zeros((n_rows,), h.dtype).at[dest].set(g_flat[order].astype(h.dtype))
    block_expert = jnp.minimum(
        jnp.searchsorted(pad_ends, jnp.arange(n_blocks) * M, side='right'), N_EXPERTS - 1)
    h_pad = jnp.concatenate([hf, jnp.zeros((1, D), h.dtype)], axis=0)

    def expert_block(args):
        rows, e = args
        return swiglu(h_pad[rows], wg[e], wu[e], wd[e])

    ys = lax.map(expert_block, (row_tok.reshape(n_blocks, M), block_expert))
    ys = ys.reshape(n_rows, D) * row_gate[:, None]
    routed = jax.ops.segment_sum(ys, row_tok, num_segments=T + 1)[:T]
    return (routed + swiglu(hf, swg, swu, swd)).reshape(B, S, D)


def setup_inputs(seed: int = 0) -> dict:
    key = jax.random.key(seed)
    ks = jax.random.split(key, 24)
    D, L, E, F = D_MODEL, DEPTH, N_EXPERTS, D_EXPERT

    def nrm(k, shape, scale):
        return jax.random.normal(k, shape, jnp.float32) * scale

    return {
        'x': nrm(ks[0], (BATCH, SEQ, D), 1.0),
        'c': nrm(ks[1], (BATCH, D), 1.0),
        'ada_w': nrm(ks[2], (L, D, 6 * D), 0.5 * D ** -0.5),
        'ada_b': nrm(ks[3], (L, 6 * D), 0.02),
        'norm1_g': 1.0 + nrm(ks[4], (L, D), 0.02),
        'w_in': nrm(ks[5], (L, D, IN_COLS), D ** -0.5),
        'hgrn_lb': nrm(ks[6], (L + 1, HG_WIDTH), 0.1),
        'hgrn_norm_g': 1.0 + nrm(ks[7], (L, HG_DV), 0.02),
        'pool_w': nrm(ks[8], (L, N_POOL, POOL_GROUP, POOL_GROUP), POOL_GROUP ** -0.5),
        'pool_b': nrm(ks[9], (L, N_POOL, POOL_GROUP), 0.02),
        'pool_scale': 1.0 + nrm(ks[10], (L, POOL_WIDTH), 0.02),
        'w_up_a': nrm(ks[11], (L, HG_WIDTH, D), HG_WIDTH ** -0.5),
        'w_up_b': nrm(ks[12], (L, POOL_WIDTH, D), POOL_WIDTH ** -0.5),
        'w_out': nrm(ks[13], (L, D, D), D ** -0.5),
        'norm2_g': 1.0 + nrm(ks[14], (L, D), 0.02),
        'router_w': nrm(ks[15], (L, D, E), D ** -0.5),
        'router_bias': nrm(ks[16], (L, E), 0.01),
        'exp_w_gate': nrm(ks[17], (L, E, D, F), D ** -0.5),
        'exp_w_up': nrm(ks[18], (L, E, D, F), D ** -0.5),
        'exp_w_down': nrm(ks[19], (L, E, F, D), F ** -0.5),
        'shared_w_gate': nrm(ks[20], (L, D, F), D ** -0.5),
        'shared_w_up': nrm(ks[21], (L, D, F), D ** -0.5),
        'shared_w_down': nrm(ks[22], (L, F, D), F ** -0.5),
        'final_norm_g': 1.0 + nrm(ks[23], (D,), 0.02),
    }


def reference(x, c, ada_w, ada_b, norm1_g, w_in, hgrn_lb, hgrn_norm_g, pool_w, pool_b,
              pool_scale, w_up_a, w_up_b, w_out, norm2_g, router_w, router_bias,
              exp_w_gate, exp_w_up, exp_w_down, shared_w_gate, shared_w_up, shared_w_down,
              final_norm_g):
    B, S, D = x.shape
    lb_all = jnp.cumsum(jax.nn.softmax(hgrn_lb.astype(jnp.float32), axis=0), axis=0)
    cond = jax.nn.silu(c)
    splits = [HG_WIDTH, 2 * HG_WIDTH, 3 * HG_WIDTH, 4 * HG_WIDTH,
              4 * HG_WIDTH + POOL_WIDTH, 4 * HG_WIDTH + POOL_WIDTH + D_MODEL]
    for l in range(DEPTH):
        mod = cond @ ada_w[l] + ada_b[l]
        sh1, sc1, g1, sh2, sc2, g2 = [m[:, None, :] for m in jnp.split(mod, 6, axis=-1)]

        h = rmsnorm(x, norm1_g[l]) * (1.0 + sc1) + sh1
        proj = h @ w_in[l]
        q, f, i, og, u, ga, gb = jnp.split(proj, splits, axis=-1)

        lb = lb_all[l].reshape(HG_HEADS, HG_DK)
        qf = jax.nn.silu(q.astype(jnp.float32)).reshape(B, S, HG_HEADS, HG_DK) * HG_DK ** -0.5
        zf = f.astype(jnp.float32).reshape(B, S, HG_HEADS, HG_DK)
        log_f = jnp.logaddexp(jnp.log(lb), jnp.log1p(-lb) + jax.nn.log_sigmoid(zf))
        kf = -jnp.expm1(log_f)
        vf = i.astype(jnp.float32).reshape(B, S, HG_HEADS, HG_DV)
        o = hgrn2_chunked(qf, kf, log_f, vf)
        o = rmsnorm(o, hgrn_norm_g[l]).reshape(B, S, HG_WIDTH).astype(x.dtype)
        y_a = (o * jax.nn.silu(og)) @ w_up_a[l]

        y_b = pool_mixer(u, pool_w[l], pool_b[l], pool_scale[l]) @ w_up_b[l]

        mixed = (jax.nn.sigmoid(ga) * y_a + jax.nn.sigmoid(gb) * y_b) @ w_out[l]
        x = x + g1 * mixed

        h2 = rmsnorm(x, norm2_g[l]) * (1.0 + sc2) + sh2
        x = x + g2 * moe_ffn(h2, router_w[l], router_bias[l], exp_w_gate[l], exp_w_up[l],
                             exp_w_down[l], shared_w_gate[l], shared_w_up[l], shared_w_down[l])
    return rmsnorm(x, final_norm_g)
```

```python
import functools

import jax
import jax.numpy as jnp
from jax import lax
from jax.experimental import pallas as pl
from jax.experimental.pallas import tpu as pltpu

F32 = jnp.float32
BF16 = jnp.bfloat16
HIGHEST = lax.Precision.HIGHEST

D_MODEL = 1024
HG_WIDTH = 512
HG_DK = 128
HG_HEADS = 4
HG_CHUNK = 64
HG_SUB = 16
POOL_WIDTH = 512
POOL_WINDOWS = (2, 4, 8, 16)
POOL_GROUP = 128
POOL_HALO = 16
N_EXPERTS = 256
TOP_K = 8
N_GROUPS = 8
TOPK_GROUPS = 4
GROUP_SIZE = N_EXPERTS // N_GROUPS
D_EXPERT = 256
ROUTED_SCALE = 2.5
EPS = 1e-6

LANES = 128
SUBLANES = 8
ROW_TILES = D_MODEL // LANES
EXPERT_TILE = 128
VMEM_LIMIT = 56 * 1024 * 1024

COL_Q, COL_F, COL_I, COL_OG, COL_U, COL_GA, COL_GB = 0, 512, 1024, 1536, 2048, 2560, 3584


def _sigmoid(x):
    return 1.0 / (1.0 + jnp.exp(-x))


def _silu(x):
    return x * _sigmoid(x)


def _dot(a, b):
    return jnp.dot(a, b, preferred_element_type=F32)


def _dot_nt(a, b):
    return lax.dot_general(a, b, (((1,), (1,)), ((), ())), preferred_element_type=F32)


def _dot_tn(a, b):
    return lax.dot_general(a, b, (((0,), (0,)), ((), ())), preferred_element_type=F32)


def _ada_kernel(c_ref, w_ref, b_ref, o_ref):
    cond = _silu(c_ref[...])
    o_ref[...] = jnp.dot(cond, w_ref[...], precision=HIGHEST, preferred_element_type=F32) + b_ref[...]


def _ada(c_pad, ada_w, ada_b):
    n = ada_w.shape[1]
    tn = 1536
    return pl.pallas_call(
        _ada_kernel,
        out_shape=jax.ShapeDtypeStruct((SUBLANES, n), F32),
        grid=(n // tn,),
        in_specs=[pl.BlockSpec((SUBLANES, D_MODEL), lambda j: (0, 0)),
                  pl.BlockSpec((D_MODEL, tn), lambda j: (0, j)),
                  pl.BlockSpec((1, tn), lambda j: (0, j))],
        out_specs=pl.BlockSpec((SUBLANES, tn), lambda j: (0, j)),
        compiler_params=pltpu.CompilerParams(vmem_limit_bytes=VMEM_LIMIT),
        name="ada",
    )(c_pad, ada_w, ada_b)


def _inproj_kernel(x_ref, sh_ref, sc_ref, g_ref, w_ref, lb_ref, pw_ref, pb_ref, ps_ref,
                   q_ref, lf_ref, k_ref, v_ref, sog_ref, pm_ref, sga_ref, sgb_ref, halo_ref):
    s = pl.program_id(1)
    tm = x_ref.shape[1]
    x = x_ref[0]
    h = x * lax.rsqrt(jnp.mean(x * x, axis=-1, keepdims=True) + EPS) * g_ref[...]
    h = h * (1.0 + sc_ref[0]) + sh_ref[0]
    hb = h.astype(BF16)

    def proj(lo, n):
        return _dot(hb, w_ref[:, lo:lo + n])

    q = proj(COL_Q, HG_WIDTH)
    q_ref[...] = _silu(q) * (HG_DK ** -0.5)
    sig = _sigmoid(proj(COL_F, HG_WIDTH))
    lb = lb_ref[...]
    lf_ref[...] = jnp.log(lb + (1.0 - lb) * sig)
    k_ref[...] = (1.0 - lb) * (1.0 - sig)
    v_ref[...] = proj(COL_I, HG_WIDTH)
    sog_ref[...] = _silu(proj(COL_OG, HG_WIDTH)).astype(BF16)
    sga_ref[...] = _sigmoid(proj(COL_GA, D_MODEL)).astype(BF16)
    sgb_ref[...] = _sigmoid(proj(COL_GB, D_MODEL)).astype(BF16)

    u = proj(COL_U, POOL_WIDTH)
    @pl.when(s == 0)
    def _():
        halo_ref[...] = jnp.zeros_like(halo_ref)

    ext = jnp.concatenate([halo_ref[...], u], axis=0)
    halo_ref[...] = u[tm - POOL_HALO:, :]
    s2 = ext + pltpu.roll(ext, 1, 0)
    s4 = s2 + pltpu.roll(s2, 2, 0)
    s8 = s4 + pltpu.roll(s4, 4, 0)
    s16 = s8 + pltpu.roll(s8, 8, 0)
    pos1 = (s * tm + 1 + lax.broadcasted_iota(jnp.int32, (tm, 1), 0)).astype(F32)
    for g, (w, sw) in enumerate(zip(POOL_WINDOWS, (s2, s4, s8, s16))):
        cols = slice(g * POOL_GROUP, (g + 1) * POOL_GROUP)
        m = sw[POOL_HALO:, cols] / jnp.minimum(pos1, float(w)) - u[:, cols]
        y = _dot(m.astype(BF16), pw_ref[g]) + pb_ref[g]
        pm_ref[:, cols] = (y * ps_ref[:, cols]).astype(BF16)


def _inproj(x, mod3, norm_g, w_in_b, lb, pool_w_b, pool_b, pool_scale, tm):
    B, S, D = x.shape
    T = B * S
    nS = S // tm
    row = lambda b, s: (b * nS + s, 0)
    const2 = lambda b, s: (0, 0)
    const3 = lambda b, s: (0, 0, 0)
    half = lambda dt: jax.ShapeDtypeStruct((T, HG_WIDTH), dt)
    full = lambda dt: jax.ShapeDtypeStruct((T, D), dt)
    return pl.pallas_call(
        _inproj_kernel,
        out_shape=(half(F32), half(F32), half(F32), half(F32), half(BF16), half(BF16), full(BF16), full(BF16)),
        grid=(B, nS),
        in_specs=[pl.BlockSpec((1, tm, D), lambda b, s: (b, s, 0)),
                  pl.BlockSpec((1, 1, D), lambda b, s: (b, 0, 0)),
                  pl.BlockSpec((1, 1, D), lambda b, s: (b, 0, 1)),
                  pl.BlockSpec((1, D), const2),
                  pl.BlockSpec(w_in_b.shape, const2),
                  pl.BlockSpec((1, HG_WIDTH), const2),
                  pl.BlockSpec(pool_w_b.shape, const3),
                  pl.BlockSpec(pool_b.shape, const3),
                  pl.BlockSpec((1, POOL_WIDTH), const2)],
        out_specs=(pl.BlockSpec((tm, HG_WIDTH), row),) * 6 + (pl.BlockSpec((tm, D), row),) * 2,
        scratch_shapes=[pltpu.VMEM((POOL_HALO, POOL_WIDTH), F32)],
        compiler_params=pltpu.CompilerParams(
            dimension_semantics=("arbitrary", "arbitrary"), vmem_limit_bytes=VMEM_LIMIT),
        name="inproj",
    )(x, mod3, mod3, norm_g, w_in_b, lb, pool_w_b, pool_b, pool_scale)


def _hgrn_kernel(q_ref, lf_ref, k_ref, v_ref, sog_ref, gn_ref, o_ref, st_ref):
    C = HG_CHUNK
    n_chunks = q_ref.shape[0] // C

    @pl.when(pl.program_id(1) == 0)
    def _():
        st_ref[...] = jnp.zeros_like(st_ref)

    r_i = lax.broadcasted_iota(jnp.int32, (C, C), 0)
    c_i = lax.broadcasted_iota(jnp.int32, (C, C), 1)
    tril = (c_i <= r_i).astype(F32)
    blk_end = (c_i <= (r_i // HG_SUB) * HG_SUB + (HG_SUB - 1)).astype(F32)
    row = lax.broadcasted_iota(jnp.int32, (C, HG_DK), 0)
    row_in_sub = row % HG_SUB
    row_sub = row // HG_SUB
    n_sub = C // HG_SUB

    def chunk(ci, carry):
        rs = pl.ds(pl.multiple_of(ci * C, C), C)
        lf_all = lf_ref[rs, :]
        b_all = jnp.dot(tril, lf_all, precision=HIGHEST, preferred_element_type=F32)
        bn_all = jnp.dot(blk_end, lf_all, precision=HIGHEST, preferred_element_type=F32)
        for h in range(HG_HEADS):
            cs = slice(h * HG_DK, (h + 1) * HG_DK)
            q = q_ref[rs, cs]
            k = k_ref[rs, cs]
            v = v_ref[rs, cs]
            b = b_all[:, cs]
            bn = bn_all[:, cs]
            vb = v.astype(BF16)

            kt = k * jnp.exp(bn - b)
            q_parts, k_parts = [], []
            for j in range(n_sub - 1):
                bj = b[HG_SUB * j + HG_SUB - 1:HG_SUB * (j + 1), :]
                after = row >= HG_SUB * (j + 1)
                q_parts.append(q * jnp.exp(jnp.where(after, b - bj, -jnp.inf)))
                k_parts.append(jnp.where(row_sub == j, kt, 0.0))
            qcat = jnp.concatenate(q_parts, axis=1).astype(BF16)
            kcat = jnp.concatenate(k_parts, axis=1).astype(BF16)
            o = _dot(_dot_nt(qcat, kcat).astype(BF16), vb)

            o = o + jnp.sum(q * k, axis=-1, keepdims=True) * v
            for d in range(1, HG_SUB):
                kd = pltpu.roll(k, d, 0)
                bd = pltpu.roll(b, d, 0)
                vd = pltpu.roll(v, d, 0)
                e = jnp.exp(jnp.where(row_in_sub >= d, b - bd, -jnp.inf))
                o = o + jnp.sum(q * kd * e, axis=-1, keepdims=True) * vd

            st = st_ref[h]
            o = o + _dot_nt((q * jnp.exp(b)).astype(BF16), st.astype(BF16))
            b_end = b[C - 1:C, :]
            k_end = (k * jnp.exp(b_end - b)).astype(BF16)
            st_ref[h] = st * jnp.exp(b_end) + _dot_tn(vb, k_end)

            on = o * lax.rsqrt(jnp.mean(o * o, axis=-1, keepdims=True) + EPS) * gn_ref[...]
            o_ref[rs, cs] = (on * sog_ref[rs, cs].astype(F32)).astype(BF16)
        return carry

    lax.fori_loop(0, n_chunks, chunk, 0)


def _hgrn(q, lf, k, v, sog, gn, B, S, tb):
    T = B * S
    nS = S // tb
    row = lambda b, s: (b * nS + s, 0)
    blk = pl.BlockSpec((tb, HG_WIDTH), row)
    return pl.pallas_call(
        _hgrn_kernel,
        out_shape=jax.ShapeDtypeStruct((T, HG_WIDTH), BF16),
        grid=(B, nS),
        in_specs=[blk, blk, blk, blk, blk, pl.BlockSpec((1, HG_DK), lambda b, s: (0, 0))],
        out_specs=blk,
        scratch_shapes=[pltpu.VMEM((HG_HEADS, HG_DK, HG_DK), F32)],
        compiler_params=pltpu.CompilerParams(
            dimension_semantics=("arbitrary", "arbitrary"), vmem_limit_bytes=VMEM_LIMIT),
        name="hgrn",
    )(q, lf, k, v, sog, gn)


def _mix_kernel(x_ref, oa_ref, pm_ref, sga_ref, sgb_ref, g1_ref, sh2_ref, sc2_ref, n2_ref,
                wua_ref, wub_ref, wo_ref, rwt_ref, x1_ref, h2_ref, lg_ref):
    tm = x_ref.shape[1]
    ya = _dot(oa_ref[...], wua_ref[...])
    yb = _dot(pm_ref[...], wub_ref[...])
    mix = sga_ref[...].astype(F32) * ya + sgb_ref[...].astype(F32) * yb
    x1 = x_ref[0] + g1_ref[0] * _dot(mix.astype(BF16), wo_ref[...])
    x1_ref[...] = x1
    h2 = x1 * lax.rsqrt(jnp.mean(x1 * x1, axis=-1, keepdims=True) + EPS) * n2_ref[...]
    h2 = h2 * (1.0 + sc2_ref[0]) + sh2_ref[0]
    for j in range(ROW_TILES):
        h2_ref[pl.ds(j, tm, stride=ROW_TILES), :] = h2[:, j * LANES:(j + 1) * LANES]
    lg_ref[...] = lax.dot_general(rwt_ref[...], h2, (((1,), (1,)), ((), ())),
                                  precision=HIGHEST, preferred_element_type=F32)


def _mix(x, oa, pm, sga, sgb, mod3, norm2_g, wua, wub, wo, rwt, tm):
    B, S, D = x.shape
    T = B * S
    nS = S // tm
    row = lambda b, s: (b * nS + s, 0)
    const2 = lambda b, s: (0, 0)
    return pl.pallas_call(
        _mix_kernel,
        out_shape=(jax.ShapeDtypeStruct((T, D), F32),
                   jax.ShapeDtypeStruct((T * ROW_TILES, LANES), F32),
                   jax.ShapeDtypeStruct((N_EXPERTS, T), F32)),
        grid=(B, nS),
        in_specs=[pl.BlockSpec((1, tm, D), lambda b, s: (b, s, 0)),
                  pl.BlockSpec((tm, HG_WIDTH), row),
                  pl.BlockSpec((tm, POOL_WIDTH), row),
                  pl.BlockSpec((tm, D), row),
                  pl.BlockSpec((tm, D), row),
                  pl.BlockSpec((1, 1, D), lambda b, s: (b, 0, 2)),
                  pl.BlockSpec((1, 1, D), lambda b, s: (b, 0, 3)),
                  pl.BlockSpec((1, 1, D), lambda b, s: (b, 0, 4)),
                  pl.BlockSpec((1, D), const2),
                  pl.BlockSpec(wua.shape, const2),
                  pl.BlockSpec(wub.shape, const2),
                  pl.BlockSpec(wo.shape, const2),
                  pl.BlockSpec(rwt.shape, const2)],
        out_specs=(pl.BlockSpec((tm, D), row),
                   pl.BlockSpec((tm * ROW_TILES, LANES), row),
                   pl.BlockSpec((N_EXPERTS, tm), lambda b, s: (0, b * nS + s))),
        compiler_params=pltpu.CompilerParams(
            dimension_semantics=("arbitrary", "arbitrary"), vmem_limit_bytes=VMEM_LIMIT),
        name="mix",
    )(x, oa, pm, sga, sgb, mod3, mod3, mod3, norm2_g, wua, wub, wo, rwt)


def _route_kernel(lg_ref, bias_ref, idx_ref, gate_ref, rank_ref, cnt_ref, carry_ref):
    tl = lg_ref.shape[1]
    neg = -jnp.inf

    @pl.when(pl.program_id(0) == 0)
    def _():
        carry_ref[...] = jnp.zeros_like(carry_ref)

    s = _sigmoid(lg_ref[...])
    biased = s + bias_ref[...]
    rowid = lax.broadcasted_iota(jnp.int32, (N_EXPERTS, tl), 0)

    def first_argmax(x, ids, sentinel):
        m = jnp.max(x, axis=0, keepdims=True)
        return jnp.min(jnp.where(x == m, ids, sentinel), axis=0, keepdims=True), m

    gscores = []
    for g in range(N_GROUPS):
        xg = biased[g * GROUP_SIZE:(g + 1) * GROUP_SIZE, :]
        rid = g * GROUP_SIZE + lax.broadcasted_iota(jnp.int32, (GROUP_SIZE, tl), 0)
        first, m1 = first_argmax(xg, rid, N_EXPERTS)
        m2 = jnp.max(jnp.where(rid == first, neg, xg), axis=0, keepdims=True)
        gscores.append(m1 + m2)
    blocks = []
    for g in range(N_GROUPS):
        beaten = jnp.zeros((1, tl), F32)
        for o in range(N_GROUPS):
            if o != g:
                wins = (gscores[o] >= gscores[g]) if o < g else (gscores[o] > gscores[g])
                beaten = beaten + jnp.where(wins, 1.0, 0.0)
        xg = biased[g * GROUP_SIZE:(g + 1) * GROUP_SIZE, :]
        blocks.append(jnp.where(beaten < float(TOPK_GROUPS), xg, neg))
    masked = jnp.concatenate(blocks, axis=0)

    idxs, gates = [], []
    chosen = jnp.zeros((N_EXPERTS, tl), F32)
    for _ in range(TOP_K):
        first, _m = first_argmax(masked, rowid, N_EXPERTS)
        sel = rowid == first
        gates.append(jnp.sum(jnp.where(sel, s, 0.0), axis=0, keepdims=True))
        idxs.append(first)
        chosen = jnp.where(sel, 1.0, chosen)
        masked = jnp.where(sel, neg, masked)
    gate_sum = functools.reduce(lambda a, b: a + b, gates)
    for k in range(TOP_K):
        gate_ref[k:k + 1, :] = gates[k] / gate_sum * ROUTED_SCALE
        idx_ref[k:k + 1, :] = idxs[k]

    lr = lax.broadcasted_iota(jnp.int32, (tl, tl), 0)
    lc = lax.broadcasted_iota(jnp.int32, (tl, tl), 1)
    prefix = (lr <= lc).astype(BF16)
    cnt_incl = _dot(chosen.astype(BF16), prefix)
    carry = carry_ref[...]
    rank_excl = cnt_incl - chosen + carry
    for k in range(TOP_K):
        rank_k = jnp.sum(jnp.where(rowid == idxs[k], rank_excl, 0.0), axis=0, keepdims=True)
        rank_ref[k:k + 1, :] = rank_k.astype(jnp.int32)
    carry = carry + jnp.sum(chosen, axis=1, keepdims=True)
    carry_ref[...] = carry
    cnt_ref[...] = carry.astype(jnp.int32)


def _route(logits_t, bias, tl):
    T = logits_t.shape[1]
    tok = lambda i: (0, i)
    return pl.pallas_call(
        _route_kernel,
        out_shape=(jax.ShapeDtypeStruct((TOP_K, T), jnp.int32),
                   jax.ShapeDtypeStruct((TOP_K, T), F32),
                   jax.ShapeDtypeStruct((TOP_K, T), jnp.int32),
                   jax.ShapeDtypeStruct((N_EXPERTS, 1), jnp.int32)),
        grid=(T // tl,),
        in_specs=[pl.BlockSpec((N_EXPERTS, tl), tok), pl.BlockSpec((N_EXPERTS, 1), lambda i: (0, 0))],
        out_specs=(pl.BlockSpec((TOP_K, tl), tok), pl.BlockSpec((TOP_K, tl), tok),
                   pl.BlockSpec((TOP_K, tl), tok), pl.BlockSpec((N_EXPERTS, 1), lambda i: (0, 0))),
        scratch_shapes=[pltpu.VMEM((N_EXPERTS, 1), F32)],
        compiler_params=pltpu.CompilerParams(
            dimension_semantics=("arbitrary",), vmem_limit_bytes=VMEM_LIMIT),
        name="route",
    )(logits_t, bias)


def _row_copy(src, src_row, dst, dst_row, sem):
    return pltpu.make_async_copy(
        src.at[pl.ds(pl.multiple_of(src_row * ROW_TILES, ROW_TILES), ROW_TILES)],
        dst.at[pl.ds(pl.multiple_of(dst_row * ROW_TILES, ROW_TILES), ROW_TILES)], sem)


def _scatter_kernel(pos_ref, h2_ref, xs_ref, sem):
    n = pos_ref.shape[2]

    def start(i, c):
        _row_copy(h2_ref, i // TOP_K, xs_ref, pos_ref[0, 0, i], sem).start()
        return c

    def wait(i, c):
        _row_copy(h2_ref, i // TOP_K, xs_ref, pos_ref[0, 0, i], sem).wait()
        return c

    lax.fori_loop(0, n, start, 0)
    lax.fori_loop(0, n, wait, 0)


def _scatter(pos_tiles, h2_tm, tt):
    n_rows = h2_tm.shape[0] // ROW_TILES * TOP_K
    return pl.pallas_call(
        _scatter_kernel,
        out_shape=jax.ShapeDtypeStruct((n_rows * ROW_TILES, LANES), F32),
        grid=(pos_tiles.shape[0],),
        in_specs=[pl.BlockSpec((1, 1, tt * TOP_K), lambda i: (i, 0, 0), memory_space=pltpu.SMEM),
                  pl.BlockSpec((tt * ROW_TILES, LANES), lambda i: (i, 0))],
        out_specs=pl.BlockSpec(memory_space=pl.ANY),
        scratch_shapes=[pltpu.SemaphoreType.DMA],
        compiler_params=pltpu.CompilerParams(
            dimension_semantics=("arbitrary",), vmem_limit_bytes=VMEM_LIMIT),
        name="scatter",
    )(pos_tiles, h2_tm)


def _experts_kernel(tile_ref, exp_ref, first_ref, newe_ref, nitems_ref, off_ref, cnt_ref,
                    xs_ref, wg_ref, wu_ref, wd_ref, ys_ref, wgb_ref, wub_ref, wdb_ref):
    i = pl.program_id(0)
    tr = EXPERT_TILE

    @pl.when(i < nitems_ref[0])
    def _():
        e = exp_ref[i]

        @pl.when(newe_ref[i] == 1)
        def _():
            wgb_ref[...] = wg_ref[0].astype(BF16)
            wub_ref[...] = wu_ref[0].astype(BF16)
            wdb_ref[...] = wd_ref[0].astype(BF16)

        x = jnp.concatenate(
            [xs_ref[pl.ds(j, tr, stride=ROW_TILES), :] for j in range(ROW_TILES)], axis=1).astype(BF16)
        hm = (_silu(_dot(x, wgb_ref[...])) * _dot(x, wub_ref[...])).astype(BF16)
        y = _dot(hm, wdb_ref[...])
        row = tile_ref[i] * tr + lax.broadcasted_iota(jnp.int32, (tr, 1), 0)
        lo = off_ref[e]
        y = jnp.where((row >= lo) & (row < lo + cnt_ref[e]), y, 0.0)

        @pl.when(first_ref[i] == 1)
        def _():
            for j in range(ROW_TILES):
                ys_ref[pl.ds(j, tr, stride=ROW_TILES), :] = y[:, j * LANES:(j + 1) * LANES]

        @pl.when(first_ref[i] == 0)
        def _():
            for j in range(ROW_TILES):
                sl = pl.ds(j, tr, stride=ROW_TILES)
                ys_ref[sl, :] = ys_ref[sl, :] + y[:, j * LANES:(j + 1) * LANES]


def _experts(meta, xs, wg, wu, wd, n_items_max):
    tile_rows = EXPERT_TILE * ROW_TILES
    tile_map = lambda i, tile, exp, *_: (tile[i], 0)
    w_map = lambda i, tile, exp, *_: (exp[i], 0, 0)
    grid_spec = pltpu.PrefetchScalarGridSpec(
        num_scalar_prefetch=len(meta),
        grid=(n_items_max,),
        in_specs=[pl.BlockSpec((tile_rows, LANES), tile_map),
                  pl.BlockSpec((1, D_MODEL, D_EXPERT), w_map),
                  pl.BlockSpec((1, D_MODEL, D_EXPERT), w_map),
                  pl.BlockSpec((1, D_EXPERT, D_MODEL), w_map)],
        out_specs=pl.BlockSpec((tile_rows, LANES), tile_map),
        scratch_shapes=[pltpu.VMEM((D_MODEL, D_EXPERT), BF16),
                        pltpu.VMEM((D_MODEL, D_EXPERT), BF16),
                        pltpu.VMEM((D_EXPERT, D_MODEL), BF16)])
    return pl.pallas_call(
        _experts_kernel,
        out_shape=jax.ShapeDtypeStruct(xs.shape, F32),
        grid_spec=grid_spec,
        compiler_params=pltpu.CompilerParams(
            dimension_semantics=("arbitrary",), vmem_limit_bytes=VMEM_LIMIT),
        name="experts",
    )(*meta, xs, wg, wu, wd)


def _combine_kernel(pos_ref, ys_ref, h2_ref, x1_ref, gate_ref, g2_ref, swg_ref, swu_ref, swd_ref, fg_ref,
                    out_ref, buf_ref, sem):
    tt = x1_ref.shape[0]
    n = pos_ref.shape[2]

    def start(i, c):
        _row_copy(ys_ref, pos_ref[0, 0, i], buf_ref, i, sem).start()
        return c

    def wait(i, c):
        _row_copy(ys_ref, pos_ref[0, 0, i], buf_ref, i, sem).wait()
        return c

    lax.fori_loop(0, n, start, 0)

    h2 = jnp.concatenate(
        [h2_ref[pl.ds(j, tt, stride=ROW_TILES), :] for j in range(ROW_TILES)], axis=1).astype(BF16)
    hm = (_silu(_dot(h2, swg_ref[...])) * _dot(h2, swu_ref[...])).astype(BF16)
    shared = _dot(hm, swd_ref[...])

    lax.fori_loop(0, n, wait, 0)

    gate = gate_ref[...]
    parts = []
    for j in range(ROW_TILES):
        acc = jnp.zeros((tt, LANES), F32)
        for k in range(TOP_K):
            acc = acc + gate[:, k:k + 1] * buf_ref[pl.ds(k * ROW_TILES + j, tt, stride=TOP_K * ROW_TILES), :]
        parts.append(acc)
    routed = jnp.concatenate(parts, axis=1)
    x2 = x1_ref[...] + g2_ref[0] * (routed + shared)
    out_ref[...] = x2 * lax.rsqrt(jnp.mean(x2 * x2, axis=-1, keepdims=True) + EPS) * fg_ref[...]


def _combine(pos_tiles, ys, h2_tm, x1, gate_tm, mod3, swg, swu, swd, fg, B, S, tt):
    T, D = x1.shape
    nS = S // tt
    const2 = lambda i: (0, 0)
    return pl.pallas_call(
        _combine_kernel,
        out_shape=jax.ShapeDtypeStruct((T, D), F32),
        grid=(T // tt,),
        in_specs=[pl.BlockSpec((1, 1, tt * TOP_K), lambda i: (i, 0, 0), memory_space=pltpu.SMEM),
                  pl.BlockSpec(memory_space=pl.ANY),
                  pl.BlockSpec((tt * ROW_TILES, LANES), lambda i: (i, 0)),
                  pl.BlockSpec((tt, D), lambda i: (i, 0)),
                  pl.BlockSpec((tt, TOP_K), lambda i: (i, 0)),
                  pl.BlockSpec((1, 1, D), lambda i: (i // nS, 0, 5)),
                  pl.BlockSpec(swg.shape, const2),
                  pl.BlockSpec(swu.shape, const2),
                  pl.BlockSpec(swd.shape, const2),
                  pl.BlockSpec((1, D), const2)],
        out_specs=pl.BlockSpec((tt, D), lambda i: (i, 0)),
        scratch_shapes=[pltpu.VMEM((tt * TOP_K * ROW_TILES, LANES), F32), pltpu.SemaphoreType.DMA],
        compiler_params=pltpu.CompilerParams(
            dimension_semantics=("arbitrary",), vmem_limit_bytes=VMEM_LIMIT),
        name="combine",
    )(pos_tiles, ys, h2_tm, x1, gate_tm, mod3, swg, swu, swd, fg)


def _plan(idx, rank, counts, n_items_max):
    counts = counts[:, 0]
    off = jnp.cumsum(counts) - counts
    pos = jnp.take(off, idx) + rank
    first_tile = off // EXPERT_TILE
    last_tile = (off + counts - 1) // EXPERT_TILE
    n_e = jnp.where(counts > 0, last_tile - first_tile + 1, 0)
    item_end = jnp.cumsum(n_e)
    item_start = item_end - n_e
    n_items = item_end[-1]
    ids = jnp.arange(n_items_max, dtype=jnp.int32)
    ids_c = jnp.minimum(ids, n_items - 1)
    item_e = jnp.minimum(jnp.searchsorted(item_end, ids_c, side='right'), N_EXPERTS - 1).astype(jnp.int32)
    item_tile = (first_tile[item_e] + ids_c - item_start[item_e]).astype(jnp.int32)
    prev_tile = jnp.concatenate([jnp.full((1,), -1, jnp.int32), item_tile[:-1]])
    prev_e = jnp.concatenate([jnp.full((1,), -1, jnp.int32), item_e[:-1]])
    item_first = (item_tile != prev_tile).astype(jnp.int32)
    item_newe = (item_e != prev_e).astype(jnp.int32)
    meta = (item_tile, item_e, item_first, item_newe, n_items.reshape(1).astype(jnp.int32),
            off.astype(jnp.int32), counts.astype(jnp.int32))
    return pos, meta


def kernel(x, c, ada_w, ada_b, norm1_g, w_in, hgrn_lb, hgrn_norm_g, pool_w, pool_b, pool_scale, w_up_a, w_up_b, w_out, norm2_g, router_w, router_bias, exp_w_gate, exp_w_up, exp_w_down, shared_w_gate, shared_w_up, shared_w_down, final_norm_g):
    B, S, D = x.shape
    T = B * S
    assert ada_w.shape[0] == 1, "single-layer trunk only: the final norm is fused into the combine step"
    lb_all = jnp.cumsum(jax.nn.softmax(hgrn_lb.astype(F32), axis=0), axis=0)
    c_pad = jnp.zeros((SUBLANES, D), F32).at[:B].set(c)
    n_items_max = T * TOP_K // EXPERT_TILE + N_EXPERTS - 1

    for l in range(1):
        mod = _ada(c_pad, ada_w[l], ada_b[l].reshape(1, -1))
        mod3 = mod[:B].reshape(B, 1, 6 * D)

        q, lf, k, v, sog, pm, sga, sgb = _inproj(
            x, mod3, norm1_g[l].reshape(1, D), w_in[l].astype(BF16), lb_all[l].reshape(1, HG_WIDTH),
            pool_w[l].astype(BF16), pool_b[l].reshape(len(POOL_WINDOWS), 1, POOL_GROUP),
            pool_scale[l].reshape(1, POOL_WIDTH), tm=256)
        oa = _hgrn(q, lf, k, v, sog, hgrn_norm_g[l].reshape(1, HG_DK), B, S, tb=512)

        x1, h2_tm, logits_t = _mix(
            x, oa, pm, sga, sgb, mod3, norm2_g[l].reshape(1, D), w_up_a[l].astype(BF16),
            w_up_b[l].astype(BF16), w_out[l].astype(BF16), router_w[l].T, tm=256)

        idx, gate, rank, counts = _route(logits_t, router_bias[l].reshape(N_EXPERTS, 1), tl=256)
        pos, meta = _plan(idx, rank, counts, n_items_max)
        pos_tm = pos.T.reshape(-1)

        tt_s = 128
        xs = _scatter(pos_tm.reshape(T // tt_s, 1, tt_s * TOP_K), h2_tm, tt_s)
        ys = _experts(meta, xs, exp_w_gate[l], exp_w_up[l], exp_w_down[l], n_items_max)

        tt_c = 64
        fg = final_norm_g.reshape(1, D)
        x = _combine(pos_tm.reshape(T // tt_c, 1, tt_c * TOP_K), ys, h2_tm, x1, gate.T, mod3,
                     shared_w_gate[l].astype(BF16), shared_w_up[l].astype(BF16),
                     shared_w_down[l].astype(BF16), fg, B, S, tt_c).reshape(B, S, D)
    return x
```

```python
import functools

import jax
import jax.numpy as jnp
from jax import lax
from jax.experimental import pallas as pl
from jax.experimental.pallas import tpu as pltpu

F32 = jnp.float32
BF16 = jnp.bfloat16
HIGHEST = lax.Precision.HIGHEST

D_MODEL = 1024
HG_WIDTH = 512
HG_DK = 128
HG_HEADS = 4
HG_CHUNK = 64
HG_SUB = 16
POOL_WIDTH = 512
POOL_WINDOWS = (2, 4, 8, 16)
POOL_GROUP = 128
POOL_HALO = 16
N_EXPERTS = 256
TOP_K = 8
N_GROUPS = 8
TOPK_GROUPS = 4
GROUP_SIZE = N_EXPERTS // N_GROUPS
D_EXPERT = 256
ROUTED_SCALE = 2.5
EPS = 1e-6

LANES = 128
SUBLANES = 8
ROW_TILES = D_MODEL // LANES
EXPERT_TILE = 128
VMEM_LIMIT = 56 * 1024 * 1024

COL_Q, COL_F, COL_I, COL_OG, COL_U, COL_GA, COL_GB = 0, 512, 1024, 1536, 2048, 2560, 3584


def _sigmoid(x):
    return 1.0 / (1.0 + jnp.exp(-x))


def _silu(x):
    return x * _sigmoid(x)


def _dot(a, b):
    return jnp.dot(a, b, preferred_element_type=F32)


def _dot_nt(a, b):
    return lax.dot_general(a, b, (((1,), (1,)), ((), ())), preferred_element_type=F32)


def _dot_tn(a, b):
    return lax.dot_general(a, b, (((0,), (0,)), ((), ())), preferred_element_type=F32)


def _row_chunks(x):
    return [x[:, j * LANES:(j + 1) * LANES] for j in range(ROW_TILES)]


def _load_rows(ref, n, first_row=0):
    return jnp.concatenate(
        [ref[pl.ds(first_row * ROW_TILES + j, n, stride=ROW_TILES), :] for j in range(ROW_TILES)], axis=1)


def _ada_kernel(c_ref, w_ref, b_ref, o_ref):
    cond = _silu(c_ref[...])
    o_ref[...] = jnp.dot(cond, w_ref[...], precision=HIGHEST, preferred_element_type=F32) + b_ref[...]


def _ada(c_pad, ada_w, ada_b):
    n = ada_w.shape[1]
    tn = 1536
    return pl.pallas_call(
        _ada_kernel,
        out_shape=jax.ShapeDtypeStruct((SUBLANES, n), F32),
        grid=(n // tn,),
        in_specs=[pl.BlockSpec((SUBLANES, D_MODEL), lambda j: (0, 0)),
                  pl.BlockSpec((D_MODEL, tn), lambda j: (0, j)),
                  pl.BlockSpec((1, tn), lambda j: (0, j))],
        out_specs=pl.BlockSpec((SUBLANES, tn), lambda j: (0, j)),
        compiler_params=pltpu.CompilerParams(vmem_limit_bytes=VMEM_LIMIT),
        name="ada",
    )(c_pad, ada_w, ada_b)


def _inproj_kernel(x_ref, sh_ref, sc_ref, g_ref, w_ref, lb_ref, pw_ref, pb_ref, ps_ref,
                   q_ref, lf_ref, k_ref, v_ref, sog_ref, pm_ref, sga_ref, sgb_ref, halo_ref):
    s = pl.program_id(1)
    tm = x_ref.shape[1]
    x = x_ref[0]
    h = x * lax.rsqrt(jnp.mean(x * x, axis=-1, keepdims=True) + EPS) * g_ref[...]
    h = h * (1.0 + sc_ref[0]) + sh_ref[0]
    hb = h.astype(BF16)

    def proj(lo, n):
        return _dot(hb, w_ref[:, lo:lo + n])

    q = proj(COL_Q, HG_WIDTH)
    q_ref[...] = _silu(q) * (HG_DK ** -0.5)
    sig = _sigmoid(proj(COL_F, HG_WIDTH))
    lb = lb_ref[...]
    lf_ref[...] = jnp.log(lb + (1.0 - lb) * sig)
    k_ref[...] = (1.0 - lb) * (1.0 - sig)
    v_ref[...] = proj(COL_I, HG_WIDTH)
    sog_ref[...] = _silu(proj(COL_OG, HG_WIDTH)).astype(BF16)
    sga_ref[...] = _sigmoid(proj(COL_GA, D_MODEL)).astype(BF16)
    sgb_ref[...] = _sigmoid(proj(COL_GB, D_MODEL)).astype(BF16)

    u = proj(COL_U, POOL_WIDTH)
    @pl.when(s == 0)
    def _():
        halo_ref[...] = jnp.zeros_like(halo_ref)

    ext = jnp.concatenate([halo_ref[...], u], axis=0)
    halo_ref[...] = u[tm - POOL_HALO:, :]
    s2 = ext + pltpu.roll(ext, 1, 0)
    s4 = s2 + pltpu.roll(s2, 2, 0)
    s8 = s4 + pltpu.roll(s4, 4, 0)
    s16 = s8 + pltpu.roll(s8, 8, 0)
    pos1 = (s * tm + 1 + lax.broadcasted_iota(jnp.int32, (tm, 1), 0)).astype(F32)
    for g, (w, sw) in enumerate(zip(POOL_WINDOWS, (s2, s4, s8, s16))):
        cols = slice(g * POOL_GROUP, (g + 1) * POOL_GROUP)
        m = sw[POOL_HALO:, cols] / jnp.minimum(pos1, float(w)) - u[:, cols]
        y = _dot(m.astype(BF16), pw_ref[g]) + pb_ref[g]
        pm_ref[:, cols] = (y * ps_ref[:, cols]).astype(BF16)


def _inproj(x, mod3, norm_g, w_in_b, lb, pool_w_b, pool_b, pool_scale, tm):
    B, S, D = x.shape
    T = B * S
    nS = S // tm
    row = lambda b, s: (b * nS + s, 0)
    const2 = lambda b, s: (0, 0)
    const3 = lambda b, s: (0, 0, 0)
    half = lambda dt: jax.ShapeDtypeStruct((T, HG_WIDTH), dt)
    full = lambda dt: jax.ShapeDtypeStruct((T, D), dt)
    return pl.pallas_call(
        _inproj_kernel,
        out_shape=(half(F32), half(F32), half(F32), half(F32), half(BF16), half(BF16), full(BF16), full(BF16)),
        grid=(B, nS),
        in_specs=[pl.BlockSpec((1, tm, D), lambda b, s: (b, s, 0)),
                  pl.BlockSpec((1, 1, D), lambda b, s: (b, 0, 0)),
                  pl.BlockSpec((1, 1, D), lambda b, s: (b, 0, 1)),
                  pl.BlockSpec((1, D), const2),
                  pl.BlockSpec(w_in_b.shape, const2),
                  pl.BlockSpec((1, HG_WIDTH), const2),
                  pl.BlockSpec(pool_w_b.shape, const3),
                  pl.BlockSpec(pool_b.shape, const3),
                  pl.BlockSpec((1, POOL_WIDTH), const2)],
        out_specs=(pl.BlockSpec((tm, HG_WIDTH), row),) * 6 + (pl.BlockSpec((tm, D), row),) * 2,
        scratch_shapes=[pltpu.VMEM((POOL_HALO, POOL_WIDTH), F32)],
        compiler_params=pltpu.CompilerParams(
            dimension_semantics=("arbitrary", "arbitrary"), vmem_limit_bytes=VMEM_LIMIT),
        name="inproj",
    )(x, mod3, mod3, norm_g, w_in_b, lb, pool_w_b, pool_b, pool_scale)


def _hgrn_kernel(q_ref, lf_ref, k_ref, v_ref, sog_ref, gn_ref, o_ref, st_ref):
    C = HG_CHUNK
    n_chunks = q_ref.shape[0] // C

    @pl.when(pl.program_id(1) == 0)
    def _():
        st_ref[...] = jnp.zeros_like(st_ref)

    r_i = lax.broadcasted_iota(jnp.int32, (C, C), 0)
    c_i = lax.broadcasted_iota(jnp.int32, (C, C), 1)
    tril = (c_i <= r_i).astype(F32)
    blk_end = (c_i <= (r_i // HG_SUB) * HG_SUB + (HG_SUB - 1)).astype(F32)
    row = lax.broadcasted_iota(jnp.int32, (C, HG_DK), 0)
    row_in_sub = row % HG_SUB
    row_sub = row // HG_SUB
    n_sub = C // HG_SUB

    def chunk(ci, carry):
        rs = pl.ds(pl.multiple_of(ci * C, C), C)
        lf_all = lf_ref[rs, :]
        b_all = jnp.dot(tril, lf_all, precision=HIGHEST, preferred_element_type=F32)
        bn_all = jnp.dot(blk_end, lf_all, precision=HIGHEST, preferred_element_type=F32)
        for h in range(HG_HEADS):
            cs = slice(h * HG_DK, (h + 1) * HG_DK)
            q = q_ref[rs, cs]
            k = k_ref[rs, cs]
            v = v_ref[rs, cs]
            b = b_all[:, cs]
            bn = bn_all[:, cs]
            vb = v.astype(BF16)

            kt = k * jnp.exp(bn - b)
            q_parts, k_parts = [], []
            for j in range(n_sub - 1):
                bj = b[HG_SUB * j + HG_SUB - 1:HG_SUB * (j + 1), :]
                after = row >= HG_SUB * (j + 1)
                q_parts.append(q * jnp.exp(jnp.where(after, b - bj, -jnp.inf)))
                k_parts.append(jnp.where(row_sub == j, kt, 0.0))
            qcat = jnp.concatenate(q_parts, axis=1).astype(BF16)
            kcat = jnp.concatenate(k_parts, axis=1).astype(BF16)
            o = _dot(_dot_nt(qcat, kcat).astype(BF16), vb)

            o = o + jnp.sum(q * k, axis=-1, keepdims=True) * v
            for d in range(1, HG_SUB):
                kd = pltpu.roll(k, d, 0)
                bd = pltpu.roll(b, d, 0)
                vd = pltpu.roll(v, d, 0)
                e = jnp.exp(jnp.where(row_in_sub >= d, b - bd, -jnp.inf))
                o = o + jnp.sum(q * kd * e, axis=-1, keepdims=True) * vd

            st = st_ref[h]
            o = o + _dot_nt((q * jnp.exp(b)).astype(BF16), st.astype(BF16))
            b_end = b[C - 1:C, :]
            k_end = (k * jnp.exp(b_end - b)).astype(BF16)
            st_ref[h] = st * jnp.exp(b_end) + _dot_tn(vb, k_end)

            on = o * lax.rsqrt(jnp.mean(o * o, axis=-1, keepdims=True) + EPS) * gn_ref[...]
            o_ref[rs, cs] = (on * sog_ref[rs, cs].astype(F32)).astype(BF16)
        return carry

    lax.fori_loop(0, n_chunks, chunk, 0)


def _hgrn(q, lf, k, v, sog, gn, B, S, tb):
    T = B * S
    nS = S // tb
    row = lambda b, s: (b * nS + s, 0)
    blk = pl.BlockSpec((tb, HG_WIDTH), row)
    return pl.pallas_call(
        _hgrn_kernel,
        out_shape=jax.ShapeDtypeStruct((T, HG_WIDTH), BF16),
        grid=(B, nS),
        in_specs=[blk, blk, blk, blk, blk, pl.BlockSpec((1, HG_DK), lambda b, s: (0, 0))],
        out_specs=blk,
        scratch_shapes=[pltpu.VMEM((HG_HEADS, HG_DK, HG_DK), F32)],
        compiler_params=pltpu.CompilerParams(
            dimension_semantics=("arbitrary", "arbitrary"), vmem_limit_bytes=VMEM_LIMIT),
        name="hgrn",
    )(q, lf, k, v, sog, gn)


def _mix_kernel(x_ref, oa_ref, pm_ref, sga_ref, sgb_ref, g1_ref, sh2_ref, sc2_ref, n2_ref,
                wua_ref, wub_ref, wo_ref, rwt_ref, x1_ref, h2_ref, lg_ref):
    tm = x_ref.shape[1]
    ya = _dot(oa_ref[...], wua_ref[...])
    yb = _dot(pm_ref[...], wub_ref[...])
    mix = sga_ref[...].astype(F32) * ya + sgb_ref[...].astype(F32) * yb
    x1 = x_ref[0] + g1_ref[0] * _dot(mix.astype(BF16), wo_ref[...])
    x1_ref[...] = x1
    h2 = x1 * lax.rsqrt(jnp.mean(x1 * x1, axis=-1, keepdims=True) + EPS) * n2_ref[...]
    h2 = h2 * (1.0 + sc2_ref[0]) + sh2_ref[0]
    for j, chunk in enumerate(_row_chunks(h2)):
        h2_ref[pl.ds(j, tm, stride=ROW_TILES), :] = chunk
    lg_ref[...] = lax.dot_general(rwt_ref[...], h2, (((1,), (1,)), ((), ())),
                                  precision=HIGHEST, preferred_element_type=F32)


def _mix(x, oa, pm, sga, sgb, mod3, norm2_g, wua, wub, wo, rwt, tm):
    B, S, D = x.shape
    T = B * S
    nS = S // tm
    row = lambda b, s: (b * nS + s, 0)
    const2 = lambda b, s: (0, 0)
    return pl.pallas_call(
        _mix_kernel,
        out_shape=(jax.ShapeDtypeStruct((T, D), F32),
                   jax.ShapeDtypeStruct((T * ROW_TILES, LANES), F32),
                   jax.ShapeDtypeStruct((N_EXPERTS, T), F32)),
        grid=(B, nS),
        in_specs=[pl.BlockSpec((1, tm, D), lambda b, s: (b, s, 0)),
                  pl.BlockSpec((tm, HG_WIDTH), row),
                  pl.BlockSpec((tm, POOL_WIDTH), row),
                  pl.BlockSpec((tm, D), row),
                  pl.BlockSpec((tm, D), row),
                  pl.BlockSpec((1, 1, D), lambda b, s: (b, 0, 2)),
                  pl.BlockSpec((1, 1, D), lambda b, s: (b, 0, 3)),
                  pl.BlockSpec((1, 1, D), lambda b, s: (b, 0, 4)),
                  pl.BlockSpec((1, D), const2),
                  pl.BlockSpec(wua.shape, const2),
                  pl.BlockSpec(wub.shape, const2),
                  pl.BlockSpec(wo.shape, const2),
                  pl.BlockSpec(rwt.shape, const2)],
        out_specs=(pl.BlockSpec((tm, D), row),
                   pl.BlockSpec((tm * ROW_TILES, LANES), row),
                   pl.BlockSpec((N_EXPERTS, tm), lambda b, s: (0, b * nS + s))),
        compiler_params=pltpu.CompilerParams(
            dimension_semantics=("arbitrary", "arbitrary"), vmem_limit_bytes=VMEM_LIMIT),
        name="mix",
    )(x, oa, pm, sga, sgb, mod3, mod3, mod3, norm2_g, wua, wub, wo, rwt)


def _route_kernel(lg_ref, bias_ref, idx_ref, gate_ref, rank_ref, cnt_ref, carry_ref):
    tl = lg_ref.shape[1]
    neg = -jnp.inf

    @pl.when(pl.program_id(0) == 0)
    def _():
        carry_ref[...] = jnp.zeros_like(carry_ref)

    s = _sigmoid(lg_ref[...])
    biased = s + bias_ref[...]
    rowid = lax.broadcasted_iota(jnp.int32, (N_EXPERTS, tl), 0)

    def first_argmax(x, ids, sentinel):
        m = jnp.max(x, axis=0, keepdims=True)
        return jnp.min(jnp.where(x == m, ids, sentinel), axis=0, keepdims=True), m

    gscores = []
    for g in range(N_GROUPS):
        xg = biased[g * GROUP_SIZE:(g + 1) * GROUP_SIZE, :]
        rid = g * GROUP_SIZE + lax.broadcasted_iota(jnp.int32, (GROUP_SIZE, tl), 0)
        first, m1 = first_argmax(xg, rid, N_EXPERTS)
        m2 = jnp.max(jnp.where(rid == first, neg, xg), axis=0, keepdims=True)
        gscores.append(m1 + m2)
    blocks = []
    for g in range(N_GROUPS):
        beaten = jnp.zeros((1, tl), F32)
        for o in range(N_GROUPS):
            if o != g:
                wins = (gscores[o] >= gscores[g]) if o < g else (gscores[o] > gscores[g])
                beaten = beaten + jnp.where(wins, 1.0, 0.0)
        xg = biased[g * GROUP_SIZE:(g + 1) * GROUP_SIZE, :]
        blocks.append(jnp.where(beaten < float(TOPK_GROUPS), xg, neg))
    masked = jnp.concatenate(blocks, axis=0)

    idxs, gates = [], []
    chosen = jnp.zeros((N_EXPERTS, tl), F32)
    for _ in range(TOP_K):
        first, _m = first_argmax(masked, rowid, N_EXPERTS)
        sel = rowid == first
        gates.append(jnp.sum(jnp.where(sel, s, 0.0), axis=0, keepdims=True))
        idxs.append(first)
        chosen = jnp.where(sel, 1.0, chosen)
        masked = jnp.where(sel, neg, masked)
    gate_sum = functools.reduce(lambda a, b: a + b, gates)
    for k in range(TOP_K):
        gate_ref[k:k + 1, :] = gates[k] / gate_sum * ROUTED_SCALE
        idx_ref[k:k + 1, :] = idxs[k]

    lr = lax.broadcasted_iota(jnp.int32, (tl, tl), 0)
    lc = lax.broadcasted_iota(jnp.int32, (tl, tl), 1)
    prefix = (lr <= lc).astype(BF16)
    cnt_incl = _dot(chosen.astype(BF16), prefix)
    carry = carry_ref[...]
    rank_excl = cnt_incl - chosen + carry
    for k in range(TOP_K):
        rank_k = jnp.sum(jnp.where(rowid == idxs[k], rank_excl, 0.0), axis=0, keepdims=True)
        rank_ref[k:k + 1, :] = rank_k.astype(jnp.int32)
    carry = carry + jnp.sum(chosen, axis=1, keepdims=True)
    carry_ref[...] = carry
    cnt_ref[...] = carry.astype(jnp.int32)


def _route(logits_t, bias, tl):
    T = logits_t.shape[1]
    tok = lambda i: (0, i)
    return pl.pallas_call(
        _route_kernel,
        out_shape=(jax.ShapeDtypeStruct((TOP_K, T), jnp.int32),
                   jax.ShapeDtypeStruct((TOP_K, T), F32),
                   jax.ShapeDtypeStruct((TOP_K, T), jnp.int32),
                   jax.ShapeDtypeStruct((N_EXPERTS, 1), jnp.int32)),
        grid=(T // tl,),
        in_specs=[pl.BlockSpec((N_EXPERTS, tl), tok), pl.BlockSpec((N_EXPERTS, 1), lambda i: (0, 0))],
        out_specs=(pl.BlockSpec((TOP_K, tl), tok), pl.BlockSpec((TOP_K, tl), tok),
                   pl.BlockSpec((TOP_K, tl), tok), pl.BlockSpec((N_EXPERTS, 1), lambda i: (0, 0))),
        scratch_shapes=[pltpu.VMEM((N_EXPERTS, 1), F32)],
        compiler_params=pltpu.CompilerParams(
            dimension_semantics=("arbitrary",), vmem_limit_bytes=VMEM_LIMIT),
        name="route",
    )(logits_t, bias)


def _as_rows(ref):
    return ref.reshape(ref.shape[0] // ROW_TILES, ROW_TILES, LANES)


def _wait_rows(rows_ref, n, sem):
    pltpu.make_async_copy(rows_ref.at[pl.ds(0, n)], rows_ref.at[pl.ds(0, n)], sem).wait()


def _scatter_kernel(pos_ref, h2_ref, xs_ref, sem):
    src = _as_rows(h2_ref)
    dst = _as_rows(xs_ref)
    tt = src.shape[0]

    def start(t, c):
        for k in range(TOP_K):
            pltpu.make_async_copy(src.at[t], dst.at[pos_ref[0, 0, t * TOP_K + k]], sem).start()
        return c

    lax.fori_loop(0, tt, start, 0)
    _wait_rows(dst, tt * TOP_K, sem)


def _scatter(pos_tiles, h2_tm, tt):
    n_rows = h2_tm.shape[0] // ROW_TILES * TOP_K
    return pl.pallas_call(
        _scatter_kernel,
        out_shape=jax.ShapeDtypeStruct((n_rows * ROW_TILES, LANES), F32),
        grid=(pos_tiles.shape[0],),
        in_specs=[pl.BlockSpec((1, 1, tt * TOP_K), lambda i: (i, 0, 0), memory_space=pltpu.SMEM),
                  pl.BlockSpec((tt * ROW_TILES, LANES), lambda i: (i, 0))],
        out_specs=pl.BlockSpec(memory_space=pl.ANY),
        scratch_shapes=[pltpu.SemaphoreType.DMA],
        compiler_params=pltpu.CompilerParams(
            dimension_semantics=("arbitrary",), vmem_limit_bytes=VMEM_LIMIT),
        name="scatter",
    )(pos_tiles, h2_tm)


def _experts_kernel(tile_ref, exp_ref, first_ref, newe_ref, nitems_ref, off_ref, cnt_ref,
                    xs_ref, wg_ref, wu_ref, wd_ref, ys_ref, wgb_ref, wub_ref, wdb_ref):
    i = pl.program_id(0)
    tr = EXPERT_TILE

    @pl.when(i < nitems_ref[0])
    def _():
        e = exp_ref[i]

        @pl.when(newe_ref[i] == 1)
        def _():
            wgb_ref[...] = wg_ref[0].astype(BF16)
            wub_ref[...] = wu_ref[0].astype(BF16)
            wdb_ref[...] = wd_ref[0].astype(BF16)

        x = _load_rows(xs_ref, tr).astype(BF16)
        hm = (_silu(_dot(x, wgb_ref[...])) * _dot(x, wub_ref[...])).astype(BF16)
        chunks = _row_chunks(_dot(hm, wdb_ref[...]))
        row = tile_ref[i] * tr + lax.broadcasted_iota(jnp.int32, (tr, 1), 0)
        lo = off_ref[e]
        mine = (row >= lo) & (row < lo + cnt_ref[e])

        @pl.when(first_ref[i] == 1)
        def _():
            for j in range(ROW_TILES):
                ys_ref[pl.ds(j, tr, stride=ROW_TILES), :] = jnp.where(mine, chunks[j], 0.0)

        @pl.when(first_ref[i] == 0)
        def _():
            for j in range(ROW_TILES):
                sl = pl.ds(j, tr, stride=ROW_TILES)
                ys_ref[sl, :] = jnp.where(mine, chunks[j], ys_ref[sl, :])


def _experts(meta, xs, wg, wu, wd, n_items_max):
    tile_rows = EXPERT_TILE * ROW_TILES
    tile_map = lambda i, tile, exp, *_: (tile[i], 0)
    w_map = lambda i, tile, exp, *_: (exp[i], 0, 0)
    grid_spec = pltpu.PrefetchScalarGridSpec(
        num_scalar_prefetch=len(meta),
        grid=(n_items_max,),
        in_specs=[pl.BlockSpec((tile_rows, LANES), tile_map),
                  pl.BlockSpec((1, D_MODEL, D_EXPERT), w_map),
                  pl.BlockSpec((1, D_MODEL, D_EXPERT), w_map),
                  pl.BlockSpec((1, D_EXPERT, D_MODEL), w_map)],
        out_specs=pl.BlockSpec((tile_rows, LANES), tile_map),
        scratch_shapes=[pltpu.VMEM((D_MODEL, D_EXPERT), BF16),
                        pltpu.VMEM((D_MODEL, D_EXPERT), BF16),
                        pltpu.VMEM((D_EXPERT, D_MODEL), BF16)])
    return pl.pallas_call(
        _experts_kernel,
        out_shape=jax.ShapeDtypeStruct(xs.shape, F32),
        grid_spec=grid_spec,
        compiler_params=pltpu.CompilerParams(
            dimension_semantics=("arbitrary",), vmem_limit_bytes=VMEM_LIMIT),
        name="experts",
    )(*meta, xs, wg, wu, wd)


def _combine_kernel(pos_ref, ys_ref, h2_ref, x1_ref, gate_ref, g2_ref, swg_ref, swu_ref, swd_ref, fg_ref,
                    out_ref, buf_ref, sem):
    tt = x1_ref.shape[0]
    src = _as_rows(ys_ref)
    dst = _as_rows(buf_ref)

    def start(t, c):
        for k in range(TOP_K):
            pltpu.make_async_copy(src.at[pos_ref[0, 0, t * TOP_K + k]], dst.at[k * tt + t], sem).start()
        return c

    lax.fori_loop(0, tt, start, 0)

    h2 = _load_rows(h2_ref, tt).astype(BF16)
    hm = (_silu(_dot(h2, swg_ref[...])) * _dot(h2, swu_ref[...])).astype(BF16)
    acc = _dot(hm, swd_ref[...])

    _wait_rows(dst, tt * TOP_K, sem)

    gate = gate_ref[...]
    for k in range(TOP_K):
        acc = acc + gate[:, k:k + 1] * _load_rows(buf_ref, tt, first_row=k * tt)
    x2 = x1_ref[...] + g2_ref[0] * acc
    out_ref[...] = x2 * lax.rsqrt(jnp.mean(x2 * x2, axis=-1, keepdims=True) + EPS) * fg_ref[...]


def _combine(pos_tiles, ys, h2_tm, x1, gate_tm, mod3, swg, swu, swd, fg, B, S, tt):
    T, D = x1.shape
    nS = S // tt
    const2 = lambda i: (0, 0)
    return pl.pallas_call(
        _combine_kernel,
        out_shape=jax.ShapeDtypeStruct((T, D), F32),
        grid=(T // tt,),
        in_specs=[pl.BlockSpec((1, 1, tt * TOP_K), lambda i: (i, 0, 0), memory_space=pltpu.SMEM),
                  pl.BlockSpec(memory_space=pl.ANY),
                  pl.BlockSpec((tt * ROW_TILES, LANES), lambda i: (i, 0)),
                  pl.BlockSpec((tt, D), lambda i: (i, 0)),
                  pl.BlockSpec((tt, TOP_K), lambda i: (i, 0)),
                  pl.BlockSpec((1, 1, D), lambda i: (i // nS, 0, 5)),
                  pl.BlockSpec(swg.shape, const2),
                  pl.BlockSpec(swu.shape, const2),
                  pl.BlockSpec(swd.shape, const2),
                  pl.BlockSpec((1, D), const2)],
        out_specs=pl.BlockSpec((tt, D), lambda i: (i, 0)),
        scratch_shapes=[pltpu.VMEM((tt * TOP_K * ROW_TILES, LANES), F32), pltpu.SemaphoreType.DMA],
        compiler_params=pltpu.CompilerParams(
            dimension_semantics=("arbitrary",), vmem_limit_bytes=VMEM_LIMIT),
        name="combine",
    )(pos_tiles, ys, h2_tm, x1, gate_tm, mod3, swg, swu, swd, fg)


def _place_kernel(idx_ref, rank_ref, off_ref, pos_ref):
    tl = idx_ref.shape[1]
    rowid = lax.broadcasted_iota(jnp.int32, (N_EXPERTS, tl), 0)
    off = off_ref[...].astype(F32)
    for k in range(TOP_K):
        base = jnp.sum(jnp.where(rowid == idx_ref[k:k + 1, :], off, 0.0), axis=0, keepdims=True)
        pos_ref[k:k + 1, :] = base.astype(jnp.int32) + rank_ref[k:k + 1, :]


def _place(idx, rank, off, tl):
    T = idx.shape[1]
    tok = pl.BlockSpec((TOP_K, tl), lambda i: (0, i))
    return pl.pallas_call(
        _place_kernel,
        out_shape=jax.ShapeDtypeStruct((TOP_K, T), jnp.int32),
        grid=(T // tl,),
        in_specs=[tok, tok, pl.BlockSpec((N_EXPERTS, 1), lambda i: (0, 0))],
        out_specs=tok,
        name="place",
    )(idx, rank, off)


def _plan(counts, n_items_max):
    counts = counts[:, 0]
    off = jnp.cumsum(counts) - counts
    first_tile = off // EXPERT_TILE
    last_tile = (off + counts - 1) // EXPERT_TILE
    n_e = jnp.where(counts > 0, last_tile - first_tile + 1, 0)
    item_end = jnp.cumsum(n_e)
    item_start = item_end - n_e
    n_items = item_end[-1]
    ids = jnp.arange(n_items_max, dtype=jnp.int32)
    ids_c = jnp.minimum(ids, n_items - 1)
    item_e = jnp.minimum(jnp.sum(item_end[None, :] <= ids_c[:, None], axis=1), N_EXPERTS - 1).astype(jnp.int32)
    item_tile = (first_tile[item_e] + ids_c - item_start[item_e]).astype(jnp.int32)
    prev_tile = jnp.concatenate([jnp.full((1,), -1, jnp.int32), item_tile[:-1]])
    prev_e = jnp.concatenate([jnp.full((1,), -1, jnp.int32), item_e[:-1]])
    item_first = (item_tile != prev_tile).astype(jnp.int32)
    item_newe = (item_e != prev_e).astype(jnp.int32)
    meta = (item_tile, item_e, item_first, item_newe, n_items.reshape(1).astype(jnp.int32),
            off.astype(jnp.int32), counts.astype(jnp.int32))
    return off.astype(jnp.int32).reshape(N_EXPERTS, 1), meta


def kernel(x, c, ada_w, ada_b, norm1_g, w_in, hgrn_lb, hgrn_norm_g, pool_w, pool_b, pool_scale, w_up_a, w_up_b, w_out, norm2_g, router_w, router_bias, exp_w_gate, exp_w_up, exp_w_down, shared_w_gate, shared_w_up, shared_w_down, final_norm_g):
    B, S, D = x.shape
    T = B * S
    assert ada_w.shape[0] == 1, "single-layer trunk only: the final norm is fused into the combine step"
    lb_all = jnp.cumsum(jax.nn.softmax(hgrn_lb.astype(F32), axis=0), axis=0)
    c_pad = jnp.zeros((SUBLANES, D), F32).at[:B].set(c)
    n_items_max = T * TOP_K // EXPERT_TILE + N_EXPERTS - 1

    for l in range(1):
        mod = _ada(c_pad, ada_w[l], ada_b[l].reshape(1, -1))
        mod3 = mod[:B].reshape(B, 1, 6 * D)

        q, lf, k, v, sog, pm, sga, sgb = _inproj(
            x, mod3, norm1_g[l].reshape(1, D), w_in[l].astype(BF16), lb_all[l].reshape(1, HG_WIDTH),
            pool_w[l].astype(BF16), pool_b[l].reshape(len(POOL_WINDOWS), 1, POOL_GROUP),
            pool_scale[l].reshape(1, POOL_WIDTH), tm=256)
        oa = _hgrn(q, lf, k, v, sog, hgrn_norm_g[l].reshape(1, HG_DK), B, S, tb=512)

        x1, h2_tm, logits_t = _mix(
            x, oa, pm, sga, sgb, mod3, norm2_g[l].reshape(1, D), w_up_a[l].astype(BF16),
            w_up_b[l].astype(BF16), w_out[l].astype(BF16), router_w[l].T, tm=256)

        idx, gate, rank, counts = _route(logits_t, router_bias[l].reshape(N_EXPERTS, 1), tl=256)
        off, meta = _plan(counts, n_items_max)
        pos_tm = _place(idx, rank, off, tl=512).T.reshape(-1)

        tt_s = 256
        xs = _scatter(pos_tm.reshape(T // tt_s, 1, tt_s * TOP_K), h2_tm, tt_s)
        ys = _experts(meta, xs, exp_w_gate[l], exp_w_up[l], exp_w_down[l], n_items_max)

        tt_c = 128
        fg = final_norm_g.reshape(1, D)
        x = _combine(pos_tm.reshape(T // tt_c, 1, tt_c * TOP_K), ys, h2_tm, x1, gate.T, mod3,
                     shared_w_gate[l].astype(BF16), shared_w_up[l].astype(BF16),
                     shared_w_down[l].astype(BF16), fg, B, S, tt_c).reshape(B, S, D)
    return x
```

```python
import functools

import jax
import jax.numpy as jnp
from jax import lax
from jax.experimental import pallas as pl
from jax.experimental.pallas import tpu as pltpu

F32 = jnp.float32
BF16 = jnp.bfloat16
HIGHEST = lax.Precision.HIGHEST

D_MODEL = 1024
HG_WIDTH = 512
HG_DK = 128
HG_HEADS = 4
HG_CHUNK = 64
HG_SUB = 16
POOL_WIDTH = 512
POOL_WINDOWS = (2, 4, 8, 16)
POOL_GROUP = 128
POOL_HALO = 16
N_EXPERTS = 256
TOP_K = 8
N_GROUPS = 8
TOPK_GROUPS = 4
GROUP_SIZE = N_EXPERTS // N_GROUPS
D_EXPERT = 256
ROUTED_SCALE = 2.5
EPS = 1e-6

LANES = 128
SUBLANES = 8
ROW_TILES = D_MODEL // LANES
EXPERT_TILE = 128
VMEM_LIMIT = 56 * 1024 * 1024

COL_Q, COL_F, COL_I, COL_OG, COL_U, COL_GA, COL_GB = 0, 512, 1024, 1536, 2048, 2560, 3584


def _sigmoid(x):
    return 1.0 / (1.0 + jnp.exp(-x))


def _silu(x):
    return x * _sigmoid(x)


def _dot(a, b):
    return jnp.dot(a, b, preferred_element_type=F32)


def _dot_nt(a, b):
    return lax.dot_general(a, b, (((1,), (1,)), ((), ())), preferred_element_type=F32)


def _dot_tn(a, b):
    return lax.dot_general(a, b, (((0,), (0,)), ((), ())), preferred_element_type=F32)


def _row_chunks(x):
    return [x[:, j * LANES:(j + 1) * LANES] for j in range(ROW_TILES)]


def _load_rows(ref, n, first_row=0):
    return jnp.concatenate(
        [ref[pl.ds(first_row * ROW_TILES + j, n, stride=ROW_TILES), :] for j in range(ROW_TILES)], axis=1)


def _ada_kernel(c_ref, w_ref, b_ref, o_ref):
    cond = _silu(c_ref[...])
    o_ref[...] = jnp.dot(cond, w_ref[...], precision=HIGHEST, preferred_element_type=F32) + b_ref[...]


def _ada(c_pad, ada_w, ada_b):
    n = ada_w.shape[1]
    tn = 1536
    return pl.pallas_call(
        _ada_kernel,
        out_shape=jax.ShapeDtypeStruct((SUBLANES, n), F32),
        grid=(n // tn,),
        in_specs=[pl.BlockSpec((SUBLANES, D_MODEL), lambda j: (0, 0)),
                  pl.BlockSpec((D_MODEL, tn), lambda j: (0, j)),
                  pl.BlockSpec((1, tn), lambda j: (0, j))],
        out_specs=pl.BlockSpec((SUBLANES, tn), lambda j: (0, j)),
        compiler_params=pltpu.CompilerParams(vmem_limit_bytes=VMEM_LIMIT),
        name="ada",
    )(c_pad, ada_w, ada_b)


def _inproj_kernel(x_ref, sh_ref, sc_ref, g_ref, w_ref, lb_ref, pw_ref, pb_ref, ps_ref,
                   q_ref, lf_ref, k_ref, v_ref, sog_ref, pm_ref, sga_ref, sgb_ref, halo_ref):
    s = pl.program_id(1)
    tm = x_ref.shape[1]
    x = x_ref[0]
    h = x * lax.rsqrt(jnp.mean(x * x, axis=-1, keepdims=True) + EPS) * g_ref[...]
    h = h * (1.0 + sc_ref[0]) + sh_ref[0]
    hb = h.astype(BF16)

    def proj(lo, n):
        return _dot(hb, w_ref[:, lo:lo + n])

    q = proj(COL_Q, HG_WIDTH)
    q_ref[...] = _silu(q) * (HG_DK ** -0.5)
    sig = _sigmoid(proj(COL_F, HG_WIDTH))
    lb = lb_ref[...]
    lf_ref[...] = jnp.log(lb + (1.0 - lb) * sig)
    k_ref[...] = (1.0 - lb) * (1.0 - sig)
    v_ref[...] = proj(COL_I, HG_WIDTH)
    sog_ref[...] = _silu(proj(COL_OG, HG_WIDTH)).astype(BF16)
    sga_ref[...] = _sigmoid(proj(COL_GA, D_MODEL)).astype(BF16)
    sgb_ref[...] = _sigmoid(proj(COL_GB, D_MODEL)).astype(BF16)

    u = proj(COL_U, POOL_WIDTH)
    @pl.when(s == 0)
    def _():
        halo_ref[...] = jnp.zeros_like(halo_ref)

    ext = jnp.concatenate([halo_ref[...], u], axis=0)
    halo_ref[...] = u[tm - POOL_HALO:, :]
    s2 = ext + pltpu.roll(ext, 1, 0)
    s4 = s2 + pltpu.roll(s2, 2, 0)
    s8 = s4 + pltpu.roll(s4, 4, 0)
    s16 = s8 + pltpu.roll(s8, 8, 0)
    pos1 = (s * tm + 1 + lax.broadcasted_iota(jnp.int32, (tm, 1), 0)).astype(F32)
    for g, (w, sw) in enumerate(zip(POOL_WINDOWS, (s2, s4, s8, s16))):
        cols = slice(g * POOL_GROUP, (g + 1) * POOL_GROUP)
        m = sw[POOL_HALO:, cols] / jnp.minimum(pos1, float(w)) - u[:, cols]
        y = _dot(m.astype(BF16), pw_ref[g]) + pb_ref[g]
        pm_ref[:, cols] = (y * ps_ref[:, cols]).astype(BF16)


def _inproj(x, mod3, norm_g, w_in_b, lb, pool_w_b, pool_b, pool_scale, tm):
    B, S, D = x.shape
    T = B * S
    nS = S // tm
    row = lambda b, s: (b * nS + s, 0)
    const2 = lambda b, s: (0, 0)
    const3 = lambda b, s: (0, 0, 0)
    half = lambda dt: jax.ShapeDtypeStruct((T, HG_WIDTH), dt)
    full = lambda dt: jax.ShapeDtypeStruct((T, D), dt)
    return pl.pallas_call(
        _inproj_kernel,
        out_shape=(half(F32), half(F32), half(F32), half(F32), half(BF16), half(BF16), full(BF16), full(BF16)),
        grid=(B, nS),
        in_specs=[pl.BlockSpec((1, tm, D), lambda b, s: (b, s, 0)),
                  pl.BlockSpec((1, 1, D), lambda b, s: (b, 0, 0)),
                  pl.BlockSpec((1, 1, D), lambda b, s: (b, 0, 1)),
                  pl.BlockSpec((1, D), const2),
                  pl.BlockSpec(w_in_b.shape, const2),
                  pl.BlockSpec((1, HG_WIDTH), const2),
                  pl.BlockSpec(pool_w_b.shape, const3),
                  pl.BlockSpec(pool_b.shape, const3),
                  pl.BlockSpec((1, POOL_WIDTH), const2)],
        out_specs=(pl.BlockSpec((tm, HG_WIDTH), row),) * 6 + (pl.BlockSpec((tm, D), row),) * 2,
        scratch_shapes=[pltpu.VMEM((POOL_HALO, POOL_WIDTH), F32)],
        compiler_params=pltpu.CompilerParams(
            dimension_semantics=("arbitrary", "arbitrary"), vmem_limit_bytes=VMEM_LIMIT),
        name="inproj",
    )(x, mod3, mod3, norm_g, w_in_b, lb, pool_w_b, pool_b, pool_scale)


def _hgrn_kernel(q_ref, lf_ref, k_ref, v_ref, sog_ref, gn_ref, o_ref, st_ref):
    C = HG_CHUNK
    n_chunks = q_ref.shape[0] // C

    @pl.when(pl.program_id(1) == 0)
    def _():
        st_ref[...] = jnp.zeros_like(st_ref)

    r_i = lax.broadcasted_iota(jnp.int32, (C, C), 0)
    c_i = lax.broadcasted_iota(jnp.int32, (C, C), 1)
    tril = (c_i <= r_i).astype(F32)
    blk_end = (c_i <= (r_i // HG_SUB) * HG_SUB + (HG_SUB - 1)).astype(F32)
    row = lax.broadcasted_iota(jnp.int32, (C, HG_DK), 0)
    row_in_sub = row % HG_SUB
    row_sub = row // HG_SUB
    n_sub = C // HG_SUB

    def chunk(ci, carry):
        rs = pl.ds(pl.multiple_of(ci * C, C), C)
        lf_all = lf_ref[rs, :]
        b_all = jnp.dot(tril, lf_all, precision=HIGHEST, preferred_element_type=F32)
        bn_all = jnp.dot(blk_end, lf_all, precision=HIGHEST, preferred_element_type=F32)
        for h in range(HG_HEADS):
            cs = slice(h * HG_DK, (h + 1) * HG_DK)
            q = q_ref[rs, cs]
            k = k_ref[rs, cs]
            v = v_ref[rs, cs]
            b = b_all[:, cs]
            bn = bn_all[:, cs]
            vb = v.astype(BF16)

            kt = k * jnp.exp(bn - b)
            q_parts, k_parts = [], []
            for j in range(n_sub - 1):
                bj = b[HG_SUB * j + HG_SUB - 1:HG_SUB * (j + 1), :]
                after = row >= HG_SUB * (j + 1)
                q_parts.append(q * jnp.exp(jnp.where(after, b - bj, -jnp.inf)))
                k_parts.append(jnp.where(row_sub == j, kt, 0.0))
            qcat = jnp.concatenate(q_parts, axis=1).astype(BF16)
            kcat = jnp.concatenate(k_parts, axis=1).astype(BF16)
            o = _dot(_dot_nt(qcat, kcat).astype(BF16), vb)

            o = o + jnp.sum(q * k, axis=-1, keepdims=True) * v
            for d in range(1, HG_SUB):
                kd = pltpu.roll(k, d, 0)
                bd = pltpu.roll(b, d, 0)
                vd = pltpu.roll(v, d, 0)
                e = jnp.exp(jnp.where(row_in_sub >= d, b - bd, -jnp.inf))
                o = o + jnp.sum(q * kd * e, axis=-1, keepdims=True) * vd

            st = st_ref[h]
            o = o + _dot_nt((q * jnp.exp(b)).astype(BF16), st.astype(BF16))
            b_end = b[C - 1:C, :]
            k_end = (k * jnp.exp(b_end - b)).astype(BF16)
            st_ref[h] = st * jnp.exp(b_end) + _dot_tn(vb, k_end)

            on = o * lax.rsqrt(jnp.mean(o * o, axis=-1, keepdims=True) + EPS) * gn_ref[...]
            o_ref[rs, cs] = (on * sog_ref[rs, cs].astype(F32)).astype(BF16)
        return carry

    lax.fori_loop(0, n_chunks, chunk, 0)


def _hgrn(q, lf, k, v, sog, gn, B, S, tb):
    T = B * S
    nS = S // tb
    row = lambda b, s: (b * nS + s, 0)
    blk = pl.BlockSpec((tb, HG_WIDTH), row)
    return pl.pallas_call(
        _hgrn_kernel,
        out_shape=jax.ShapeDtypeStruct((T, HG_WIDTH), BF16),
        grid=(B, nS),
        in_specs=[blk, blk, blk, blk, blk, pl.BlockSpec((1, HG_DK), lambda b, s: (0, 0))],
        out_specs=blk,
        scratch_shapes=[pltpu.VMEM((HG_HEADS, HG_DK, HG_DK), F32)],
        compiler_params=pltpu.CompilerParams(
            dimension_semantics=("arbitrary", "arbitrary"), vmem_limit_bytes=VMEM_LIMIT),
        name="hgrn",
    )(q, lf, k, v, sog, gn)


def _mix_kernel(x_ref, oa_ref, pm_ref, sga_ref, sgb_ref, g1_ref, sh2_ref, sc2_ref, n2_ref,
                wua_ref, wub_ref, wo_ref, rwt_ref, x1_ref, h2_ref, lg_ref):
    tm = x_ref.shape[1]
    ya = _dot(oa_ref[...], wua_ref[...])
    yb = _dot(pm_ref[...], wub_ref[...])
    mix = sga_ref[...].astype(F32) * ya + sgb_ref[...].astype(F32) * yb
    x1 = x_ref[0] + g1_ref[0] * _dot(mix.astype(BF16), wo_ref[...])
    x1_ref[...] = x1
    h2 = x1 * lax.rsqrt(jnp.mean(x1 * x1, axis=-1, keepdims=True) + EPS) * n2_ref[...]
    h2 = h2 * (1.0 + sc2_ref[0]) + sh2_ref[0]
    for j, chunk in enumerate(_row_chunks(h2)):
        h2_ref[pl.ds(j, tm, stride=ROW_TILES), :] = chunk
    lg_ref[...] = lax.dot_general(rwt_ref[...], h2, (((1,), (1,)), ((), ())),
                                  precision=HIGHEST, preferred_element_type=F32)


def _mix(x, oa, pm, sga, sgb, mod3, norm2_g, wua, wub, wo, rwt, tm):
    B, S, D = x.shape
    T = B * S
    nS = S // tm
    row = lambda b, s: (b * nS + s, 0)
    const2 = lambda b, s: (0, 0)
    return pl.pallas_call(
        _mix_kernel,
        out_shape=(jax.ShapeDtypeStruct((T, D), F32),
                   jax.ShapeDtypeStruct((T * ROW_TILES, LANES), F32),
                   jax.ShapeDtypeStruct((N_EXPERTS, T), F32)),
        grid=(B, nS),
        in_specs=[pl.BlockSpec((1, tm, D), lambda b, s: (b, s, 0)),
                  pl.BlockSpec((tm, HG_WIDTH), row),
                  pl.BlockSpec((tm, POOL_WIDTH), row),
                  pl.BlockSpec((tm, D), row),
                  pl.BlockSpec((tm, D), row),
                  pl.BlockSpec((1, 1, D), lambda b, s: (b, 0, 2)),
                  pl.BlockSpec((1, 1, D), lambda b, s: (b, 0, 3)),
                  pl.BlockSpec((1, 1, D), lambda b, s: (b, 0, 4)),
                  pl.BlockSpec((1, D), const2),
                  pl.BlockSpec(wua.shape, const2),
                  pl.BlockSpec(wub.shape, const2),
                  pl.BlockSpec(wo.shape, const2),
                  pl.BlockSpec(rwt.shape, const2)],
        out_specs=(pl.BlockSpec((tm, D), row),
                   pl.BlockSpec((tm * ROW_TILES, LANES), row),
                   pl.BlockSpec((N_EXPERTS, tm), lambda b, s: (0, b * nS + s))),
        compiler_params=pltpu.CompilerParams(
            dimension_semantics=("arbitrary", "arbitrary"), vmem_limit_bytes=VMEM_LIMIT),
        name="mix",
    )(x, oa, pm, sga, sgb, mod3, mod3, mod3, norm2_g, wua, wub, wo, rwt)


def _route_kernel(lg_ref, bias_ref, idx_ref, gate_ref, rank_ref, cnt_ref, carry_ref):
    tl = lg_ref.shape[1]
    neg = -jnp.inf

    @pl.when(pl.program_id(0) == 0)
    def _():
        carry_ref[...] = jnp.zeros_like(carry_ref)

    s = _sigmoid(lg_ref[...])
    biased = s + bias_ref[...]
    rowid = lax.broadcasted_iota(jnp.int32, (N_EXPERTS, tl), 0)

    def first_argmax(x, ids, sentinel):
        m = jnp.max(x, axis=0, keepdims=True)
        return jnp.min(jnp.where(x == m, ids, sentinel), axis=0, keepdims=True), m

    gscores = []
    for g in range(N_GROUPS):
        xg = biased[g * GROUP_SIZE:(g + 1) * GROUP_SIZE, :]
        rid = g * GROUP_SIZE + lax.broadcasted_iota(jnp.int32, (GROUP_SIZE, tl), 0)
        first, m1 = first_argmax(xg, rid, N_EXPERTS)
        m2 = jnp.max(jnp.where(rid == first, neg, xg), axis=0, keepdims=True)
        gscores.append(m1 + m2)
    blocks = []
    for g in range(N_GROUPS):
        beaten = jnp.zeros((1, tl), F32)
        for o in range(N_GROUPS):
            if o != g:
                wins = (gscores[o] >= gscores[g]) if o < g else (gscores[o] > gscores[g])
                beaten = beaten + jnp.where(wins, 1.0, 0.0)
        xg = biased[g * GROUP_SIZE:(g + 1) * GROUP_SIZE, :]
        blocks.append(jnp.where(beaten < float(TOPK_GROUPS), xg, neg))
    masked = jnp.concatenate(blocks, axis=0)

    idxs, gates = [], []
    chosen = jnp.zeros((N_EXPERTS, tl), F32)
    for _ in range(TOP_K):
        first, _m = first_argmax(masked, rowid, N_EXPERTS)
        sel = rowid == first
        gates.append(jnp.sum(jnp.where(sel, s, 0.0), axis=0, keepdims=True))
        idxs.append(first)
        chosen = jnp.where(sel, 1.0, chosen)
        masked = jnp.where(sel, neg, masked)
    gate_sum = functools.reduce(lambda a, b: a + b, gates)
    for k in range(TOP_K):
        gate_ref[k:k + 1, :] = gates[k] / gate_sum * ROUTED_SCALE
        idx_ref[k:k + 1, :] = idxs[k]

    lr = lax.broadcasted_iota(jnp.int32, (tl, tl), 0)
    lc = lax.broadcasted_iota(jnp.int32, (tl, tl), 1)
    prefix = (lr <= lc).astype(BF16)
    cnt_incl = _dot(chosen.astype(BF16), prefix)
    carry = carry_ref[...]
    rank_excl = cnt_incl - chosen + carry
    for k in range(TOP_K):
        rank_k = jnp.sum(jnp.where(rowid == idxs[k], rank_excl, 0.0), axis=0, keepdims=True)
        rank_ref[k:k + 1, :] = rank_k.astype(jnp.int32)
    carry = carry + jnp.sum(chosen, axis=1, keepdims=True)
    carry_ref[...] = carry
    cnt_ref[...] = carry.astype(jnp.int32)


def _route(logits_t, bias, tl):
    T = logits_t.shape[1]
    tok = lambda i: (0, i)
    return pl.pallas_call(
        _route_kernel,
        out_shape=(jax.ShapeDtypeStruct((TOP_K, T), jnp.int32),
                   jax.ShapeDtypeStruct((TOP_K, T), F32),
                   jax.ShapeDtypeStruct((TOP_K, T), jnp.int32),
                   jax.ShapeDtypeStruct((N_EXPERTS, 1), jnp.int32)),
        grid=(T // tl,),
        in_specs=[pl.BlockSpec((N_EXPERTS, tl), tok), pl.BlockSpec((N_EXPERTS, 1), lambda i: (0, 0))],
        out_specs=(pl.BlockSpec((TOP_K, tl), tok), pl.BlockSpec((TOP_K, tl), tok),
                   pl.BlockSpec((TOP_K, tl), tok), pl.BlockSpec((N_EXPERTS, 1), lambda i: (0, 0))),
        scratch_shapes=[pltpu.VMEM((N_EXPERTS, 1), F32)],
        compiler_params=pltpu.CompilerParams(
            dimension_semantics=("arbitrary",), vmem_limit_bytes=VMEM_LIMIT),
        name="route",
    )(logits_t, bias)


def _as_rows(ref):
    return ref.reshape(ref.shape[0] // ROW_TILES, ROW_TILES, LANES)


def _wait_rows(rows_ref, n, sem):
    pltpu.make_async_copy(rows_ref.at[pl.ds(0, n)], rows_ref.at[pl.ds(0, n)], sem).wait()


def _scatter_kernel(pos_ref, h2_ref, xs_ref, sem):
    src = _as_rows(h2_ref)
    dst = _as_rows(xs_ref)
    tt = src.shape[0]

    def start(t, c):
        for k in range(TOP_K):
            pltpu.make_async_copy(src.at[t], dst.at[pos_ref[0, 0, t * TOP_K + k]], sem).start()
        return c

    lax.fori_loop(0, tt, start, 0)
    _wait_rows(dst, tt * TOP_K, sem)


def _scatter(pos_tiles, h2_tm, tt):
    n_rows = h2_tm.shape[0] // ROW_TILES * TOP_K
    return pl.pallas_call(
        _scatter_kernel,
        out_shape=jax.ShapeDtypeStruct((n_rows * ROW_TILES, LANES), F32),
        grid=(pos_tiles.shape[0],),
        in_specs=[pl.BlockSpec((1, 1, tt * TOP_K), lambda i: (i, 0, 0), memory_space=pltpu.SMEM),
                  pl.BlockSpec((tt * ROW_TILES, LANES), lambda i: (i, 0))],
        out_specs=pl.BlockSpec(memory_space=pl.ANY),
        scratch_shapes=[pltpu.SemaphoreType.DMA],
        compiler_params=pltpu.CompilerParams(
            dimension_semantics=("arbitrary",), vmem_limit_bytes=VMEM_LIMIT),
        name="scatter",
    )(pos_tiles, h2_tm)


def _experts_kernel(tile_ref, exp_ref, first_ref, newe_ref, nitems_ref, off_ref, cnt_ref, slot_ref, nexte_ref,
                    xs_ref, wg_hbm, wu_hbm, wd_hbm, ys_ref,
                    sg_ref, su_ref, sd_ref, wgb_ref, wub_ref, wdb_ref, sem):
    i = pl.program_id(0)
    tr = EXPERT_TILE

    def weight_copies(e, slot):
        return (pltpu.make_async_copy(wg_hbm.at[e], sg_ref.at[slot], sem.at[slot]),
                pltpu.make_async_copy(wu_hbm.at[e], su_ref.at[slot], sem.at[slot]),
                pltpu.make_async_copy(wd_hbm.at[e], sd_ref.at[slot], sem.at[slot]))

    @pl.when(i < nitems_ref[0])
    def _():
        e = exp_ref[i]

        @pl.when(newe_ref[i] == 1)
        def _():
            slot = slot_ref[i]
            nxt = nexte_ref[i]

            @pl.when(i == 0)
            def _():
                for c in weight_copies(e, slot):
                    c.start()

            @pl.when(nxt >= 0)
            def _():
                for c in weight_copies(nxt, 1 - slot):
                    c.start()

            for c in weight_copies(e, slot):
                c.wait()
            wgb_ref[...] = sg_ref[slot].astype(BF16)
            wub_ref[...] = su_ref[slot].astype(BF16)
            wdb_ref[...] = sd_ref[slot].astype(BF16)

        x = _load_rows(xs_ref, tr).astype(BF16)
        hm = (_silu(_dot(x, wgb_ref[...])) * _dot(x, wub_ref[...])).astype(BF16)
        chunks = _row_chunks(_dot(hm, wdb_ref[...]))
        row = tile_ref[i] * tr + lax.broadcasted_iota(jnp.int32, (tr, 1), 0)
        lo = off_ref[e]
        mine = (row >= lo) & (row < lo + cnt_ref[e])

        @pl.when(first_ref[i] == 1)
        def _():
            for j in range(ROW_TILES):
                ys_ref[pl.ds(j, tr, stride=ROW_TILES), :] = chunks[j]

        @pl.when(first_ref[i] == 0)
        def _():
            for j in range(ROW_TILES):
                sl = pl.ds(j, tr, stride=ROW_TILES)
                ys_ref[sl, :] = jnp.where(mine, chunks[j], ys_ref[sl, :])


def _experts(meta, xs, wg, wu, wd, n_items_max):
    tile_rows = EXPERT_TILE * ROW_TILES
    tile_map = lambda i, tile, exp, *_: (tile[i], 0)
    hbm = pl.BlockSpec(memory_space=pl.ANY)
    n_slots = 2
    grid_spec = pltpu.PrefetchScalarGridSpec(
        num_scalar_prefetch=len(meta),
        grid=(n_items_max,),
        in_specs=[pl.BlockSpec((tile_rows, LANES), tile_map), hbm, hbm, hbm],
        out_specs=pl.BlockSpec((tile_rows, LANES), tile_map),
        scratch_shapes=[pltpu.VMEM((n_slots, D_MODEL, D_EXPERT), F32),
                        pltpu.VMEM((n_slots, D_MODEL, D_EXPERT), F32),
                        pltpu.VMEM((n_slots, D_EXPERT, D_MODEL), F32),
                        pltpu.VMEM((D_MODEL, D_EXPERT), BF16),
                        pltpu.VMEM((D_MODEL, D_EXPERT), BF16),
                        pltpu.VMEM((D_EXPERT, D_MODEL), BF16),
                        pltpu.SemaphoreType.DMA((n_slots,))])
    return pl.pallas_call(
        _experts_kernel,
        out_shape=jax.ShapeDtypeStruct(xs.shape, F32),
        grid_spec=grid_spec,
        compiler_params=pltpu.CompilerParams(
            dimension_semantics=("arbitrary",), vmem_limit_bytes=VMEM_LIMIT),
        name="experts",
    )(*meta, xs, wg, wu, wd)


def _combine_kernel(pos_ref, pos_next_ref, ys_ref, h2_ref, x1_ref, gate_ref, g2_ref, swg_ref, swu_ref, swd_ref,
                    fg_ref, out_ref, buf_ref, sem):
    i = pl.program_id(0)
    tt = x1_ref.shape[0]
    slot_rows = tt * TOP_K
    src = _as_rows(ys_ref)
    dst = _as_rows(buf_ref)

    def gather(p_ref, slot):
        def start(t, c):
            for k in range(TOP_K):
                pltpu.make_async_copy(src.at[p_ref[0, 0, t * TOP_K + k]],
                                      dst.at[slot * slot_rows + k * tt + t], sem.at[slot]).start()
            return c

        lax.fori_loop(0, tt, start, 0)

    slot = i % 2

    @pl.when(i == 0)
    def _():
        gather(pos_ref, slot)

    @pl.when(i + 1 < pl.num_programs(0))
    def _():
        gather(pos_next_ref, 1 - slot)

    h2 = _load_rows(h2_ref, tt).astype(BF16)
    hm = (_silu(_dot(h2, swg_ref[...])) * _dot(h2, swu_ref[...])).astype(BF16)
    acc = _dot(hm, swd_ref[...])

    _wait_rows(dst, slot_rows, sem.at[slot])

    gate = gate_ref[...]
    for k in range(TOP_K):
        acc = acc + gate[:, k:k + 1] * _load_rows(buf_ref, tt, first_row=slot * slot_rows + k * tt)
    x2 = x1_ref[...] + g2_ref[0] * acc
    out_ref[...] = x2 * lax.rsqrt(jnp.mean(x2 * x2, axis=-1, keepdims=True) + EPS) * fg_ref[...]


def _combine(pos_tiles, ys, h2_tm, x1, gate_tm, mod3, swg, swu, swd, fg, B, S, tt):
    T, D = x1.shape
    nS = S // tt
    const2 = lambda i: (0, 0)
    n_tiles = T // tt
    n_slots = 2
    return pl.pallas_call(
        _combine_kernel,
        out_shape=jax.ShapeDtypeStruct((T, D), F32),
        grid=(n_tiles,),
        in_specs=[pl.BlockSpec((1, 1, tt * TOP_K), lambda i: (i, 0, 0), memory_space=pltpu.SMEM),
                  pl.BlockSpec((1, 1, tt * TOP_K), lambda i: (jnp.minimum(i + 1, n_tiles - 1), 0, 0),
                               memory_space=pltpu.SMEM),
                  pl.BlockSpec(memory_space=pl.ANY),
                  pl.BlockSpec((tt * ROW_TILES, LANES), lambda i: (i, 0)),
                  pl.BlockSpec((tt, D), lambda i: (i, 0)),
                  pl.BlockSpec((tt, TOP_K), lambda i: (i, 0)),
                  pl.BlockSpec((1, 1, D), lambda i: (i // nS, 0, 5)),
                  pl.BlockSpec(swg.shape, const2),
                  pl.BlockSpec(swu.shape, const2),
                  pl.BlockSpec(swd.shape, const2),
                  pl.BlockSpec((1, D), const2)],
        out_specs=pl.BlockSpec((tt, D), lambda i: (i, 0)),
        scratch_shapes=[pltpu.VMEM((n_slots * tt * TOP_K * ROW_TILES, LANES), F32),
                        pltpu.SemaphoreType.DMA((n_slots,))],
        compiler_params=pltpu.CompilerParams(
            dimension_semantics=("arbitrary",), vmem_limit_bytes=VMEM_LIMIT),
        name="combine",
    )(pos_tiles, pos_tiles, ys, h2_tm, x1, gate_tm, mod3, swg, swu, swd, fg)


def _place_kernel(idx_ref, rank_ref, off_ref, pos_ref):
    tl = idx_ref.shape[1]
    rowid = lax.broadcasted_iota(jnp.int32, (N_EXPERTS, tl), 0)
    off = off_ref[...].astype(F32)
    for k in range(TOP_K):
        base = jnp.sum(jnp.where(rowid == idx_ref[k:k + 1, :], off, 0.0), axis=0, keepdims=True)
        pos_ref[k:k + 1, :] = base.astype(jnp.int32) + rank_ref[k:k + 1, :]


def _place(idx, rank, off, tl):
    T = idx.shape[1]
    tok = pl.BlockSpec((TOP_K, tl), lambda i: (0, i))
    return pl.pallas_call(
        _place_kernel,
        out_shape=jax.ShapeDtypeStruct((TOP_K, T), jnp.int32),
        grid=(T // tl,),
        in_specs=[tok, tok, pl.BlockSpec((N_EXPERTS, 1), lambda i: (0, 0))],
        out_specs=tok,
        name="place",
    )(idx, rank, off)


def _plan(counts, n_items_max):
    counts = counts[:, 0]
    off = jnp.cumsum(counts) - counts
    first_tile = off // EXPERT_TILE
    last_tile = (off + counts - 1) // EXPERT_TILE
    n_e = jnp.where(counts > 0, last_tile - first_tile + 1, 0)
    item_end = jnp.cumsum(n_e)
    item_start = item_end - n_e
    n_items = item_end[-1]
    ids = jnp.arange(n_items_max, dtype=jnp.int32)
    ids_c = jnp.minimum(ids, n_items - 1)
    item_e = jnp.minimum(jnp.sum(item_end[None, :] <= ids_c[:, None], axis=1), N_EXPERTS - 1).astype(jnp.int32)
    item_tile = (first_tile[item_e] + ids_c - item_start[item_e]).astype(jnp.int32)
    prev_tile = jnp.concatenate([jnp.full((1,), -1, jnp.int32), item_tile[:-1]])
    prev_e = jnp.concatenate([jnp.full((1,), -1, jnp.int32), item_e[:-1]])
    item_first = (item_tile != prev_tile).astype(jnp.int32)
    item_newe = (item_e != prev_e).astype(jnp.int32)
    item_slot = ((jnp.cumsum(item_newe) - 1) % 2).astype(jnp.int32)
    ids_e = jnp.arange(N_EXPERTS, dtype=jnp.int32)
    later = jnp.where((counts[None, :] > 0) & (ids_e[None, :] > ids_e[:, None]), ids_e[None, :], N_EXPERTS)
    next_e = jnp.min(later, axis=1)
    next_e = jnp.where(next_e < N_EXPERTS, next_e, -1).astype(jnp.int32)
    meta = (item_tile, item_e, item_first, item_newe, n_items.reshape(1).astype(jnp.int32),
            off.astype(jnp.int32), counts.astype(jnp.int32), item_slot, next_e[item_e])
    return off.astype(jnp.int32).reshape(N_EXPERTS, 1), meta


def kernel(x, c, ada_w, ada_b, norm1_g, w_in, hgrn_lb, hgrn_norm_g, pool_w, pool_b, pool_scale, w_up_a, w_up_b, w_out, norm2_g, router_w, router_bias, exp_w_gate, exp_w_up, exp_w_down, shared_w_gate, shared_w_up, shared_w_down, final_norm_g):
    B, S, D = x.shape
    T = B * S
    assert ada_w.shape[0] == 1, "single-layer trunk only: the final norm is fused into the combine step"
    lb_all = jnp.cumsum(jax.nn.softmax(hgrn_lb.astype(F32), axis=0), axis=0)
    c_pad = jnp.zeros((SUBLANES, D), F32).at[:B].set(c)
    n_items_max = T * TOP_K // EXPERT_TILE + N_EXPERTS - 1

    for l in range(1):
        mod = _ada(c_pad, ada_w[l], ada_b[l].reshape(1, -1))
        mod3 = mod[:B].reshape(B, 1, 6 * D)

        q, lf, k, v, sog, pm, sga, sgb = _inproj(
            x, mod3, norm1_g[l].reshape(1, D), w_in[l].astype(BF16), lb_all[l].reshape(1, HG_WIDTH),
            pool_w[l].astype(BF16), pool_b[l].reshape(len(POOL_WINDOWS), 1, POOL_GROUP),
            pool_scale[l].reshape(1, POOL_WIDTH), tm=256)
        oa = _hgrn(q, lf, k, v, sog, hgrn_norm_g[l].reshape(1, HG_DK), B, S, tb=512)

        x1, h2_tm, logits_t = _mix(
            x, oa, pm, sga, sgb, mod3, norm2_g[l].reshape(1, D), w_up_a[l].astype(BF16),
            w_up_b[l].astype(BF16), w_out[l].astype(BF16), router_w[l].T, tm=256)

        idx, gate, rank, counts = _route(logits_t, router_bias[l].reshape(N_EXPERTS, 1), tl=256)
        off, meta = _plan(counts, n_items_max)
        pos_tm = _place(idx, rank, off, tl=512).T.reshape(-1)

        tt_s = 256
        xs = _scatter(pos_tm.reshape(T // tt_s, 1, tt_s * TOP_K), h2_tm, tt_s)
        ys = _experts(meta, xs, exp_w_gate[l], exp_w_up[l], exp_w_down[l], n_items_max)

        tt_c = 128
        fg = final_norm_g.reshape(1, D)
        x = _combine(pos_tm.reshape(T // tt_c, 1, tt_c * TOP_K), ys, h2_tm, x1, gate.T, mod3,
                     shared_w_gate[l].astype(BF16), shared_w_up[l].astype(BF16),
                     shared_w_down[l].astype(BF16), fg, B, S, tt_c).reshape(B, S, D)
    return x
```

```python
import functools

import jax
import jax.numpy as jnp
from jax import lax
from jax.experimental import pallas as pl
from jax.experimental.pallas import tpu as pltpu

F32 = jnp.float32
BF16 = jnp.bfloat16
HIGHEST = lax.Precision.HIGHEST

D_MODEL = 1024
HG_WIDTH = 512
HG_DK = 128
HG_HEADS = 4
HG_CHUNK = 64
HG_SUB = 16
POOL_WIDTH = 512
POOL_WINDOWS = (2, 4, 8, 16)
POOL_GROUP = 128
POOL_HALO = 16
N_EXPERTS = 256
TOP_K = 8
N_GROUPS = 8
TOPK_GROUPS = 4
GROUP_SIZE = N_EXPERTS // N_GROUPS
D_EXPERT = 256
ROUTED_SCALE = 2.5
EPS = 1e-6

LANES = 128
SUBLANES = 8
ROW_TILES = D_MODEL // LANES
EXPERT_TILE = 128
TILE_RING = 4
TILE_AHEAD = TILE_RING - 1
VMEM_LIMIT = 56 * 1024 * 1024

COL_Q, COL_F, COL_I, COL_OG, COL_U, COL_GA, COL_GB = 0, 512, 1024, 1536, 2048, 2560, 3584


def _sigmoid(x):
    return 1.0 / (1.0 + jnp.exp(-x))


def _silu(x):
    return x * _sigmoid(x)


def _dot(a, b):
    return jnp.dot(a, b, preferred_element_type=F32)


def _dot_nt(a, b):
    return lax.dot_general(a, b, (((1,), (1,)), ((), ())), preferred_element_type=F32)


def _dot_tn(a, b):
    return lax.dot_general(a, b, (((0,), (0,)), ((), ())), preferred_element_type=F32)


def _row_chunks(x):
    return [x[:, j * LANES:(j + 1) * LANES] for j in range(ROW_TILES)]


def _load_rows(ref, n, first_row=0):
    return jnp.concatenate(
        [ref[pl.ds(first_row * ROW_TILES + j, n, stride=ROW_TILES), :] for j in range(ROW_TILES)], axis=1)


def _ada_kernel(c_ref, w_ref, b_ref, o_ref):
    cond = _silu(c_ref[...])
    o_ref[...] = jnp.dot(cond, w_ref[...], precision=HIGHEST, preferred_element_type=F32) + b_ref[...]


def _ada(c_pad, ada_w, ada_b):
    n = ada_w.shape[1]
    tn = 1536
    return pl.pallas_call(
        _ada_kernel,
        out_shape=jax.ShapeDtypeStruct((SUBLANES, n), F32),
        grid=(n // tn,),
        in_specs=[pl.BlockSpec((SUBLANES, D_MODEL), lambda j: (0, 0)),
                  pl.BlockSpec((D_MODEL, tn), lambda j: (0, j)),
                  pl.BlockSpec((1, tn), lambda j: (0, j))],
        out_specs=pl.BlockSpec((SUBLANES, tn), lambda j: (0, j)),
        compiler_params=pltpu.CompilerParams(vmem_limit_bytes=VMEM_LIMIT),
        name="ada",
    )(c_pad, ada_w, ada_b)


def _inproj_kernel(x_ref, sh_ref, sc_ref, g_ref, w_ref, lb_ref, pw_ref, pb_ref, ps_ref,
                   q_ref, lf_ref, k_ref, v_ref, sog_ref, pm_ref, sga_ref, sgb_ref, halo_ref):
    s = pl.program_id(1)
    tm = x_ref.shape[1]
    x = x_ref[0]
    h = x * lax.rsqrt(jnp.mean(x * x, axis=-1, keepdims=True) + EPS) * g_ref[...]
    h = h * (1.0 + sc_ref[0]) + sh_ref[0]
    hb = h.astype(BF16)

    def proj(lo, n):
        return _dot(hb, w_ref[:, lo:lo + n])

    q = proj(COL_Q, HG_WIDTH)
    q_ref[...] = _silu(q) * (HG_DK ** -0.5)
    sig = _sigmoid(proj(COL_F, HG_WIDTH))
    lb = lb_ref[...]
    lf_ref[...] = jnp.log(lb + (1.0 - lb) * sig)
    k_ref[...] = (1.0 - lb) * (1.0 - sig)
    v_ref[...] = proj(COL_I, HG_WIDTH)
    sog_ref[...] = _silu(proj(COL_OG, HG_WIDTH)).astype(BF16)
    sga_ref[...] = _sigmoid(proj(COL_GA, D_MODEL)).astype(BF16)
    sgb_ref[...] = _sigmoid(proj(COL_GB, D_MODEL)).astype(BF16)

    u = proj(COL_U, POOL_WIDTH)
    @pl.when(s == 0)
    def _():
        halo_ref[...] = jnp.zeros_like(halo_ref)

    ext = jnp.concatenate([halo_ref[...], u], axis=0)
    halo_ref[...] = u[tm - POOL_HALO:, :]
    s2 = ext + pltpu.roll(ext, 1, 0)
    s4 = s2 + pltpu.roll(s2, 2, 0)
    s8 = s4 + pltpu.roll(s4, 4, 0)
    s16 = s8 + pltpu.roll(s8, 8, 0)
    pos1 = (s * tm + 1 + lax.broadcasted_iota(jnp.int32, (tm, 1), 0)).astype(F32)
    for g, (w, sw) in enumerate(zip(POOL_WINDOWS, (s2, s4, s8, s16))):
        cols = slice(g * POOL_GROUP, (g + 1) * POOL_GROUP)
        m = sw[POOL_HALO:, cols] / jnp.minimum(pos1, float(w)) - u[:, cols]
        y = _dot(m.astype(BF16), pw_ref[g]) + pb_ref[g]
        pm_ref[:, cols] = (y * ps_ref[:, cols]).astype(BF16)


def _inproj(x, mod3, norm_g, w_in_b, lb, pool_w_b, pool_b, pool_scale, tm):
    B, S, D = x.shape
    T = B * S
    nS = S // tm
    row = lambda b, s: (b * nS + s, 0)
    const2 = lambda b, s: (0, 0)
    const3 = lambda b, s: (0, 0, 0)
    half = lambda dt: jax.ShapeDtypeStruct((T, HG_WIDTH), dt)
    full = lambda dt: jax.ShapeDtypeStruct((T, D), dt)
    return pl.pallas_call(
        _inproj_kernel,
        out_shape=(half(F32), half(F32), half(F32), half(F32), half(BF16), half(BF16), full(BF16), full(BF16)),
        grid=(B, nS),
        in_specs=[pl.BlockSpec((1, tm, D), lambda b, s: (b, s, 0)),
                  pl.BlockSpec((1, 1, D), lambda b, s: (b, 0, 0)),
                  pl.BlockSpec((1, 1, D), lambda b, s: (b, 0, 1)),
                  pl.BlockSpec((1, D), const2),
                  pl.BlockSpec(w_in_b.shape, const2),
                  pl.BlockSpec((1, HG_WIDTH), const2),
                  pl.BlockSpec(pool_w_b.shape, const3),
                  pl.BlockSpec(pool_b.shape, const3),
                  pl.BlockSpec((1, POOL_WIDTH), const2)],
        out_specs=(pl.BlockSpec((tm, HG_WIDTH), row),) * 6 + (pl.BlockSpec((tm, D), row),) * 2,
        scratch_shapes=[pltpu.VMEM((POOL_HALO, POOL_WIDTH), F32)],
        compiler_params=pltpu.CompilerParams(
            dimension_semantics=("arbitrary", "arbitrary"), vmem_limit_bytes=VMEM_LIMIT),
        name="inproj",
    )(x, mod3, mod3, norm_g, w_in_b, lb, pool_w_b, pool_b, pool_scale)


def _hgrn_kernel(q_ref, lf_ref, k_ref, v_ref, sog_ref, gn_ref, o_ref, st_ref):
    C = HG_CHUNK
    n_chunks = q_ref.shape[0] // C

    @pl.when(pl.program_id(1) == 0)
    def _():
        st_ref[...] = jnp.zeros_like(st_ref)

    r_i = lax.broadcasted_iota(jnp.int32, (C, C), 0)
    c_i = lax.broadcasted_iota(jnp.int32, (C, C), 1)
    tril = (c_i <= r_i).astype(F32)
    blk_end = (c_i <= (r_i // HG_SUB) * HG_SUB + (HG_SUB - 1)).astype(F32)
    row = lax.broadcasted_iota(jnp.int32, (C, HG_DK), 0)
    row_in_sub = row % HG_SUB
    row_sub = row // HG_SUB
    n_sub = C // HG_SUB

    def chunk(ci, carry):
        rs = pl.ds(pl.multiple_of(ci * C, C), C)
        lf_all = lf_ref[rs, :]
        b_all = jnp.dot(tril, lf_all, precision=HIGHEST, preferred_element_type=F32)
        bn_all = jnp.dot(blk_end, lf_all, precision=HIGHEST, preferred_element_type=F32)
        for h in range(HG_HEADS):
            cs = slice(h * HG_DK, (h + 1) * HG_DK)
            q = q_ref[rs, cs]
            k = k_ref[rs, cs]
            v = v_ref[rs, cs]
            b = b_all[:, cs]
            bn = bn_all[:, cs]
            vb = v.astype(BF16)

            kt = k * jnp.exp(bn - b)
            q_parts, k_parts = [], []
            for j in range(n_sub - 1):
                bj = b[HG_SUB * j + HG_SUB - 1:HG_SUB * (j + 1), :]
                after = row >= HG_SUB * (j + 1)
                q_parts.append(q * jnp.exp(jnp.where(after, b - bj, -jnp.inf)))
                k_parts.append(jnp.where(row_sub == j, kt, 0.0))
            qcat = jnp.concatenate(q_parts, axis=1).astype(BF16)
            kcat = jnp.concatenate(k_parts, axis=1).astype(BF16)
            o = _dot(_dot_nt(qcat, kcat).astype(BF16), vb)

            o = o + jnp.sum(q * k, axis=-1, keepdims=True) * v
            for d in range(1, HG_SUB):
                kd = pltpu.roll(k, d, 0)
                bd = pltpu.roll(b, d, 0)
                vd = pltpu.roll(v, d, 0)
                e = jnp.exp(jnp.where(row_in_sub >= d, b - bd, -jnp.inf))
                o = o + jnp.sum(q * kd * e, axis=-1, keepdims=True) * vd

            st = st_ref[h]
            o = o + _dot_nt((q * jnp.exp(b)).astype(BF16), st.astype(BF16))
            b_end = b[C - 1:C, :]
            k_end = (k * jnp.exp(b_end - b)).astype(BF16)
            st_ref[h] = st * jnp.exp(b_end) + _dot_tn(vb, k_end)

            on = o * lax.rsqrt(jnp.mean(o * o, axis=-1, keepdims=True) + EPS) * gn_ref[...]
            o_ref[rs, cs] = (on * sog_ref[rs, cs].astype(F32)).astype(BF16)
        return carry

    lax.fori_loop(0, n_chunks, chunk, 0)


def _hgrn(q, lf, k, v, sog, gn, B, S, tb):
    T = B * S
    nS = S // tb
    row = lambda b, s: (b * nS + s, 0)
    blk = pl.BlockSpec((tb, HG_WIDTH), row)
    return pl.pallas_call(
        _hgrn_kernel,
        out_shape=jax.ShapeDtypeStruct((T, HG_WIDTH), BF16),
        grid=(B, nS),
        in_specs=[blk, blk, blk, blk, blk, pl.BlockSpec((1, HG_DK), lambda b, s: (0, 0))],
        out_specs=blk,
        scratch_shapes=[pltpu.VMEM((HG_HEADS, HG_DK, HG_DK), F32)],
        compiler_params=pltpu.CompilerParams(
            dimension_semantics=("arbitrary", "arbitrary"), vmem_limit_bytes=VMEM_LIMIT),
        name="hgrn",
    )(q, lf, k, v, sog, gn)


def _mix_kernel(x_ref, oa_ref, pm_ref, sga_ref, sgb_ref, g1_ref, sh2_ref, sc2_ref, n2_ref,
                wua_ref, wub_ref, wo_ref, rwt_ref, x1_ref, h2_ref, lg_ref):
    tm = x_ref.shape[1]
    ya = _dot(oa_ref[...], wua_ref[...])
    yb = _dot(pm_ref[...], wub_ref[...])
    mix = sga_ref[...].astype(F32) * ya + sgb_ref[...].astype(F32) * yb
    x1 = x_ref[0] + g1_ref[0] * _dot(mix.astype(BF16), wo_ref[...])
    x1_ref[...] = x1
    h2 = x1 * lax.rsqrt(jnp.mean(x1 * x1, axis=-1, keepdims=True) + EPS) * n2_ref[...]
    h2 = h2 * (1.0 + sc2_ref[0]) + sh2_ref[0]
    for j, chunk in enumerate(_row_chunks(h2)):
        h2_ref[pl.ds(j, tm, stride=ROW_TILES), :] = chunk
    lg_ref[...] = lax.dot_general(rwt_ref[...], h2, (((1,), (1,)), ((), ())),
                                  precision=HIGHEST, preferred_element_type=F32)


def _mix(x, oa, pm, sga, sgb, mod3, norm2_g, wua, wub, wo, rwt, tm):
    B, S, D = x.shape
    T = B * S
    nS = S // tm
    row = lambda b, s: (b * nS + s, 0)
    const2 = lambda b, s: (0, 0)
    return pl.pallas_call(
        _mix_kernel,
        out_shape=(jax.ShapeDtypeStruct((T, D), F32),
                   jax.ShapeDtypeStruct((T * ROW_TILES, LANES), F32),
                   jax.ShapeDtypeStruct((N_EXPERTS, T), F32)),
        grid=(B, nS),
        in_specs=[pl.BlockSpec((1, tm, D), lambda b, s: (b, s, 0)),
                  pl.BlockSpec((tm, HG_WIDTH), row),
                  pl.BlockSpec((tm, POOL_WIDTH), row),
                  pl.BlockSpec((tm, D), row),
                  pl.BlockSpec((tm, D), row),
                  pl.BlockSpec((1, 1, D), lambda b, s: (b, 0, 2)),
                  pl.BlockSpec((1, 1, D), lambda b, s: (b, 0, 3)),
                  pl.BlockSpec((1, 1, D), lambda b, s: (b, 0, 4)),
                  pl.BlockSpec((1, D), const2),
                  pl.BlockSpec(wua.shape, const2),
                  pl.BlockSpec(wub.shape, const2),
                  pl.BlockSpec(wo.shape, const2),
                  pl.BlockSpec(rwt.shape, const2)],
        out_specs=(pl.BlockSpec((tm, D), row),
                   pl.BlockSpec((tm * ROW_TILES, LANES), row),
                   pl.BlockSpec((N_EXPERTS, tm), lambda b, s: (0, b * nS + s))),
        compiler_params=pltpu.CompilerParams(
            dimension_semantics=("arbitrary", "arbitrary"), vmem_limit_bytes=VMEM_LIMIT),
        name="mix",
    )(x, oa, pm, sga, sgb, mod3, mod3, mod3, norm2_g, wua, wub, wo, rwt)


def _route_kernel(lg_ref, bias_ref, idx_ref, gate_ref, rank_ref, cnt_ref, carry_ref):
    tl = lg_ref.shape[1]
    neg = -jnp.inf

    @pl.when(pl.program_id(0) == 0)
    def _():
        carry_ref[...] = jnp.zeros_like(carry_ref)

    s = _sigmoid(lg_ref[...])
    biased = s + bias_ref[...]
    rowid = lax.broadcasted_iota(jnp.int32, (N_EXPERTS, tl), 0)

    def first_argmax(x, ids, sentinel):
        m = jnp.max(x, axis=0, keepdims=True)
        return jnp.min(jnp.where(x == m, ids, sentinel), axis=0, keepdims=True), m

    gscores = []
    for g in range(N_GROUPS):
        xg = biased[g * GROUP_SIZE:(g + 1) * GROUP_SIZE, :]
        rid = g * GROUP_SIZE + lax.broadcasted_iota(jnp.int32, (GROUP_SIZE, tl), 0)
        first, m1 = first_argmax(xg, rid, N_EXPERTS)
        m2 = jnp.max(jnp.where(rid == first, neg, xg), axis=0, keepdims=True)
        gscores.append(m1 + m2)
    blocks = []
    for g in range(N_GROUPS):
        beaten = jnp.zeros((1, tl), F32)
        for o in range(N_GROUPS):
            if o != g:
                wins = (gscores[o] >= gscores[g]) if o < g else (gscores[o] > gscores[g])
                beaten = beaten + jnp.where(wins, 1.0, 0.0)
        xg = biased[g * GROUP_SIZE:(g + 1) * GROUP_SIZE, :]
        blocks.append(jnp.where(beaten < float(TOPK_GROUPS), xg, neg))
    masked = jnp.concatenate(blocks, axis=0)

    idxs, gates = [], []
    chosen = jnp.zeros((N_EXPERTS, tl), F32)
    for _ in range(TOP_K):
        first, _m = first_argmax(masked, rowid, N_EXPERTS)
        sel = rowid == first
        gates.append(jnp.sum(jnp.where(sel, s, 0.0), axis=0, keepdims=True))
        idxs.append(first)
        chosen = jnp.where(sel, 1.0, chosen)
        masked = jnp.where(sel, neg, masked)
    gate_sum = functools.reduce(lambda a, b: a + b, gates)
    for k in range(TOP_K):
        gate_ref[k:k + 1, :] = gates[k] / gate_sum * ROUTED_SCALE
        idx_ref[k:k + 1, :] = idxs[k]

    lr = lax.broadcasted_iota(jnp.int32, (tl, tl), 0)
    lc = lax.broadcasted_iota(jnp.int32, (tl, tl), 1)
    prefix = (lr <= lc).astype(BF16)
    cnt_incl = _dot(chosen.astype(BF16), prefix)
    carry = carry_ref[...]
    rank_excl = cnt_incl - chosen + carry
    for k in range(TOP_K):
        rank_k = jnp.sum(jnp.where(rowid == idxs[k], rank_excl, 0.0), axis=0, keepdims=True)
        rank_ref[k:k + 1, :] = rank_k.astype(jnp.int32)
    carry = carry + jnp.sum(chosen, axis=1, keepdims=True)
    carry_ref[...] = carry
    cnt_ref[...] = carry.astype(jnp.int32)


def _route(logits_t, bias, tl):
    T = logits_t.shape[1]
    tok = lambda i: (0, i)
    return pl.pallas_call(
        _route_kernel,
        out_shape=(jax.ShapeDtypeStruct((TOP_K, T), jnp.int32),
                   jax.ShapeDtypeStruct((TOP_K, T), F32),
                   jax.ShapeDtypeStruct((TOP_K, T), jnp.int32),
                   jax.ShapeDtypeStruct((N_EXPERTS, 1), jnp.int32)),
        grid=(T // tl,),
        in_specs=[pl.BlockSpec((N_EXPERTS, tl), tok), pl.BlockSpec((N_EXPERTS, 1), lambda i: (0, 0))],
        out_specs=(pl.BlockSpec((TOP_K, tl), tok), pl.BlockSpec((TOP_K, tl), tok),
                   pl.BlockSpec((TOP_K, tl), tok), pl.BlockSpec((N_EXPERTS, 1), lambda i: (0, 0))),
        scratch_shapes=[pltpu.VMEM((N_EXPERTS, 1), F32)],
        compiler_params=pltpu.CompilerParams(
            dimension_semantics=("arbitrary",), vmem_limit_bytes=VMEM_LIMIT),
        name="route",
    )(logits_t, bias)


def _as_rows(ref):
    return ref.reshape(ref.shape[0] // ROW_TILES, ROW_TILES, LANES)


def _wait_rows(rows_ref, n, sem):
    pltpu.make_async_copy(rows_ref.at[pl.ds(0, n)], rows_ref.at[pl.ds(0, n)], sem).wait()


def _scatter_kernel(pos_ref, h2_ref, xs_ref, sem):
    src = _as_rows(h2_ref)
    dst = _as_rows(xs_ref)
    tt = src.shape[0]

    def start(t, c):
        for k in range(TOP_K):
            pltpu.make_async_copy(src.at[t], dst.at[pos_ref[0, 0, t * TOP_K + k]], sem).start()
        return c

    lax.fori_loop(0, tt, start, 0)
    _wait_rows(dst, tt * TOP_K, sem)


def _scatter(pos_tiles, h2_tm, tt):
    n_rows = h2_tm.shape[0] // ROW_TILES * TOP_K
    return pl.pallas_call(
        _scatter_kernel,
        out_shape=jax.ShapeDtypeStruct((n_rows * ROW_TILES, LANES), F32),
        grid=(pos_tiles.shape[0],),
        in_specs=[pl.BlockSpec((1, 1, tt * TOP_K), lambda i: (i, 0, 0), memory_space=pltpu.SMEM),
                  pl.BlockSpec((tt * ROW_TILES, LANES), lambda i: (i, 0))],
        out_specs=pl.BlockSpec(memory_space=pl.ANY),
        scratch_shapes=[pltpu.SemaphoreType.DMA],
        compiler_params=pltpu.CompilerParams(
            dimension_semantics=("arbitrary",), vmem_limit_bytes=VMEM_LIMIT),
        name="scatter",
    )(pos_tiles, h2_tm)


def _experts_kernel(tile_ref, exp_ref, first_ref, last_ref, newe_ref, nitems_ref, off_ref, cnt_ref, slot_ref,
                    nexte_ref, xs_hbm, wg_hbm, wu_hbm, wd_hbm, ys_hbm,
                    xbuf_ref, ybuf_ref, sg_ref, su_ref, sd_ref, wgb_ref, wub_ref, wdb_ref, xsem, ysem, wsem):
    i = pl.program_id(0)
    tr = EXPERT_TILE
    tile_rows = tr * ROW_TILES
    n_tiles = xs_hbm.shape[0] // tile_rows
    n_items = nitems_ref[0]

    def ring(t):
        return pl.ds(pl.multiple_of((t % TILE_RING) * tile_rows, tile_rows), tile_rows)

    def hbm_tile(t):
        return pl.ds(pl.multiple_of(t * tile_rows, tile_rows), tile_rows)

    def x_copy(t):
        return pltpu.make_async_copy(xs_hbm.at[hbm_tile(t)], xbuf_ref.at[ring(t)], xsem.at[t % TILE_RING])

    def y_copy(t):
        return pltpu.make_async_copy(ybuf_ref.at[ring(t)], ys_hbm.at[hbm_tile(t)], ysem.at[t % TILE_RING])

    def weight_copies(e, slot):
        return (pltpu.make_async_copy(wg_hbm.at[e], sg_ref.at[slot], wsem.at[slot]),
                pltpu.make_async_copy(wu_hbm.at[e], su_ref.at[slot], wsem.at[slot]),
                pltpu.make_async_copy(wd_hbm.at[e], sd_ref.at[slot], wsem.at[slot]))

    @pl.when(i < n_items)
    def _():
        e = exp_ref[i]
        t = tile_ref[i]

        @pl.when(i == 0)
        def _():
            for t0 in range(TILE_AHEAD):
                x_copy(t0).start()

        @pl.when(first_ref[i] == 1)
        def _():
            @pl.when(t + TILE_AHEAD < n_tiles)
            def _():
                x_copy(t + TILE_AHEAD).start()

            x_copy(t).wait()

            @pl.when(t >= TILE_RING)
            def _():
                y_copy(t - TILE_RING).wait()

        @pl.when(newe_ref[i] == 1)
        def _():
            slot = slot_ref[i]
            nxt = nexte_ref[i]

            @pl.when(i == 0)
            def _():
                for c in weight_copies(e, slot):
                    c.start()

            @pl.when(nxt >= 0)
            def _():
                for c in weight_copies(nxt, 1 - slot):
                    c.start()

            for c in weight_copies(e, slot):
                c.wait()
            wgb_ref[...] = sg_ref[slot].astype(BF16)
            wub_ref[...] = su_ref[slot].astype(BF16)
            wdb_ref[...] = sd_ref[slot].astype(BF16)

        buf_row = (t % TILE_RING) * tr
        x = _load_rows(xbuf_ref, tr, first_row=buf_row).astype(BF16)
        hm = (_silu(_dot(x, wgb_ref[...])) * _dot(x, wub_ref[...])).astype(BF16)
        chunks = _row_chunks(_dot(hm, wdb_ref[...]))
        row = t * tr + lax.broadcasted_iota(jnp.int32, (tr, 1), 0)
        lo = off_ref[e]
        mine = (row >= lo) & (row < lo + cnt_ref[e])

        @pl.when(first_ref[i] == 1)
        def _():
            for j in range(ROW_TILES):
                ybuf_ref[pl.ds(buf_row * ROW_TILES + j, tr, stride=ROW_TILES), :] = chunks[j]

        @pl.when(first_ref[i] == 0)
        def _():
            for j in range(ROW_TILES):
                sl = pl.ds(buf_row * ROW_TILES + j, tr, stride=ROW_TILES)
                ybuf_ref[sl, :] = jnp.where(mine, chunks[j], ybuf_ref[sl, :])

        @pl.when(last_ref[i] == 1)
        def _():
            y_copy(t).start()

        @pl.when(i == n_items - 1)
        def _():
            for t0 in range(n_tiles - TILE_RING, n_tiles):
                y_copy(t0).wait()


def _experts(meta, xs, wg, wu, wd, n_items_max):
    tile_rows = EXPERT_TILE * ROW_TILES
    assert xs.shape[0] % tile_rows == 0 and xs.shape[0] // tile_rows >= TILE_RING
    hbm = pl.BlockSpec(memory_space=pl.ANY)
    n_slots = 2
    grid_spec = pltpu.PrefetchScalarGridSpec(
        num_scalar_prefetch=len(meta),
        grid=(n_items_max,),
        in_specs=[hbm, hbm, hbm, hbm],
        out_specs=hbm,
        scratch_shapes=[pltpu.VMEM((TILE_RING * tile_rows, LANES), F32),
                        pltpu.VMEM((TILE_RING * tile_rows, LANES), F32),
                        pltpu.VMEM((n_slots, D_MODEL, D_EXPERT), F32),
                        pltpu.VMEM((n_slots, D_MODEL, D_EXPERT), F32),
                        pltpu.VMEM((n_slots, D_EXPERT, D_MODEL), F32),
                        pltpu.VMEM((D_MODEL, D_EXPERT), BF16),
                        pltpu.VMEM((D_MODEL, D_EXPERT), BF16),
                        pltpu.VMEM((D_EXPERT, D_MODEL), BF16),
                        pltpu.SemaphoreType.DMA((TILE_RING,)),
                        pltpu.SemaphoreType.DMA((TILE_RING,)),
                        pltpu.SemaphoreType.DMA((n_slots,))])
    return pl.pallas_call(
        _experts_kernel,
        out_shape=jax.ShapeDtypeStruct(xs.shape, F32),
        grid_spec=grid_spec,
        compiler_params=pltpu.CompilerParams(
            dimension_semantics=("arbitrary",), vmem_limit_bytes=VMEM_LIMIT),
        name="experts",
    )(*meta, xs, wg, wu, wd)


def _combine_kernel(pos_ref, pos_next_ref, ys_ref, h2_ref, x1_ref, gate_ref, g2_ref, swg_ref, swu_ref, swd_ref,
                    fg_ref, out_ref, buf_ref, sem):
    i = pl.program_id(0)
    tt = x1_ref.shape[0]
    slot_rows = tt * TOP_K
    src = _as_rows(ys_ref)
    dst = _as_rows(buf_ref)

    def gather(p_ref, slot):
        def start(t, c):
            for k in range(TOP_K):
                pltpu.make_async_copy(src.at[p_ref[0, 0, t * TOP_K + k]],
                                      dst.at[slot * slot_rows + k * tt + t], sem.at[slot]).start()
            return c

        lax.fori_loop(0, tt, start, 0)

    slot = i % 2

    @pl.when(i == 0)
    def _():
        gather(pos_ref, slot)

    @pl.when(i + 1 < pl.num_programs(0))
    def _():
        gather(pos_next_ref, 1 - slot)

    h2 = _load_rows(h2_ref, tt).astype(BF16)
    hm = (_silu(_dot(h2, swg_ref[...])) * _dot(h2, swu_ref[...])).astype(BF16)
    acc = _dot(hm, swd_ref[...])

    _wait_rows(dst, slot_rows, sem.at[slot])

    gate = gate_ref[...]
    for k in range(TOP_K):
        acc = acc + gate[:, k:k + 1] * _load_rows(buf_ref, tt, first_row=slot * slot_rows + k * tt)
    x2 = x1_ref[...] + g2_ref[0] * acc
    out_ref[...] = x2 * lax.rsqrt(jnp.mean(x2 * x2, axis=-1, keepdims=True) + EPS) * fg_ref[...]


def _combine(pos_tiles, ys, h2_tm, x1, gate_tm, mod3, swg, swu, swd, fg, B, S, tt):
    T, D = x1.shape
    nS = S // tt
    const2 = lambda i: (0, 0)
    n_tiles = T // tt
    n_slots = 2
    return pl.pallas_call(
        _combine_kernel,
        out_shape=jax.ShapeDtypeStruct((T, D), F32),
        grid=(n_tiles,),
        in_specs=[pl.BlockSpec((1, 1, tt * TOP_K), lambda i: (i, 0, 0), memory_space=pltpu.SMEM),
                  pl.BlockSpec((1, 1, tt * TOP_K), lambda i: (jnp.minimum(i + 1, n_tiles - 1), 0, 0),
                               memory_space=pltpu.SMEM),
                  pl.BlockSpec(memory_space=pl.ANY),
                  pl.BlockSpec((tt * ROW_TILES, LANES), lambda i: (i, 0)),
                  pl.BlockSpec((tt, D), lambda i: (i, 0)),
                  pl.BlockSpec((tt, TOP_K), lambda i: (i, 0)),
                  pl.BlockSpec((1, 1, D), lambda i: (i // nS, 0, 5)),
                  pl.BlockSpec(swg.shape, const2),
                  pl.BlockSpec(swu.shape, const2),
                  pl.BlockSpec(swd.shape, const2),
                  pl.BlockSpec((1, D), const2)],
        out_specs=pl.BlockSpec((tt, D), lambda i: (i, 0)),
        scratch_shapes=[pltpu.VMEM((n_slots * tt * TOP_K * ROW_TILES, LANES), F32),
                        pltpu.SemaphoreType.DMA((n_slots,))],
        compiler_params=pltpu.CompilerParams(
            dimension_semantics=("arbitrary",), vmem_limit_bytes=VMEM_LIMIT),
        name="combine",
    )(pos_tiles, pos_tiles, ys, h2_tm, x1, gate_tm, mod3, swg, swu, swd, fg)


def _place_kernel(idx_ref, rank_ref, off_ref, pos_ref):
    tl = idx_ref.shape[1]
    rowid = lax.broadcasted_iota(jnp.int32, (N_EXPERTS, tl), 0)
    off = off_ref[...].astype(F32)
    for k in range(TOP_K):
        base = jnp.sum(jnp.where(rowid == idx_ref[k:k + 1, :], off, 0.0), axis=0, keepdims=True)
        pos_ref[k:k + 1, :] = base.astype(jnp.int32) + rank_ref[k:k + 1, :]


def _place(idx, rank, off, tl):
    T = idx.shape[1]
    tok = pl.BlockSpec((TOP_K, tl), lambda i: (0, i))
    return pl.pallas_call(
        _place_kernel,
        out_shape=jax.ShapeDtypeStruct((TOP_K, T), jnp.int32),
        grid=(T // tl,),
        in_specs=[tok, tok, pl.BlockSpec((N_EXPERTS, 1), lambda i: (0, 0))],
        out_specs=tok,
        name="place",
    )(idx, rank, off)


def _plan(counts, n_items_max):
    counts = counts[:, 0]
    off = jnp.cumsum(counts) - counts
    first_tile = off // EXPERT_TILE
    last_tile = (off + counts - 1) // EXPERT_TILE
    n_e = jnp.where(counts > 0, last_tile - first_tile + 1, 0)
    item_end = jnp.cumsum(n_e)
    item_start = item_end - n_e
    n_items = item_end[-1]
    ids = jnp.arange(n_items_max, dtype=jnp.int32)
    ids_c = jnp.minimum(ids, n_items - 1)
    item_e = jnp.minimum(jnp.sum(item_end[None, :] <= ids_c[:, None], axis=1), N_EXPERTS - 1).astype(jnp.int32)
    item_tile = (first_tile[item_e] + ids_c - item_start[item_e]).astype(jnp.int32)
    prev_tile = jnp.concatenate([jnp.full((1,), -1, jnp.int32), item_tile[:-1]])
    prev_e = jnp.concatenate([jnp.full((1,), -1, jnp.int32), item_e[:-1]])
    next_tile = jnp.concatenate([item_tile[1:], jnp.full((1,), -1, jnp.int32)])
    item_first = (item_tile != prev_tile).astype(jnp.int32)
    item_last = ((item_tile != next_tile) | (ids == n_items - 1)).astype(jnp.int32)
    item_newe = (item_e != prev_e).astype(jnp.int32)
    item_slot = ((jnp.cumsum(item_newe) - 1) % 2).astype(jnp.int32)
    ids_e = jnp.arange(N_EXPERTS, dtype=jnp.int32)
    later = jnp.where((counts[None, :] > 0) & (ids_e[None, :] > ids_e[:, None]), ids_e[None, :], N_EXPERTS)
    next_e = jnp.min(later, axis=1)
    next_e = jnp.where(next_e < N_EXPERTS, next_e, -1).astype(jnp.int32)
    meta = (item_tile, item_e, item_first, item_last, item_newe, n_items.reshape(1).astype(jnp.int32),
            off.astype(jnp.int32), counts.astype(jnp.int32), item_slot, next_e[item_e])
    return off.astype(jnp.int32).reshape(N_EXPERTS, 1), meta


def kernel(x, c, ada_w, ada_b, norm1_g, w_in, hgrn_lb, hgrn_norm_g, pool_w, pool_b, pool_scale, w_up_a, w_up_b, w_out, norm2_g, router_w, router_bias, exp_w_gate, exp_w_up, exp_w_down, shared_w_gate, shared_w_up, shared_w_down, final_norm_g):
    B, S, D = x.shape
    T = B * S
    assert ada_w.shape[0] == 1, "single-layer trunk only: the final norm is fused into the combine step"
    lb_all = jnp.cumsum(jax.nn.softmax(hgrn_lb.astype(F32), axis=0), axis=0)
    c_pad = jnp.zeros((SUBLANES, D), F32).at[:B].set(c)
    n_items_max = T * TOP_K // EXPERT_TILE + N_EXPERTS - 1

    for l in range(1):
        mod = _ada(c_pad, ada_w[l], ada_b[l].reshape(1, -1))
        mod3 = mod[:B].reshape(B, 1, 6 * D)

        q, lf, k, v, sog, pm, sga, sgb = _inproj(
            x, mod3, norm1_g[l].reshape(1, D), w_in[l].astype(BF16), lb_all[l].reshape(1, HG_WIDTH),
            pool_w[l].astype(BF16), pool_b[l].reshape(len(POOL_WINDOWS), 1, POOL_GROUP),
            pool_scale[l].reshape(1, POOL_WIDTH), tm=256)
        oa = _hgrn(q, lf, k, v, sog, hgrn_norm_g[l].reshape(1, HG_DK), B, S, tb=512)

        x1, h2_tm, logits_t = _mix(
            x, oa, pm, sga, sgb, mod3, norm2_g[l].reshape(1, D), w_up_a[l].astype(BF16),
            w_up_b[l].astype(BF16), w_out[l].astype(BF16), router_w[l].T, tm=256)

        idx, gate, rank, counts = _route(logits_t, router_bias[l].reshape(N_EXPERTS, 1), tl=256)
        off, meta = _plan(counts, n_items_max)
        pos_tm = _place(idx, rank, off, tl=512).T.reshape(-1)

        tt_s = 256
        xs = _scatter(pos_tm.reshape(T // tt_s, 1, tt_s * TOP_K), h2_tm, tt_s)
        ys = _experts(meta, xs, exp_w_gate[l], exp_w_up[l], exp_w_down[l], n_items_max)

        tt_c = 128
        fg = final_norm_g.reshape(1, D)
        x = _combine(pos_tm.reshape(T // tt_c, 1, tt_c * TOP_K), ys, h2_tm, x1, gate.T, mod3,
                     shared_w_gate[l].astype(BF16), shared_w_up[l].astype(BF16),
                     shared_w_down[l].astype(BF16), fg, B, S, tt_c).reshape(B, S, D)
    return x
```

```python
import functools

import jax
import jax.numpy as jnp
from jax import lax
from jax.experimental import pallas as pl
from jax.experimental.pallas import tpu as pltpu

F32 = jnp.float32
BF16 = jnp.bfloat16
HIGHEST = lax.Precision.HIGHEST

D_MODEL = 1024
HG_WIDTH = 512
HG_DK = 128
HG_HEADS = 4
HG_CHUNK = 64
HG_BLK = 16
HG_SUB = 8
POOL_WIDTH = 512
POOL_WINDOWS = (2, 4, 8, 16)
POOL_GROUP = 128
POOL_HALO = 16
N_EXPERTS = 256
TOP_K = 8
N_GROUPS = 8
TOPK_GROUPS = 4
GROUP_SIZE = N_EXPERTS // N_GROUPS
D_EXPERT = 256
ROUTED_SCALE = 2.5
EPS = 1e-6

LANES = 128
SUBLANES = 8
ROW_TILES = D_MODEL // LANES
EXPERT_TILE = 128
TILE_RING = 8
TILE_AHEAD = TILE_RING - 1
VMEM_LIMIT = 56 * 1024 * 1024

COL_Q, COL_F, COL_I, COL_OG, COL_U, COL_GA, COL_GB = 0, 512, 1024, 1536, 2048, 2560, 3584


def _sigmoid(x):
    return 1.0 / (1.0 + jnp.exp(-x))


def _silu(x):
    return x * _sigmoid(x)


def _dot(a, b):
    return jnp.dot(a, b, preferred_element_type=F32)


def _dot_nt(a, b):
    return lax.dot_general(a, b, (((1,), (1,)), ((), ())), preferred_element_type=F32)


def _dot_tn(a, b):
    return lax.dot_general(a, b, (((0,), (0,)), ((), ())), preferred_element_type=F32)


def _row_chunks(x):
    return [x[:, j * LANES:(j + 1) * LANES] for j in range(ROW_TILES)]


def _load_rows(ref, n, first_row=0):
    return jnp.concatenate(
        [ref[pl.ds(first_row * ROW_TILES + j, n, stride=ROW_TILES), :] for j in range(ROW_TILES)], axis=1)


def _ada_kernel(c_ref, w_ref, b_ref, o_ref):
    cond = _silu(c_ref[...])
    o_ref[...] = jnp.dot(cond, w_ref[...], precision=HIGHEST, preferred_element_type=F32) + b_ref[...]


def _ada(c_pad, ada_w, ada_b):
    n = ada_w.shape[1]
    tn = 1536
    return pl.pallas_call(
        _ada_kernel,
        out_shape=jax.ShapeDtypeStruct((SUBLANES, n), F32),
        grid=(n // tn,),
        in_specs=[pl.BlockSpec((SUBLANES, D_MODEL), lambda j: (0, 0)),
                  pl.BlockSpec((D_MODEL, tn), lambda j: (0, j)),
                  pl.BlockSpec((1, tn), lambda j: (0, j))],
        out_specs=pl.BlockSpec((SUBLANES, tn), lambda j: (0, j)),
        compiler_params=pltpu.CompilerParams(vmem_limit_bytes=VMEM_LIMIT),
        name="ada",
    )(c_pad, ada_w, ada_b)


def _inproj_kernel(x_ref, sh_ref, sc_ref, g_ref, w_ref, lb_ref, pw_ref, pb_ref, ps_ref,
                   q_ref, lf_ref, k_ref, v_ref, sog_ref, pm_ref, sga_ref, sgb_ref, halo_ref):
    s = pl.program_id(1)
    tm = x_ref.shape[1]
    x = x_ref[0]
    h = x * lax.rsqrt(jnp.mean(x * x, axis=-1, keepdims=True) + EPS) * g_ref[...]
    h = h * (1.0 + sc_ref[0]) + sh_ref[0]
    hb = h.astype(BF16)

    def proj(lo, n):
        return _dot(hb, w_ref[:, lo:lo + n])

    q = proj(COL_Q, HG_WIDTH)
    q_ref[...] = _silu(q) * (HG_DK ** -0.5)
    sig = _sigmoid(proj(COL_F, HG_WIDTH))
    lb = lb_ref[...]
    lf_ref[...] = jnp.log(lb + (1.0 - lb) * sig)
    k_ref[...] = (1.0 - lb) * (1.0 - sig)
    v_ref[...] = proj(COL_I, HG_WIDTH)
    sog_ref[...] = _silu(proj(COL_OG, HG_WIDTH)).astype(BF16)
    sga_ref[...] = _sigmoid(proj(COL_GA, D_MODEL)).astype(BF16)
    sgb_ref[...] = _sigmoid(proj(COL_GB, D_MODEL)).astype(BF16)

    u = proj(COL_U, POOL_WIDTH)
    @pl.when(s == 0)
    def _():
        halo_ref[...] = jnp.zeros_like(halo_ref)

    ext = jnp.concatenate([halo_ref[...], u], axis=0)
    halo_ref[...] = u[tm - POOL_HALO:, :]
    s2 = ext + pltpu.roll(ext, 1, 0)
    s4 = s2 + pltpu.roll(s2, 2, 0)
    s8 = s4 + pltpu.roll(s4, 4, 0)
    s16 = s8 + pltpu.roll(s8, 8, 0)
    pos1 = (s * tm + 1 + lax.broadcasted_iota(jnp.int32, (tm, 1), 0)).astype(F32)
    for g, (w, sw) in enumerate(zip(POOL_WINDOWS, (s2, s4, s8, s16))):
        cols = slice(g * POOL_GROUP, (g + 1) * POOL_GROUP)
        m = sw[POOL_HALO:, cols] / jnp.minimum(pos1, float(w)) - u[:, cols]
        y = _dot(m.astype(BF16), pw_ref[g]) + pb_ref[g]
        pm_ref[:, cols] = (y * ps_ref[:, cols]).astype(BF16)


def _inproj(x, mod3, norm_g, w_in_b, lb, pool_w_b, pool_b, pool_scale, tm):
    B, S, D = x.shape
    T = B * S
    nS = S // tm
    row = lambda b, s: (b * nS + s, 0)
    const2 = lambda b, s: (0, 0)
    const3 = lambda b, s: (0, 0, 0)
    half = lambda dt: jax.ShapeDtypeStruct((T, HG_WIDTH), dt)
    full = lambda dt: jax.ShapeDtypeStruct((T, D), dt)
    return pl.pallas_call(
        _inproj_kernel,
        out_shape=(half(F32), half(F32), half(F32), half(F32), half(BF16), half(BF16), full(BF16), full(BF16)),
        grid=(B, nS),
        in_specs=[pl.BlockSpec((1, tm, D), lambda b, s: (b, s, 0)),
                  pl.BlockSpec((1, 1, D), lambda b, s: (b, 0, 0)),
                  pl.BlockSpec((1, 1, D), lambda b, s: (b, 0, 1)),
                  pl.BlockSpec((1, D), const2),
                  pl.BlockSpec(w_in_b.shape, const2),
                  pl.BlockSpec((1, HG_WIDTH), const2),
                  pl.BlockSpec(pool_w_b.shape, const3),
                  pl.BlockSpec(pool_b.shape, const3),
                  pl.BlockSpec((1, POOL_WIDTH), const2)],
        out_specs=(pl.BlockSpec((tm, HG_WIDTH), row),) * 6 + (pl.BlockSpec((tm, D), row),) * 2,
        scratch_shapes=[pltpu.VMEM((POOL_HALO, POOL_WIDTH), F32)],
        compiler_params=pltpu.CompilerParams(
            dimension_semantics=("arbitrary", "arbitrary"), vmem_limit_bytes=VMEM_LIMIT),
        name="inproj",
    )(x, mod3, mod3, norm_g, w_in_b, lb, pool_w_b, pool_b, pool_scale)


def _hgrn_kernel(q_ref, lf_ref, k_ref, v_ref, sog_ref, gn_ref, o_ref, *st_refs):
    C = HG_CHUNK
    n_chunks = q_ref.shape[0] // C

    @pl.when(pl.program_id(1) == 0)
    def _():
        for st_ref in st_refs:
            st_ref[...] = jnp.zeros_like(st_ref)

    r_i = lax.broadcasted_iota(jnp.int32, (C, C), 0)
    c_i = lax.broadcasted_iota(jnp.int32, (C, C), 1)
    tril = (c_i <= r_i).astype(BF16)
    same_blk = (r_i // HG_BLK) == (c_i // HG_BLK)
    row = lax.broadcasted_iota(jnp.int32, (C, HG_DK), 0)
    row_in_sub = row % HG_SUB
    upper_half = (row % HG_BLK) >= HG_SUB
    row_blk = row // HG_BLK
    n_blk = C // HG_BLK

    def cumsum_rows(x):
        hi = x.astype(BF16)
        r1 = x - hi.astype(F32)
        mid = r1.astype(BF16)
        lo = (r1 - mid.astype(F32)).astype(BF16)
        return _dot(tril, hi) + _dot(tril, mid) + _dot(tril, lo)

    def block_rows(x, size, which):
        pieces = []
        for g in range(C // size):
            src = g * size + which
            pieces.append(jnp.zeros((size, x.shape[1]), F32) if src < 0
                          else jnp.broadcast_to(x[src:src + 1, :], (size, x.shape[1])))
        return jnp.concatenate(pieces, axis=0)

    def chunk(ci, carry):
        rs = pl.ds(pl.multiple_of(ci * C, C), C)
        b_all = cumsum_rows(lf_ref[rs, :])
        for h in range(HG_HEADS):
            cs = slice(h * HG_DK, (h + 1) * HG_DK)
            q = q_ref[rs, cs]
            k = k_ref[rs, cs]
            v = v_ref[rs, cs]
            b = b_all[:, cs]
            vb = v.astype(BF16)

            kt = k * jnp.exp(block_rows(b, HG_BLK, HG_BLK - 1) - b)
            q_parts, k_parts = [], []
            for j in range(n_blk - 1):
                bj = b[HG_BLK * j + HG_BLK - 1:HG_BLK * (j + 1), :]
                after = row >= HG_BLK * (j + 1)
                q_parts.append(q * jnp.exp(jnp.where(after, b - bj, -jnp.inf)))
                k_parts.append(jnp.where(row_blk == j, kt, 0.0))
            qcat = jnp.concatenate(q_parts, axis=1).astype(BF16)
            kcat = jnp.concatenate(k_parts, axis=1).astype(BF16)
            scores = _dot_nt(qcat, kcat)
            b_prev = block_rows(b, HG_SUB, -1)
            b_sub = block_rows(b, HG_SUB, HG_SUB - 1)
            qh = (q * jnp.exp(jnp.where(upper_half, b - b_prev, -jnp.inf))).astype(BF16)
            kh = jnp.where(upper_half, 0.0, k * jnp.exp(b_sub - b)).astype(BF16)
            scores = scores + jnp.where(same_blk, _dot_nt(qh, kh), 0.0)
            o = _dot(scores.astype(BF16), vb)

            o = o + jnp.sum(q * k, axis=-1, keepdims=True) * v
            for d in range(1, HG_SUB):
                kd = pltpu.roll(k, d, 0)
                bd = pltpu.roll(b, d, 0)
                vd = pltpu.roll(v, d, 0)
                e = jnp.exp(jnp.where(row_in_sub >= d, b - bd, -jnp.inf))
                o = o + jnp.sum(q * kd * e, axis=-1, keepdims=True) * vd

            st = st_refs[h][...]
            o = o + _dot_nt((q * jnp.exp(b)).astype(BF16), st.astype(BF16))
            b_end = b[C - 1:C, :]
            k_end = (k * jnp.exp(b_end - b)).astype(BF16)
            st_refs[h][...] = st * jnp.exp(b_end) + _dot_tn(vb, k_end)

            on = o * lax.rsqrt(jnp.mean(o * o, axis=-1, keepdims=True) + EPS) * gn_ref[...]
            o_ref[rs, cs] = (on * sog_ref[rs, cs].astype(F32)).astype(BF16)
        return carry

    lax.fori_loop(0, n_chunks, chunk, 0)


def _hgrn(q, lf, k, v, sog, gn, B, S, tb):
    T = B * S
    nS = S // tb
    row = lambda b, s: (b * nS + s, 0)
    blk = pl.BlockSpec((tb, HG_WIDTH), row)
    return pl.pallas_call(
        _hgrn_kernel,
        out_shape=jax.ShapeDtypeStruct((T, HG_WIDTH), BF16),
        grid=(B, nS),
        in_specs=[blk, blk, blk, blk, blk, pl.BlockSpec((1, HG_DK), lambda b, s: (0, 0))],
        out_specs=blk,
        scratch_shapes=[pltpu.VMEM((HG_DK, HG_DK), F32)] * HG_HEADS,
        compiler_params=pltpu.CompilerParams(
            dimension_semantics=("arbitrary", "arbitrary"), vmem_limit_bytes=VMEM_LIMIT),
        name="hgrn",
    )(q, lf, k, v, sog, gn)


def _mix_kernel(x_ref, oa_ref, pm_ref, sga_ref, sgb_ref, g1_ref, sh2_ref, sc2_ref, n2_ref,
                wua_ref, wub_ref, wo_ref, rwt_ref, x1_ref, h2_ref, lg_ref):
    tm = x_ref.shape[1]
    ya = _dot(oa_ref[...], wua_ref[...])
    yb = _dot(pm_ref[...], wub_ref[...])
    mix = sga_ref[...].astype(F32) * ya + sgb_ref[...].astype(F32) * yb
    x1 = x_ref[0] + g1_ref[0] * _dot(mix.astype(BF16), wo_ref[...])
    x1_ref[...] = x1
    h2 = x1 * lax.rsqrt(jnp.mean(x1 * x1, axis=-1, keepdims=True) + EPS) * n2_ref[...]
    h2 = h2 * (1.0 + sc2_ref[0]) + sh2_ref[0]
    for j, chunk in enumerate(_row_chunks(h2)):
        h2_ref[pl.ds(j, tm, stride=ROW_TILES), :] = chunk
    lg_ref[...] = lax.dot_general(rwt_ref[...], h2, (((1,), (1,)), ((), ())),
                                  precision=HIGHEST, preferred_element_type=F32)


def _mix(x, oa, pm, sga, sgb, mod3, norm2_g, wua, wub, wo, rwt, tm):
    B, S, D = x.shape
    T = B * S
    nS = S // tm
    row = lambda b, s: (b * nS + s, 0)
    const2 = lambda b, s: (0, 0)
    return pl.pallas_call(
        _mix_kernel,
        out_shape=(jax.ShapeDtypeStruct((T, D), F32),
                   jax.ShapeDtypeStruct((T * ROW_TILES, LANES), F32),
                   jax.ShapeDtypeStruct((N_EXPERTS, T), F32)),
        grid=(B, nS),
        in_specs=[pl.BlockSpec((1, tm, D), lambda b, s: (b, s, 0)),
                  pl.BlockSpec((tm, HG_WIDTH), row),
                  pl.BlockSpec((tm, POOL_WIDTH), row),
                  pl.BlockSpec((tm, D), row),
                  pl.BlockSpec((tm, D), row),
                  pl.BlockSpec((1, 1, D), lambda b, s: (b, 0, 2)),
                  pl.BlockSpec((1, 1, D), lambda b, s: (b, 0, 3)),
                  pl.BlockSpec((1, 1, D), lambda b, s: (b, 0, 4)),
                  pl.BlockSpec((1, D), const2),
                  pl.BlockSpec(wua.shape, const2),
                  pl.BlockSpec(wub.shape, const2),
                  pl.BlockSpec(wo.shape, const2),
                  pl.BlockSpec(rwt.shape, const2)],
        out_specs=(pl.BlockSpec((tm, D), row),
                   pl.BlockSpec((tm * ROW_TILES, LANES), row),
                   pl.BlockSpec((N_EXPERTS, tm), lambda b, s: (0, b * nS + s))),
        compiler_params=pltpu.CompilerParams(
            dimension_semantics=("arbitrary", "arbitrary"), vmem_limit_bytes=VMEM_LIMIT),
        name="mix",
    )(x, oa, pm, sga, sgb, mod3, mod3, mod3, norm2_g, wua, wub, wo, rwt)


def _route_kernel(lg_ref, bias_ref, idx_ref, gate_ref, rank_ref, cnt_ref, carry_ref):
    tl = lg_ref.shape[1]
    neg = -jnp.inf

    @pl.when(pl.program_id(0) == 0)
    def _():
        carry_ref[...] = jnp.zeros_like(carry_ref)

    s = _sigmoid(lg_ref[...])
    biased = s + bias_ref[...]
    rowid = lax.broadcasted_iota(jnp.int32, (N_EXPERTS, tl), 0)

    def first_argmax(x, ids, sentinel):
        m = jnp.max(x, axis=0, keepdims=True)
        return jnp.min(jnp.where(x == m, ids, sentinel), axis=0, keepdims=True), m

    gscores = []
    for g in range(N_GROUPS):
        xg = biased[g * GROUP_SIZE:(g + 1) * GROUP_SIZE, :]
        rid = g * GROUP_SIZE + lax.broadcasted_iota(jnp.int32, (GROUP_SIZE, tl), 0)
        first, m1 = first_argmax(xg, rid, N_EXPERTS)
        m2 = jnp.max(jnp.where(rid == first, neg, xg), axis=0, keepdims=True)
        gscores.append(m1 + m2)
    blocks = []
    for g in range(N_GROUPS):
        beaten = jnp.zeros((1, tl), F32)
        for o in range(N_GROUPS):
            if o != g:
                wins = (gscores[o] >= gscores[g]) if o < g else (gscores[o] > gscores[g])
                beaten = beaten + jnp.where(wins, 1.0, 0.0)
        xg = biased[g * GROUP_SIZE:(g + 1) * GROUP_SIZE, :]
        blocks.append(jnp.where(beaten < float(TOPK_GROUPS), xg, neg))
    masked = jnp.concatenate(blocks, axis=0)

    idxs, gates = [], []
    chosen = jnp.zeros((N_EXPERTS, tl), F32)
    for _ in range(TOP_K):
        first, _m = first_argmax(masked, rowid, N_EXPERTS)
        sel = rowid == first
        gates.append(jnp.sum(jnp.where(sel, s, 0.0), axis=0, keepdims=True))
        idxs.append(first)
        chosen = jnp.where(sel, 1.0, chosen)
        masked = jnp.where(sel, neg, masked)
    gate_sum = functools.reduce(lambda a, b: a + b, gates)
    for k in range(TOP_K):
        gate_ref[k:k + 1, :] = gates[k] / gate_sum * ROUTED_SCALE
        idx_ref[k:k + 1, :] = idxs[k]

    lr = lax.broadcasted_iota(jnp.int32, (tl, tl), 0)
    lc = lax.broadcasted_iota(jnp.int32, (tl, tl), 1)
    prefix = (lr <= lc).astype(BF16)
    cnt_incl = _dot(chosen.astype(BF16), prefix)
    carry = carry_ref[...]
    rank_excl = cnt_incl - chosen + carry
    for k in range(TOP_K):
        rank_k = jnp.sum(jnp.where(rowid == idxs[k], rank_excl, 0.0), axis=0, keepdims=True)
        rank_ref[k:k + 1, :] = rank_k.astype(jnp.int32)
    carry = carry + jnp.sum(chosen, axis=1, keepdims=True)
    carry_ref[...] = carry
    cnt_ref[...] = carry.astype(jnp.int32)


def _route(logits_t, bias, tl):
    T = logits_t.shape[1]
    tok = lambda i: (0, i)
    return pl.pallas_call(
        _route_kernel,
        out_shape=(jax.ShapeDtypeStruct((TOP_K, T), jnp.int32),
                   jax.ShapeDtypeStruct((TOP_K, T), F32),
                   jax.ShapeDtypeStruct((TOP_K, T), jnp.int32),
                   jax.ShapeDtypeStruct((N_EXPERTS, 1), jnp.int32)),
        grid=(T // tl,),
        in_specs=[pl.BlockSpec((N_EXPERTS, tl), tok), pl.BlockSpec((N_EXPERTS, 1), lambda i: (0, 0))],
        out_specs=(pl.BlockSpec((TOP_K, tl), tok), pl.BlockSpec((TOP_K, tl), tok),
                   pl.BlockSpec((TOP_K, tl), tok), pl.BlockSpec((N_EXPERTS, 1), lambda i: (0, 0))),
        scratch_shapes=[pltpu.VMEM((N_EXPERTS, 1), F32)],
        compiler_params=pltpu.CompilerParams(
            dimension_semantics=("arbitrary",), vmem_limit_bytes=VMEM_LIMIT),
        name="route",
    )(logits_t, bias)


def _as_rows(ref):
    return ref.reshape(ref.shape[0] // ROW_TILES, ROW_TILES, LANES)


def _wait_rows(rows_ref, n, sem):
    pltpu.make_async_copy(rows_ref.at[pl.ds(0, n)], rows_ref.at[pl.ds(0, n)], sem).wait()


def _scatter_kernel(pos_ref, h2_ref, xs_ref, sem):
    src = _as_rows(h2_ref)
    dst = _as_rows(xs_ref)
    tt = src.shape[0]

    def start(t, c):
        for k in range(TOP_K):
            pltpu.make_async_copy(src.at[t], dst.at[pos_ref[0, 0, t * TOP_K + k]], sem).start()
        return c

    lax.fori_loop(0, tt, start, 0)
    _wait_rows(dst, tt * TOP_K, sem)


def _scatter(pos_tiles, h2_tm, tt):
    n_rows = h2_tm.shape[0] // ROW_TILES * TOP_K
    return pl.pallas_call(
        _scatter_kernel,
        out_shape=jax.ShapeDtypeStruct((n_rows * ROW_TILES, LANES), F32),
        grid=(pos_tiles.shape[0],),
        in_specs=[pl.BlockSpec((1, 1, tt * TOP_K), lambda i: (i, 0, 0), memory_space=pltpu.SMEM),
                  pl.BlockSpec((tt * ROW_TILES, LANES), lambda i: (i, 0))],
        out_specs=pl.BlockSpec(memory_space=pl.ANY),
        scratch_shapes=[pltpu.SemaphoreType.DMA],
        compiler_params=pltpu.CompilerParams(
            dimension_semantics=("arbitrary",), vmem_limit_bytes=VMEM_LIMIT),
        name="scatter",
    )(pos_tiles, h2_tm)


def _experts_kernel(tile_ref, exp_ref, first_ref, last_ref, newe_ref, nitems_ref, off_ref, cnt_ref, slot_ref,
                    nexte_ref, xs_hbm, wg_hbm, wu_hbm, wd_hbm, ys_hbm,
                    xbuf_ref, ybuf_ref, sg_ref, su_ref, sd_ref, wgb_ref, wub_ref, wdb_ref, hm_ref,
                    xsem, ysem, wsem):
    i = pl.program_id(0)
    tr = EXPERT_TILE
    tile_rows = tr * ROW_TILES
    n_tiles = xs_hbm.shape[0] // tile_rows
    n_items = nitems_ref[0]

    def ring(t):
        return pl.ds(pl.multiple_of((t % TILE_RING) * tile_rows, tile_rows), tile_rows)

    def hbm_tile(t):
        return pl.ds(pl.multiple_of(t * tile_rows, tile_rows), tile_rows)

    def x_copy(t):
        return pltpu.make_async_copy(xs_hbm.at[hbm_tile(t)], xbuf_ref.at[ring(t)], xsem.at[t % TILE_RING])

    def y_copy(t):
        return pltpu.make_async_copy(ybuf_ref.at[ring(t)], ys_hbm.at[hbm_tile(t)], ysem.at[t % TILE_RING])

    def weight_copies(e, slot):
        return (pltpu.make_async_copy(wg_hbm.at[e], sg_ref.at[slot], wsem.at[slot]),
                pltpu.make_async_copy(wu_hbm.at[e], su_ref.at[slot], wsem.at[slot]),
                pltpu.make_async_copy(wd_hbm.at[e], sd_ref.at[slot], wsem.at[slot]))

    a_on = i < n_items
    j = jnp.maximum(i - 1, 0)
    b_on = (i >= 1) & (i - 1 < n_items)
    e = exp_ref[i]
    t = tile_ref[i]

    @pl.when(i == 0)
    def _():
        hm_ref[...] = jnp.zeros_like(hm_ref)
        for t0 in range(TILE_AHEAD):
            x_copy(t0).start()

    @pl.when(a_on & (first_ref[i] == 1))
    def _():
        @pl.when(t + TILE_AHEAD < n_tiles)
        def _():
            x_copy(t + TILE_AHEAD).start()

        x_copy(t).wait()

    @pl.when(a_on & (newe_ref[i] == 1))
    def _():
        slot = slot_ref[i]
        nxt = nexte_ref[i]

        @pl.when(i == 0)
        def _():
            for c in weight_copies(e, slot):
                c.start()

        @pl.when(nxt >= 0)
        def _():
            for c in weight_copies(nxt, 1 - slot):
                c.start()

        for c in weight_copies(e, slot):
            c.wait()
        wgb_ref[...] = sg_ref[slot].astype(BF16)
        wub_ref[...] = su_ref[slot].astype(BF16)
        wdb_ref[slot] = sd_ref[slot].astype(BF16)

    tj = tile_ref[j]
    ej = exp_ref[j]
    chunks = _row_chunks(_dot(hm_ref[j % 2], wdb_ref[slot_ref[j]]))
    row = tj * tr + lax.broadcasted_iota(jnp.int32, (tr, 1), 0)
    lo = off_ref[ej]
    mine = (row >= lo) & (row < lo + cnt_ref[ej])

    x = _load_rows(xbuf_ref, tr, first_row=(t % TILE_RING) * tr).astype(BF16)
    hm_ref[i % 2] = (_silu(_dot(x, wgb_ref[...])) * _dot(x, wub_ref[...])).astype(BF16)

    out_row = (tj % TILE_RING) * tile_rows

    @pl.when(b_on & (first_ref[j] == 1))
    def _():
        @pl.when(tj >= TILE_RING)
        def _():
            y_copy(tj - TILE_RING).wait()

        for c in range(ROW_TILES):
            ybuf_ref[pl.ds(out_row + c, tr, stride=ROW_TILES), :] = chunks[c]

    @pl.when(b_on & (first_ref[j] == 0))
    def _():
        for c in range(ROW_TILES):
            sl = pl.ds(out_row + c, tr, stride=ROW_TILES)
            ybuf_ref[sl, :] = jnp.where(mine, chunks[c], ybuf_ref[sl, :])

    @pl.when(b_on & (last_ref[j] == 1))
    def _():
        y_copy(tj).start()

    @pl.when(b_on & (j == n_items - 1))
    def _():
        for t0 in range(n_tiles - TILE_RING, n_tiles):
            y_copy(t0).wait()


def _experts(meta, xs, wg, wu, wd, n_items_max):
    tile_rows = EXPERT_TILE * ROW_TILES
    assert xs.shape[0] % tile_rows == 0 and xs.shape[0] // tile_rows >= TILE_RING
    hbm = pl.BlockSpec(memory_space=pl.ANY)
    n_slots = 2
    grid_spec = pltpu.PrefetchScalarGridSpec(
        num_scalar_prefetch=len(meta),
        grid=(n_items_max + 1,),
        in_specs=[hbm, hbm, hbm, hbm],
        out_specs=hbm,
        scratch_shapes=[pltpu.VMEM((TILE_RING * tile_rows, LANES), F32),
                        pltpu.VMEM((TILE_RING * tile_rows, LANES), F32),
                        pltpu.VMEM((n_slots, D_MODEL, D_EXPERT), F32),
                        pltpu.VMEM((n_slots, D_MODEL, D_EXPERT), F32),
                        pltpu.VMEM((n_slots, D_EXPERT, D_MODEL), F32),
                        pltpu.VMEM((D_MODEL, D_EXPERT), BF16),
                        pltpu.VMEM((D_MODEL, D_EXPERT), BF16),
                        pltpu.VMEM((n_slots, D_EXPERT, D_MODEL), BF16),
                        pltpu.VMEM((2, EXPERT_TILE, D_EXPERT), BF16),
                        pltpu.SemaphoreType.DMA((TILE_RING,)),
                        pltpu.SemaphoreType.DMA((TILE_RING,)),
                        pltpu.SemaphoreType.DMA((n_slots,))])
    return pl.pallas_call(
        _experts_kernel,
        out_shape=jax.ShapeDtypeStruct(xs.shape, F32),
        grid_spec=grid_spec,
        compiler_params=pltpu.CompilerParams(
            dimension_semantics=("arbitrary",), vmem_limit_bytes=VMEM_LIMIT),
        name="experts",
    )(*meta, xs, wg, wu, wd)


def _combine_kernel(pos_ref, pos_next_ref, ys_ref, h2_ref, x1_ref, gate_ref, g2_ref, swg_ref, swu_ref, swd_ref,
                    fg_ref, out_ref, buf_ref, sem):
    i = pl.program_id(0)
    tt = x1_ref.shape[0]
    slot_rows = tt * TOP_K
    src = _as_rows(ys_ref)
    dst = _as_rows(buf_ref)

    def gather(p_ref, slot):
        def start(t, c):
            for k in range(TOP_K):
                pltpu.make_async_copy(src.at[p_ref[0, 0, t * TOP_K + k]],
                                      dst.at[slot * slot_rows + k * tt + t], sem.at[slot]).start()
            return c

        lax.fori_loop(0, tt, start, 0)

    slot = i % 2

    @pl.when(i == 0)
    def _():
        gather(pos_ref, slot)

    @pl.when(i + 1 < pl.num_programs(0))
    def _():
        gather(pos_next_ref, 1 - slot)

    h2 = _load_rows(h2_ref, tt).astype(BF16)
    hm = (_silu(_dot(h2, swg_ref[...])) * _dot(h2, swu_ref[...])).astype(BF16)
    acc = _dot(hm, swd_ref[...])

    _wait_rows(dst, slot_rows, sem.at[slot])

    gate = gate_ref[...]
    for k in range(TOP_K):
        acc = acc + gate[:, k:k + 1] * _load_rows(buf_ref, tt, first_row=slot * slot_rows + k * tt)
    x2 = x1_ref[...] + g2_ref[0] * acc
    out_ref[...] = x2 * lax.rsqrt(jnp.mean(x2 * x2, axis=-1, keepdims=True) + EPS) * fg_ref[...]


def _combine(pos_tiles, ys, h2_tm, x1, gate_tm, mod3, swg, swu, swd, fg, B, S, tt):
    T, D = x1.shape
    nS = S // tt
    const2 = lambda i: (0, 0)
    n_tiles = T // tt
    n_slots = 2
    return pl.pallas_call(
        _combine_kernel,
        out_shape=jax.ShapeDtypeStruct((T, D), F32),
        grid=(n_tiles,),
        in_specs=[pl.BlockSpec((1, 1, tt * TOP_K), lambda i: (i, 0, 0), memory_space=pltpu.SMEM),
                  pl.BlockSpec((1, 1, tt * TOP_K), lambda i: (jnp.minimum(i + 1, n_tiles - 1), 0, 0),
                               memory_space=pltpu.SMEM),
                  pl.BlockSpec(memory_space=pl.ANY),
                  pl.BlockSpec((tt * ROW_TILES, LANES), lambda i: (i, 0)),
                  pl.BlockSpec((tt, D), lambda i: (i, 0)),
                  pl.BlockSpec((tt, TOP_K), lambda i: (i, 0)),
                  pl.BlockSpec((1, 1, D), lambda i: (i // nS, 0, 5)),
                  pl.BlockSpec(swg.shape, const2),
                  pl.BlockSpec(swu.shape, const2),
                  pl.BlockSpec(swd.shape, const2),
                  pl.BlockSpec((1, D), const2)],
        out_specs=pl.BlockSpec((tt, D), lambda i: (i, 0)),
        scratch_shapes=[pltpu.VMEM((n_slots * tt * TOP_K * ROW_TILES, LANES), F32),
                        pltpu.SemaphoreType.DMA((n_slots,))],
        compiler_params=pltpu.CompilerParams(
            dimension_semantics=("arbitrary",), vmem_limit_bytes=VMEM_LIMIT),
        name="combine",
    )(pos_tiles, pos_tiles, ys, h2_tm, x1, gate_tm, mod3, swg, swu, swd, fg)


def _place_kernel(idx_ref, rank_ref, off_ref, pos_ref):
    tl = idx_ref.shape[1]
    rowid = lax.broadcasted_iota(jnp.int32, (N_EXPERTS, tl), 0)
    off = off_ref[...].astype(F32)
    for k in range(TOP_K):
        base = jnp.sum(jnp.where(rowid == idx_ref[k:k + 1, :], off, 0.0), axis=0, keepdims=True)
        pos_ref[k:k + 1, :] = base.astype(jnp.int32) + rank_ref[k:k + 1, :]


def _place(idx, rank, off, tl):
    T = idx.shape[1]
    tok = pl.BlockSpec((TOP_K, tl), lambda i: (0, i))
    return pl.pallas_call(
        _place_kernel,
        out_shape=jax.ShapeDtypeStruct((TOP_K, T), jnp.int32),
        grid=(T // tl,),
        in_specs=[tok, tok, pl.BlockSpec((N_EXPERTS, 1), lambda i: (0, 0))],
        out_specs=tok,
        name="place",
    )(idx, rank, off)


def _plan(counts, n_items_max):
    counts = counts[:, 0]
    off = jnp.cumsum(counts) - counts
    first_tile = off // EXPERT_TILE
    last_tile = (off + counts - 1) // EXPERT_TILE
    n_e = jnp.where(counts > 0, last_tile - first_tile + 1, 0)
    item_end = jnp.cumsum(n_e)
    item_start = item_end - n_e
    n_items = item_end[-1]
    ids = jnp.arange(n_items_max + 1, dtype=jnp.int32)
    ids_c = jnp.minimum(ids, n_items - 1)
    item_e = jnp.minimum(jnp.sum(item_end[None, :] <= ids_c[:, None], axis=1), N_EXPERTS - 1).astype(jnp.int32)
    item_tile = (first_tile[item_e] + ids_c - item_start[item_e]).astype(jnp.int32)
    prev_tile = jnp.concatenate([jnp.full((1,), -1, jnp.int32), item_tile[:-1]])
    prev_e = jnp.concatenate([jnp.full((1,), -1, jnp.int32), item_e[:-1]])
    next_tile = jnp.concatenate([item_tile[1:], jnp.full((1,), -1, jnp.int32)])
    item_first = (item_tile != prev_tile).astype(jnp.int32)
    item_last = ((item_tile != next_tile) | (ids == n_items - 1)).astype(jnp.int32)
    item_newe = (item_e != prev_e).astype(jnp.int32)
    item_slot = ((jnp.cumsum(item_newe) - 1) % 2).astype(jnp.int32)
    ids_e = jnp.arange(N_EXPERTS, dtype=jnp.int32)
    later = jnp.where((counts[None, :] > 0) & (ids_e[None, :] > ids_e[:, None]), ids_e[None, :], N_EXPERTS)
    next_e = jnp.min(later, axis=1)
    next_e = jnp.where(next_e < N_EXPERTS, next_e, -1).astype(jnp.int32)
    meta = (item_tile, item_e, item_first, item_last, item_newe, n_items.reshape(1).astype(jnp.int32),
            off.astype(jnp.int32), counts.astype(jnp.int32), item_slot, next_e[item_e])
    return off.astype(jnp.int32).reshape(N_EXPERTS, 1), meta


def kernel(x, c, ada_w, ada_b, norm1_g, w_in, hgrn_lb, hgrn_norm_g, pool_w, pool_b, pool_scale, w_up_a, w_up_b, w_out, norm2_g, router_w, router_bias, exp_w_gate, exp_w_up, exp_w_down, shared_w_gate, shared_w_up, shared_w_down, final_norm_g):
    B, S, D = x.shape
    T = B * S
    assert ada_w.shape[0] == 1, "single-layer trunk only: the final norm is fused into the combine step"
    lb_all = jnp.cumsum(jax.nn.softmax(hgrn_lb.astype(F32), axis=0), axis=0)
    c_pad = jnp.zeros((SUBLANES, D), F32).at[:B].set(c)
    n_items_max = T * TOP_K // EXPERT_TILE + N_EXPERTS - 1

    for l in range(1):
        mod = _ada(c_pad, ada_w[l], ada_b[l].reshape(1, -1))
        mod3 = mod[:B].reshape(B, 1, 6 * D)

        q, lf, k, v, sog, pm, sga, sgb = _inproj(
            x, mod3, norm1_g[l].reshape(1, D), w_in[l].astype(BF16), lb_all[l].reshape(1, HG_WIDTH),
            pool_w[l].astype(BF16), pool_b[l].reshape(len(POOL_WINDOWS), 1, POOL_GROUP),
            pool_scale[l].reshape(1, POOL_WIDTH), tm=256)
        oa = _hgrn(q, lf, k, v, sog, hgrn_norm_g[l].reshape(1, HG_DK), B, S, tb=512)

        x1, h2_tm, logits_t = _mix(
            x, oa, pm, sga, sgb, mod3, norm2_g[l].reshape(1, D), w_up_a[l].astype(BF16),
            w_up_b[l].astype(BF16), w_out[l].astype(BF16), router_w[l].T, tm=256)

        idx, gate, rank, counts = _route(logits_t, router_bias[l].reshape(N_EXPERTS, 1), tl=256)
        off, meta = _plan(counts, n_items_max)
        pos_tm = _place(idx, rank, off, tl=512).T.reshape(-1)

        tt_s = 256
        xs = _scatter(pos_tm.reshape(T // tt_s, 1, tt_s * TOP_K), h2_tm, tt_s)
        ys = _experts(meta, xs, exp_w_gate[l], exp_w_up[l], exp_w_down[l], n_items_max)

        tt_c = 128
        fg = final_norm_g.reshape(1, D)
        x = _combine(pos_tm.reshape(T // tt_c, 1, tt_c * TOP_K), ys, h2_tm, x1, gate.T, mod3,
                     shared_w_gate[l].astype(BF16), shared_w_up[l].astype(BF16),
                     shared_w_down[l].astype(BF16), fg, B, S, tt_c).reshape(B, S, D)
    return x
```

```python
import functools

import jax
import jax.numpy as jnp
from jax import lax
from jax.experimental import pallas as pl
from jax.experimental.pallas import tpu as pltpu

F32 = jnp.float32
BF16 = jnp.bfloat16
HIGHEST = lax.Precision.HIGHEST

D_MODEL = 1024
HG_WIDTH = 512
HG_DK = 128
HG_HEADS = 4
HG_CHUNK = 64
HG_BLK = 16
HG_SUB = 8
POOL_WIDTH = 512
POOL_WINDOWS = (2, 4, 8, 16)
POOL_GROUP = 128
POOL_HALO = 16
N_EXPERTS = 256
TOP_K = 8
N_GROUPS = 8
TOPK_GROUPS = 4
GROUP_SIZE = N_EXPERTS // N_GROUPS
D_EXPERT = 256
ROUTED_SCALE = 2.5
EPS = 1e-6

LANES = 128
SUBLANES = 8
ROW_TILES = D_MODEL // LANES
EXPERT_TILE = 128
TILE_RING = 8
TILE_AHEAD = TILE_RING - 1
VMEM_LIMIT = 56 * 1024 * 1024

COL_Q, COL_F, COL_I, COL_OG, COL_U, COL_GA, COL_GB = 0, 512, 1024, 1536, 2048, 2560, 3584


def _sigmoid(x):
    return 1.0 / (1.0 + jnp.exp(-x))


def _silu(x):
    return x * _sigmoid(x)


def _dot(a, b):
    return jnp.dot(a, b, preferred_element_type=F32)


def _dot_nt(a, b):
    return lax.dot_general(a, b, (((1,), (1,)), ((), ())), preferred_element_type=F32)


def _dot_tn(a, b):
    return lax.dot_general(a, b, (((0,), (0,)), ((), ())), preferred_element_type=F32)


def _row_chunks(x):
    return [x[:, j * LANES:(j + 1) * LANES] for j in range(ROW_TILES)]


def _load_rows(ref, n, first_row=0):
    return jnp.concatenate(
        [ref[pl.ds(first_row * ROW_TILES + j, n, stride=ROW_TILES), :] for j in range(ROW_TILES)], axis=1)


def _ada_kernel(c_ref, w_ref, b_ref, o_ref):
    cond = _silu(c_ref[...])
    o_ref[...] = jnp.dot(cond, w_ref[...], precision=HIGHEST, preferred_element_type=F32) + b_ref[...]


def _ada(c_pad, ada_w, ada_b):
    n = ada_w.shape[1]
    tn = 1536
    return pl.pallas_call(
        _ada_kernel,
        out_shape=jax.ShapeDtypeStruct((SUBLANES, n), F32),
        grid=(n // tn,),
        in_specs=[pl.BlockSpec((SUBLANES, D_MODEL), lambda j: (0, 0)),
                  pl.BlockSpec((D_MODEL, tn), lambda j: (0, j)),
                  pl.BlockSpec((1, tn), lambda j: (0, j))],
        out_specs=pl.BlockSpec((SUBLANES, tn), lambda j: (0, j)),
        compiler_params=pltpu.CompilerParams(vmem_limit_bytes=VMEM_LIMIT),
        name="ada",
    )(c_pad, ada_w, ada_b)


def _inproj_kernel(x_ref, sh_ref, sc_ref, g_ref, w_ref, lb_ref, pw_ref, pb_ref, ps_ref,
                   q_ref, lf_ref, k_ref, v_ref, sog_ref, pm_ref, sga_ref, sgb_ref, halo_ref):
    s = pl.program_id(1)
    tm = x_ref.shape[1]
    x = x_ref[0]
    h = x * lax.rsqrt(jnp.mean(x * x, axis=-1, keepdims=True) + EPS) * g_ref[...]
    h = h * (1.0 + sc_ref[0]) + sh_ref[0]
    hb = h.astype(BF16)

    def proj(lo, n):
        return _dot(hb, w_ref[:, lo:lo + n])

    q = proj(COL_Q, HG_WIDTH)
    q_ref[...] = _silu(q) * (HG_DK ** -0.5)
    sig = _sigmoid(proj(COL_F, HG_WIDTH))
    lb = lb_ref[...]
    lf_ref[...] = jnp.log(lb + (1.0 - lb) * sig)
    k_ref[...] = (1.0 - lb) * (1.0 - sig)
    v_ref[...] = proj(COL_I, HG_WIDTH)
    sog_ref[...] = _silu(proj(COL_OG, HG_WIDTH)).astype(BF16)
    sga_ref[...] = _sigmoid(proj(COL_GA, D_MODEL)).astype(BF16)
    sgb_ref[...] = _sigmoid(proj(COL_GB, D_MODEL)).astype(BF16)

    u = proj(COL_U, POOL_WIDTH)
    @pl.when(s == 0)
    def _():
        halo_ref[...] = jnp.zeros_like(halo_ref)

    ext = jnp.concatenate([halo_ref[...], u], axis=0)
    halo_ref[...] = u[tm - POOL_HALO:, :]
    s2 = ext + pltpu.roll(ext, 1, 0)
    s4 = s2 + pltpu.roll(s2, 2, 0)
    s8 = s4 + pltpu.roll(s4, 4, 0)
    s16 = s8 + pltpu.roll(s8, 8, 0)
    pos1 = (s * tm + 1 + lax.broadcasted_iota(jnp.int32, (tm, 1), 0)).astype(F32)
    for g, (w, sw) in enumerate(zip(POOL_WINDOWS, (s2, s4, s8, s16))):
        cols = slice(g * POOL_GROUP, (g + 1) * POOL_GROUP)
        m = sw[POOL_HALO:, cols] / jnp.minimum(pos1, float(w)) - u[:, cols]
        y = _dot(m.astype(BF16), pw_ref[g]) + pb_ref[g]
        pm_ref[:, cols] = (y * ps_ref[:, cols]).astype(BF16)


def _inproj(x, mod3, norm_g, w_in_b, lb, pool_w_b, pool_b, pool_scale, tm):
    B, S, D = x.shape
    T = B * S
    nS = S // tm
    row = lambda b, s: (b * nS + s, 0)
    const2 = lambda b, s: (0, 0)
    const3 = lambda b, s: (0, 0, 0)
    half = lambda dt: jax.ShapeDtypeStruct((T, HG_WIDTH), dt)
    full = lambda dt: jax.ShapeDtypeStruct((T, D), dt)
    return pl.pallas_call(
        _inproj_kernel,
        out_shape=(half(F32), half(F32), half(F32), half(F32), half(BF16), half(BF16), full(BF16), full(BF16)),
        grid=(B, nS),
        in_specs=[pl.BlockSpec((1, tm, D), lambda b, s: (b, s, 0)),
                  pl.BlockSpec((1, 1, D), lambda b, s: (b, 0, 0)),
                  pl.BlockSpec((1, 1, D), lambda b, s: (b, 0, 1)),
                  pl.BlockSpec((1, D), const2),
                  pl.BlockSpec(w_in_b.shape, const2),
                  pl.BlockSpec((1, HG_WIDTH), const2),
                  pl.BlockSpec(pool_w_b.shape, const3),
                  pl.BlockSpec(pool_b.shape, const3),
                  pl.BlockSpec((1, POOL_WIDTH), const2)],
        out_specs=(pl.BlockSpec((tm, HG_WIDTH), row),) * 6 + (pl.BlockSpec((tm, D), row),) * 2,
        scratch_shapes=[pltpu.VMEM((POOL_HALO, POOL_WIDTH), F32)],
        compiler_params=pltpu.CompilerParams(
            dimension_semantics=("arbitrary", "arbitrary"), vmem_limit_bytes=VMEM_LIMIT),
        name="inproj",
    )(x, mod3, mod3, norm_g, w_in_b, lb, pool_w_b, pool_b, pool_scale)


def _hgrn_kernel(q_ref, lf_ref, k_ref, v_ref, sog_ref, gn_ref, o_ref, *st_refs):
    C = HG_CHUNK
    n_chunks = q_ref.shape[0] // C

    @pl.when(pl.program_id(1) == 0)
    def _():
        for st_ref in st_refs:
            st_ref[...] = jnp.zeros_like(st_ref)

    r_i = lax.broadcasted_iota(jnp.int32, (C, C), 0)
    c_i = lax.broadcasted_iota(jnp.int32, (C, C), 1)
    tril = (c_i <= r_i).astype(BF16)
    same_blk = (r_i // HG_BLK) == (c_i // HG_BLK)
    row = lax.broadcasted_iota(jnp.int32, (C, HG_DK), 0)
    row_in_sub = row % HG_SUB
    upper_half = (row % HG_BLK) >= HG_SUB
    row_blk = row // HG_BLK
    n_blk = C // HG_BLK

    def cumsum_rows(x):
        hi = x.astype(BF16)
        r1 = x - hi.astype(F32)
        mid = r1.astype(BF16)
        lo = (r1 - mid.astype(F32)).astype(BF16)
        return _dot(tril, hi) + _dot(tril, mid) + _dot(tril, lo)

    def block_rows(x, size, which):
        pieces = []
        for g in range(C // size):
            src = g * size + which
            pieces.append(jnp.zeros((size, x.shape[1]), F32) if src < 0
                          else jnp.broadcast_to(x[src:src + 1, :], (size, x.shape[1])))
        return jnp.concatenate(pieces, axis=0)

    def chunk(ci, carry):
        rs = pl.ds(pl.multiple_of(ci * C, C), C)
        b_all = cumsum_rows(lf_ref[rs, :])
        for h in range(HG_HEADS):
            cs = slice(h * HG_DK, (h + 1) * HG_DK)
            q = q_ref[rs, cs]
            k = k_ref[rs, cs]
            v = v_ref[rs, cs]
            b = b_all[:, cs]
            vb = v.astype(BF16)

            kt = k * jnp.exp(block_rows(b, HG_BLK, HG_BLK - 1) - b)
            q_parts, k_parts = [], []
            for j in range(n_blk - 1):
                bj = b[HG_BLK * j + HG_BLK - 1:HG_BLK * (j + 1), :]
                after = row >= HG_BLK * (j + 1)
                q_parts.append(q * jnp.exp(jnp.where(after, b - bj, -jnp.inf)))
                k_parts.append(jnp.where(row_blk == j, kt, 0.0))
            qcat = jnp.concatenate(q_parts, axis=1).astype(BF16)
            kcat = jnp.concatenate(k_parts, axis=1).astype(BF16)
            scores = _dot_nt(qcat, kcat)
            b_prev = block_rows(b, HG_SUB, -1)
            b_sub = block_rows(b, HG_SUB, HG_SUB - 1)
            qh = (q * jnp.exp(jnp.where(upper_half, b - b_prev, -jnp.inf))).astype(BF16)
            kh = jnp.where(upper_half, 0.0, k * jnp.exp(b_sub - b)).astype(BF16)
            scores = scores + jnp.where(same_blk, _dot_nt(qh, kh), 0.0)
            o = _dot(scores.astype(BF16), vb)

            o = o + jnp.sum(q * k, axis=-1, keepdims=True) * v
            for d in range(1, HG_SUB):
                kd = pltpu.roll(k, d, 0)
                bd = pltpu.roll(b, d, 0)
                vd = pltpu.roll(v, d, 0)
                e = jnp.exp(jnp.where(row_in_sub >= d, b - bd, -jnp.inf))
                o = o + jnp.sum(q * kd * e, axis=-1, keepdims=True) * vd

            st = st_refs[h][...]
            o = o + _dot_nt((q * jnp.exp(b)).astype(BF16), st.astype(BF16))
            b_end = b[C - 1:C, :]
            k_end = (k * jnp.exp(b_end - b)).astype(BF16)
            st_refs[h][...] = st * jnp.exp(b_end) + _dot_tn(vb, k_end)

            on = o * lax.rsqrt(jnp.mean(o * o, axis=-1, keepdims=True) + EPS) * gn_ref[...]
            o_ref[rs, cs] = (on * sog_ref[rs, cs].astype(F32)).astype(BF16)
        return carry

    lax.fori_loop(0, n_chunks, chunk, 0)


def _hgrn(q, lf, k, v, sog, gn, B, S, tb):
    T = B * S
    nS = S // tb
    row = lambda b, s: (b * nS + s, 0)
    blk = pl.BlockSpec((tb, HG_WIDTH), row)
    return pl.pallas_call(
        _hgrn_kernel,
        out_shape=jax.ShapeDtypeStruct((T, HG_WIDTH), BF16),
        grid=(B, nS),
        in_specs=[blk, blk, blk, blk, blk, pl.BlockSpec((1, HG_DK), lambda b, s: (0, 0))],
        out_specs=blk,
        scratch_shapes=[pltpu.VMEM((HG_DK, HG_DK), F32)] * HG_HEADS,
        compiler_params=pltpu.CompilerParams(
            dimension_semantics=("arbitrary", "arbitrary"), vmem_limit_bytes=VMEM_LIMIT),
        name="hgrn",
    )(q, lf, k, v, sog, gn)


def _mix_kernel(x_ref, oa_ref, pm_ref, sga_ref, sgb_ref, g1_ref, sh2_ref, sc2_ref, n2_ref,
                wua_ref, wub_ref, wo_ref, rwt_ref, x1_ref, h2_ref, lg_ref):
    tm = x_ref.shape[1]
    ya = _dot(oa_ref[...], wua_ref[...])
    yb = _dot(pm_ref[...], wub_ref[...])
    mix = sga_ref[...].astype(F32) * ya + sgb_ref[...].astype(F32) * yb
    x1 = x_ref[0] + g1_ref[0] * _dot(mix.astype(BF16), wo_ref[...])
    x1_ref[...] = x1
    h2 = x1 * lax.rsqrt(jnp.mean(x1 * x1, axis=-1, keepdims=True) + EPS) * n2_ref[...]
    h2 = h2 * (1.0 + sc2_ref[0]) + sh2_ref[0]
    for j, chunk in enumerate(_row_chunks(h2)):
        h2_ref[pl.ds(j, tm, stride=ROW_TILES), :] = chunk
    lg_ref[...] = lax.dot_general(rwt_ref[...], h2, (((1,), (1,)), ((), ())),
                                  precision=HIGHEST, preferred_element_type=F32)


def _mix(x, oa, pm, sga, sgb, mod3, norm2_g, wua, wub, wo, rwt, tm):
    B, S, D = x.shape
    T = B * S
    nS = S // tm
    row = lambda b, s: (b * nS + s, 0)
    const2 = lambda b, s: (0, 0)
    return pl.pallas_call(
        _mix_kernel,
        out_shape=(jax.ShapeDtypeStruct((T, D), F32),
                   jax.ShapeDtypeStruct((T * ROW_TILES, LANES), F32),
                   jax.ShapeDtypeStruct((N_EXPERTS, T), F32)),
        grid=(B, nS),
        in_specs=[pl.BlockSpec((1, tm, D), lambda b, s: (b, s, 0)),
                  pl.BlockSpec((tm, HG_WIDTH), row),
                  pl.BlockSpec((tm, POOL_WIDTH), row),
                  pl.BlockSpec((tm, D), row),
                  pl.BlockSpec((tm, D), row),
                  pl.BlockSpec((1, 1, D), lambda b, s: (b, 0, 2)),
                  pl.BlockSpec((1, 1, D), lambda b, s: (b, 0, 3)),
                  pl.BlockSpec((1, 1, D), lambda b, s: (b, 0, 4)),
                  pl.BlockSpec((1, D), const2),
                  pl.BlockSpec(wua.shape, const2),
                  pl.BlockSpec(wub.shape, const2),
                  pl.BlockSpec(wo.shape, const2),
                  pl.BlockSpec(rwt.shape, const2)],
        out_specs=(pl.BlockSpec((tm, D), row),
                   pl.BlockSpec((tm * ROW_TILES, LANES), row),
                   pl.BlockSpec((N_EXPERTS, tm), lambda b, s: (0, b * nS + s))),
        compiler_params=pltpu.CompilerParams(
            dimension_semantics=("arbitrary", "arbitrary"), vmem_limit_bytes=VMEM_LIMIT),
        name="mix",
    )(x, oa, pm, sga, sgb, mod3, mod3, mod3, norm2_g, wua, wub, wo, rwt)


def _route_kernel(lg_ref, bias_ref, idx_ref, gate_ref, rank_ref, cnt_ref, carry_ref):
    tl = lg_ref.shape[1]
    neg = -jnp.inf

    @pl.when(pl.program_id(0) == 0)
    def _():
        carry_ref[...] = jnp.zeros_like(carry_ref)

    s = _sigmoid(lg_ref[...])
    biased = s + bias_ref[...]
    rowid = lax.broadcasted_iota(jnp.int32, (N_EXPERTS, tl), 0)

    def first_argmax(x, ids, sentinel):
        m = jnp.max(x, axis=0, keepdims=True)
        return jnp.min(jnp.where(x == m, ids, sentinel), axis=0, keepdims=True), m

    gscores = []
    for g in range(N_GROUPS):
        xg = biased[g * GROUP_SIZE:(g + 1) * GROUP_SIZE, :]
        rid = g * GROUP_SIZE + lax.broadcasted_iota(jnp.int32, (GROUP_SIZE, tl), 0)
        first, m1 = first_argmax(xg, rid, N_EXPERTS)
        m2 = jnp.max(jnp.where(rid == first, neg, xg), axis=0, keepdims=True)
        gscores.append(m1 + m2)
    blocks = []
    for g in range(N_GROUPS):
        beaten = jnp.zeros((1, tl), F32)
        for o in range(N_GROUPS):
            if o != g:
                wins = (gscores[o] >= gscores[g]) if o < g else (gscores[o] > gscores[g])
                beaten = beaten + jnp.where(wins, 1.0, 0.0)
        xg = biased[g * GROUP_SIZE:(g + 1) * GROUP_SIZE, :]
        blocks.append(jnp.where(beaten < float(TOPK_GROUPS), xg, neg))
    masked = jnp.concatenate(blocks, axis=0)

    idxs, gates = [], []
    chosen = jnp.zeros((N_EXPERTS, tl), F32)
    for _ in range(TOP_K):
        first, _m = first_argmax(masked, rowid, N_EXPERTS)
        sel = rowid == first
        gates.append(jnp.sum(jnp.where(sel, s, 0.0), axis=0, keepdims=True))
        idxs.append(first)
        chosen = jnp.where(sel, 1.0, chosen)
        masked = jnp.where(sel, neg, masked)
    gate_sum = functools.reduce(lambda a, b: a + b, gates)
    for k in range(TOP_K):
        gate_ref[k:k + 1, :] = gates[k] / gate_sum * ROUTED_SCALE
        idx_ref[k:k + 1, :] = idxs[k]

    lr = lax.broadcasted_iota(jnp.int32, (tl, tl), 0)
    lc = lax.broadcasted_iota(jnp.int32, (tl, tl), 1)
    prefix = (lr <= lc).astype(BF16)
    cnt_incl = _dot(chosen.astype(BF16), prefix)
    carry = carry_ref[...]
    rank_excl = cnt_incl - chosen + carry
    for k in range(TOP_K):
        rank_k = jnp.sum(jnp.where(rowid == idxs[k], rank_excl, 0.0), axis=0, keepdims=True)
        rank_ref[k:k + 1, :] = rank_k.astype(jnp.int32)
    carry = carry + jnp.sum(chosen, axis=1, keepdims=True)
    carry_ref[...] = carry
    cnt_ref[...] = carry.astype(jnp.int32)


def _route(logits_t, bias, tl):
    T = logits_t.shape[1]
    tok = lambda i: (0, i)
    return pl.pallas_call(
        _route_kernel,
        out_shape=(jax.ShapeDtypeStruct((TOP_K, T), jnp.int32),
                   jax.ShapeDtypeStruct((TOP_K, T), F32),
                   jax.ShapeDtypeStruct((TOP_K, T), jnp.int32),
                   jax.ShapeDtypeStruct((N_EXPERTS, 1), jnp.int32)),
        grid=(T // tl,),
        in_specs=[pl.BlockSpec((N_EXPERTS, tl), tok), pl.BlockSpec((N_EXPERTS, 1), lambda i: (0, 0))],
        out_specs=(pl.BlockSpec((TOP_K, tl), tok), pl.BlockSpec((TOP_K, tl), tok),
                   pl.BlockSpec((TOP_K, tl), tok), pl.BlockSpec((N_EXPERTS, 1), lambda i: (0, 0))),
        scratch_shapes=[pltpu.VMEM((N_EXPERTS, 1), F32)],
        compiler_params=pltpu.CompilerParams(
            dimension_semantics=("arbitrary",), vmem_limit_bytes=VMEM_LIMIT),
        name="route",
    )(logits_t, bias)


def _as_rows(ref):
    return ref.reshape(ref.shape[0] // ROW_TILES, ROW_TILES, LANES)


def _wait_rows(rows_ref, n, sem):
    pltpu.make_async_copy(rows_ref.at[pl.ds(0, n)], rows_ref.at[pl.ds(0, n)], sem).wait()


def _scatter_kernel(pos_ref, h2_ref, xs_ref, sem):
    src = _as_rows(h2_ref)
    dst = _as_rows(xs_ref)
    tt = src.shape[0]

    def start(t, c):
        for k in range(TOP_K):
            pltpu.make_async_copy(src.at[t], dst.at[pos_ref[0, 0, t * TOP_K + k]], sem).start(priority=k % 2)
        return c

    lax.fori_loop(0, tt, start, 0)
    _wait_rows(dst, tt * TOP_K, sem)


def _scatter(pos_tiles, h2_tm, tt):
    n_rows = h2_tm.shape[0] // ROW_TILES * TOP_K
    return pl.pallas_call(
        _scatter_kernel,
        out_shape=jax.ShapeDtypeStruct((n_rows * ROW_TILES, LANES), F32),
        grid=(pos_tiles.shape[0],),
        in_specs=[pl.BlockSpec((1, 1, tt * TOP_K), lambda i: (i, 0, 0), memory_space=pltpu.SMEM),
                  pl.BlockSpec((tt * ROW_TILES, LANES), lambda i: (i, 0))],
        out_specs=pl.BlockSpec(memory_space=pl.ANY),
        scratch_shapes=[pltpu.SemaphoreType.DMA],
        compiler_params=pltpu.CompilerParams(
            dimension_semantics=("arbitrary",), vmem_limit_bytes=VMEM_LIMIT),
        name="scatter",
    )(pos_tiles, h2_tm)


def _experts_kernel(tile_ref, exp_ref, first_ref, last_ref, newe_ref, nitems_ref, off_ref, cnt_ref, slot_ref,
                    nexte_ref, xs_hbm, wg_hbm, wu_hbm, wd_hbm, ys_hbm,
                    xbuf_ref, ybuf_ref, sg_ref, su_ref, sd_ref, wgb_ref, wub_ref, wdb_ref, hm_ref,
                    xsem, ysem, wsem):
    i = pl.program_id(0)
    tr = EXPERT_TILE
    tile_rows = tr * ROW_TILES
    n_tiles = xs_hbm.shape[0] // tile_rows
    n_items = nitems_ref[0]

    def ring(t):
        return pl.ds(pl.multiple_of((t % TILE_RING) * tile_rows, tile_rows), tile_rows)

    def hbm_tile(t):
        return pl.ds(pl.multiple_of(t * tile_rows, tile_rows), tile_rows)

    def x_copy(t):
        return pltpu.make_async_copy(xs_hbm.at[hbm_tile(t)], xbuf_ref.at[ring(t)], xsem.at[t % TILE_RING])

    def y_copy(t):
        return pltpu.make_async_copy(ybuf_ref.at[ring(t)], ys_hbm.at[hbm_tile(t)], ysem.at[t % TILE_RING])

    def weight_copies(e, slot):
        return (pltpu.make_async_copy(wg_hbm.at[e], sg_ref.at[slot], wsem.at[slot]),
                pltpu.make_async_copy(wu_hbm.at[e], su_ref.at[slot], wsem.at[slot]),
                pltpu.make_async_copy(wd_hbm.at[e], sd_ref.at[slot], wsem.at[slot]))

    a_on = i < n_items
    j = jnp.maximum(i - 1, 0)
    b_on = (i >= 1) & (i - 1 < n_items)
    e = exp_ref[i]
    t = tile_ref[i]

    @pl.when(i == 0)
    def _():
        hm_ref[...] = jnp.zeros_like(hm_ref)
        for t0 in range(TILE_AHEAD):
            x_copy(t0).start()

    @pl.when(a_on & (first_ref[i] == 1))
    def _():
        @pl.when(t + TILE_AHEAD < n_tiles)
        def _():
            x_copy(t + TILE_AHEAD).start()

        x_copy(t).wait()

    @pl.when(a_on & (newe_ref[i] == 1))
    def _():
        slot = slot_ref[i]
        nxt = nexte_ref[i]

        @pl.when(i == 0)
        def _():
            for c in weight_copies(e, slot):
                c.start()

        @pl.when(nxt >= 0)
        def _():
            for c in weight_copies(nxt, 1 - slot):
                c.start()

        for c in weight_copies(e, slot):
            c.wait()
        wgb_ref[...] = sg_ref[slot].astype(BF16)
        wub_ref[...] = su_ref[slot].astype(BF16)
        wdb_ref[slot] = sd_ref[slot].astype(BF16)

    tj = tile_ref[j]
    ej = exp_ref[j]
    chunks = _row_chunks(_dot(hm_ref[j % 2], wdb_ref[slot_ref[j]]))
    row = tj * tr + lax.broadcasted_iota(jnp.int32, (tr, 1), 0)
    lo = off_ref[ej]
    mine = (row >= lo) & (row < lo + cnt_ref[ej])

    x = _load_rows(xbuf_ref, tr, first_row=(t % TILE_RING) * tr).astype(BF16)
    hm_ref[i % 2] = (_silu(_dot(x, wgb_ref[...])) * _dot(x, wub_ref[...])).astype(BF16)

    out_row = (tj % TILE_RING) * tile_rows

    @pl.when(b_on & (first_ref[j] == 1))
    def _():
        @pl.when(tj >= TILE_RING)
        def _():
            y_copy(tj - TILE_RING).wait()

        for c in range(ROW_TILES):
            ybuf_ref[pl.ds(out_row + c, tr, stride=ROW_TILES), :] = chunks[c]

    @pl.when(b_on & (first_ref[j] == 0))
    def _():
        for c in range(ROW_TILES):
            sl = pl.ds(out_row + c, tr, stride=ROW_TILES)
            ybuf_ref[sl, :] = jnp.where(mine, chunks[c], ybuf_ref[sl, :])

    @pl.when(b_on & (last_ref[j] == 1))
    def _():
        y_copy(tj).start()

    @pl.when(b_on & (j == n_items - 1))
    def _():
        for t0 in range(n_tiles - TILE_RING, n_tiles):
            y_copy(t0).wait()


def _experts(meta, xs, wg, wu, wd, n_items_max):
    tile_rows = EXPERT_TILE * ROW_TILES
    assert xs.shape[0] % tile_rows == 0 and xs.shape[0] // tile_rows >= TILE_RING
    hbm = pl.BlockSpec(memory_space=pl.ANY)
    n_slots = 2
    grid_spec = pltpu.PrefetchScalarGridSpec(
        num_scalar_prefetch=len(meta),
        grid=(n_items_max + 1,),
        in_specs=[hbm, hbm, hbm, hbm],
        out_specs=hbm,
        scratch_shapes=[pltpu.VMEM((TILE_RING * tile_rows, LANES), F32),
                        pltpu.VMEM((TILE_RING * tile_rows, LANES), F32),
                        pltpu.VMEM((n_slots, D_MODEL, D_EXPERT), F32),
                        pltpu.VMEM((n_slots, D_MODEL, D_EXPERT), F32),
                        pltpu.VMEM((n_slots, D_EXPERT, D_MODEL), F32),
                        pltpu.VMEM((D_MODEL, D_EXPERT), BF16),
                        pltpu.VMEM((D_MODEL, D_EXPERT), BF16),
                        pltpu.VMEM((n_slots, D_EXPERT, D_MODEL), BF16),
                        pltpu.VMEM((2, EXPERT_TILE, D_EXPERT), BF16),
                        pltpu.SemaphoreType.DMA((TILE_RING,)),
                        pltpu.SemaphoreType.DMA((TILE_RING,)),
                        pltpu.SemaphoreType.DMA((n_slots,))])
    return pl.pallas_call(
        _experts_kernel,
        out_shape=jax.ShapeDtypeStruct(xs.shape, F32),
        grid_spec=grid_spec,
        compiler_params=pltpu.CompilerParams(
            dimension_semantics=("arbitrary",), vmem_limit_bytes=VMEM_LIMIT),
        name="experts",
    )(*meta, xs, wg, wu, wd)


def _combine_kernel(pos_ref, pos_next_ref, ys_ref, h2_ref, x1_ref, gate_ref, g2_ref, swg_ref, swu_ref, swd_ref,
                    fg_ref, out_ref, buf_ref, sem):
    i = pl.program_id(0)
    tt = x1_ref.shape[0]
    slot_rows = tt * TOP_K
    src = _as_rows(ys_ref)
    dst = _as_rows(buf_ref)

    def gather(p_ref, slot):
        def start(t, c):
            for k in range(TOP_K):
                pltpu.make_async_copy(src.at[p_ref[0, 0, t * TOP_K + k]],
                                      dst.at[slot * slot_rows + k * tt + t], sem.at[slot]).start(priority=k % 2)
            return c

        lax.fori_loop(0, tt, start, 0)

    slot = i % 2

    @pl.when(i == 0)
    def _():
        gather(pos_ref, slot)

    @pl.when(i + 1 < pl.num_programs(0))
    def _():
        gather(pos_next_ref, 1 - slot)

    h2 = _load_rows(h2_ref, tt).astype(BF16)
    hm = (_silu(_dot(h2, swg_ref[...])) * _dot(h2, swu_ref[...])).astype(BF16)
    acc = _dot(hm, swd_ref[...])

    _wait_rows(dst, slot_rows, sem.at[slot])

    gate = gate_ref[...]
    for k in range(TOP_K):
        acc = acc + gate[:, k:k + 1] * _load_rows(buf_ref, tt, first_row=slot * slot_rows + k * tt)
    x2 = x1_ref[...] + g2_ref[0] * acc
    out_ref[...] = x2 * lax.rsqrt(jnp.mean(x2 * x2, axis=-1, keepdims=True) + EPS) * fg_ref[...]


def _combine(pos_tiles, ys, h2_tm, x1, gate_tm, mod3, swg, swu, swd, fg, B, S, tt):
    T, D = x1.shape
    nS = S // tt
    const2 = lambda i: (0, 0)
    n_tiles = T // tt
    n_slots = 2
    return pl.pallas_call(
        _combine_kernel,
        out_shape=jax.ShapeDtypeStruct((T, D), F32),
        grid=(n_tiles,),
        in_specs=[pl.BlockSpec((1, 1, tt * TOP_K), lambda i: (i, 0, 0), memory_space=pltpu.SMEM),
                  pl.BlockSpec((1, 1, tt * TOP_K), lambda i: (jnp.minimum(i + 1, n_tiles - 1), 0, 0),
                               memory_space=pltpu.SMEM),
                  pl.BlockSpec(memory_space=pl.ANY),
                  pl.BlockSpec((tt * ROW_TILES, LANES), lambda i: (i, 0)),
                  pl.BlockSpec((tt, D), lambda i: (i, 0)),
                  pl.BlockSpec((tt, TOP_K), lambda i: (i, 0)),
                  pl.BlockSpec((1, 1, D), lambda i: (i // nS, 0, 5)),
                  pl.BlockSpec(swg.shape, const2),
                  pl.BlockSpec(swu.shape, const2),
                  pl.BlockSpec(swd.shape, const2),
                  pl.BlockSpec((1, D), const2)],
        out_specs=pl.BlockSpec((tt, D), lambda i: (i, 0)),
        scratch_shapes=[pltpu.VMEM((n_slots * tt * TOP_K * ROW_TILES, LANES), F32),
                        pltpu.SemaphoreType.DMA((n_slots,))],
        compiler_params=pltpu.CompilerParams(
            dimension_semantics=("arbitrary",), vmem_limit_bytes=VMEM_LIMIT),
        name="combine",
    )(pos_tiles, pos_tiles, ys, h2_tm, x1, gate_tm, mod3, swg, swu, swd, fg)


def _place_kernel(idx_ref, rank_ref, off_ref, pos_ref):
    tl = idx_ref.shape[1]
    rowid = lax.broadcasted_iota(jnp.int32, (N_EXPERTS, tl), 0)
    off = off_ref[...].astype(F32)
    for k in range(TOP_K):
        base = jnp.sum(jnp.where(rowid == idx_ref[k:k + 1, :], off, 0.0), axis=0, keepdims=True)
        pos_ref[k:k + 1, :] = base.astype(jnp.int32) + rank_ref[k:k + 1, :]


def _place(idx, rank, off, tl):
    T = idx.shape[1]
    tok = pl.BlockSpec((TOP_K, tl), lambda i: (0, i))
    return pl.pallas_call(
        _place_kernel,
        out_shape=jax.ShapeDtypeStruct((TOP_K, T), jnp.int32),
        grid=(T // tl,),
        in_specs=[tok, tok, pl.BlockSpec((N_EXPERTS, 1), lambda i: (0, 0))],
        out_specs=tok,
        name="place",
    )(idx, rank, off)


def _plan(counts, n_items_max):
    counts = counts[:, 0]
    off = jnp.cumsum(counts) - counts
    first_tile = off // EXPERT_TILE
    last_tile = (off + counts - 1) // EXPERT_TILE
    n_e = jnp.where(counts > 0, last_tile - first_tile + 1, 0)
    item_end = jnp.cumsum(n_e)
    item_start = item_end - n_e
    n_items = item_end[-1]
    ids = jnp.arange(n_items_max + 1, dtype=jnp.int32)
    ids_c = jnp.minimum(ids, n_items - 1)
    item_e = jnp.minimum(jnp.sum(item_end[None, :] <= ids_c[:, None], axis=1), N_EXPERTS - 1).astype(jnp.int32)
    item_tile = (first_tile[item_e] + ids_c - item_start[item_e]).astype(jnp.int32)
    prev_tile = jnp.concatenate([jnp.full((1,), -1, jnp.int32), item_tile[:-1]])
    prev_e = jnp.concatenate([jnp.full((1,), -1, jnp.int32), item_e[:-1]])
    next_tile = jnp.concatenate([item_tile[1:], jnp.full((1,), -1, jnp.int32)])
    item_first = (item_tile != prev_tile).astype(jnp.int32)
    item_last = ((item_tile != next_tile) | (ids == n_items - 1)).astype(jnp.int32)
    item_newe = (item_e != prev_e).astype(jnp.int32)
    item_slot = ((jnp.cumsum(item_newe) - 1) % 2).astype(jnp.int32)
    ids_e = jnp.arange(N_EXPERTS, dtype=jnp.int32)
    later = jnp.where((counts[None, :] > 0) & (ids_e[None, :] > ids_e[:, None]), ids_e[None, :], N_EXPERTS)
    next_e = jnp.min(later, axis=1)
    next_e = jnp.where(next_e < N_EXPERTS, next_e, -1).astype(jnp.int32)
    meta = (item_tile, item_e, item_first, item_last, item_newe, n_items.reshape(1).astype(jnp.int32),
            off.astype(jnp.int32), counts.astype(jnp.int32), item_slot, next_e[item_e])
    return off.astype(jnp.int32).reshape(N_EXPERTS, 1), meta


def kernel(x, c, ada_w, ada_b, norm1_g, w_in, hgrn_lb, hgrn_norm_g, pool_w, pool_b, pool_scale, w_up_a, w_up_b, w_out, norm2_g, router_w, router_bias, exp_w_gate, exp_w_up, exp_w_down, shared_w_gate, shared_w_up, shared_w_down, final_norm_g):
    B, S, D = x.shape
    T = B * S
    assert ada_w.shape[0] == 1, "single-layer trunk only: the final norm is fused into the combine step"
    lb_all = jnp.cumsum(jax.nn.softmax(hgrn_lb.astype(F32), axis=0), axis=0)
    c_pad = jnp.zeros((SUBLANES, D), F32).at[:B].set(c)
    n_items_max = T * TOP_K // EXPERT_TILE + N_EXPERTS - 1

    for l in range(1):
        mod = _ada(c_pad, ada_w[l], ada_b[l].reshape(1, -1))
        mod3 = mod[:B].reshape(B, 1, 6 * D)

        q, lf, k, v, sog, pm, sga, sgb = _inproj(
            x, mod3, norm1_g[l].reshape(1, D), w_in[l].astype(BF16), lb_all[l].reshape(1, HG_WIDTH),
            pool_w[l].astype(BF16), pool_b[l].reshape(len(POOL_WINDOWS), 1, POOL_GROUP),
            pool_scale[l].reshape(1, POOL_WIDTH), tm=256)
        oa = _hgrn(q, lf, k, v, sog, hgrn_norm_g[l].reshape(1, HG_DK), B, S, tb=512)

        x1, h2_tm, logits_t = _mix(
            x, oa, pm, sga, sgb, mod3, norm2_g[l].reshape(1, D), w_up_a[l].astype(BF16),
            w_up_b[l].astype(BF16), w_out[l].astype(BF16), router_w[l].T, tm=256)

        idx, gate, rank, counts = _route(logits_t, router_bias[l].reshape(N_EXPERTS, 1), tl=256)
        off, meta = _plan(counts, n_items_max)
        pos_tm = _place(idx, rank, off, tl=512).T.reshape(-1)

        tt_s = 256
        xs = _scatter(pos_tm.reshape(T // tt_s, 1, tt_s * TOP_K), h2_tm, tt_s)
        ys = _experts(meta, xs, exp_w_gate[l], exp_w_up[l], exp_w_down[l], n_items_max)

        tt_c = 128
        fg = final_norm_g.reshape(1, D)
        x = _combine(pos_tm.reshape(T // tt_c, 1, tt_c * TOP_K), ys, h2_tm, x1, gate.T, mod3,
                     shared_w_gate[l].astype(BF16), shared_w_up[l].astype(BF16),
                     shared_w_down[l].astype(BF16), fg, B, S, tt_c).reshape(B, S, D)
    return x
```

```python
import functools

import jax
import jax.numpy as jnp
from jax import lax
from jax.experimental import pallas as pl
from jax.experimental.pallas import tpu as pltpu

F32 = jnp.float32
BF16 = jnp.bfloat16
HIGHEST = lax.Precision.HIGHEST

D_MODEL = 1024
HG_WIDTH = 512
HG_DK = 128
HG_HEADS = 4
HG_CHUNK = 64
HG_BLK = 16
HG_SUB = 8
POOL_WIDTH = 512
POOL_WINDOWS = (2, 4, 8, 16)
POOL_GROUP = 128
POOL_HALO = 16
N_EXPERTS = 256
TOP_K = 8
N_GROUPS = 8
TOPK_GROUPS = 4
GROUP_SIZE = N_EXPERTS // N_GROUPS
D_EXPERT = 256
ROUTED_SCALE = 2.5
EPS = 1e-6

LANES = 128
SUBLANES = 8
ROW_TILES = D_MODEL // LANES
EXPERT_TILE = 128
TILE_RING = 8
TILE_AHEAD = TILE_RING - 1
VMEM_LIMIT = 56 * 1024 * 1024

COL_Q, COL_F, COL_I, COL_OG, COL_U, COL_GA, COL_GB = 0, 512, 1024, 1536, 2048, 2560, 3584


def _sigmoid(x):
    return 1.0 / (1.0 + jnp.exp(-x))


def _silu(x):
    return x * _sigmoid(x)


def _dot(a, b):
    return jnp.dot(a, b, preferred_element_type=F32)


def _dot_nt(a, b):
    return lax.dot_general(a, b, (((1,), (1,)), ((), ())), preferred_element_type=F32)


def _dot_tn(a, b):
    return lax.dot_general(a, b, (((0,), (0,)), ((), ())), preferred_element_type=F32)


def _row_chunks(x):
    return [x[:, j * LANES:(j + 1) * LANES] for j in range(ROW_TILES)]


def _load_rows(ref, n, first_row=0):
    return jnp.concatenate(
        [ref[pl.ds(first_row * ROW_TILES + j, n, stride=ROW_TILES), :] for j in range(ROW_TILES)], axis=1)


def _ada_kernel(c_ref, w_ref, b_ref, o_ref):
    cond = _silu(c_ref[...])
    o_ref[...] = jnp.dot(cond, w_ref[...], precision=HIGHEST, preferred_element_type=F32) + b_ref[...]


def _ada(c_pad, ada_w, ada_b):
    n = ada_w.shape[1]
    tn = 1536
    return pl.pallas_call(
        _ada_kernel,
        out_shape=jax.ShapeDtypeStruct((SUBLANES, n), F32),
        grid=(n // tn,),
        in_specs=[pl.BlockSpec((SUBLANES, D_MODEL), lambda j: (0, 0)),
                  pl.BlockSpec((D_MODEL, tn), lambda j: (0, j)),
                  pl.BlockSpec((1, tn), lambda j: (0, j))],
        out_specs=pl.BlockSpec((SUBLANES, tn), lambda j: (0, j)),
        compiler_params=pltpu.CompilerParams(vmem_limit_bytes=VMEM_LIMIT),
        name="ada",
    )(c_pad, ada_w, ada_b)


def _inproj_kernel(x_ref, sh_ref, sc_ref, g_ref, w_ref, lb_ref, pw_ref, pb_ref, ps_ref,
                   q_ref, lf_ref, k_ref, v_ref, sog_ref, pm_ref, sga_ref, sgb_ref, halo_ref):
    s = pl.program_id(1)
    tm = x_ref.shape[1]
    x = x_ref[0]
    h = x * lax.rsqrt(jnp.mean(x * x, axis=-1, keepdims=True) + EPS) * g_ref[...]
    h = h * (1.0 + sc_ref[0]) + sh_ref[0]
    hb = h.astype(BF16)

    def proj(lo, n):
        return _dot(hb, w_ref[:, lo:lo + n])

    q = proj(COL_Q, HG_WIDTH)
    q_ref[...] = _silu(q) * (HG_DK ** -0.5)
    sig = _sigmoid(proj(COL_F, HG_WIDTH))
    lb = lb_ref[...]
    lf_ref[...] = jnp.log(lb + (1.0 - lb) * sig)
    k_ref[...] = (1.0 - lb) * (1.0 - sig)
    v_ref[...] = proj(COL_I, HG_WIDTH)
    sog_ref[...] = _silu(proj(COL_OG, HG_WIDTH)).astype(BF16)
    sga_ref[...] = _sigmoid(proj(COL_GA, D_MODEL)).astype(BF16)
    sgb_ref[...] = _sigmoid(proj(COL_GB, D_MODEL)).astype(BF16)

    u = proj(COL_U, POOL_WIDTH)
    @pl.when(s == 0)
    def _():
        halo_ref[...] = jnp.zeros_like(halo_ref)

    ext = jnp.concatenate([halo_ref[...], u], axis=0)
    halo_ref[...] = u[tm - POOL_HALO:, :]
    s2 = ext + pltpu.roll(ext, 1, 0)
    s4 = s2 + pltpu.roll(s2, 2, 0)
    s8 = s4 + pltpu.roll(s4, 4, 0)
    s16 = s8 + pltpu.roll(s8, 8, 0)
    pos1 = (s * tm + 1 + lax.broadcasted_iota(jnp.int32, (tm, 1), 0)).astype(F32)
    for g, (w, sw) in enumerate(zip(POOL_WINDOWS, (s2, s4, s8, s16))):
        cols = slice(g * POOL_GROUP, (g + 1) * POOL_GROUP)
        m = sw[POOL_HALO:, cols] / jnp.minimum(pos1, float(w)) - u[:, cols]
        y = _dot(m.astype(BF16), pw_ref[g]) + pb_ref[g]
        pm_ref[:, cols] = (y * ps_ref[:, cols]).astype(BF16)


def _inproj(x, mod3, norm_g, w_in_b, lb, pool_w_b, pool_b, pool_scale, tm):
    B, S, D = x.shape
    T = B * S
    nS = S // tm
    row = lambda b, s: (b * nS + s, 0)
    const2 = lambda b, s: (0, 0)
    const3 = lambda b, s: (0, 0, 0)
    half = lambda dt: jax.ShapeDtypeStruct((T, HG_WIDTH), dt)
    full = lambda dt: jax.ShapeDtypeStruct((T, D), dt)
    return pl.pallas_call(
        _inproj_kernel,
        out_shape=(half(F32), half(F32), half(F32), half(F32), half(BF16), half(BF16), full(BF16), full(BF16)),
        grid=(B, nS),
        in_specs=[pl.BlockSpec((1, tm, D), lambda b, s: (b, s, 0)),
                  pl.BlockSpec((1, 1, D), lambda b, s: (b, 0, 0)),
                  pl.BlockSpec((1, 1, D), lambda b, s: (b, 0, 1)),
                  pl.BlockSpec((1, D), const2),
                  pl.BlockSpec(w_in_b.shape, const2),
                  pl.BlockSpec((1, HG_WIDTH), const2),
                  pl.BlockSpec(pool_w_b.shape, const3),
                  pl.BlockSpec(pool_b.shape, const3),
                  pl.BlockSpec((1, POOL_WIDTH), const2)],
        out_specs=(pl.BlockSpec((tm, HG_WIDTH), row),) * 6 + (pl.BlockSpec((tm, D), row),) * 2,
        scratch_shapes=[pltpu.VMEM((POOL_HALO, POOL_WIDTH), F32)],
        compiler_params=pltpu.CompilerParams(
            dimension_semantics=("arbitrary", "arbitrary"), vmem_limit_bytes=VMEM_LIMIT),
        name="inproj",
    )(x, mod3, mod3, norm_g, w_in_b, lb, pool_w_b, pool_b, pool_scale)


def _hgrn_kernel(q_ref, lf_ref, k_ref, v_ref, sog_ref, gn_ref, o_ref, *st_refs):
    C = HG_CHUNK
    n_chunks = q_ref.shape[0] // C

    @pl.when(pl.program_id(1) == 0)
    def _():
        for st_ref in st_refs:
            st_ref[...] = jnp.zeros_like(st_ref)

    r_i = lax.broadcasted_iota(jnp.int32, (C, C), 0)
    c_i = lax.broadcasted_iota(jnp.int32, (C, C), 1)
    tril = (c_i <= r_i).astype(BF16)
    same_blk = (r_i // HG_BLK) == (c_i // HG_BLK)
    row = lax.broadcasted_iota(jnp.int32, (C, HG_DK), 0)
    row_in_sub = row % HG_SUB
    upper_half = (row % HG_BLK) >= HG_SUB
    row_blk = row // HG_BLK
    n_blk = C // HG_BLK

    def cumsum_rows(x):
        hi = x.astype(BF16)
        r1 = x - hi.astype(F32)
        mid = r1.astype(BF16)
        lo = (r1 - mid.astype(F32)).astype(BF16)
        return _dot(tril, hi) + _dot(tril, mid) + _dot(tril, lo)

    def block_rows(x, size, which):
        pieces = []
        for g in range(C // size):
            src = g * size + which
            pieces.append(jnp.zeros((size, x.shape[1]), F32) if src < 0
                          else jnp.broadcast_to(x[src:src + 1, :], (size, x.shape[1])))
        return jnp.concatenate(pieces, axis=0)

    def chunk(ci, carry):
        rs = pl.ds(pl.multiple_of(ci * C, C), C)
        b_all = cumsum_rows(lf_ref[rs, :])
        for h in range(HG_HEADS):
            cs = slice(h * HG_DK, (h + 1) * HG_DK)
            q = q_ref[rs, cs]
            k = k_ref[rs, cs]
            v = v_ref[rs, cs]
            b = b_all[:, cs]
            vb = v.astype(BF16)

            kt = k * jnp.exp(block_rows(b, HG_BLK, HG_BLK - 1) - b)
            q_parts, k_parts = [], []
            for j in range(n_blk - 1):
                bj = b[HG_BLK * j + HG_BLK - 1:HG_BLK * (j + 1), :]
                after = row >= HG_BLK * (j + 1)
                q_parts.append(q * jnp.exp(jnp.where(after, b - bj, -jnp.inf)))
                k_parts.append(jnp.where(row_blk == j, kt, 0.0))
            qcat = jnp.concatenate(q_parts, axis=1).astype(BF16)
            kcat = jnp.concatenate(k_parts, axis=1).astype(BF16)
            scores = _dot_nt(qcat, kcat)
            b_prev = block_rows(b, HG_SUB, -1)
            b_sub = block_rows(b, HG_SUB, HG_SUB - 1)
            qh = (q * jnp.exp(jnp.where(upper_half, b - b_prev, -jnp.inf))).astype(BF16)
            kh = jnp.where(upper_half, 0.0, k * jnp.exp(b_sub - b)).astype(BF16)
            scores = scores + jnp.where(same_blk, _dot_nt(qh, kh), 0.0)
            o = _dot(scores.astype(BF16), vb)

            o = o + jnp.sum(q * k, axis=-1, keepdims=True) * v
            for d in range(1, HG_SUB):
                kd = pltpu.roll(k, d, 0)
                bd = pltpu.roll(b, d, 0)
                vd = pltpu.roll(v, d, 0)
                e = jnp.exp(jnp.where(row_in_sub >= d, b - bd, -jnp.inf))
                o = o + jnp.sum(q * kd * e, axis=-1, keepdims=True) * vd

            st = st_refs[h][...]
            o = o + _dot_nt((q * jnp.exp(b)).astype(BF16), st.astype(BF16))
            b_end = b[C - 1:C, :]
            k_end = (k * jnp.exp(b_end - b)).astype(BF16)
            st_refs[h][...] = st * jnp.exp(b_end) + _dot_tn(vb, k_end)

            on = o * lax.rsqrt(jnp.mean(o * o, axis=-1, keepdims=True) + EPS) * gn_ref[...]
            o_ref[rs, cs] = (on * sog_ref[rs, cs].astype(F32)).astype(BF16)
        return carry

    lax.fori_loop(0, n_chunks, chunk, 0)


def _hgrn(q, lf, k, v, sog, gn, B, S, tb):
    T = B * S
    nS = S // tb
    row = lambda b, s: (b * nS + s, 0)
    blk = pl.BlockSpec((tb, HG_WIDTH), row)
    return pl.pallas_call(
        _hgrn_kernel,
        out_shape=jax.ShapeDtypeStruct((T, HG_WIDTH), BF16),
        grid=(B, nS),
        in_specs=[blk, blk, blk, blk, blk, pl.BlockSpec((1, HG_DK), lambda b, s: (0, 0))],
        out_specs=blk,
        scratch_shapes=[pltpu.VMEM((HG_DK, HG_DK), F32)] * HG_HEADS,
        compiler_params=pltpu.CompilerParams(
            dimension_semantics=("arbitrary", "arbitrary"), vmem_limit_bytes=VMEM_LIMIT),
        name="hgrn",
    )(q, lf, k, v, sog, gn)


def _split_kernel(w_ref, hi_ref, lo_ref):
    w = w_ref[...]
    hi = w.astype(BF16)
    hi_ref[...] = hi
    lo_ref[...] = (w - hi.astype(F32)).astype(BF16)


def _split_bf16(w):
    out = jax.ShapeDtypeStruct(w.shape, BF16)
    return pl.pallas_call(_split_kernel, out_shape=(out, out), name="split")(w)


def _mix_kernel(x_ref, oa_ref, pm_ref, sga_ref, sgb_ref, g1_ref, sh2_ref, sc2_ref, n2_ref,
                wua_ref, wub_ref, wo_ref, rw_hi_ref, rw_lo_ref, x1_ref, h2_ref, lg_ref):
    tm = x_ref.shape[1]
    ya = _dot(oa_ref[...], wua_ref[...])
    yb = _dot(pm_ref[...], wub_ref[...])
    mix = sga_ref[...].astype(F32) * ya + sgb_ref[...].astype(F32) * yb
    x1 = x_ref[0] + g1_ref[0] * _dot(mix.astype(BF16), wo_ref[...])
    x1_ref[...] = x1
    h2 = x1 * lax.rsqrt(jnp.mean(x1 * x1, axis=-1, keepdims=True) + EPS) * n2_ref[...]
    h2 = h2 * (1.0 + sc2_ref[0]) + sh2_ref[0]
    for j, chunk in enumerate(_row_chunks(h2)):
        h2_ref[pl.ds(j, tm, stride=ROW_TILES), :] = chunk
    h_hi = h2.astype(BF16)
    h_lo = (h2 - h_hi.astype(F32)).astype(BF16)
    rw_hi = rw_hi_ref[...]
    lg_ref[...] = _dot_nt(rw_hi, h_hi) + _dot_nt(rw_hi, h_lo) + _dot_nt(rw_lo_ref[...], h_hi)


def _mix(x, oa, pm, sga, sgb, mod3, norm2_g, wua, wub, wo, rw_hi, rw_lo, tm):
    B, S, D = x.shape
    T = B * S
    nS = S // tm
    row = lambda b, s: (b * nS + s, 0)
    const2 = lambda b, s: (0, 0)
    return pl.pallas_call(
        _mix_kernel,
        out_shape=(jax.ShapeDtypeStruct((T, D), F32),
                   jax.ShapeDtypeStruct((T * ROW_TILES, LANES), F32),
                   jax.ShapeDtypeStruct((N_EXPERTS, T), F32)),
        grid=(B, nS),
        in_specs=[pl.BlockSpec((1, tm, D), lambda b, s: (b, s, 0)),
                  pl.BlockSpec((tm, HG_WIDTH), row),
                  pl.BlockSpec((tm, POOL_WIDTH), row),
                  pl.BlockSpec((tm, D), row),
                  pl.BlockSpec((tm, D), row),
                  pl.BlockSpec((1, 1, D), lambda b, s: (b, 0, 2)),
                  pl.BlockSpec((1, 1, D), lambda b, s: (b, 0, 3)),
                  pl.BlockSpec((1, 1, D), lambda b, s: (b, 0, 4)),
                  pl.BlockSpec((1, D), const2),
                  pl.BlockSpec(wua.shape, const2),
                  pl.BlockSpec(wub.shape, const2),
                  pl.BlockSpec(wo.shape, const2),
                  pl.BlockSpec(rw_hi.shape, const2),
                  pl.BlockSpec(rw_lo.shape, const2)],
        out_specs=(pl.BlockSpec((tm, D), row),
                   pl.BlockSpec((tm * ROW_TILES, LANES), row),
                   pl.BlockSpec((N_EXPERTS, tm), lambda b, s: (0, b * nS + s))),
        compiler_params=pltpu.CompilerParams(
            dimension_semantics=("arbitrary", "arbitrary"), vmem_limit_bytes=VMEM_LIMIT),
        name="mix",
    )(x, oa, pm, sga, sgb, mod3, mod3, mod3, norm2_g, wua, wub, wo, rw_hi, rw_lo)


def _route_kernel(lg_ref, bias_ref, idx_ref, gate_ref, rank_ref, cnt_ref, carry_ref):
    tl = lg_ref.shape[1]
    neg = -jnp.inf

    @pl.when(pl.program_id(0) == 0)
    def _():
        carry_ref[...] = jnp.zeros_like(carry_ref)

    s = _sigmoid(lg_ref[...])
    biased = s + bias_ref[...]
    rowid = lax.broadcasted_iota(jnp.int32, (N_EXPERTS, tl), 0)

    def first_argmax(x, ids, sentinel):
        m = jnp.max(x, axis=0, keepdims=True)
        return jnp.min(jnp.where(x == m, ids, sentinel), axis=0, keepdims=True), m

    gscores = []
    for g in range(N_GROUPS):
        xg = biased[g * GROUP_SIZE:(g + 1) * GROUP_SIZE, :]
        rid = g * GROUP_SIZE + lax.broadcasted_iota(jnp.int32, (GROUP_SIZE, tl), 0)
        first, m1 = first_argmax(xg, rid, N_EXPERTS)
        m2 = jnp.max(jnp.where(rid == first, neg, xg), axis=0, keepdims=True)
        gscores.append(m1 + m2)
    blocks = []
    for g in range(N_GROUPS):
        beaten = jnp.zeros((1, tl), F32)
        for o in range(N_GROUPS):
            if o != g:
                wins = (gscores[o] >= gscores[g]) if o < g else (gscores[o] > gscores[g])
                beaten = beaten + jnp.where(wins, 1.0, 0.0)
        xg = biased[g * GROUP_SIZE:(g + 1) * GROUP_SIZE, :]
        blocks.append(jnp.where(beaten < float(TOPK_GROUPS), xg, neg))
    masked = jnp.concatenate(blocks, axis=0)

    idxs, gates = [], []
    chosen = jnp.zeros((N_EXPERTS, tl), F32)
    for _ in range(TOP_K):
        first, _m = first_argmax(masked, rowid, N_EXPERTS)
        sel = rowid == first
        gates.append(jnp.sum(jnp.where(sel, s, 0.0), axis=0, keepdims=True))
        idxs.append(first)
        chosen = jnp.where(sel, 1.0, chosen)
        masked = jnp.where(sel, neg, masked)
    gate_sum = functools.reduce(lambda a, b: a + b, gates)
    for k in range(TOP_K):
        gate_ref[k:k + 1, :] = gates[k] / gate_sum * ROUTED_SCALE
        idx_ref[k:k + 1, :] = idxs[k]

    lr = lax.broadcasted_iota(jnp.int32, (tl, tl), 0)
    lc = lax.broadcasted_iota(jnp.int32, (tl, tl), 1)
    prefix = (lr <= lc).astype(BF16)
    cnt_incl = _dot(chosen.astype(BF16), prefix)
    carry = carry_ref[...]
    rank_excl = cnt_incl - chosen + carry
    for k in range(TOP_K):
        rank_k = jnp.sum(jnp.where(rowid == idxs[k], rank_excl, 0.0), axis=0, keepdims=True)
        rank_ref[k:k + 1, :] = rank_k.astype(jnp.int32)
    carry = carry + jnp.sum(chosen, axis=1, keepdims=True)
    carry_ref[...] = carry
    cnt_ref[...] = carry.astype(jnp.int32)


def _route(logits_t, bias, tl):
    T = logits_t.shape[1]
    tok = lambda i: (0, i)
    return pl.pallas_call(
        _route_kernel,
        out_shape=(jax.ShapeDtypeStruct((TOP_K, T), jnp.int32),
                   jax.ShapeDtypeStruct((TOP_K, T), F32),
                   jax.ShapeDtypeStruct((TOP_K, T), jnp.int32),
                   jax.ShapeDtypeStruct((N_EXPERTS, 1), jnp.int32)),
        grid=(T // tl,),
        in_specs=[pl.BlockSpec((N_EXPERTS, tl), tok), pl.BlockSpec((N_EXPERTS, 1), lambda i: (0, 0))],
        out_specs=(pl.BlockSpec((TOP_K, tl), tok), pl.BlockSpec((TOP_K, tl), tok),
                   pl.BlockSpec((TOP_K, tl), tok), pl.BlockSpec((N_EXPERTS, 1), lambda i: (0, 0))),
        scratch_shapes=[pltpu.VMEM((N_EXPERTS, 1), F32)],
        compiler_params=pltpu.CompilerParams(
            dimension_semantics=("arbitrary",), vmem_limit_bytes=VMEM_LIMIT),
        name="route",
    )(logits_t, bias)


def _as_rows(ref):
    return ref.reshape(ref.shape[0] // ROW_TILES, ROW_TILES, LANES)


def _wait_rows(rows_ref, n, sem):
    pltpu.make_async_copy(rows_ref.at[pl.ds(0, n)], rows_ref.at[pl.ds(0, n)], sem).wait()


def _scatter_kernel(pos_ref, h2_ref, xs_ref, sem):
    src = _as_rows(h2_ref)
    dst = _as_rows(xs_ref)
    tt = src.shape[0]

    def start(t, c):
        for k in range(TOP_K):
            pltpu.make_async_copy(src.at[t], dst.at[pos_ref[0, 0, t * TOP_K + k]], sem).start(priority=k % 2)
        return c

    lax.fori_loop(0, tt, start, 0)
    _wait_rows(dst, tt * TOP_K, sem)


def _scatter(pos_tiles, h2_tm, tt):
    n_rows = h2_tm.shape[0] // ROW_TILES * TOP_K
    return pl.pallas_call(
        _scatter_kernel,
        out_shape=jax.ShapeDtypeStruct((n_rows * ROW_TILES, LANES), F32),
        grid=(pos_tiles.shape[0],),
        in_specs=[pl.BlockSpec((1, 1, tt * TOP_K), lambda i: (i, 0, 0), memory_space=pltpu.SMEM),
                  pl.BlockSpec((tt * ROW_TILES, LANES), lambda i: (i, 0))],
        out_specs=pl.BlockSpec(memory_space=pl.ANY),
        scratch_shapes=[pltpu.SemaphoreType.DMA],
        compiler_params=pltpu.CompilerParams(
            dimension_semantics=("arbitrary",), vmem_limit_bytes=VMEM_LIMIT),
        name="scatter",
    )(pos_tiles, h2_tm)


def _experts_kernel(tile_ref, exp_ref, first_ref, last_ref, newe_ref, nitems_ref, off_ref, cnt_ref, slot_ref,
                    nexte_ref, xs_hbm, wg_hbm, wu_hbm, wd_hbm, ys_hbm,
                    xbuf_ref, ybuf_ref, sg_ref, su_ref, sd_ref, wgb_ref, wub_ref, wdb_ref, hm_ref,
                    xsem, ysem, wsem):
    i = pl.program_id(0)
    tr = EXPERT_TILE
    tile_rows = tr * ROW_TILES
    n_tiles = xs_hbm.shape[0] // tile_rows
    n_items = nitems_ref[0]

    def ring(t):
        return pl.ds(pl.multiple_of((t % TILE_RING) * tile_rows, tile_rows), tile_rows)

    def hbm_tile(t):
        return pl.ds(pl.multiple_of(t * tile_rows, tile_rows), tile_rows)

    def x_copy(t):
        return pltpu.make_async_copy(xs_hbm.at[hbm_tile(t)], xbuf_ref.at[ring(t)], xsem.at[t % TILE_RING])

    def y_copy(t):
        return pltpu.make_async_copy(ybuf_ref.at[ring(t)], ys_hbm.at[hbm_tile(t)], ysem.at[t % TILE_RING])

    def weight_copies(e, slot):
        return (pltpu.make_async_copy(wg_hbm.at[e], sg_ref.at[slot], wsem.at[slot]),
                pltpu.make_async_copy(wu_hbm.at[e], su_ref.at[slot], wsem.at[slot]),
                pltpu.make_async_copy(wd_hbm.at[e], sd_ref.at[slot], wsem.at[slot]))

    a_on = i < n_items
    j = jnp.maximum(i - 1, 0)
    b_on = (i >= 1) & (i - 1 < n_items)
    e = exp_ref[i]
    t = tile_ref[i]

    @pl.when(i == 0)
    def _():
        hm_ref[...] = jnp.zeros_like(hm_ref)
        for t0 in range(TILE_AHEAD):
            x_copy(t0).start()

    @pl.when(a_on & (first_ref[i] == 1))
    def _():
        @pl.when(t + TILE_AHEAD < n_tiles)
        def _():
            x_copy(t + TILE_AHEAD).start()

        x_copy(t).wait()

    @pl.when(a_on & (newe_ref[i] == 1))
    def _():
        slot = slot_ref[i]
        nxt = nexte_ref[i]

        @pl.when(i == 0)
        def _():
            for c in weight_copies(e, slot):
                c.start()

        @pl.when(nxt >= 0)
        def _():
            for c in weight_copies(nxt, 1 - slot):
                c.start()

        for c in weight_copies(e, slot):
            c.wait()
        wgb_ref[...] = sg_ref[slot].astype(BF16)
        wub_ref[...] = su_ref[slot].astype(BF16)
        wdb_ref[slot] = sd_ref[slot].astype(BF16)

    tj = tile_ref[j]
    ej = exp_ref[j]
    chunks = _row_chunks(_dot(hm_ref[j % 2], wdb_ref[slot_ref[j]]))
    row = tj * tr + lax.broadcasted_iota(jnp.int32, (tr, 1), 0)
    lo = off_ref[ej]
    mine = (row >= lo) & (row < lo + cnt_ref[ej])

    x = _load_rows(xbuf_ref, tr, first_row=(t % TILE_RING) * tr).astype(BF16)
    hm_ref[i % 2] = (_silu(_dot(x, wgb_ref[...])) * _dot(x, wub_ref[...])).astype(BF16)

    out_row = (tj % TILE_RING) * tile_rows

    @pl.when(b_on & (first_ref[j] == 1))
    def _():
        @pl.when(tj >= TILE_RING)
        def _():
            y_copy(tj - TILE_RING).wait()

        for c in range(ROW_TILES):
            ybuf_ref[pl.ds(out_row + c, tr, stride=ROW_TILES), :] = chunks[c]

    @pl.when(b_on & (first_ref[j] == 0))
    def _():
        for c in range(ROW_TILES):
            sl = pl.ds(out_row + c, tr, stride=ROW_TILES)
            ybuf_ref[sl, :] = jnp.where(mine, chunks[c], ybuf_ref[sl, :])

    @pl.when(b_on & (last_ref[j] == 1))
    def _():
        y_copy(tj).start()

    @pl.when(b_on & (j == n_items - 1))
    def _():
        for t0 in range(n_tiles - TILE_RING, n_tiles):
            y_copy(t0).wait()


def _experts(meta, xs, wg, wu, wd, n_items_max):
    tile_rows = EXPERT_TILE * ROW_TILES
    assert xs.shape[0] % tile_rows == 0 and xs.shape[0] // tile_rows >= TILE_RING
    hbm = pl.BlockSpec(memory_space=pl.ANY)
    n_slots = 2
    grid_spec = pltpu.PrefetchScalarGridSpec(
        num_scalar_prefetch=len(meta),
        grid=(n_items_max + 1,),
        in_specs=[hbm, hbm, hbm, hbm],
        out_specs=hbm,
        scratch_shapes=[pltpu.VMEM((TILE_RING * tile_rows, LANES), F32),
                        pltpu.VMEM((TILE_RING * tile_rows, LANES), F32),
                        pltpu.VMEM((n_slots, D_MODEL, D_EXPERT), F32),
                        pltpu.VMEM((n_slots, D_MODEL, D_EXPERT), F32),
                        pltpu.VMEM((n_slots, D_EXPERT, D_MODEL), F32),
                        pltpu.VMEM((D_MODEL, D_EXPERT), BF16),
                        pltpu.VMEM((D_MODEL, D_EXPERT), BF16),
                        pltpu.VMEM((n_slots, D_EXPERT, D_MODEL), BF16),
                        pltpu.VMEM((2, EXPERT_TILE, D_EXPERT), BF16),
                        pltpu.SemaphoreType.DMA((TILE_RING,)),
                        pltpu.SemaphoreType.DMA((TILE_RING,)),
                        pltpu.SemaphoreType.DMA((n_slots,))])
    return pl.pallas_call(
        _experts_kernel,
        out_shape=jax.ShapeDtypeStruct(xs.shape, F32),
        grid_spec=grid_spec,
        compiler_params=pltpu.CompilerParams(
            dimension_semantics=("arbitrary",), vmem_limit_bytes=VMEM_LIMIT),
        name="experts",
    )(*meta, xs, wg, wu, wd)


def _combine_kernel(pos_ref, pos_next_ref, ys_ref, h2_ref, x1_ref, gate_ref, g2_ref, swg_ref, swu_ref, swd_ref,
                    fg_ref, out_ref, buf_ref, sem):
    i = pl.program_id(0)
    tt = x1_ref.shape[0]
    slot_rows = tt * TOP_K
    src = _as_rows(ys_ref)
    dst = _as_rows(buf_ref)
    slot = i % 2
    spare = 1 - slot

    def row_copy(p_ref, t, k, s):
        return pltpu.make_async_copy(src.at[p_ref[0, 0, t * TOP_K + k]], dst.at[s * slot_rows + k * tt + t],
                                     sem.at[s])

    @pl.when(i == 0)
    def _():
        def start(t, c):
            for k in range(TOP_K):
                row_copy(pos_ref, t, k, slot).start(priority=k % 2)
            return c

        lax.fori_loop(0, tt, start, 0)

    def issue_next(t_lo, t_hi):
        for t in range(t_lo, t_hi):
            for k in range(TOP_K):
                row_copy(pos_next_ref, t, k, spare).start(priority=k % 2)

    issue_next(0, tt // 2)
    h2 = _load_rows(h2_ref, tt).astype(BF16)
    hm = (_silu(_dot(h2, swg_ref[...])) * _dot(h2, swu_ref[...])).astype(BF16)

    _wait_rows(dst, slot_rows, sem.at[slot])
    issue_next(tt // 2, tt)

    gate = gate_ref[...]
    ssq = jnp.zeros((tt, 1), F32)
    for c in range(ROW_TILES):
        cols = slice(c * LANES, (c + 1) * LANES)
        acc = _dot(hm, swd_ref[:, cols])
        for k in range(TOP_K):
            rows = pl.ds((slot * slot_rows + k * tt) * ROW_TILES + c, tt, stride=ROW_TILES)
            acc = acc + gate[:, k:k + 1] * buf_ref[rows, :]
        x2 = x1_ref[:, cols] + g2_ref[0, :, cols] * acc
        out_ref[:, cols] = x2
        ssq = ssq + jnp.sum(x2 * x2, axis=-1, keepdims=True)
    out_ref[...] = out_ref[...] * lax.rsqrt(ssq * (1.0 / D_MODEL) + EPS) * fg_ref[...]

    @pl.when(i == pl.num_programs(0) - 1)
    def _():
        _wait_rows(dst, slot_rows, sem.at[spare])


def _combine(pos_tiles, ys, h2_tm, x1, gate_tm, mod3, swg, swu, swd, fg, B, S, tt):
    T, D = x1.shape
    nS = S // tt
    const2 = lambda i: (0, 0)
    n_tiles = T // tt
    n_slots = 2
    return pl.pallas_call(
        _combine_kernel,
        out_shape=jax.ShapeDtypeStruct((T, D), F32),
        grid=(n_tiles,),
        in_specs=[pl.BlockSpec((1, 1, tt * TOP_K), lambda i: (i, 0, 0), memory_space=pltpu.SMEM),
                  pl.BlockSpec((1, 1, tt * TOP_K), lambda i: (jnp.minimum(i + 1, n_tiles - 1), 0, 0),
                               memory_space=pltpu.SMEM),
                  pl.BlockSpec(memory_space=pl.ANY),
                  pl.BlockSpec((tt * ROW_TILES, LANES), lambda i: (i, 0)),
                  pl.BlockSpec((tt, D), lambda i: (i, 0)),
                  pl.BlockSpec((tt, TOP_K), lambda i: (i, 0)),
                  pl.BlockSpec((1, 1, D), lambda i: (i // nS, 0, 5)),
                  pl.BlockSpec(swg.shape, const2),
                  pl.BlockSpec(swu.shape, const2),
                  pl.BlockSpec(swd.shape, const2),
                  pl.BlockSpec((1, D), const2)],
        out_specs=pl.BlockSpec((tt, D), lambda i: (i, 0)),
        scratch_shapes=[pltpu.VMEM((n_slots * tt * TOP_K * ROW_TILES, LANES), F32),
                        pltpu.SemaphoreType.DMA((n_slots,))],
        compiler_params=pltpu.CompilerParams(
            dimension_semantics=("arbitrary",), vmem_limit_bytes=VMEM_LIMIT),
        name="combine",
    )(pos_tiles, pos_tiles, ys, h2_tm, x1, gate_tm, mod3, swg, swu, swd, fg)


def _place_kernel(idx_ref, rank_ref, off_ref, pos_ref):
    tl = idx_ref.shape[1]
    rowid = lax.broadcasted_iota(jnp.int32, (N_EXPERTS, tl), 0)
    off = off_ref[...].astype(F32)
    for k in range(TOP_K):
        base = jnp.sum(jnp.where(rowid == idx_ref[k:k + 1, :], off, 0.0), axis=0, keepdims=True)
        pos_ref[k:k + 1, :] = base.astype(jnp.int32) + rank_ref[k:k + 1, :]


def _place(idx, rank, off, tl):
    T = idx.shape[1]
    tok = pl.BlockSpec((TOP_K, tl), lambda i: (0, i))
    return pl.pallas_call(
        _place_kernel,
        out_shape=jax.ShapeDtypeStruct((TOP_K, T), jnp.int32),
        grid=(T // tl,),
        in_specs=[tok, tok, pl.BlockSpec((N_EXPERTS, 1), lambda i: (0, 0))],
        out_specs=tok,
        name="place",
    )(idx, rank, off)


def _plan(counts, n_items_max):
    counts = counts[:, 0]
    off = jnp.cumsum(counts) - counts
    first_tile = off // EXPERT_TILE
    last_tile = (off + counts - 1) // EXPERT_TILE
    n_e = jnp.where(counts > 0, last_tile - first_tile + 1, 0)
    item_end = jnp.cumsum(n_e)
    item_start = item_end - n_e
    n_items = item_end[-1]
    ids = jnp.arange(n_items_max + 1, dtype=jnp.int32)
    ids_c = jnp.minimum(ids, n_items - 1)
    item_e = jnp.minimum(jnp.sum(item_end[None, :] <= ids_c[:, None], axis=1), N_EXPERTS - 1).astype(jnp.int32)
    item_tile = (first_tile[item_e] + ids_c - item_start[item_e]).astype(jnp.int32)
    prev_tile = jnp.concatenate([jnp.full((1,), -1, jnp.int32), item_tile[:-1]])
    prev_e = jnp.concatenate([jnp.full((1,), -1, jnp.int32), item_e[:-1]])
    next_tile = jnp.concatenate([item_tile[1:], jnp.full((1,), -1, jnp.int32)])
    item_first = (item_tile != prev_tile).astype(jnp.int32)
    item_last = ((item_tile != next_tile) | (ids == n_items - 1)).astype(jnp.int32)
    item_newe = (item_e != prev_e).astype(jnp.int32)
    item_slot = ((jnp.cumsum(item_newe) - 1) % 2).astype(jnp.int32)
    ids_e = jnp.arange(N_EXPERTS, dtype=jnp.int32)
    later = jnp.where((counts[None, :] > 0) & (ids_e[None, :] > ids_e[:, None]), ids_e[None, :], N_EXPERTS)
    next_e = jnp.min(later, axis=1)
    next_e = jnp.where(next_e < N_EXPERTS, next_e, -1).astype(jnp.int32)
    meta = (item_tile, item_e, item_first, item_last, item_newe, n_items.reshape(1).astype(jnp.int32),
            off.astype(jnp.int32), counts.astype(jnp.int32), item_slot, next_e[item_e])
    return off.astype(jnp.int32).reshape(N_EXPERTS, 1), meta


def kernel(x, c, ada_w, ada_b, norm1_g, w_in, hgrn_lb, hgrn_norm_g, pool_w, pool_b, pool_scale, w_up_a, w_up_b, w_out, norm2_g, router_w, router_bias, exp_w_gate, exp_w_up, exp_w_down, shared_w_gate, shared_w_up, shared_w_down, final_norm_g):
    B, S, D = x.shape
    T = B * S
    assert ada_w.shape[0] == 1, "single-layer trunk only: the final norm is fused into the combine step"
    lb_all = jnp.cumsum(jax.nn.softmax(hgrn_lb.astype(F32), axis=0), axis=0)
    c_pad = jnp.zeros((SUBLANES, D), F32).at[:B].set(c)
    n_items_max = T * TOP_K // EXPERT_TILE + N_EXPERTS - 1

    for l in range(1):
        mod = _ada(c_pad, ada_w[l], ada_b[l].reshape(1, -1))
        mod3 = mod[:B].reshape(B, 1, 6 * D)

        q, lf, k, v, sog, pm, sga, sgb = _inproj(
            x, mod3, norm1_g[l].reshape(1, D), w_in[l].astype(BF16), lb_all[l].reshape(1, HG_WIDTH),
            pool_w[l].astype(BF16), pool_b[l].reshape(len(POOL_WINDOWS), 1, POOL_GROUP),
            pool_scale[l].reshape(1, POOL_WIDTH), tm=256)
        oa = _hgrn(q, lf, k, v, sog, hgrn_norm_g[l].reshape(1, HG_DK), B, S, tb=512)

        rw_hi, rw_lo = _split_bf16(router_w[l].T)
        x1, h2_tm, logits_t = _mix(
            x, oa, pm, sga, sgb, mod3, norm2_g[l].reshape(1, D), w_up_a[l].astype(BF16),
            w_up_b[l].astype(BF16), w_out[l].astype(BF16), rw_hi, rw_lo, tm=256)

        idx, gate, rank, counts = _route(logits_t, router_bias[l].reshape(N_EXPERTS, 1), tl=256)
        off, meta = _plan(counts, n_items_max)
        pos_tm = _place(idx, rank, off, tl=512).T.reshape(-1)

        tt_s = 256
        xs = _scatter(pos_tm.reshape(T // tt_s, 1, tt_s * TOP_K), h2_tm, tt_s)
        ys = _experts(meta, xs, exp_w_gate[l], exp_w_up[l], exp_w_down[l], n_items_max)

        tt_c = 128
        fg = final_norm_g.reshape(1, D)
        x = _combine(pos_tm.reshape(T // tt_c, 1, tt_c * TOP_K), ys, h2_tm, x1, gate.T, mod3,
                     shared_w_gate[l].astype(BF16), shared_w_up[l].astype(BF16),
                     shared_w_down[l].astype(BF16), fg, B, S, tt_c).reshape(B, S, D)
    return x
```

```python
import functools

import jax
import jax.numpy as jnp
from jax import lax
from jax.experimental import pallas as pl
from jax.experimental.pallas import tpu as pltpu

F32 = jnp.float32
BF16 = jnp.bfloat16
HIGHEST = lax.Precision.HIGHEST

D_MODEL = 1024
HG_WIDTH = 512
HG_DK = 128
HG_HEADS = 4
HG_CHUNK = 64
HG_BLK = 16
HG_SUB = 8
POOL_WIDTH = 512
POOL_WINDOWS = (2, 4, 8, 16)
POOL_GROUP = 128
POOL_HALO = 16
N_EXPERTS = 256
TOP_K = 8
N_GROUPS = 8
TOPK_GROUPS = 4
GROUP_SIZE = N_EXPERTS // N_GROUPS
D_EXPERT = 256
ROUTED_SCALE = 2.5
EPS = 1e-6

LANES = 128
SUBLANES = 8
ROW_TILES = D_MODEL // LANES
EXPERT_TILE = 128
TILE_RING = 8
TILE_AHEAD = TILE_RING - 1
VMEM_LIMIT = 56 * 1024 * 1024

COL_Q, COL_F, COL_I, COL_OG, COL_U, COL_GA, COL_GB = 0, 512, 1024, 1536, 2048, 2560, 3584


def _sigmoid(x):
    return 1.0 / (1.0 + jnp.exp(-x))


def _silu(x):
    return x * _sigmoid(x)


def _dot(a, b):
    return jnp.dot(a, b, preferred_element_type=F32)


def _dot_nt(a, b):
    return lax.dot_general(a, b, (((1,), (1,)), ((), ())), preferred_element_type=F32)


def _dot_tn(a, b):
    return lax.dot_general(a, b, (((0,), (0,)), ((), ())), preferred_element_type=F32)


def _row_chunks(x):
    return [x[:, j * LANES:(j + 1) * LANES] for j in range(ROW_TILES)]


def _load_rows(ref, n, first_row=0):
    return jnp.concatenate(
        [ref[pl.ds(first_row * ROW_TILES + j, n, stride=ROW_TILES), :] for j in range(ROW_TILES)], axis=1)


def _ada_kernel(c_ref, w_ref, b_ref, o_ref):
    cond = _silu(c_ref[...])
    o_ref[...] = jnp.dot(cond, w_ref[...], precision=HIGHEST, preferred_element_type=F32) + b_ref[...]


def _ada(c_pad, ada_w, ada_b):
    n = ada_w.shape[1]
    tn = 1536
    return pl.pallas_call(
        _ada_kernel,
        out_shape=jax.ShapeDtypeStruct((SUBLANES, n), F32),
        grid=(n // tn,),
        in_specs=[pl.BlockSpec((SUBLANES, D_MODEL), lambda j: (0, 0)),
                  pl.BlockSpec((D_MODEL, tn), lambda j: (0, j)),
                  pl.BlockSpec((1, tn), lambda j: (0, j))],
        out_specs=pl.BlockSpec((SUBLANES, tn), lambda j: (0, j)),
        compiler_params=pltpu.CompilerParams(vmem_limit_bytes=VMEM_LIMIT),
        name="ada",
    )(c_pad, ada_w, ada_b)


def _inproj_kernel(x_ref, sh_ref, sc_ref, g_ref, w_ref, lb_ref, pw_ref, pb_ref, ps_ref,
                   q_ref, lf_ref, k_ref, v_ref, sog_ref, pm_ref, sga_ref, sgb_ref, halo_ref):
    s = pl.program_id(1)
    tm = x_ref.shape[1]
    x = x_ref[0]
    h = x * lax.rsqrt(jnp.mean(x * x, axis=-1, keepdims=True) + EPS) * g_ref[...]
    h = h * (1.0 + sc_ref[0]) + sh_ref[0]
    hb = h.astype(BF16)

    def proj(lo, n):
        return _dot(hb, w_ref[:, lo:lo + n])

    q = proj(COL_Q, HG_WIDTH)
    q_ref[...] = _silu(q) * (HG_DK ** -0.5)
    sig = _sigmoid(proj(COL_F, HG_WIDTH))
    lb = lb_ref[...]
    lf_ref[...] = jnp.log(lb + (1.0 - lb) * sig)
    k_ref[...] = (1.0 - lb) * (1.0 - sig)
    v_ref[...] = proj(COL_I, HG_WIDTH)
    sog_ref[...] = _silu(proj(COL_OG, HG_WIDTH)).astype(BF16)
    sga_ref[...] = _sigmoid(proj(COL_GA, D_MODEL)).astype(BF16)
    sgb_ref[...] = _sigmoid(proj(COL_GB, D_MODEL)).astype(BF16)

    u = proj(COL_U, POOL_WIDTH)
    @pl.when(s == 0)
    def _():
        halo_ref[...] = jnp.zeros_like(halo_ref)

    ext = jnp.concatenate([halo_ref[...], u], axis=0)
    halo_ref[...] = u[tm - POOL_HALO:, :]
    s2 = ext + pltpu.roll(ext, 1, 0)
    s4 = s2 + pltpu.roll(s2, 2, 0)
    s8 = s4 + pltpu.roll(s4, 4, 0)
    s16 = s8 + pltpu.roll(s8, 8, 0)
    pos1 = (s * tm + 1 + lax.broadcasted_iota(jnp.int32, (tm, 1), 0)).astype(F32)
    for g, (w, sw) in enumerate(zip(POOL_WINDOWS, (s2, s4, s8, s16))):
        cols = slice(g * POOL_GROUP, (g + 1) * POOL_GROUP)
        m = sw[POOL_HALO:, cols] / jnp.minimum(pos1, float(w)) - u[:, cols]
        y = _dot(m.astype(BF16), pw_ref[g]) + pb_ref[g]
        pm_ref[:, cols] = (y * ps_ref[:, cols]).astype(BF16)


def _inproj(x, mod3, norm_g, w_in_b, lb, pool_w_b, pool_b, pool_scale, tm):
    B, S, D = x.shape
    T = B * S
    nS = S // tm
    row = lambda b, s: (b * nS + s, 0)
    const2 = lambda b, s: (0, 0)
    const3 = lambda b, s: (0, 0, 0)
    half = lambda dt: jax.ShapeDtypeStruct((T, HG_WIDTH), dt)
    full = lambda dt: jax.ShapeDtypeStruct((T, D), dt)
    return pl.pallas_call(
        _inproj_kernel,
        out_shape=(half(F32), half(F32), half(F32), half(F32), half(BF16), half(BF16), full(BF16), full(BF16)),
        grid=(B, nS),
        in_specs=[pl.BlockSpec((1, tm, D), lambda b, s: (b, s, 0)),
                  pl.BlockSpec((1, 1, D), lambda b, s: (b, 0, 0)),
                  pl.BlockSpec((1, 1, D), lambda b, s: (b, 0, 1)),
                  pl.BlockSpec((1, D), const2),
                  pl.BlockSpec(w_in_b.shape, const2),
                  pl.BlockSpec((1, HG_WIDTH), const2),
                  pl.BlockSpec(pool_w_b.shape, const3),
                  pl.BlockSpec(pool_b.shape, const3),
                  pl.BlockSpec((1, POOL_WIDTH), const2)],
        out_specs=(pl.BlockSpec((tm, HG_WIDTH), row),) * 6 + (pl.BlockSpec((tm, D), row),) * 2,
        scratch_shapes=[pltpu.VMEM((POOL_HALO, POOL_WIDTH), F32)],
        compiler_params=pltpu.CompilerParams(
            dimension_semantics=("arbitrary", "arbitrary"), vmem_limit_bytes=VMEM_LIMIT),
        name="inproj",
    )(x, mod3, mod3, norm_g, w_in_b, lb, pool_w_b, pool_b, pool_scale)


def _hgrn_kernel(q_ref, lf_ref, k_ref, v_ref, sog_ref, gn_ref, o_ref, *st_refs):
    C = HG_CHUNK
    n_chunks = q_ref.shape[0] // C

    @pl.when(pl.program_id(1) == 0)
    def _():
        for st_ref in st_refs:
            st_ref[...] = jnp.zeros_like(st_ref)

    r_i = lax.broadcasted_iota(jnp.int32, (C, C), 0)
    c_i = lax.broadcasted_iota(jnp.int32, (C, C), 1)
    tril = (c_i <= r_i).astype(BF16)
    same_blk = (r_i // HG_BLK) == (c_i // HG_BLK)
    row = lax.broadcasted_iota(jnp.int32, (C, HG_DK), 0)
    row_in_sub = row % HG_SUB
    upper_half = (row % HG_BLK) >= HG_SUB
    row_blk = row // HG_BLK
    n_blk = C // HG_BLK

    def cumsum_rows(x):
        hi = x.astype(BF16)
        r1 = x - hi.astype(F32)
        mid = r1.astype(BF16)
        lo = (r1 - mid.astype(F32)).astype(BF16)
        return _dot(tril, hi) + _dot(tril, mid) + _dot(tril, lo)

    def block_rows(x, size, which):
        pieces = []
        for g in range(C // size):
            src = g * size + which
            pieces.append(jnp.zeros((size, x.shape[1]), F32) if src < 0
                          else jnp.broadcast_to(x[src:src + 1, :], (size, x.shape[1])))
        return jnp.concatenate(pieces, axis=0)

    def chunk(ci, carry):
        rs = pl.ds(pl.multiple_of(ci * C, C), C)
        b_all = cumsum_rows(lf_ref[rs, :])
        for h in range(HG_HEADS):
            cs = slice(h * HG_DK, (h + 1) * HG_DK)
            q = q_ref[rs, cs]
            k = k_ref[rs, cs]
            v = v_ref[rs, cs]
            b = b_all[:, cs]
            vb = v.astype(BF16)

            kt = k * jnp.exp(block_rows(b, HG_BLK, HG_BLK - 1) - b)
            q_parts, k_parts = [], []
            for j in range(n_blk - 1):
                bj = b[HG_BLK * j + HG_BLK - 1:HG_BLK * (j + 1), :]
                after = row >= HG_BLK * (j + 1)
                q_parts.append(q * jnp.exp(jnp.where(after, b - bj, -jnp.inf)))
                k_parts.append(jnp.where(row_blk == j, kt, 0.0))
            qcat = jnp.concatenate(q_parts, axis=1).astype(BF16)
            kcat = jnp.concatenate(k_parts, axis=1).astype(BF16)
            scores = _dot_nt(qcat, kcat)
            b_prev = block_rows(b, HG_SUB, -1)
            b_sub = block_rows(b, HG_SUB, HG_SUB - 1)
            qh = (q * jnp.exp(jnp.where(upper_half, b - b_prev, -jnp.inf))).astype(BF16)
            kh = jnp.where(upper_half, 0.0, k * jnp.exp(b_sub - b)).astype(BF16)
            scores = scores + jnp.where(same_blk, _dot_nt(qh, kh), 0.0)
            o = _dot(scores.astype(BF16), vb)

            o = o + jnp.sum(q * k, axis=-1, keepdims=True) * v
            for d in range(1, HG_SUB):
                kd = pltpu.roll(k, d, 0)
                bd = pltpu.roll(b, d, 0)
                vd = pltpu.roll(v, d, 0)
                e = jnp.exp(jnp.where(row_in_sub >= d, b - bd, -jnp.inf))
                o = o + jnp.sum(q * kd * e, axis=-1, keepdims=True) * vd

            st = st_refs[h][...]
            o = o + _dot_nt((q * jnp.exp(b)).astype(BF16), st.astype(BF16))
            b_end = b[C - 1:C, :]
            k_end = (k * jnp.exp(b_end - b)).astype(BF16)
            st_refs[h][...] = st * jnp.exp(b_end) + _dot_tn(vb, k_end)

            on = o * lax.rsqrt(jnp.mean(o * o, axis=-1, keepdims=True) + EPS) * gn_ref[...]
            o_ref[rs, cs] = (on * sog_ref[rs, cs].astype(F32)).astype(BF16)
        return carry

    lax.fori_loop(0, n_chunks, chunk, 0)


def _hgrn(q, lf, k, v, sog, gn, B, S, tb):
    T = B * S
    nS = S // tb
    row = lambda b, s: (b * nS + s, 0)
    blk = pl.BlockSpec((tb, HG_WIDTH), row)
    return pl.pallas_call(
        _hgrn_kernel,
        out_shape=jax.ShapeDtypeStruct((T, HG_WIDTH), BF16),
        grid=(B, nS),
        in_specs=[blk, blk, blk, blk, blk, pl.BlockSpec((1, HG_DK), lambda b, s: (0, 0))],
        out_specs=blk,
        scratch_shapes=[pltpu.VMEM((HG_DK, HG_DK), F32)] * HG_HEADS,
        compiler_params=pltpu.CompilerParams(
            dimension_semantics=("arbitrary", "arbitrary"), vmem_limit_bytes=VMEM_LIMIT),
        name="hgrn",
    )(q, lf, k, v, sog, gn)


def _split_kernel(w_ref, hi_ref, lo_ref):
    w = w_ref[...]
    hi = w.astype(BF16)
    hi_ref[...] = hi
    lo_ref[...] = (w - hi.astype(F32)).astype(BF16)


def _split_bf16(w):
    out = jax.ShapeDtypeStruct(w.shape, BF16)
    return pl.pallas_call(_split_kernel, out_shape=(out, out), name="split")(w)


def _mix_kernel(x_ref, oa_ref, pm_ref, sga_ref, sgb_ref, g1_ref, sh2_ref, sc2_ref, n2_ref,
                wua_ref, wub_ref, wo_ref, rw_hi_ref, rw_lo_ref, x1_ref, h2_ref, lg_ref):
    tm = x_ref.shape[1]
    ya = _dot(oa_ref[...], wua_ref[...])
    yb = _dot(pm_ref[...], wub_ref[...])
    mix = sga_ref[...].astype(F32) * ya + sgb_ref[...].astype(F32) * yb
    x1 = x_ref[0] + g1_ref[0] * _dot(mix.astype(BF16), wo_ref[...])
    x1_ref[...] = x1
    h2 = x1 * lax.rsqrt(jnp.mean(x1 * x1, axis=-1, keepdims=True) + EPS) * n2_ref[...]
    h2 = h2 * (1.0 + sc2_ref[0]) + sh2_ref[0]
    for j, chunk in enumerate(_row_chunks(h2)):
        h2_ref[pl.ds(j, tm, stride=ROW_TILES), :] = chunk
    h_hi = h2.astype(BF16)
    h_lo = (h2 - h_hi.astype(F32)).astype(BF16)
    rw_hi = rw_hi_ref[...]
    lg_ref[...] = _dot_nt(rw_hi, h_hi) + _dot_nt(rw_hi, h_lo) + _dot_nt(rw_lo_ref[...], h_hi)


def _mix(x, oa, pm, sga, sgb, mod3, norm2_g, wua, wub, wo, rw_hi, rw_lo, tm):
    B, S, D = x.shape
    T = B * S
    nS = S // tm
    row = lambda b, s: (b * nS + s, 0)
    const2 = lambda b, s: (0, 0)
    return pl.pallas_call(
        _mix_kernel,
        out_shape=(jax.ShapeDtypeStruct((T, D), F32),
                   jax.ShapeDtypeStruct((T * ROW_TILES, LANES), F32),
                   jax.ShapeDtypeStruct((N_EXPERTS, T), F32)),
        grid=(B, nS),
        in_specs=[pl.BlockSpec((1, tm, D), lambda b, s: (b, s, 0)),
                  pl.BlockSpec((tm, HG_WIDTH), row),
                  pl.BlockSpec((tm, POOL_WIDTH), row),
                  pl.BlockSpec((tm, D), row),
                  pl.BlockSpec((tm, D), row),
                  pl.BlockSpec((1, 1, D), lambda b, s: (b, 0, 2)),
                  pl.BlockSpec((1, 1, D), lambda b, s: (b, 0, 3)),
                  pl.BlockSpec((1, 1, D), lambda b, s: (b, 0, 4)),
                  pl.BlockSpec((1, D), const2),
                  pl.BlockSpec(wua.shape, const2),
                  pl.BlockSpec(wub.shape, const2),
                  pl.BlockSpec(wo.shape, const2),
                  pl.BlockSpec(rw_hi.shape, const2),
                  pl.BlockSpec(rw_lo.shape, const2)],
        out_specs=(pl.BlockSpec((tm, D), row),
                   pl.BlockSpec((tm * ROW_TILES, LANES), row),
                   pl.BlockSpec((N_EXPERTS, tm), lambda b, s: (0, b * nS + s))),
        compiler_params=pltpu.CompilerParams(
            dimension_semantics=("arbitrary", "arbitrary"), vmem_limit_bytes=VMEM_LIMIT),
        name="mix",
    )(x, oa, pm, sga, sgb, mod3, mod3, mod3, norm2_g, wua, wub, wo, rw_hi, rw_lo)


def _route_kernel(lg_ref, bias_ref, idx_ref, gate_ref, rank_ref, cnt_ref, carry_ref):
    tl = lg_ref.shape[1]
    neg = -jnp.inf

    @pl.when(pl.program_id(0) == 0)
    def _():
        carry_ref[...] = jnp.zeros_like(carry_ref)

    s = _sigmoid(lg_ref[...])
    biased = s + bias_ref[...]
    rowid = lax.broadcasted_iota(jnp.int32, (N_EXPERTS, tl), 0)

    def first_argmax(x, ids, sentinel):
        m = jnp.max(x, axis=0, keepdims=True)
        return jnp.min(jnp.where(x == m, ids, sentinel), axis=0, keepdims=True), m

    gscores = []
    for g in range(N_GROUPS):
        xg = biased[g * GROUP_SIZE:(g + 1) * GROUP_SIZE, :]
        rid = g * GROUP_SIZE + lax.broadcasted_iota(jnp.int32, (GROUP_SIZE, tl), 0)
        first, m1 = first_argmax(xg, rid, N_EXPERTS)
        m2 = jnp.max(jnp.where(rid == first, neg, xg), axis=0, keepdims=True)
        gscores.append(m1 + m2)
    blocks = []
    for g in range(N_GROUPS):
        beaten = jnp.zeros((1, tl), F32)
        for o in range(N_GROUPS):
            if o != g:
                wins = (gscores[o] >= gscores[g]) if o < g else (gscores[o] > gscores[g])
                beaten = beaten + jnp.where(wins, 1.0, 0.0)
        xg = biased[g * GROUP_SIZE:(g + 1) * GROUP_SIZE, :]
        blocks.append(jnp.where(beaten < float(TOPK_GROUPS), xg, neg))
    masked = jnp.concatenate(blocks, axis=0)

    idxs, gates = [], []
    chosen = jnp.zeros((N_EXPERTS, tl), F32)
    for _ in range(TOP_K):
        first, _m = first_argmax(masked, rowid, N_EXPERTS)
        sel = rowid == first
        gates.append(jnp.sum(jnp.where(sel, s, 0.0), axis=0, keepdims=True))
        idxs.append(first)
        chosen = jnp.where(sel, 1.0, chosen)
        masked = jnp.where(sel, neg, masked)
    gate_sum = functools.reduce(lambda a, b: a + b, gates)
    for k in range(TOP_K):
        gate_ref[k:k + 1, :] = gates[k] / gate_sum * ROUTED_SCALE
        idx_ref[k:k + 1, :] = idxs[k]

    lr = lax.broadcasted_iota(jnp.int32, (tl, tl), 0)
    lc = lax.broadcasted_iota(jnp.int32, (tl, tl), 1)
    prefix = (lr <= lc).astype(BF16)
    cnt_incl = _dot(chosen.astype(BF16), prefix)
    carry = carry_ref[...]
    rank_excl = cnt_incl - chosen + carry
    for k in range(TOP_K):
        rank_k = jnp.sum(jnp.where(rowid == idxs[k], rank_excl, 0.0), axis=0, keepdims=True)
        rank_ref[k:k + 1, :] = rank_k.astype(jnp.int32)
    carry = carry + jnp.sum(chosen, axis=1, keepdims=True)
    carry_ref[...] = carry
    cnt_ref[...] = carry.astype(jnp.int32)


def _route(logits_t, bias, tl):
    T = logits_t.shape[1]
    tok = lambda i: (0, i)
    return pl.pallas_call(
        _route_kernel,
        out_shape=(jax.ShapeDtypeStruct((TOP_K, T), jnp.int32),
                   jax.ShapeDtypeStruct((TOP_K, T), F32),
                   jax.ShapeDtypeStruct((TOP_K, T), jnp.int32),
                   jax.ShapeDtypeStruct((N_EXPERTS, 1), jnp.int32)),
        grid=(T // tl,),
        in_specs=[pl.BlockSpec((N_EXPERTS, tl), tok), pl.BlockSpec((N_EXPERTS, 1), lambda i: (0, 0))],
        out_specs=(pl.BlockSpec((TOP_K, tl), tok), pl.BlockSpec((TOP_K, tl), tok),
                   pl.BlockSpec((TOP_K, tl), tok), pl.BlockSpec((N_EXPERTS, 1), lambda i: (0, 0))),
        scratch_shapes=[pltpu.VMEM((N_EXPERTS, 1), F32)],
        compiler_params=pltpu.CompilerParams(
            dimension_semantics=("arbitrary",), vmem_limit_bytes=VMEM_LIMIT),
        name="route",
    )(logits_t, bias)


def _as_rows(ref):
    return ref.reshape(ref.shape[0] // ROW_TILES, ROW_TILES, LANES)


def _wait_rows(rows_ref, n, sem):
    pltpu.make_async_copy(rows_ref.at[pl.ds(0, n)], rows_ref.at[pl.ds(0, n)], sem).wait()


def _scatter_kernel(pos_ref, h2_ref, xs_ref, sem):
    src = _as_rows(h2_ref)
    dst = _as_rows(xs_ref)
    tt = src.shape[0]

    def start(t, c):
        for k in range(TOP_K):
            pltpu.make_async_copy(src.at[t], dst.at[pos_ref[k, t]], sem).start(priority=k % 2)
        return c

    lax.fori_loop(0, tt, start, 0)
    _wait_rows(dst, tt * TOP_K, sem)


def _scatter(pos, h2_tm, tt):
    n_rows = h2_tm.shape[0] // ROW_TILES * TOP_K
    return pl.pallas_call(
        _scatter_kernel,
        out_shape=jax.ShapeDtypeStruct((n_rows * ROW_TILES, LANES), F32),
        grid=(pos.shape[1] // tt,),
        in_specs=[pl.BlockSpec((TOP_K, tt), lambda i: (0, i), memory_space=pltpu.SMEM),
                  pl.BlockSpec((tt * ROW_TILES, LANES), lambda i: (i, 0))],
        out_specs=pl.BlockSpec(memory_space=pl.ANY),
        scratch_shapes=[pltpu.SemaphoreType.DMA],
        compiler_params=pltpu.CompilerParams(
            dimension_semantics=("arbitrary",), vmem_limit_bytes=VMEM_LIMIT),
        name="scatter",
    )(pos, h2_tm)


def _experts_kernel(tile_ref, exp_ref, first_ref, last_ref, newe_ref, nitems_ref, off_ref, cnt_ref, slot_ref,
                    nexte_ref, xs_hbm, wg_hbm, wu_hbm, wd_hbm, ys_hbm,
                    xbuf_ref, ybuf_ref, sg_ref, su_ref, sd_ref, wgb_ref, wub_ref, wdb_ref, hm_ref,
                    xsem, ysem, wsem):
    i = pl.program_id(0)
    tr = EXPERT_TILE
    tile_rows = tr * ROW_TILES
    n_tiles = xs_hbm.shape[0] // tile_rows
    n_items = nitems_ref[0]

    def ring(t):
        return pl.ds(pl.multiple_of((t % TILE_RING) * tile_rows, tile_rows), tile_rows)

    def hbm_tile(t):
        return pl.ds(pl.multiple_of(t * tile_rows, tile_rows), tile_rows)

    def x_copy(t):
        return pltpu.make_async_copy(xs_hbm.at[hbm_tile(t)], xbuf_ref.at[ring(t)], xsem.at[t % TILE_RING])

    def y_copy(t):
        return pltpu.make_async_copy(ybuf_ref.at[ring(t)], ys_hbm.at[hbm_tile(t)], ysem.at[t % TILE_RING])

    def weight_copies(e, slot):
        return (pltpu.make_async_copy(wg_hbm.at[e], sg_ref.at[slot], wsem.at[slot]),
                pltpu.make_async_copy(wu_hbm.at[e], su_ref.at[slot], wsem.at[slot]),
                pltpu.make_async_copy(wd_hbm.at[e], sd_ref.at[slot], wsem.at[slot]))

    a_on = i < n_items
    j = jnp.maximum(i - 1, 0)
    b_on = (i >= 1) & (i - 1 < n_items)
    e = exp_ref[i]
    t = tile_ref[i]

    @pl.when(i == 0)
    def _():
        hm_ref[...] = jnp.zeros_like(hm_ref)
        for t0 in range(TILE_AHEAD):
            x_copy(t0).start()

    @pl.when(a_on & (first_ref[i] == 1))
    def _():
        @pl.when(t + TILE_AHEAD < n_tiles)
        def _():
            x_copy(t + TILE_AHEAD).start()

        x_copy(t).wait()

    @pl.when(a_on & (newe_ref[i] == 1))
    def _():
        slot = slot_ref[i]
        nxt = nexte_ref[i]

        @pl.when(i == 0)
        def _():
            for c in weight_copies(e, slot):
                c.start()

        @pl.when(nxt >= 0)
        def _():
            for c in weight_copies(nxt, 1 - slot):
                c.start()

        for c in weight_copies(e, slot):
            c.wait()
        wgb_ref[...] = sg_ref[slot].astype(BF16)
        wub_ref[...] = su_ref[slot].astype(BF16)
        wdb_ref[slot] = sd_ref[slot].astype(BF16)

    tj = tile_ref[j]
    ej = exp_ref[j]
    chunks = _row_chunks(_dot(hm_ref[j % 2], wdb_ref[slot_ref[j]]))
    row = tj * tr + lax.broadcasted_iota(jnp.int32, (tr, 1), 0)
    lo = off_ref[ej]
    mine = (row >= lo) & (row < lo + cnt_ref[ej])

    x = _load_rows(xbuf_ref, tr, first_row=(t % TILE_RING) * tr).astype(BF16)
    hm_ref[i % 2] = (_silu(_dot(x, wgb_ref[...])) * _dot(x, wub_ref[...])).astype(BF16)

    out_row = (tj % TILE_RING) * tile_rows

    @pl.when(b_on & (first_ref[j] == 1))
    def _():
        @pl.when(tj >= TILE_RING)
        def _():
            y_copy(tj - TILE_RING).wait()

        for c in range(ROW_TILES):
            ybuf_ref[pl.ds(out_row + c, tr, stride=ROW_TILES), :] = chunks[c]

    @pl.when(b_on & (first_ref[j] == 0))
    def _():
        for c in range(ROW_TILES):
            sl = pl.ds(out_row + c, tr, stride=ROW_TILES)
            ybuf_ref[sl, :] = jnp.where(mine, chunks[c], ybuf_ref[sl, :])

    @pl.when(b_on & (last_ref[j] == 1))
    def _():
        y_copy(tj).start()

    @pl.when(b_on & (j == n_items - 1))
    def _():
        for t0 in range(n_tiles - TILE_RING, n_tiles):
            y_copy(t0).wait()


def _experts(meta, xs, wg, wu, wd, n_items_max):
    tile_rows = EXPERT_TILE * ROW_TILES
    assert xs.shape[0] % tile_rows == 0 and xs.shape[0] // tile_rows >= TILE_RING
    hbm = pl.BlockSpec(memory_space=pl.ANY)
    n_slots = 2
    grid_spec = pltpu.PrefetchScalarGridSpec(
        num_scalar_prefetch=len(meta),
        grid=(n_items_max + 1,),
        in_specs=[hbm, hbm, hbm, hbm],
        out_specs=hbm,
        scratch_shapes=[pltpu.VMEM((TILE_RING * tile_rows, LANES), F32),
                        pltpu.VMEM((TILE_RING * tile_rows, LANES), F32),
                        pltpu.VMEM((n_slots, D_MODEL, D_EXPERT), F32),
                        pltpu.VMEM((n_slots, D_MODEL, D_EXPERT), F32),
                        pltpu.VMEM((n_slots, D_EXPERT, D_MODEL), F32),
                        pltpu.VMEM((D_MODEL, D_EXPERT), BF16),
                        pltpu.VMEM((D_MODEL, D_EXPERT), BF16),
                        pltpu.VMEM((n_slots, D_EXPERT, D_MODEL), BF16),
                        pltpu.VMEM((2, EXPERT_TILE, D_EXPERT), BF16),
                        pltpu.SemaphoreType.DMA((TILE_RING,)),
                        pltpu.SemaphoreType.DMA((TILE_RING,)),
                        pltpu.SemaphoreType.DMA((n_slots,))])
    return pl.pallas_call(
        _experts_kernel,
        out_shape=jax.ShapeDtypeStruct(xs.shape, F32),
        grid_spec=grid_spec,
        compiler_params=pltpu.CompilerParams(
            dimension_semantics=("arbitrary",), vmem_limit_bytes=VMEM_LIMIT),
        name="experts",
    )(*meta, xs, wg, wu, wd)


def _combine_kernel(pos_ref, pos_next_ref, ys_ref, h2_ref, x1_ref, gate_ref, g2_ref, swg_ref, swu_ref, swd_ref,
                    fg_ref, out_ref, buf_a, buf_b, sem):
    i = pl.program_id(0)
    tt = x1_ref.shape[0] // 2
    src = _as_rows(ys_ref)

    def gather(p_ref, col0, buf, s):
        dst = _as_rows(buf)
        for t in range(tt):
            for k in range(TOP_K):
                pltpu.make_async_copy(src.at[p_ref[k, col0 + t]], dst.at[k * tt + t],
                                      sem.at[s]).start(priority=k % 2)

    @pl.when(i == 0)
    def _():
        dst = _as_rows(buf_a)

        def start(t, c):
            for k in range(TOP_K):
                pltpu.make_async_copy(src.at[pos_ref[k, t]], dst.at[k * tt + t], sem.at[0]).start(priority=k % 2)
            return c

        lax.fori_loop(0, tt, start, 0)

    def tile(row0, buf, s, prefetch):
        tok = pl.ds(row0, tt)
        h2 = _load_rows(h2_ref, tt, first_row=row0).astype(BF16)
        hm = (_silu(_dot(h2, swg_ref[...])) * _dot(h2, swu_ref[...])).astype(BF16)
        _wait_rows(_as_rows(buf), tt * TOP_K, sem.at[s])
        prefetch()
        gate = gate_ref[tok, :]
        ssq = jnp.zeros((tt, 1), F32)
        for c in range(ROW_TILES):
            cols = slice(c * LANES, (c + 1) * LANES)
            acc = _dot(hm, swd_ref[:, cols])
            for k in range(TOP_K):
                acc = acc + gate[:, k:k + 1] * buf[pl.ds(k * tt * ROW_TILES + c, tt, stride=ROW_TILES), :]
            x2 = x1_ref[tok, cols] + g2_ref[0, :, cols] * acc
            out_ref[tok, cols] = x2
            ssq = ssq + jnp.sum(x2 * x2, axis=-1, keepdims=True)
        out_ref[tok, :] = out_ref[tok, :] * lax.rsqrt(ssq * (1.0 / D_MODEL) + EPS) * fg_ref[...]

    tile(0, buf_a, 0, lambda: gather(pos_ref, tt, buf_b, 1))
    tile(tt, buf_b, 1, lambda: gather(pos_next_ref, 0, buf_a, 0))

    @pl.when(i == pl.num_programs(0) - 1)
    def _():
        _wait_rows(_as_rows(buf_a), tt * TOP_K, sem.at[0])


def _combine(pos, ys, h2_tm, x1, gate_tm, mod3, swg, swu, swd, fg, B, S, tt):
    T, D = x1.shape
    gather_rows = tt * TOP_K * ROW_TILES
    tt = 2 * tt
    nS = S // tt
    const2 = lambda i: (0, 0)
    n_tiles = T // tt
    return pl.pallas_call(
        _combine_kernel,
        out_shape=jax.ShapeDtypeStruct((T, D), F32),
        grid=(n_tiles,),
        in_specs=[pl.BlockSpec((TOP_K, tt), lambda i: (0, i), memory_space=pltpu.SMEM),
                  pl.BlockSpec((TOP_K, tt), lambda i: (0, jnp.minimum(i + 1, n_tiles - 1)),
                               memory_space=pltpu.SMEM),
                  pl.BlockSpec(memory_space=pl.ANY),
                  pl.BlockSpec((tt * ROW_TILES, LANES), lambda i: (i, 0)),
                  pl.BlockSpec((tt, D), lambda i: (i, 0)),
                  pl.BlockSpec((tt, TOP_K), lambda i: (i, 0)),
                  pl.BlockSpec((1, 1, D), lambda i: (i // nS, 0, 5)),
                  pl.BlockSpec(swg.shape, const2),
                  pl.BlockSpec(swu.shape, const2),
                  pl.BlockSpec(swd.shape, const2),
                  pl.BlockSpec((1, D), const2)],
        out_specs=pl.BlockSpec((tt, D), lambda i: (i, 0)),
        scratch_shapes=[pltpu.VMEM((gather_rows, LANES), F32),
                        pltpu.VMEM((gather_rows, LANES), F32),
                        pltpu.SemaphoreType.DMA((2,))],
        compiler_params=pltpu.CompilerParams(
            dimension_semantics=("arbitrary",), vmem_limit_bytes=VMEM_LIMIT),
        name="combine",
    )(pos, pos, ys, h2_tm, x1, gate_tm, mod3, swg, swu, swd, fg)


def _place_kernel(idx_ref, rank_ref, off_ref, pos_ref):
    tl = idx_ref.shape[1]
    rowid = lax.broadcasted_iota(jnp.int32, (N_EXPERTS, tl), 0)
    off = off_ref[...].astype(F32)
    for k in range(TOP_K):
        base = jnp.sum(jnp.where(rowid == idx_ref[k:k + 1, :], off, 0.0), axis=0, keepdims=True)
        pos_ref[k:k + 1, :] = base.astype(jnp.int32) + rank_ref[k:k + 1, :]


def _place(idx, rank, off, tl):
    T = idx.shape[1]
    tok = pl.BlockSpec((TOP_K, tl), lambda i: (0, i))
    return pl.pallas_call(
        _place_kernel,
        out_shape=jax.ShapeDtypeStruct((TOP_K, T), jnp.int32),
        grid=(T // tl,),
        in_specs=[tok, tok, pl.BlockSpec((N_EXPERTS, 1), lambda i: (0, 0))],
        out_specs=tok,
        name="place",
    )(idx, rank, off)


def _plan(counts, n_items_max):
    counts = counts[:, 0]
    off = jnp.cumsum(counts) - counts
    first_tile = off // EXPERT_TILE
    last_tile = (off + counts - 1) // EXPERT_TILE
    n_e = jnp.where(counts > 0, last_tile - first_tile + 1, 0)
    item_end = jnp.cumsum(n_e)
    item_start = item_end - n_e
    n_items = item_end[-1]
    ids = jnp.arange(n_items_max + 1, dtype=jnp.int32)
    ids_c = jnp.minimum(ids, n_items - 1)
    item_e = jnp.minimum(jnp.sum(item_end[None, :] <= ids_c[:, None], axis=1), N_EXPERTS - 1).astype(jnp.int32)
    ids_e = jnp.arange(N_EXPERTS, dtype=jnp.int32)
    of_item = item_e[:, None] == ids_e[None, :]

    def per_item(table):
        return jnp.sum(jnp.where(of_item, table[None, :], 0), axis=1).astype(jnp.int32)

    item_tile = per_item(first_tile) + ids_c - per_item(item_start)
    prev_tile = jnp.concatenate([jnp.full((1,), -1, jnp.int32), item_tile[:-1]])
    prev_e = jnp.concatenate([jnp.full((1,), -1, jnp.int32), item_e[:-1]])
    next_tile = jnp.concatenate([item_tile[1:], jnp.full((1,), -1, jnp.int32)])
    item_first = (item_tile != prev_tile).astype(jnp.int32)
    item_last = ((item_tile != next_tile) | (ids == n_items - 1)).astype(jnp.int32)
    item_newe = (item_e != prev_e).astype(jnp.int32)
    item_slot = ((jnp.cumsum(item_newe) - 1) % 2).astype(jnp.int32)
    later = jnp.where((counts[None, :] > 0) & (ids_e[None, :] > ids_e[:, None]), ids_e[None, :], N_EXPERTS)
    next_e = jnp.min(later, axis=1)
    next_e = jnp.where(next_e < N_EXPERTS, next_e, -1).astype(jnp.int32)
    meta = (item_tile, item_e, item_first, item_last, item_newe, n_items.reshape(1).astype(jnp.int32),
            off.astype(jnp.int32), counts.astype(jnp.int32), item_slot, per_item(next_e))
    return off.astype(jnp.int32).reshape(N_EXPERTS, 1), meta


def kernel(x, c, ada_w, ada_b, norm1_g, w_in, hgrn_lb, hgrn_norm_g, pool_w, pool_b, pool_scale, w_up_a, w_up_b, w_out, norm2_g, router_w, router_bias, exp_w_gate, exp_w_up, exp_w_down, shared_w_gate, shared_w_up, shared_w_down, final_norm_g):
    B, S, D = x.shape
    T = B * S
    assert ada_w.shape[0] == 1, "single-layer trunk only: the final norm is fused into the combine step"
    lb_all = jnp.cumsum(jax.nn.softmax(hgrn_lb.astype(F32), axis=0), axis=0)
    c_pad = jnp.zeros((SUBLANES, D), F32).at[:B].set(c)
    n_items_max = T * TOP_K // EXPERT_TILE + N_EXPERTS - 1

    for l in range(1):
        mod = _ada(c_pad, ada_w[l], ada_b[l].reshape(1, -1))
        mod3 = mod[:B].reshape(B, 1, 6 * D)

        q, lf, k, v, sog, pm, sga, sgb = _inproj(
            x, mod3, norm1_g[l].reshape(1, D), w_in[l].astype(BF16), lb_all[l].reshape(1, HG_WIDTH),
            pool_w[l].astype(BF16), pool_b[l].reshape(len(POOL_WINDOWS), 1, POOL_GROUP),
            pool_scale[l].reshape(1, POOL_WIDTH), tm=256)
        oa = _hgrn(q, lf, k, v, sog, hgrn_norm_g[l].reshape(1, HG_DK), B, S, tb=512)

        rw_hi, rw_lo = _split_bf16(router_w[l].T)
        x1, h2_tm, logits_t = _mix(
            x, oa, pm, sga, sgb, mod3, norm2_g[l].reshape(1, D), w_up_a[l].astype(BF16),
            w_up_b[l].astype(BF16), w_out[l].astype(BF16), rw_hi, rw_lo, tm=256)

        idx, gate, rank, counts = _route(logits_t, router_bias[l].reshape(N_EXPERTS, 1), tl=256)
        off, meta = _plan(counts, n_items_max)
        pos = _place(idx, rank, off, tl=512)

        xs = _scatter(pos, h2_tm, tt=256)
        ys = _experts(meta, xs, exp_w_gate[l], exp_w_up[l], exp_w_down[l], n_items_max)

        tt_c = 128
        fg = final_norm_g.reshape(1, D)
        x = _combine(pos, ys, h2_tm, x1, gate.T, mod3,
                     shared_w_gate[l].astype(BF16), shared_w_up[l].astype(BF16),
                     shared_w_down[l].astype(BF16), fg, B, S, tt_c).reshape(B, S, D)
    return x
```

```python
import functools

import jax
import jax.numpy as jnp
from jax import lax
from jax.experimental import pallas as pl
from jax.experimental.pallas import tpu as pltpu

F32 = jnp.float32
BF16 = jnp.bfloat16
HIGHEST = lax.Precision.HIGHEST

D_MODEL = 1024
HG_WIDTH = 512
HG_DK = 128
HG_HEADS = 4
HG_CHUNK = 64
HG_BLK = 16
HG_SUB = 8
POOL_WIDTH = 512
POOL_WINDOWS = (2, 4, 8, 16)
POOL_GROUP = 128
POOL_HALO = 16
N_EXPERTS = 256
TOP_K = 8
N_GROUPS = 8
TOPK_GROUPS = 4
GROUP_SIZE = N_EXPERTS // N_GROUPS
D_EXPERT = 256
ROUTED_SCALE = 2.5
EPS = 1e-6

LANES = 128
SUBLANES = 8
ROW_TILES = D_MODEL // LANES
EXPERT_TILE = 128
TILE_RING = 8
TILE_AHEAD = TILE_RING - 1
VMEM_LIMIT = 56 * 1024 * 1024

COL_Q, COL_F, COL_I, COL_OG, COL_U, COL_GA, COL_GB = 0, 512, 1024, 1536, 2048, 2560, 3584


def _sigmoid(x):
    return 1.0 / (1.0 + jnp.exp(-x))


def _silu(x):
    return x * _sigmoid(x)


def _dot(a, b):
    return jnp.dot(a, b, preferred_element_type=F32)


def _dot_nt(a, b):
    return lax.dot_general(a, b, (((1,), (1,)), ((), ())), preferred_element_type=F32)


def _dot_tn(a, b):
    return lax.dot_general(a, b, (((0,), (0,)), ((), ())), preferred_element_type=F32)


def _row_chunks(x):
    return [x[:, j * LANES:(j + 1) * LANES] for j in range(ROW_TILES)]


def _load_rows(ref, n, first_row=0):
    return jnp.concatenate(
        [ref[pl.ds(first_row * ROW_TILES + j, n, stride=ROW_TILES), :] for j in range(ROW_TILES)], axis=1)


def _ada_kernel(c_ref, w_ref, b_ref, o_ref):
    cond = _silu(c_ref[...])
    o_ref[...] = jnp.dot(cond, w_ref[...], precision=HIGHEST, preferred_element_type=F32) + b_ref[...]


def _ada(c_pad, ada_w, ada_b):
    n = ada_w.shape[1]
    tn = 1536
    return pl.pallas_call(
        _ada_kernel,
        out_shape=jax.ShapeDtypeStruct((SUBLANES, n), F32),
        grid=(n // tn,),
        in_specs=[pl.BlockSpec((SUBLANES, D_MODEL), lambda j: (0, 0)),
                  pl.BlockSpec((D_MODEL, tn), lambda j: (0, j)),
                  pl.BlockSpec((1, tn), lambda j: (0, j))],
        out_specs=pl.BlockSpec((SUBLANES, tn), lambda j: (0, j)),
        compiler_params=pltpu.CompilerParams(vmem_limit_bytes=VMEM_LIMIT),
        name="ada",
    )(c_pad, ada_w, ada_b)


def _inproj_kernel(x_ref, sh_ref, sc_ref, g_ref, w_ref, lb_ref, pw_ref, pb_ref, ps_ref,
                   q_ref, lf_ref, k_ref, v_ref, sog_ref, pm_ref, sga_ref, sgb_ref, halo_ref):
    s = pl.program_id(1)
    tm = x_ref.shape[1]
    x = x_ref[0]
    h = x * lax.rsqrt(jnp.mean(x * x, axis=-1, keepdims=True) + EPS) * g_ref[...]
    h = h * (1.0 + sc_ref[0]) + sh_ref[0]
    hb = h.astype(BF16)

    def proj(lo, n):
        return _dot(hb, w_ref[:, lo:lo + n])

    q = proj(COL_Q, HG_WIDTH)
    q_ref[...] = _silu(q) * (HG_DK ** -0.5)
    sig = _sigmoid(proj(COL_F, HG_WIDTH))
    lb = lb_ref[...]
    lf_ref[...] = jnp.log(lb + (1.0 - lb) * sig)
    k_ref[...] = (1.0 - lb) * (1.0 - sig)
    v_ref[...] = proj(COL_I, HG_WIDTH)
    sog_ref[...] = _silu(proj(COL_OG, HG_WIDTH)).astype(BF16)
    sga_ref[...] = _sigmoid(proj(COL_GA, D_MODEL)).astype(BF16)
    sgb_ref[...] = _sigmoid(proj(COL_GB, D_MODEL)).astype(BF16)

    u = proj(COL_U, POOL_WIDTH)
    @pl.when(s == 0)
    def _():
        halo_ref[...] = jnp.zeros_like(halo_ref)

    ext = jnp.concatenate([halo_ref[...], u], axis=0)
    halo_ref[...] = u[tm - POOL_HALO:, :]
    s2 = ext + pltpu.roll(ext, 1, 0)
    s4 = s2 + pltpu.roll(s2, 2, 0)
    s8 = s4 + pltpu.roll(s4, 4, 0)
    s16 = s8 + pltpu.roll(s8, 8, 0)
    pos1 = (s * tm + 1 + lax.broadcasted_iota(jnp.int32, (tm, 1), 0)).astype(F32)
    for g, (w, sw) in enumerate(zip(POOL_WINDOWS, (s2, s4, s8, s16))):
        cols = slice(g * POOL_GROUP, (g + 1) * POOL_GROUP)
        m = sw[POOL_HALO:, cols] / jnp.minimum(pos1, float(w)) - u[:, cols]
        y = _dot(m.astype(BF16), pw_ref[g]) + pb_ref[g]
        pm_ref[:, cols] = (y * ps_ref[:, cols]).astype(BF16)


def _inproj(x, mod3, norm_g, w_in_b, lb, pool_w_b, pool_b, pool_scale, tm):
    B, S, D = x.shape
    T = B * S
    nS = S // tm
    row = lambda b, s: (b * nS + s, 0)
    const2 = lambda b, s: (0, 0)
    const3 = lambda b, s: (0, 0, 0)
    half = lambda dt: jax.ShapeDtypeStruct((T, HG_WIDTH), dt)
    full = lambda dt: jax.ShapeDtypeStruct((T, D), dt)
    return pl.pallas_call(
        _inproj_kernel,
        out_shape=(half(F32), half(F32), half(F32), half(F32), half(BF16), half(BF16), full(BF16), full(BF16)),
        grid=(B, nS),
        in_specs=[pl.BlockSpec((1, tm, D), lambda b, s: (b, s, 0)),
                  pl.BlockSpec((1, 1, D), lambda b, s: (b, 0, 0)),
                  pl.BlockSpec((1, 1, D), lambda b, s: (b, 0, 1)),
                  pl.BlockSpec((1, D), const2),
                  pl.BlockSpec(w_in_b.shape, const2),
                  pl.BlockSpec((1, HG_WIDTH), const2),
                  pl.BlockSpec(pool_w_b.shape, const3),
                  pl.BlockSpec(pool_b.shape, const3),
                  pl.BlockSpec((1, POOL_WIDTH), const2)],
        out_specs=(pl.BlockSpec((tm, HG_WIDTH), row),) * 6 + (pl.BlockSpec((tm, D), row),) * 2,
        scratch_shapes=[pltpu.VMEM((POOL_HALO, POOL_WIDTH), F32)],
        compiler_params=pltpu.CompilerParams(
            dimension_semantics=("arbitrary", "arbitrary"), vmem_limit_bytes=VMEM_LIMIT),
        name="inproj",
    )(x, mod3, mod3, norm_g, w_in_b, lb, pool_w_b, pool_b, pool_scale)


def _hgrn_kernel(q_ref, lf_ref, k_ref, v_ref, sog_ref, gn_ref, o_ref, *st_refs):
    C = HG_CHUNK
    n_chunks = q_ref.shape[0] // C

    @pl.when(pl.program_id(1) == 0)
    def _():
        for st_ref in st_refs:
            st_ref[...] = jnp.zeros_like(st_ref)

    r_i = lax.broadcasted_iota(jnp.int32, (C, C), 0)
    c_i = lax.broadcasted_iota(jnp.int32, (C, C), 1)
    tril = (c_i <= r_i).astype(BF16)
    same_blk = (r_i // HG_BLK) == (c_i // HG_BLK)
    row = lax.broadcasted_iota(jnp.int32, (C, HG_DK), 0)
    row_in_sub = row % HG_SUB
    upper_half = (row % HG_BLK) >= HG_SUB
    row_blk = row // HG_BLK
    n_blk = C // HG_BLK

    def cumsum_rows(x):
        hi = x.astype(BF16)
        r1 = x - hi.astype(F32)
        mid = r1.astype(BF16)
        lo = (r1 - mid.astype(F32)).astype(BF16)
        return _dot(tril, hi) + _dot(tril, mid) + _dot(tril, lo)

    def block_rows(x, size, which):
        pieces = []
        for g in range(C // size):
            src = g * size + which
            pieces.append(jnp.zeros((size, x.shape[1]), F32) if src < 0
                          else jnp.broadcast_to(x[src:src + 1, :], (size, x.shape[1])))
        return jnp.concatenate(pieces, axis=0)

    def chunk(ci, carry):
        rs = pl.ds(pl.multiple_of(ci * C, C), C)
        b_all = cumsum_rows(lf_ref[rs, :])
        for h in range(HG_HEADS):
            cs = slice(h * HG_DK, (h + 1) * HG_DK)
            q = q_ref[rs, cs]
            k = k_ref[rs, cs]
            v = v_ref[rs, cs]
            b = b_all[:, cs]
            vb = v.astype(BF16)

            kt = k * jnp.exp(block_rows(b, HG_BLK, HG_BLK - 1) - b)
            q_parts, k_parts = [], []
            for j in range(n_blk - 1):
                bj = b[HG_BLK * j + HG_BLK - 1:HG_BLK * (j + 1), :]
                after = row >= HG_BLK * (j + 1)
                q_parts.append(q * jnp.exp(jnp.where(after, b - bj, -jnp.inf)))
                k_parts.append(jnp.where(row_blk == j, kt, 0.0))
            qcat = jnp.concatenate(q_parts, axis=1).astype(BF16)
            kcat = jnp.concatenate(k_parts, axis=1).astype(BF16)
            scores = _dot_nt(qcat, kcat)
            b_prev = block_rows(b, HG_SUB, -1)
            b_sub = block_rows(b, HG_SUB, HG_SUB - 1)
            qh = (q * jnp.exp(jnp.where(upper_half, b - b_prev, -jnp.inf))).astype(BF16)
            kh = jnp.where(upper_half, 0.0, k * jnp.exp(b_sub - b)).astype(BF16)
            scores = scores + jnp.where(same_blk, _dot_nt(qh, kh), 0.0)
            o = _dot(scores.astype(BF16), vb)

            o = o + jnp.sum(q * k, axis=-1, keepdims=True) * v
            for d in range(1, HG_SUB):
                kd = pltpu.roll(k, d, 0)
                bd = pltpu.roll(b, d, 0)
                vd = pltpu.roll(v, d, 0)
                e = jnp.exp(jnp.where(row_in_sub >= d, b - bd, -jnp.inf))
                o = o + jnp.sum(q * kd * e, axis=-1, keepdims=True) * vd

            st = st_refs[h][...]
            o = o + _dot_nt((q * jnp.exp(b)).astype(BF16), st.astype(BF16))
            b_end = b[C - 1:C, :]
            k_end = (k * jnp.exp(b_end - b)).astype(BF16)
            st_refs[h][...] = st * jnp.exp(b_end) + _dot_tn(vb, k_end)

            on = o * lax.rsqrt(jnp.mean(o * o, axis=-1, keepdims=True) + EPS) * gn_ref[...]
            o_ref[rs, cs] = (on * sog_ref[rs, cs].astype(F32)).astype(BF16)
        return carry

    lax.fori_loop(0, n_chunks, chunk, 0)


def _hgrn(q, lf, k, v, sog, gn, B, S, tb):
    T = B * S
    nS = S // tb
    row = lambda b, s: (b * nS + s, 0)
    blk = pl.BlockSpec((tb, HG_WIDTH), row)
    return pl.pallas_call(
        _hgrn_kernel,
        out_shape=jax.ShapeDtypeStruct((T, HG_WIDTH), BF16),
        grid=(B, nS),
        in_specs=[blk, blk, blk, blk, blk, pl.BlockSpec((1, HG_DK), lambda b, s: (0, 0))],
        out_specs=blk,
        scratch_shapes=[pltpu.VMEM((HG_DK, HG_DK), F32)] * HG_HEADS,
        compiler_params=pltpu.CompilerParams(
            dimension_semantics=("arbitrary", "arbitrary"), vmem_limit_bytes=VMEM_LIMIT),
        name="hgrn",
    )(q, lf, k, v, sog, gn)


def _split_kernel(w_ref, hi_ref, lo_ref):
    w = w_ref[...]
    hi = w.astype(BF16)
    hi_ref[...] = hi
    lo_ref[...] = (w - hi.astype(F32)).astype(BF16)


def _split_bf16(w):
    out = jax.ShapeDtypeStruct(w.shape, BF16)
    return pl.pallas_call(_split_kernel, out_shape=(out, out), name="split")(w)


def _mix_kernel(x_ref, oa_ref, pm_ref, sga_ref, sgb_ref, g1_ref, sh2_ref, sc2_ref, n2_ref,
                wua_ref, wub_ref, wo_ref, rw_hi_ref, rw_lo_ref, x1_ref, h2_ref, lg_ref):
    tm = x_ref.shape[1]
    ya = _dot(oa_ref[...], wua_ref[...])
    yb = _dot(pm_ref[...], wub_ref[...])
    mix = sga_ref[...].astype(F32) * ya + sgb_ref[...].astype(F32) * yb
    x1 = x_ref[0] + g1_ref[0] * _dot(mix.astype(BF16), wo_ref[...])
    x1_ref[...] = x1
    h2 = x1 * lax.rsqrt(jnp.mean(x1 * x1, axis=-1, keepdims=True) + EPS) * n2_ref[...]
    h2 = h2 * (1.0 + sc2_ref[0]) + sh2_ref[0]
    for j, chunk in enumerate(_row_chunks(h2)):
        h2_ref[pl.ds(j, tm, stride=ROW_TILES), :] = chunk
    h_hi = h2.astype(BF16)
    h_lo = (h2 - h_hi.astype(F32)).astype(BF16)
    rw_hi = rw_hi_ref[...]
    lg_ref[...] = _dot_nt(rw_hi, h_hi) + _dot_nt(rw_hi, h_lo) + _dot_nt(rw_lo_ref[...], h_hi)


def _mix(x, oa, pm, sga, sgb, mod3, norm2_g, wua, wub, wo, rw_hi, rw_lo, tm):
    B, S, D = x.shape
    T = B * S
    nS = S // tm
    row = lambda b, s: (b * nS + s, 0)
    const2 = lambda b, s: (0, 0)
    return pl.pallas_call(
        _mix_kernel,
        out_shape=(jax.ShapeDtypeStruct((T, D), F32),
                   jax.ShapeDtypeStruct((T * ROW_TILES, LANES), F32),
                   jax.ShapeDtypeStruct((N_EXPERTS, T), F32)),
        grid=(B, nS),
        in_specs=[pl.BlockSpec((1, tm, D), lambda b, s: (b, s, 0)),
                  pl.BlockSpec((tm, HG_WIDTH), row),
                  pl.BlockSpec((tm, POOL_WIDTH), row),
                  pl.BlockSpec((tm, D), row),
                  pl.BlockSpec((tm, D), row),
                  pl.BlockSpec((1, 1, D), lambda b, s: (b, 0, 2)),
                  pl.BlockSpec((1, 1, D), lambda b, s: (b, 0, 3)),
                  pl.BlockSpec((1, 1, D), lambda b, s: (b, 0, 4)),
                  pl.BlockSpec((1, D), const2),
                  pl.BlockSpec(wua.shape, const2),
                  pl.BlockSpec(wub.shape, const2),
                  pl.BlockSpec(wo.shape, const2),
                  pl.BlockSpec(rw_hi.shape, const2),
                  pl.BlockSpec(rw_lo.shape, const2)],
        out_specs=(pl.BlockSpec((tm, D), row),
                   pl.BlockSpec((tm * ROW_TILES, LANES), row),
                   pl.BlockSpec((N_EXPERTS, tm), lambda b, s: (0, b * nS + s))),
        compiler_params=pltpu.CompilerParams(
            dimension_semantics=("arbitrary", "arbitrary"), vmem_limit_bytes=VMEM_LIMIT),
        name="mix",
    )(x, oa, pm, sga, sgb, mod3, mod3, mod3, norm2_g, wua, wub, wo, rw_hi, rw_lo)


def _route_kernel(lg_ref, bias_ref, idx_ref, gate_ref, rank_ref, cnt_ref, carry_ref):
    tl = lg_ref.shape[1]
    neg = -jnp.inf

    @pl.when(pl.program_id(0) == 0)
    def _():
        carry_ref[...] = jnp.zeros_like(carry_ref)

    s = _sigmoid(lg_ref[...])
    biased = s + bias_ref[...]
    rowid = lax.broadcasted_iota(jnp.int32, (N_EXPERTS, tl), 0)

    def first_argmax(x, ids, sentinel):
        m = jnp.max(x, axis=0, keepdims=True)
        return jnp.min(jnp.where(x == m, ids, sentinel), axis=0, keepdims=True), m

    gscores = []
    for g in range(N_GROUPS):
        xg = biased[g * GROUP_SIZE:(g + 1) * GROUP_SIZE, :]
        rid = g * GROUP_SIZE + lax.broadcasted_iota(jnp.int32, (GROUP_SIZE, tl), 0)
        first, m1 = first_argmax(xg, rid, N_EXPERTS)
        m2 = jnp.max(jnp.where(rid == first, neg, xg), axis=0, keepdims=True)
        gscores.append(m1 + m2)
    blocks = []
    for g in range(N_GROUPS):
        beaten = jnp.zeros((1, tl), F32)
        for o in range(N_GROUPS):
            if o != g:
                wins = (gscores[o] >= gscores[g]) if o < g else (gscores[o] > gscores[g])
                beaten = beaten + jnp.where(wins, 1.0, 0.0)
        xg = biased[g * GROUP_SIZE:(g + 1) * GROUP_SIZE, :]
        blocks.append(jnp.where(beaten < float(TOPK_GROUPS), xg, neg))
    masked = jnp.concatenate(blocks, axis=0)

    idxs, gates = [], []
    chosen = jnp.zeros((N_EXPERTS, tl), F32)
    for _ in range(TOP_K):
        first, _m = first_argmax(masked, rowid, N_EXPERTS)
        sel = rowid == first
        gates.append(jnp.sum(jnp.where(sel, s, 0.0), axis=0, keepdims=True))
        idxs.append(first)
        chosen = jnp.where(sel, 1.0, chosen)
        masked = jnp.where(sel, neg, masked)
    gate_sum = functools.reduce(lambda a, b: a + b, gates)
    for k in range(TOP_K):
        gate_ref[k:k + 1, :] = gates[k] / gate_sum * ROUTED_SCALE
        idx_ref[k:k + 1, :] = idxs[k]

    lr = lax.broadcasted_iota(jnp.int32, (tl, tl), 0)
    lc = lax.broadcasted_iota(jnp.int32, (tl, tl), 1)
    prefix = (lr <= lc).astype(BF16)
    cnt_incl = _dot(chosen.astype(BF16), prefix)
    carry = carry_ref[...]
    rank_excl = cnt_incl - chosen + carry
    for k in range(TOP_K):
        rank_k = jnp.sum(jnp.where(rowid == idxs[k], rank_excl, 0.0), axis=0, keepdims=True)
        rank_ref[k:k + 1, :] = rank_k.astype(jnp.int32)
    carry = carry + jnp.sum(chosen, axis=1, keepdims=True)
    carry_ref[...] = carry
    cnt_ref[...] = carry.astype(jnp.int32)


def _route(logits_t, bias, tl):
    T = logits_t.shape[1]
    tok = lambda i: (0, i)
    return pl.pallas_call(
        _route_kernel,
        out_shape=(jax.ShapeDtypeStruct((TOP_K, T), jnp.int32),
                   jax.ShapeDtypeStruct((TOP_K, T), F32),
                   jax.ShapeDtypeStruct((TOP_K, T), jnp.int32),
                   jax.ShapeDtypeStruct((N_EXPERTS, 1), jnp.int32)),
        grid=(T // tl,),
        in_specs=[pl.BlockSpec((N_EXPERTS, tl), tok), pl.BlockSpec((N_EXPERTS, 1), lambda i: (0, 0))],
        out_specs=(pl.BlockSpec((TOP_K, tl), tok), pl.BlockSpec((TOP_K, tl), tok),
                   pl.BlockSpec((TOP_K, tl), tok), pl.BlockSpec((N_EXPERTS, 1), lambda i: (0, 0))),
        scratch_shapes=[pltpu.VMEM((N_EXPERTS, 1), F32)],
        compiler_params=pltpu.CompilerParams(
            dimension_semantics=("arbitrary",), vmem_limit_bytes=VMEM_LIMIT),
        name="route",
    )(logits_t, bias)


def _as_rows(ref):
    return ref.reshape(ref.shape[0] // ROW_TILES, ROW_TILES, LANES)


def _wait_rows(rows_ref, n, sem):
    pltpu.make_async_copy(rows_ref.at[pl.ds(0, n)], rows_ref.at[pl.ds(0, n)], sem).wait()


def _scatter_kernel(pos_ref, h2_ref, xs_ref, sem):
    src = _as_rows(h2_ref)
    dst = _as_rows(xs_ref)
    tt = src.shape[0]

    def start(t, c):
        for k in range(TOP_K):
            pltpu.make_async_copy(src.at[t], dst.at[pos_ref[k, t]], sem).start(priority=k % 2)
        return c

    lax.fori_loop(0, tt, start, 0)
    _wait_rows(dst, tt * TOP_K, sem)


def _scatter(pos, h2_tm, tt):
    n_rows = h2_tm.shape[0] // ROW_TILES * TOP_K
    return pl.pallas_call(
        _scatter_kernel,
        out_shape=jax.ShapeDtypeStruct((n_rows * ROW_TILES, LANES), F32),
        grid=(pos.shape[1] // tt,),
        in_specs=[pl.BlockSpec((TOP_K, tt), lambda i: (0, i), memory_space=pltpu.SMEM),
                  pl.BlockSpec((tt * ROW_TILES, LANES), lambda i: (i, 0))],
        out_specs=pl.BlockSpec(memory_space=pl.ANY),
        scratch_shapes=[pltpu.SemaphoreType.DMA],
        compiler_params=pltpu.CompilerParams(
            dimension_semantics=("arbitrary",), vmem_limit_bytes=VMEM_LIMIT),
        name="scatter",
    )(pos, h2_tm)


def _experts_kernel(tile_ref, exp_ref, first_ref, last_ref, newe_ref, nitems_ref, off_ref, cnt_ref, slot_ref,
                    nexte_ref, xs_hbm, wg_hbm, wu_hbm, wd_hbm, ys_hbm,
                    xbuf_ref, ybuf_ref, sg_ref, su_ref, sd_ref, wgb_ref, wub_ref, wdb_ref, hm_ref,
                    xsem, ysem, wsem):
    i = pl.program_id(0)
    tr = EXPERT_TILE
    tile_rows = tr * ROW_TILES
    n_tiles = xs_hbm.shape[0] // tile_rows
    n_items = nitems_ref[0]

    def ring(t):
        return pl.ds(pl.multiple_of((t % TILE_RING) * tile_rows, tile_rows), tile_rows)

    def hbm_tile(t):
        return pl.ds(pl.multiple_of(t * tile_rows, tile_rows), tile_rows)

    def x_copy(t):
        return pltpu.make_async_copy(xs_hbm.at[hbm_tile(t)], xbuf_ref.at[ring(t)], xsem.at[t % TILE_RING])

    def y_copy(t):
        return pltpu.make_async_copy(ybuf_ref.at[ring(t)], ys_hbm.at[hbm_tile(t)], ysem.at[t % TILE_RING])

    def weight_copies(e, slot):
        return (pltpu.make_async_copy(wg_hbm.at[e], sg_ref.at[slot], wsem.at[slot]),
                pltpu.make_async_copy(wu_hbm.at[e], su_ref.at[slot], wsem.at[slot]),
                pltpu.make_async_copy(wd_hbm.at[e], sd_ref.at[slot], wsem.at[slot]))

    a_on = i < n_items
    j = jnp.maximum(i - 1, 0)
    b_on = (i >= 1) & (i - 1 < n_items)
    e = exp_ref[i]
    t = tile_ref[i]

    @pl.when(i == 0)
    def _():
        hm_ref[...] = jnp.zeros_like(hm_ref)
        for t0 in range(TILE_AHEAD):
            x_copy(t0).start()

    @pl.when(a_on & (first_ref[i] == 1))
    def _():
        @pl.when(t + TILE_AHEAD < n_tiles)
        def _():
            x_copy(t + TILE_AHEAD).start()

        x_copy(t).wait()

    @pl.when(a_on & (newe_ref[i] == 1))
    def _():
        slot = slot_ref[i]
        nxt = nexte_ref[i]

        @pl.when(i == 0)
        def _():
            for c in weight_copies(e, slot):
                c.start()

        @pl.when(nxt >= 0)
        def _():
            for c in weight_copies(nxt, 1 - slot):
                c.start()

        for c in weight_copies(e, slot):
            c.wait()
        wgb_ref[...] = sg_ref[slot].astype(BF16)
        wub_ref[...] = su_ref[slot].astype(BF16)
        wdb_ref[slot] = sd_ref[slot].astype(BF16)

    tj = tile_ref[j]
    ej = exp_ref[j]
    chunks = _row_chunks(_dot(hm_ref[j % 2], wdb_ref[slot_ref[j]]))
    row = tj * tr + lax.broadcasted_iota(jnp.int32, (tr, 1), 0)
    lo = off_ref[ej]
    mine = (row >= lo) & (row < lo + cnt_ref[ej])

    x = _load_rows(xbuf_ref, tr, first_row=(t % TILE_RING) * tr).astype(BF16)
    hm_ref[i % 2] = (_silu(_dot(x, wgb_ref[...])) * _dot(x, wub_ref[...])).astype(BF16)

    out_row = (tj % TILE_RING) * tile_rows

    @pl.when(b_on & (first_ref[j] == 1))
    def _():
        @pl.when(tj >= TILE_RING)
        def _():
            y_copy(tj - TILE_RING).wait()

        for c in range(ROW_TILES):
            ybuf_ref[pl.ds(out_row + c, tr, stride=ROW_TILES), :] = chunks[c]

    @pl.when(b_on & (first_ref[j] == 0))
    def _():
        for c in range(ROW_TILES):
            sl = pl.ds(out_row + c, tr, stride=ROW_TILES)
            ybuf_ref[sl, :] = jnp.where(mine, chunks[c], ybuf_ref[sl, :])

    @pl.when(b_on & (last_ref[j] == 1))
    def _():
        y_copy(tj).start()

    @pl.when(b_on & (j == n_items - 1))
    def _():
        for t0 in range(n_tiles - TILE_RING, n_tiles):
            y_copy(t0).wait()


def _experts(meta, xs, wg, wu, wd, n_items_max):
    tile_rows = EXPERT_TILE * ROW_TILES
    assert xs.shape[0] % tile_rows == 0 and xs.shape[0] // tile_rows >= TILE_RING
    hbm = pl.BlockSpec(memory_space=pl.ANY)
    n_slots = 2
    grid_spec = pltpu.PrefetchScalarGridSpec(
        num_scalar_prefetch=len(meta),
        grid=(n_items_max + 1,),
        in_specs=[hbm, hbm, hbm, hbm],
        out_specs=hbm,
        scratch_shapes=[pltpu.VMEM((TILE_RING * tile_rows, LANES), F32),
                        pltpu.VMEM((TILE_RING * tile_rows, LANES), F32),
                        pltpu.VMEM((n_slots, D_MODEL, D_EXPERT), F32),
                        pltpu.VMEM((n_slots, D_MODEL, D_EXPERT), F32),
                        pltpu.VMEM((n_slots, D_EXPERT, D_MODEL), F32),
                        pltpu.VMEM((D_MODEL, D_EXPERT), BF16),
                        pltpu.VMEM((D_MODEL, D_EXPERT), BF16),
                        pltpu.VMEM((n_slots, D_EXPERT, D_MODEL), BF16),
                        pltpu.VMEM((2, EXPERT_TILE, D_EXPERT), BF16),
                        pltpu.SemaphoreType.DMA((TILE_RING,)),
                        pltpu.SemaphoreType.DMA((TILE_RING,)),
                        pltpu.SemaphoreType.DMA((n_slots,))])
    return pl.pallas_call(
        _experts_kernel,
        out_shape=jax.ShapeDtypeStruct(xs.shape, F32),
        grid_spec=grid_spec,
        compiler_params=pltpu.CompilerParams(
            dimension_semantics=("arbitrary",), vmem_limit_bytes=VMEM_LIMIT),
        name="experts",
    )(*meta, xs, wg, wu, wd)


def _combine_kernel(pos_ref, pos_next_ref, ys_ref, h2_ref, x1_ref, gate_ref, g2_ref, swg_ref, swu_ref, swd_ref,
                    fg_ref, out_ref, buf_a, buf_b, sem):
    i = pl.program_id(0)
    tt = x1_ref.shape[0] // 2
    src = _as_rows(ys_ref)

    def gather(p_ref, col0, buf, s):
        dst = _as_rows(buf)
        for t in range(tt):
            for k in range(TOP_K):
                pltpu.make_async_copy(src.at[p_ref[k, col0 + t]], dst.at[k * tt + t],
                                      sem.at[s]).start(priority=k % 2)

    @pl.when(i == 0)
    def _():
        dst = _as_rows(buf_a)

        def start(t, c):
            for k in range(TOP_K):
                pltpu.make_async_copy(src.at[pos_ref[k, t]], dst.at[k * tt + t], sem.at[0]).start(priority=k % 2)
            return c

        lax.fori_loop(0, tt, start, 0)

    def tile(row0, buf, s, prefetch):
        tok = pl.ds(row0, tt)
        h2 = _load_rows(h2_ref, tt, first_row=row0).astype(BF16)
        hm = (_silu(_dot(h2, swg_ref[...])) * _dot(h2, swu_ref[...])).astype(BF16)
        _wait_rows(_as_rows(buf), tt * TOP_K, sem.at[s])
        prefetch()
        gate = gate_ref[tok, :]
        ssq = jnp.zeros((tt, 1), F32)
        for c in range(ROW_TILES):
            cols = slice(c * LANES, (c + 1) * LANES)
            acc = _dot(hm, swd_ref[:, cols])
            for k in range(TOP_K):
                acc = acc + gate[:, k:k + 1] * buf[pl.ds(k * tt * ROW_TILES + c, tt, stride=ROW_TILES), :]
            x2 = x1_ref[tok, cols] + g2_ref[0, :, cols] * acc
            out_ref[tok, cols] = x2
            ssq = ssq + jnp.sum(x2 * x2, axis=-1, keepdims=True)
        out_ref[tok, :] = out_ref[tok, :] * lax.rsqrt(ssq * (1.0 / D_MODEL) + EPS) * fg_ref[...]

    tile(0, buf_a, 0, lambda: gather(pos_ref, tt, buf_b, 1))
    tile(tt, buf_b, 1, lambda: gather(pos_next_ref, 0, buf_a, 0))

    @pl.when(i == pl.num_programs(0) - 1)
    def _():
        _wait_rows(_as_rows(buf_a), tt * TOP_K, sem.at[0])


def _combine(pos, ys, h2_tm, x1, gate_tm, mod3, swg, swu, swd, fg, B, S, tt):
    T, D = x1.shape
    gather_rows = tt * TOP_K * ROW_TILES
    tt = 2 * tt
    nS = S // tt
    const2 = lambda i: (0, 0)
    n_tiles = T // tt
    return pl.pallas_call(
        _combine_kernel,
        out_shape=jax.ShapeDtypeStruct((T, D), F32),
        grid=(n_tiles,),
        in_specs=[pl.BlockSpec((TOP_K, tt), lambda i: (0, i), memory_space=pltpu.SMEM),
                  pl.BlockSpec((TOP_K, tt), lambda i: (0, jnp.minimum(i + 1, n_tiles - 1)),
                               memory_space=pltpu.SMEM),
                  pl.BlockSpec(memory_space=pl.ANY),
                  pl.BlockSpec((tt * ROW_TILES, LANES), lambda i: (i, 0)),
                  pl.BlockSpec((tt, D), lambda i: (i, 0)),
                  pl.BlockSpec((tt, TOP_K), lambda i: (i, 0)),
                  pl.BlockSpec((1, 1, D), lambda i: (i // nS, 0, 5)),
                  pl.BlockSpec(swg.shape, const2),
                  pl.BlockSpec(swu.shape, const2),
                  pl.BlockSpec(swd.shape, const2),
                  pl.BlockSpec((1, D), const2)],
        out_specs=pl.BlockSpec((tt, D), lambda i: (i, 0)),
        scratch_shapes=[pltpu.VMEM((gather_rows, LANES), F32),
                        pltpu.VMEM((gather_rows, LANES), F32),
                        pltpu.SemaphoreType.DMA((2,))],
        compiler_params=pltpu.CompilerParams(
            dimension_semantics=("arbitrary",), vmem_limit_bytes=VMEM_LIMIT),
        name="combine",
    )(pos, pos, ys, h2_tm, x1, gate_tm, mod3, swg, swu, swd, fg)


def _place_kernel(idx_ref, rank_ref, off_ref, pos_ref):
    tl = idx_ref.shape[1]
    rowid = lax.broadcasted_iota(jnp.int32, (N_EXPERTS, tl), 0)
    off = off_ref[...].astype(F32)
    for k in range(TOP_K):
        base = jnp.sum(jnp.where(rowid == idx_ref[k:k + 1, :], off, 0.0), axis=0, keepdims=True)
        pos_ref[k:k + 1, :] = base.astype(jnp.int32) + rank_ref[k:k + 1, :]


def _place(idx, rank, off, tl):
    T = idx.shape[1]
    tok = pl.BlockSpec((TOP_K, tl), lambda i: (0, i))
    return pl.pallas_call(
        _place_kernel,
        out_shape=jax.ShapeDtypeStruct((TOP_K, T), jnp.int32),
        grid=(T // tl,),
        in_specs=[tok, tok, pl.BlockSpec((N_EXPERTS, 1), lambda i: (0, 0))],
        out_specs=tok,
        name="place",
    )(idx, rank, off)


def _plan(counts, n_items_max):
    counts = counts[:, 0]
    off = jnp.cumsum(counts) - counts
    first_tile = off // EXPERT_TILE
    last_tile = (off + counts - 1) // EXPERT_TILE
    n_e = jnp.where(counts > 0, last_tile - first_tile + 1, 0)
    item_end = jnp.cumsum(n_e)
    item_start = item_end - n_e
    n_items = item_end[-1]
    ids = jnp.arange(n_items_max + 1, dtype=jnp.int32)
    ids_c = jnp.minimum(ids, n_items - 1)
    item_e = jnp.minimum(jnp.sum(item_end[None, :] <= ids_c[:, None], axis=1), N_EXPERTS - 1).astype(jnp.int32)
    ids_e = jnp.arange(N_EXPERTS, dtype=jnp.int32)
    of_item = item_e[:, None] == ids_e[None, :]

    def per_item(table):
        return jnp.sum(jnp.where(of_item, table[None, :], 0), axis=1).astype(jnp.int32)

    item_tile = per_item(first_tile) + ids_c - per_item(item_start)
    prev_tile = jnp.concatenate([jnp.full((1,), -1, jnp.int32), item_tile[:-1]])
    prev_e = jnp.concatenate([jnp.full((1,), -1, jnp.int32), item_e[:-1]])
    next_tile = jnp.concatenate([item_tile[1:], jnp.full((1,), -1, jnp.int32)])
    item_first = (item_tile != prev_tile).astype(jnp.int32)
    item_last = ((item_tile != next_tile) | (ids == n_items - 1)).astype(jnp.int32)
    item_newe = (item_e != prev_e).astype(jnp.int32)
    item_slot = ((jnp.cumsum(item_newe) - 1) % 2).astype(jnp.int32)
    later = jnp.where((counts[None, :] > 0) & (ids_e[None, :] > ids_e[:, None]), ids_e[None, :], N_EXPERTS)
    next_e = jnp.min(later, axis=1)
    next_e = jnp.where(next_e < N_EXPERTS, next_e, -1).astype(jnp.int32)
    meta = (item_tile, item_e, item_first, item_last, item_newe, n_items.reshape(1).astype(jnp.int32),
            off.astype(jnp.int32), counts.astype(jnp.int32), item_slot, per_item(next_e))
    return off.astype(jnp.int32).reshape(N_EXPERTS, 1), meta


def kernel(x, c, ada_w, ada_b, norm1_g, w_in, hgrn_lb, hgrn_norm_g, pool_w, pool_b, pool_scale, w_up_a, w_up_b, w_out, norm2_g, router_w, router_bias, exp_w_gate, exp_w_up, exp_w_down, shared_w_gate, shared_w_up, shared_w_down, final_norm_g):
    B, S, D = x.shape
    T = B * S
    assert ada_w.shape[0] == 1, "single-layer trunk only: the final norm is fused into the combine step"
    lb_all = jnp.cumsum(jax.nn.softmax(hgrn_lb.astype(F32), axis=0), axis=0)
    c_pad = jnp.zeros((SUBLANES, D), F32).at[:B].set(c)
    n_items_max = T * TOP_K // EXPERT_TILE + N_EXPERTS - 1

    for l in range(1):
        mod = _ada(c_pad, ada_w[l], ada_b[l].reshape(1, -1))
        mod3 = mod[:B].reshape(B, 1, 6 * D)

        q, lf, k, v, sog, pm, sga, sgb = _inproj(
            x, mod3, norm1_g[l].reshape(1, D), w_in[l].astype(BF16), lb_all[l].reshape(1, HG_WIDTH),
            pool_w[l].astype(BF16), pool_b[l].reshape(len(POOL_WINDOWS), 1, POOL_GROUP),
            pool_scale[l].reshape(1, POOL_WIDTH), tm=256)
        oa = _hgrn(q, lf, k, v, sog, hgrn_norm_g[l].reshape(1, HG_DK), B, S, tb=512)

        rw_hi, rw_lo = _split_bf16(router_w[l].T)
        x1, h2_tm, logits_t = _mix(
            x, oa, pm, sga, sgb, mod3, norm2_g[l].reshape(1, D), w_up_a[l].astype(BF16),
            w_up_b[l].astype(BF16), w_out[l].astype(BF16), rw_hi, rw_lo, tm=512)

        idx, gate, rank, counts = _route(logits_t, router_bias[l].reshape(N_EXPERTS, 1), tl=256)
        off, meta = _plan(counts, n_items_max)
        pos = _place(idx, rank, off, tl=512)

        xs = _scatter(pos, h2_tm, tt=256)
        ys = _experts(meta, xs, exp_w_gate[l], exp_w_up[l], exp_w_down[l], n_items_max)

        tt_c = 128
        fg = final_norm_g.reshape(1, D)
        x = _combine(pos, ys, h2_tm, x1, gate.T, mod3,
                     shared_w_gate[l].astype(BF16), shared_w_up[l].astype(BF16),
                     shared_w_down[l].astype(BF16), fg, B, S, tt_c).reshape(B, S, D)
    return x
```

```python
import functools

import jax
import jax.numpy as jnp
from jax import lax
from jax.experimental import pallas as pl
from jax.experimental.pallas import tpu as pltpu

F32 = jnp.float32
BF16 = jnp.bfloat16
HIGHEST = lax.Precision.HIGHEST

D_MODEL = 1024
HG_WIDTH = 512
HG_DK = 128
HG_HEADS = 4
HG_CHUNK = 64
HG_BLK = 16
HG_SUB = 8
POOL_WIDTH = 512
POOL_WINDOWS = (2, 4, 8, 16)
POOL_GROUP = 128
POOL_HALO = 16
N_EXPERTS = 256
TOP_K = 8
N_GROUPS = 8
TOPK_GROUPS = 4
GROUP_SIZE = N_EXPERTS // N_GROUPS
D_EXPERT = 256
ROUTED_SCALE = 2.5
EPS = 1e-6

LANES = 128
SUBLANES = 8
ROW_TILES = D_MODEL // LANES
EXPERT_TILE = 128
TILE_RING = 8
TILE_AHEAD = TILE_RING - 1
VMEM_LIMIT = 56 * 1024 * 1024

COL_Q, COL_F, COL_I, COL_OG, COL_U, COL_GA, COL_GB = 0, 512, 1024, 1536, 2048, 2560, 3584


def _sigmoid(x):
    return 1.0 / (1.0 + jnp.exp(-x))


def _silu(x):
    return x * _sigmoid(x)


def _dot(a, b):
    return jnp.dot(a, b, preferred_element_type=F32)


def _dot_nt(a, b):
    return lax.dot_general(a, b, (((1,), (1,)), ((), ())), preferred_element_type=F32)


def _dot_tn(a, b):
    return lax.dot_general(a, b, (((0,), (0,)), ((), ())), preferred_element_type=F32)


def _row_chunks(x):
    return [x[:, j * LANES:(j + 1) * LANES] for j in range(ROW_TILES)]


def _load_rows(ref, n, first_row=0):
    return jnp.concatenate(
        [ref[pl.ds(first_row * ROW_TILES + j, n, stride=ROW_TILES), :] for j in range(ROW_TILES)], axis=1)


def _ada_kernel(c_ref, w_ref, b_ref, o_ref):
    cond = _silu(c_ref[...])
    o_ref[...] = jnp.dot(cond, w_ref[...], precision=HIGHEST, preferred_element_type=F32) + b_ref[...]


def _ada(c_pad, ada_w, ada_b):
    n = ada_w.shape[1]
    tn = 1536
    return pl.pallas_call(
        _ada_kernel,
        out_shape=jax.ShapeDtypeStruct((SUBLANES, n), F32),
        grid=(n // tn,),
        in_specs=[pl.BlockSpec((SUBLANES, D_MODEL), lambda j: (0, 0)),
                  pl.BlockSpec((D_MODEL, tn), lambda j: (0, j)),
                  pl.BlockSpec((1, tn), lambda j: (0, j))],
        out_specs=pl.BlockSpec((SUBLANES, tn), lambda j: (0, j)),
        compiler_params=pltpu.CompilerParams(vmem_limit_bytes=VMEM_LIMIT),
        name="ada",
    )(c_pad, ada_w, ada_b)


def _inproj_kernel(x_ref, sh_ref, sc_ref, g_ref, w_ref, lb_ref, pw_ref, pb_ref, ps_ref,
                   q_ref, lf_ref, k_ref, v_ref, sog_ref, pm_ref, sga_ref, sgb_ref, halo_ref):
    s = pl.program_id(1)
    tm = x_ref.shape[1]
    x = x_ref[0]
    h = x * lax.rsqrt(jnp.mean(x * x, axis=-1, keepdims=True) + EPS) * g_ref[...]
    h = h * (1.0 + sc_ref[0]) + sh_ref[0]
    hb = h.astype(BF16)

    def proj(lo, n):
        return _dot(hb, w_ref[:, lo:lo + n])

    q = proj(COL_Q, HG_WIDTH)
    q_ref[...] = _silu(q) * (HG_DK ** -0.5)
    sig = _sigmoid(proj(COL_F, HG_WIDTH))
    lb = lb_ref[...]
    lf_ref[...] = jnp.log(lb + (1.0 - lb) * sig)
    k_ref[...] = (1.0 - lb) * (1.0 - sig)
    v_ref[...] = proj(COL_I, HG_WIDTH)
    sog_ref[...] = _silu(proj(COL_OG, HG_WIDTH)).astype(BF16)
    sga_ref[...] = _sigmoid(proj(COL_GA, D_MODEL)).astype(BF16)
    sgb_ref[...] = _sigmoid(proj(COL_GB, D_MODEL)).astype(BF16)

    u = proj(COL_U, POOL_WIDTH)
    @pl.when(s == 0)
    def _():
        halo_ref[...] = jnp.zeros_like(halo_ref)

    ext = jnp.concatenate([halo_ref[...], u], axis=0)
    halo_ref[...] = u[tm - POOL_HALO:, :]
    s2 = ext + pltpu.roll(ext, 1, 0)
    s4 = s2 + pltpu.roll(s2, 2, 0)
    s8 = s4 + pltpu.roll(s4, 4, 0)
    s16 = s8 + pltpu.roll(s8, 8, 0)
    pos1 = (s * tm + 1 + lax.broadcasted_iota(jnp.int32, (tm, 1), 0)).astype(F32)
    for g, (w, sw) in enumerate(zip(POOL_WINDOWS, (s2, s4, s8, s16))):
        cols = slice(g * POOL_GROUP, (g + 1) * POOL_GROUP)
        m = sw[POOL_HALO:, cols] / jnp.minimum(pos1, float(w)) - u[:, cols]
        y = _dot(m.astype(BF16), pw_ref[g]) + pb_ref[g]
        pm_ref[:, cols] = (y * ps_ref[:, cols]).astype(BF16)


def _inproj(x, mod3, norm_g, w_in_b, lb, pool_w_b, pool_b, pool_scale, tm):
    B, S, D = x.shape
    T = B * S
    nS = S // tm
    row = lambda b, s: (b * nS + s, 0)
    const2 = lambda b, s: (0, 0)
    const3 = lambda b, s: (0, 0, 0)
    half = lambda dt: jax.ShapeDtypeStruct((T, HG_WIDTH), dt)
    full = lambda dt: jax.ShapeDtypeStruct((T, D), dt)
    return pl.pallas_call(
        _inproj_kernel,
        out_shape=(half(F32), half(F32), half(F32), half(F32), half(BF16), half(BF16), full(BF16), full(BF16)),
        grid=(B, nS),
        in_specs=[pl.BlockSpec((1, tm, D), lambda b, s: (b, s, 0)),
                  pl.BlockSpec((1, 1, D), lambda b, s: (b, 0, 0)),
                  pl.BlockSpec((1, 1, D), lambda b, s: (b, 0, 1)),
                  pl.BlockSpec((1, D), const2),
                  pl.BlockSpec(w_in_b.shape, const2),
                  pl.BlockSpec((1, HG_WIDTH), const2),
                  pl.BlockSpec(pool_w_b.shape, const3),
                  pl.BlockSpec(pool_b.shape, const3),
                  pl.BlockSpec((1, POOL_WIDTH), const2)],
        out_specs=(pl.BlockSpec((tm, HG_WIDTH), row),) * 6 + (pl.BlockSpec((tm, D), row),) * 2,
        scratch_shapes=[pltpu.VMEM((POOL_HALO, POOL_WIDTH), F32)],
        compiler_params=pltpu.CompilerParams(
            dimension_semantics=("arbitrary", "arbitrary"), vmem_limit_bytes=VMEM_LIMIT),
        name="inproj",
    )(x, mod3, mod3, norm_g, w_in_b, lb, pool_w_b, pool_b, pool_scale)


def _hgrn_kernel(q_ref, lf_ref, k_ref, v_ref, sog_ref, gn_ref, o_ref, *st_refs):
    C = HG_CHUNK
    n_chunks = q_ref.shape[0] // C

    @pl.when(pl.program_id(1) == 0)
    def _():
        for st_ref in st_refs:
            st_ref[...] = jnp.zeros_like(st_ref)

    r_i = lax.broadcasted_iota(jnp.int32, (C, C), 0)
    c_i = lax.broadcasted_iota(jnp.int32, (C, C), 1)
    tril = (c_i <= r_i).astype(BF16)
    same_blk = (r_i // HG_BLK) == (c_i // HG_BLK)
    row = lax.broadcasted_iota(jnp.int32, (C, HG_DK), 0)
    row_in_sub = row % HG_SUB
    upper_half = (row % HG_BLK) >= HG_SUB
    row_blk = row // HG_BLK
    n_blk = C // HG_BLK

    def cumsum_rows(x):
        hi = x.astype(BF16)
        r1 = x - hi.astype(F32)
        mid = r1.astype(BF16)
        lo = (r1 - mid.astype(F32)).astype(BF16)
        return _dot(tril, hi) + _dot(tril, mid) + _dot(tril, lo)

    def block_rows(x, size, which):
        pieces = []
        for g in range(C // size):
            src = g * size + which
            pieces.append(jnp.zeros((size, x.shape[1]), F32) if src < 0
                          else jnp.broadcast_to(x[src:src + 1, :], (size, x.shape[1])))
        return jnp.concatenate(pieces, axis=0)

    def chunk(ci, carry):
        rs = pl.ds(pl.multiple_of(ci * C, C), C)
        b_all = cumsum_rows(lf_ref[rs, :])
        for h in range(HG_HEADS):
            cs = slice(h * HG_DK, (h + 1) * HG_DK)
            q = q_ref[rs, cs]
            k = k_ref[rs, cs]
            v = v_ref[rs, cs]
            b = b_all[:, cs]
            vb = v.astype(BF16)

            kt = k * jnp.exp(block_rows(b, HG_BLK, HG_BLK - 1) - b)
            q_parts, k_parts = [], []
            for j in range(n_blk - 1):
                bj = b[HG_BLK * j + HG_BLK - 1:HG_BLK * (j + 1), :]
                after = row >= HG_BLK * (j + 1)
                q_parts.append(q * jnp.exp(jnp.where(after, b - bj, -jnp.inf)))
                k_parts.append(jnp.where(row_blk == j, kt, 0.0))
            qcat = jnp.concatenate(q_parts, axis=1).astype(BF16)
            kcat = jnp.concatenate(k_parts, axis=1).astype(BF16)
            scores = _dot_nt(qcat, kcat)
            b_prev = block_rows(b, HG_SUB, -1)
            b_sub = block_rows(b, HG_SUB, HG_SUB - 1)
            qh = (q * jnp.exp(jnp.where(upper_half, b - b_prev, -jnp.inf))).astype(BF16)
            kh = jnp.where(upper_half, 0.0, k * jnp.exp(b_sub - b)).astype(BF16)
            scores = scores + jnp.where(same_blk, _dot_nt(qh, kh), 0.0)
            o = _dot(scores.astype(BF16), vb)

            o = o + jnp.sum(q * k, axis=-1, keepdims=True) * v
            for d in range(1, HG_SUB):
                kd = pltpu.roll(k, d, 0)
                bd = pltpu.roll(b, d, 0)
                vd = pltpu.roll(v, d, 0)
                e = jnp.exp(jnp.where(row_in_sub >= d, b - bd, -jnp.inf))
                o = o + jnp.sum(q * kd * e, axis=-1, keepdims=True) * vd

            st = st_refs[h][...]
            o = o + _dot_nt((q * jnp.exp(b)).astype(BF16), st.astype(BF16))
            b_end = b[C - 1:C, :]
            k_end = (k * jnp.exp(b_end - b)).astype(BF16)
            st_refs[h][...] = st * jnp.exp(b_end) + _dot_tn(vb, k_end)

            on = o * lax.rsqrt(jnp.mean(o * o, axis=-1, keepdims=True) + EPS) * gn_ref[...]
            o_ref[rs, cs] = (on * sog_ref[rs, cs].astype(F32)).astype(BF16)
        return carry

    lax.fori_loop(0, n_chunks, chunk, 0)


def _hgrn(q, lf, k, v, sog, gn, B, S, tb):
    T = B * S
    nS = S // tb
    row = lambda b, s: (b * nS + s, 0)
    blk = pl.BlockSpec((tb, HG_WIDTH), row)
    return pl.pallas_call(
        _hgrn_kernel,
        out_shape=jax.ShapeDtypeStruct((T, HG_WIDTH), BF16),
        grid=(B, nS),
        in_specs=[blk, blk, blk, blk, blk, pl.BlockSpec((1, HG_DK), lambda b, s: (0, 0))],
        out_specs=blk,
        scratch_shapes=[pltpu.VMEM((HG_DK, HG_DK), F32)] * HG_HEADS,
        compiler_params=pltpu.CompilerParams(
            dimension_semantics=("arbitrary", "arbitrary"), vmem_limit_bytes=VMEM_LIMIT),
        name="hgrn",
    )(q, lf, k, v, sog, gn)


def _split_kernel(w_ref, hi_ref, lo_ref):
    w = w_ref[...]
    hi = w.astype(BF16)
    hi_ref[...] = hi
    lo_ref[...] = (w - hi.astype(F32)).astype(BF16)


def _split_bf16(w):
    out = jax.ShapeDtypeStruct(w.shape, BF16)
    return pl.pallas_call(_split_kernel, out_shape=(out, out), name="split")(w)


def _mix_kernel(x_ref, oa_ref, pm_ref, sga_ref, sgb_ref, g1_ref, sh2_ref, sc2_ref, n2_ref,
                wua_ref, wub_ref, wo_ref, rw_hi_ref, rw_lo_ref, x1_ref, h2_ref, lg_ref):
    tm = x_ref.shape[1]
    ya = _dot(oa_ref[...], wua_ref[...])
    yb = _dot(pm_ref[...], wub_ref[...])
    mix = sga_ref[...].astype(F32) * ya + sgb_ref[...].astype(F32) * yb
    x1 = x_ref[0] + g1_ref[0] * _dot(mix.astype(BF16), wo_ref[...])
    x1_ref[...] = x1
    h2 = x1 * lax.rsqrt(jnp.mean(x1 * x1, axis=-1, keepdims=True) + EPS) * n2_ref[...]
    h2 = h2 * (1.0 + sc2_ref[0]) + sh2_ref[0]
    for j, chunk in enumerate(_row_chunks(h2)):
        h2_ref[pl.ds(j, tm, stride=ROW_TILES), :] = chunk
    h_hi = h2.astype(BF16)
    h_lo = (h2 - h_hi.astype(F32)).astype(BF16)
    rw_hi = rw_hi_ref[...]
    lg_ref[...] = _dot_nt(rw_hi, h_hi) + _dot_nt(rw_hi, h_lo) + _dot_nt(rw_lo_ref[...], h_hi)


def _mix(x, oa, pm, sga, sgb, mod3, norm2_g, wua, wub, wo, rw_hi, rw_lo, tm):
    B, S, D = x.shape
    T = B * S
    nS = S // tm
    row = lambda b, s: (b * nS + s, 0)
    const2 = lambda b, s: (0, 0)
    return pl.pallas_call(
        _mix_kernel,
        out_shape=(jax.ShapeDtypeStruct((T, D), F32),
                   jax.ShapeDtypeStruct((T * ROW_TILES, LANES), F32),
                   jax.ShapeDtypeStruct((N_EXPERTS, T), F32)),
        grid=(B, nS),
        in_specs=[pl.BlockSpec((1, tm, D), lambda b, s: (b, s, 0)),
                  pl.BlockSpec((tm, HG_WIDTH), row),
                  pl.BlockSpec((tm, POOL_WIDTH), row),
                  pl.BlockSpec((tm, D), row),
                  pl.BlockSpec((tm, D), row),
                  pl.BlockSpec((1, 1, D), lambda b, s: (b, 0, 2)),
                  pl.BlockSpec((1, 1, D), lambda b, s: (b, 0, 3)),
                  pl.BlockSpec((1, 1, D), lambda b, s: (b, 0, 4)),
                  pl.BlockSpec((1, D), const2),
                  pl.BlockSpec(wua.shape, const2),
                  pl.BlockSpec(wub.shape, const2),
                  pl.BlockSpec(wo.shape, const2),
                  pl.BlockSpec(rw_hi.shape, const2),
                  pl.BlockSpec(rw_lo.shape, const2)],
        out_specs=(pl.BlockSpec((tm, D), row),
                   pl.BlockSpec((tm * ROW_TILES, LANES), row),
                   pl.BlockSpec((N_EXPERTS, tm), lambda b, s: (0, b * nS + s))),
        compiler_params=pltpu.CompilerParams(
            dimension_semantics=("arbitrary", "arbitrary"), vmem_limit_bytes=VMEM_LIMIT),
        name="mix",
    )(x, oa, pm, sga, sgb, mod3, mod3, mod3, norm2_g, wua, wub, wo, rw_hi, rw_lo)


def _route_kernel(lg_ref, bias_ref, idx_ref, gate_ref, rank_ref, cnt_ref, carry_ref):
    tl = lg_ref.shape[1]
    neg = -jnp.inf

    @pl.when(pl.program_id(0) == 0)
    def _():
        carry_ref[...] = jnp.zeros_like(carry_ref)

    s = _sigmoid(lg_ref[...])
    biased = s + bias_ref[...]
    rowid = lax.broadcasted_iota(jnp.int32, (N_EXPERTS, tl), 0)

    def first_argmax(x, ids, sentinel):
        m = jnp.max(x, axis=0, keepdims=True)
        return jnp.min(jnp.where(x == m, ids, sentinel), axis=0, keepdims=True), m

    gscores = []
    for g in range(N_GROUPS):
        xg = biased[g * GROUP_SIZE:(g + 1) * GROUP_SIZE, :]
        rid = g * GROUP_SIZE + lax.broadcasted_iota(jnp.int32, (GROUP_SIZE, tl), 0)
        first, m1 = first_argmax(xg, rid, N_EXPERTS)
        m2 = jnp.max(jnp.where(rid == first, neg, xg), axis=0, keepdims=True)
        gscores.append(m1 + m2)
    blocks = []
    for g in range(N_GROUPS):
        beaten = jnp.zeros((1, tl), F32)
        for o in range(N_GROUPS):
            if o != g:
                wins = (gscores[o] >= gscores[g]) if o < g else (gscores[o] > gscores[g])
                beaten = beaten + jnp.where(wins, 1.0, 0.0)
        xg = biased[g * GROUP_SIZE:(g + 1) * GROUP_SIZE, :]
        blocks.append(jnp.where(beaten < float(TOPK_GROUPS), xg, neg))
    masked = jnp.concatenate(blocks, axis=0)

    idxs, gates = [], []
    chosen = jnp.zeros((N_EXPERTS, tl), F32)
    for _ in range(TOP_K):
        first, _m = first_argmax(masked, rowid, N_EXPERTS)
        sel = rowid == first
        gates.append(jnp.sum(jnp.where(sel, s, 0.0), axis=0, keepdims=True))
        idxs.append(first)
        chosen = jnp.where(sel, 1.0, chosen)
        masked = jnp.where(sel, neg, masked)
    gate_sum = functools.reduce(lambda a, b: a + b, gates)
    for k in range(TOP_K):
        gate_ref[k:k + 1, :] = gates[k] / gate_sum * ROUTED_SCALE
        idx_ref[k:k + 1, :] = idxs[k]

    lr = lax.broadcasted_iota(jnp.int32, (tl, tl), 0)
    lc = lax.broadcasted_iota(jnp.int32, (tl, tl), 1)
    prefix = (lr <= lc).astype(BF16)
    cnt_incl = _dot(chosen.astype(BF16), prefix)
    carry = carry_ref[...]
    rank_excl = cnt_incl - chosen + carry
    for k in range(TOP_K):
        rank_k = jnp.sum(jnp.where(rowid == idxs[k], rank_excl, 0.0), axis=0, keepdims=True)
        rank_ref[k:k + 1, :] = rank_k.astype(jnp.int32)
    carry = carry + jnp.sum(chosen, axis=1, keepdims=True)
    carry_ref[...] = carry
    cnt_ref[...] = carry.astype(jnp.int32)


def _route(logits_t, bias, tl):
    T = logits_t.shape[1]
    tok = lambda i: (0, i)
    return pl.pallas_call(
        _route_kernel,
        out_shape=(jax.ShapeDtypeStruct((TOP_K, T), jnp.int32),
                   jax.ShapeDtypeStruct((TOP_K, T), F32),
                   jax.ShapeDtypeStruct((TOP_K, T), jnp.int32),
                   jax.ShapeDtypeStruct((N_EXPERTS, 1), jnp.int32)),
        grid=(T // tl,),
        in_specs=[pl.BlockSpec((N_EXPERTS, tl), tok), pl.BlockSpec((N_EXPERTS, 1), lambda i: (0, 0))],
        out_specs=(pl.BlockSpec((TOP_K, tl), tok), pl.BlockSpec((TOP_K, tl), tok),
                   pl.BlockSpec((TOP_K, tl), tok), pl.BlockSpec((N_EXPERTS, 1), lambda i: (0, 0))),
        scratch_shapes=[pltpu.VMEM((N_EXPERTS, 1), F32)],
        compiler_params=pltpu.CompilerParams(
            dimension_semantics=("arbitrary",), vmem_limit_bytes=VMEM_LIMIT),
        name="route",
    )(logits_t, bias)


def _as_rows(ref):
    return ref.reshape(ref.shape[0] // ROW_TILES, ROW_TILES, LANES)


def _wait_rows(rows_ref, n, sem):
    pltpu.make_async_copy(rows_ref.at[pl.ds(0, n)], rows_ref.at[pl.ds(0, n)], sem).wait()


def _scatter_kernel(pos_ref, h2_ref, xs_ref, zero_ref, sem, zsem):
    src = _as_rows(h2_ref)
    dst = _as_rows(xs_ref)
    tt = src.shape[0]

    @pl.when(pl.program_id(0) == 0)
    def _():
        zero_ref[...] = jnp.zeros_like(zero_ref)
        tail = xs_ref.at[pl.ds(xs_ref.shape[0] - zero_ref.shape[0], zero_ref.shape[0])]
        fill = pltpu.make_async_copy(zero_ref, tail, zsem)
        fill.start()
        fill.wait()

    def start(t, c):
        for k in range(TOP_K):
            pltpu.make_async_copy(src.at[t], dst.at[pos_ref[k, t]], sem).start(priority=k % 2)
        return c

    lax.fori_loop(0, tt, start, 0)
    _wait_rows(dst, tt * TOP_K, sem)


def _scatter(pos, h2_tm, tt):
    n_rows = h2_tm.shape[0] // ROW_TILES * TOP_K + EXPERT_TILE
    return pl.pallas_call(
        _scatter_kernel,
        out_shape=jax.ShapeDtypeStruct((n_rows * ROW_TILES, LANES), F32),
        grid=(pos.shape[1] // tt,),
        in_specs=[pl.BlockSpec((TOP_K, tt), lambda i: (0, i), memory_space=pltpu.SMEM),
                  pl.BlockSpec((tt * ROW_TILES, LANES), lambda i: (i, 0))],
        out_specs=pl.BlockSpec(memory_space=pl.ANY),
        scratch_shapes=[pltpu.VMEM((EXPERT_TILE * ROW_TILES, LANES), F32),
                        pltpu.SemaphoreType.DMA, pltpu.SemaphoreType.DMA],
        compiler_params=pltpu.CompilerParams(
            dimension_semantics=("arbitrary",), vmem_limit_bytes=VMEM_LIMIT),
        name="scatter",
    )(pos, h2_tm)


def _experts_kernel(exp_ref, row0_ref, valid_ref, newe_ref, nitems_ref, slot_ref, nexte_ref,
                    xs_hbm, wg_hbm, wu_hbm, wd_hbm, ys_hbm,
                    xbuf_ref, ybuf_ref, sg_ref, su_ref, sd_ref, wgb_ref, wub_ref, wdb_ref, hm_ref,
                    xsem, ysem, wsem):
    i = pl.program_id(0)
    tr = EXPERT_TILE
    tile_rows = tr * ROW_TILES
    n_items = nitems_ref[0]
    part_sizes = tuple(tr >> (b + 1) for b in range(tr.bit_length() - 1))

    def ring(item, first_row=0, n_rows=tr):
        start = ((item % TILE_RING) * tr + first_row) * ROW_TILES
        return pl.ds(pl.multiple_of(start, ROW_TILES), n_rows * ROW_TILES)

    def hbm_rows(first_row, n_rows=tr):
        return pl.ds(pl.multiple_of(first_row * ROW_TILES, ROW_TILES), n_rows * ROW_TILES)

    def x_copy(item):
        return pltpu.make_async_copy(xs_hbm.at[hbm_rows(row0_ref[item])], xbuf_ref.at[ring(item)],
                                     xsem.at[item % TILE_RING])

    def y_copies(item, go):
        v = valid_ref[item]
        sem = ysem.at[item % TILE_RING]

        @pl.when(v == tr)
        def _():
            go(pltpu.make_async_copy(ybuf_ref.at[ring(item)], ys_hbm.at[hbm_rows(row0_ref[item])], sem))

        @pl.when(v != tr)
        def _():
            for size in part_sizes:
                @pl.when((v & size) != 0)
                def _():
                    first = v & ~(2 * size - 1)
                    go(pltpu.make_async_copy(ybuf_ref.at[ring(item, first, size)],
                                             ys_hbm.at[hbm_rows(row0_ref[item] + first, size)], sem))

    def weight_copies(e, slot):
        return (pltpu.make_async_copy(wg_hbm.at[e], sg_ref.at[slot], wsem.at[slot]),
                pltpu.make_async_copy(wu_hbm.at[e], su_ref.at[slot], wsem.at[slot]),
                pltpu.make_async_copy(wd_hbm.at[e], sd_ref.at[slot], wsem.at[slot]))

    a_on = i < n_items
    j = jnp.maximum(i - 1, 0)
    b_on = (i >= 1) & (i - 1 < n_items)
    e = exp_ref[i]

    @pl.when(i == 0)
    def _():
        hm_ref[...] = jnp.zeros_like(hm_ref)
        for i0 in range(TILE_AHEAD):
            @pl.when(i0 < n_items)
            def _():
                x_copy(i0).start()

    @pl.when(a_on)
    def _():
        @pl.when(i + TILE_AHEAD < n_items)
        def _():
            x_copy(i + TILE_AHEAD).start()

        x_copy(i).wait()

    @pl.when(b_on & (j >= TILE_RING))
    def _():
        y_copies(j - TILE_RING, lambda c: c.wait())

    @pl.when(a_on & (newe_ref[i] == 1))
    def _():
        slot = slot_ref[i]
        nxt = nexte_ref[i]

        @pl.when(i == 0)
        def _():
            for c in weight_copies(e, slot):
                c.start()

        @pl.when(nxt >= 0)
        def _():
            for c in weight_copies(nxt, 1 - slot):
                c.start()

        for c in weight_copies(e, slot):
            c.wait()
        wgb_ref[...] = sg_ref[slot].astype(BF16)
        wub_ref[...] = su_ref[slot].astype(BF16)
        wdb_ref[slot] = sd_ref[slot].astype(BF16)

    out_row = (j % TILE_RING) * tile_rows
    for c, chunk in enumerate(_row_chunks(_dot(hm_ref[j % 2], wdb_ref[slot_ref[j]]))):
        ybuf_ref[pl.ds(out_row + c, tr, stride=ROW_TILES), :] = chunk

    x = _load_rows(xbuf_ref, tr, first_row=(i % TILE_RING) * tr).astype(BF16)
    hm_ref[i % 2] = (_silu(_dot(x, wgb_ref[...])) * _dot(x, wub_ref[...])).astype(BF16)

    @pl.when(b_on)
    def _():
        y_copies(j, lambda c: c.start())

    @pl.when(b_on & (j == n_items - 1))
    def _():
        for back in range(TILE_RING):
            @pl.when(j - back >= 0)
            def _():
                y_copies(j - back, lambda c: c.wait())


def _experts(meta, xs, wg, wu, wd, n_items_max):
    tile_rows = EXPERT_TILE * ROW_TILES
    out_rows = xs.shape[0] - tile_rows
    hbm = pl.BlockSpec(memory_space=pl.ANY)
    n_slots = 2
    grid_spec = pltpu.PrefetchScalarGridSpec(
        num_scalar_prefetch=len(meta),
        grid=(n_items_max + 1,),
        in_specs=[hbm, hbm, hbm, hbm],
        out_specs=hbm,
        scratch_shapes=[pltpu.VMEM((TILE_RING * tile_rows, LANES), F32),
                        pltpu.VMEM((TILE_RING * tile_rows, LANES), F32),
                        pltpu.VMEM((n_slots, D_MODEL, D_EXPERT), F32),
                        pltpu.VMEM((n_slots, D_MODEL, D_EXPERT), F32),
                        pltpu.VMEM((n_slots, D_EXPERT, D_MODEL), F32),
                        pltpu.VMEM((D_MODEL, D_EXPERT), BF16),
                        pltpu.VMEM((D_MODEL, D_EXPERT), BF16),
                        pltpu.VMEM((n_slots, D_EXPERT, D_MODEL), BF16),
                        pltpu.VMEM((2, EXPERT_TILE, D_EXPERT), BF16),
                        pltpu.SemaphoreType.DMA((TILE_RING,)),
                        pltpu.SemaphoreType.DMA((TILE_RING,)),
                        pltpu.SemaphoreType.DMA((n_slots,))])
    return pl.pallas_call(
        _experts_kernel,
        out_shape=jax.ShapeDtypeStruct((out_rows, LANES), F32),
        grid_spec=grid_spec,
        compiler_params=pltpu.CompilerParams(
            dimension_semantics=("arbitrary",), vmem_limit_bytes=VMEM_LIMIT),
        name="experts",
    )(*meta, xs, wg, wu, wd)


def _combine_kernel(pos_ref, pos_next_ref, ys_ref, h2_ref, x1_ref, gate_ref, g2_ref, swg_ref, swu_ref, swd_ref,
                    fg_ref, out_ref, buf_a, buf_b, sem):
    i = pl.program_id(0)
    tt = x1_ref.shape[0] // 2
    src = _as_rows(ys_ref)

    def gather(p_ref, col0, buf, s):
        dst = _as_rows(buf)
        for t in range(tt):
            for k in range(TOP_K):
                pltpu.make_async_copy(src.at[p_ref[k, col0 + t]], dst.at[k * tt + t],
                                      sem.at[s]).start(priority=k % 2)

    @pl.when(i == 0)
    def _():
        dst = _as_rows(buf_a)

        def start(t, c):
            for k in range(TOP_K):
                pltpu.make_async_copy(src.at[pos_ref[k, t]], dst.at[k * tt + t], sem.at[0]).start(priority=k % 2)
            return c

        lax.fori_loop(0, tt, start, 0)

    def tile(row0, buf, s, prefetch):
        tok = pl.ds(row0, tt)
        h2 = _load_rows(h2_ref, tt, first_row=row0).astype(BF16)
        hm = (_silu(_dot(h2, swg_ref[...])) * _dot(h2, swu_ref[...])).astype(BF16)
        _wait_rows(_as_rows(buf), tt * TOP_K, sem.at[s])
        prefetch()
        gate = gate_ref[tok, :]
        ssq = jnp.zeros((tt, 1), F32)
        for c in range(ROW_TILES):
            cols = slice(c * LANES, (c + 1) * LANES)
            acc = _dot(hm, swd_ref[:, cols])
            for k in range(TOP_K):
                acc = acc + gate[:, k:k + 1] * buf[pl.ds(k * tt * ROW_TILES + c, tt, stride=ROW_TILES), :]
            x2 = x1_ref[tok, cols] + g2_ref[0, :, cols] * acc
            out_ref[tok, cols] = x2
            ssq = ssq + jnp.sum(x2 * x2, axis=-1, keepdims=True)
        out_ref[tok, :] = out_ref[tok, :] * lax.rsqrt(ssq * (1.0 / D_MODEL) + EPS) * fg_ref[...]

    tile(0, buf_a, 0, lambda: gather(pos_ref, tt, buf_b, 1))
    tile(tt, buf_b, 1, lambda: gather(pos_next_ref, 0, buf_a, 0))

    @pl.when(i == pl.num_programs(0) - 1)
    def _():
        _wait_rows(_as_rows(buf_a), tt * TOP_K, sem.at[0])


def _combine(pos, ys, h2_tm, x1, gate_tm, mod3, swg, swu, swd, fg, B, S, tt):
    T, D = x1.shape
    gather_rows = tt * TOP_K * ROW_TILES
    tt = 2 * tt
    nS = S // tt
    const2 = lambda i: (0, 0)
    n_tiles = T // tt
    return pl.pallas_call(
        _combine_kernel,
        out_shape=jax.ShapeDtypeStruct((T, D), F32),
        grid=(n_tiles,),
        in_specs=[pl.BlockSpec((TOP_K, tt), lambda i: (0, i), memory_space=pltpu.SMEM),
                  pl.BlockSpec((TOP_K, tt), lambda i: (0, jnp.minimum(i + 1, n_tiles - 1)),
                               memory_space=pltpu.SMEM),
                  pl.BlockSpec(memory_space=pl.ANY),
                  pl.BlockSpec((tt * ROW_TILES, LANES), lambda i: (i, 0)),
                  pl.BlockSpec((tt, D), lambda i: (i, 0)),
                  pl.BlockSpec((tt, TOP_K), lambda i: (i, 0)),
                  pl.BlockSpec((1, 1, D), lambda i: (i // nS, 0, 5)),
                  pl.BlockSpec(swg.shape, const2),
                  pl.BlockSpec(swu.shape, const2),
                  pl.BlockSpec(swd.shape, const2),
                  pl.BlockSpec((1, D), const2)],
        out_specs=pl.BlockSpec((tt, D), lambda i: (i, 0)),
        scratch_shapes=[pltpu.VMEM((gather_rows, LANES), F32),
                        pltpu.VMEM((gather_rows, LANES), F32),
                        pltpu.SemaphoreType.DMA((2,))],
        compiler_params=pltpu.CompilerParams(
            dimension_semantics=("arbitrary",), vmem_limit_bytes=VMEM_LIMIT),
        name="combine",
    )(pos, pos, ys, h2_tm, x1, gate_tm, mod3, swg, swu, swd, fg)


def _place_kernel(idx_ref, rank_ref, off_ref, pos_ref):
    tl = idx_ref.shape[1]
    rowid = lax.broadcasted_iota(jnp.int32, (N_EXPERTS, tl), 0)
    off = off_ref[...].astype(F32)
    for k in range(TOP_K):
        base = jnp.sum(jnp.where(rowid == idx_ref[k:k + 1, :], off, 0.0), axis=0, keepdims=True)
        pos_ref[k:k + 1, :] = base.astype(jnp.int32) + rank_ref[k:k + 1, :]


def _place(idx, rank, off, tl):
    T = idx.shape[1]
    tok = pl.BlockSpec((TOP_K, tl), lambda i: (0, i))
    return pl.pallas_call(
        _place_kernel,
        out_shape=jax.ShapeDtypeStruct((TOP_K, T), jnp.int32),
        grid=(T // tl,),
        in_specs=[tok, tok, pl.BlockSpec((N_EXPERTS, 1), lambda i: (0, 0))],
        out_specs=tok,
        name="place",
    )(idx, rank, off)


def _plan(counts, n_items_max):
    counts = counts[:, 0]
    off = jnp.cumsum(counts) - counts
    n_e = (counts + EXPERT_TILE - 1) // EXPERT_TILE
    item_end = jnp.cumsum(n_e)
    item_start = item_end - n_e
    n_items = item_end[-1]
    ids = jnp.arange(n_items_max + 1, dtype=jnp.int32)
    ids_c = jnp.minimum(ids, n_items - 1)
    item_e = jnp.minimum(jnp.sum(item_end[None, :] <= ids_c[:, None], axis=1), N_EXPERTS - 1).astype(jnp.int32)
    ids_e = jnp.arange(N_EXPERTS, dtype=jnp.int32)
    of_item = item_e[:, None] == ids_e[None, :]

    def per_item(table):
        return jnp.sum(jnp.where(of_item, table[None, :], 0), axis=1).astype(jnp.int32)

    first_row = (ids_c - per_item(item_start)) * EXPERT_TILE
    item_row0 = per_item(off) + first_row
    item_valid = jnp.minimum(per_item(counts) - first_row, EXPERT_TILE).astype(jnp.int32)
    prev_e = jnp.concatenate([jnp.full((1,), -1, jnp.int32), item_e[:-1]])
    item_newe = (item_e != prev_e).astype(jnp.int32)
    item_slot = ((jnp.cumsum(item_newe) - 1) % 2).astype(jnp.int32)
    later = jnp.where((counts[None, :] > 0) & (ids_e[None, :] > ids_e[:, None]), ids_e[None, :], N_EXPERTS)
    next_e = jnp.min(later, axis=1)
    next_e = jnp.where(next_e < N_EXPERTS, next_e, -1).astype(jnp.int32)
    meta = (item_e, item_row0, item_valid, item_newe, n_items.reshape(1).astype(jnp.int32),
            item_slot, per_item(next_e))
    return off.astype(jnp.int32).reshape(N_EXPERTS, 1), meta


def kernel(x, c, ada_w, ada_b, norm1_g, w_in, hgrn_lb, hgrn_norm_g, pool_w, pool_b, pool_scale, w_up_a, w_up_b, w_out, norm2_g, router_w, router_bias, exp_w_gate, exp_w_up, exp_w_down, shared_w_gate, shared_w_up, shared_w_down, final_norm_g):
    B, S, D = x.shape
    T = B * S
    assert ada_w.shape[0] == 1, "single-layer trunk only: the final norm is fused into the combine step"
    lb_all = jnp.cumsum(jax.nn.softmax(hgrn_lb.astype(F32), axis=0), axis=0)
    c_pad = jnp.zeros((SUBLANES, D), F32).at[:B].set(c)
    n_items_max = T * TOP_K // EXPERT_TILE + N_EXPERTS

    for l in range(1):
        mod = _ada(c_pad, ada_w[l], ada_b[l].reshape(1, -1))
        mod3 = mod[:B].reshape(B, 1, 6 * D)

        q, lf, k, v, sog, pm, sga, sgb = _inproj(
            x, mod3, norm1_g[l].reshape(1, D), w_in[l].astype(BF16), lb_all[l].reshape(1, HG_WIDTH),
            pool_w[l].astype(BF16), pool_b[l].reshape(len(POOL_WINDOWS), 1, POOL_GROUP),
            pool_scale[l].reshape(1, POOL_WIDTH), tm=256)
        oa = _hgrn(q, lf, k, v, sog, hgrn_norm_g[l].reshape(1, HG_DK), B, S, tb=512)

        rw_hi, rw_lo = _split_bf16(router_w[l].T)
        x1, h2_tm, logits_t = _mix(
            x, oa, pm, sga, sgb, mod3, norm2_g[l].reshape(1, D), w_up_a[l].astype(BF16),
            w_up_b[l].astype(BF16), w_out[l].astype(BF16), rw_hi, rw_lo, tm=512)

        idx, gate, rank, counts = _route(logits_t, router_bias[l].reshape(N_EXPERTS, 1), tl=256)
        off, meta = _plan(counts, n_items_max)
        pos = _place(idx, rank, off, tl=512)

        xs = _scatter(pos, h2_tm, tt=256)
        ys = _experts(meta, xs, exp_w_gate[l], exp_w_up[l], exp_w_down[l], n_items_max)

        tt_c = 128
        fg = final_norm_g.reshape(1, D)
        x = _combine(pos, ys, h2_tm, x1, gate.T, mod3,
                     shared_w_gate[l].astype(BF16), shared_w_up[l].astype(BF16),
                     shared_w_down[l].astype(BF16), fg, B, S, tt_c).reshape(B, S, D)
    return x
```

```python
import functools

import jax
import jax.numpy as jnp
from jax import lax
from jax.experimental import pallas as pl
from jax.experimental.pallas import tpu as pltpu

F32 = jnp.float32
BF16 = jnp.bfloat16
HIGHEST = lax.Precision.HIGHEST

D_MODEL = 1024
HG_WIDTH = 512
HG_DK = 128
HG_HEADS = 4
HG_CHUNK = 64
HG_BLK = 16
HG_SUB = 8
POOL_WIDTH = 512
POOL_WINDOWS = (2, 4, 8, 16)
POOL_GROUP = 128
POOL_HALO = 16
N_EXPERTS = 256
TOP_K = 8
N_GROUPS = 8
TOPK_GROUPS = 4
GROUP_SIZE = N_EXPERTS // N_GROUPS
D_EXPERT = 256
ROUTED_SCALE = 2.5
EPS = 1e-6

LANES = 128
SUBLANES = 8
ROW_TILES = D_MODEL // LANES
EXPERT_TILE = 128
TILE_RING = 8
TILE_AHEAD = TILE_RING - 1
VMEM_LIMIT = 56 * 1024 * 1024

COL_Q, COL_F, COL_I, COL_OG, COL_U, COL_GA, COL_GB = 0, 512, 1024, 1536, 2048, 2560, 3584


def _sigmoid(x):
    return 1.0 / (1.0 + jnp.exp(-x))


def _silu(x):
    return x * _sigmoid(x)


def _dot(a, b):
    return jnp.dot(a, b, preferred_element_type=F32)


def _dot_nt(a, b):
    return lax.dot_general(a, b, (((1,), (1,)), ((), ())), preferred_element_type=F32)


def _dot_tn(a, b):
    return lax.dot_general(a, b, (((0,), (0,)), ((), ())), preferred_element_type=F32)


def _row_chunks(x):
    return [x[:, j * LANES:(j + 1) * LANES] for j in range(ROW_TILES)]


def _load_rows(ref, n, first_row=0):
    return jnp.concatenate(
        [ref[pl.ds(first_row * ROW_TILES + j, n, stride=ROW_TILES), :] for j in range(ROW_TILES)], axis=1)


def _ada_kernel(c_ref, w_ref, b_ref, o_ref):
    cond = _silu(c_ref[...])
    o_ref[...] = jnp.dot(cond, w_ref[...], precision=HIGHEST, preferred_element_type=F32) + b_ref[...]


def _ada(c_pad, ada_w, ada_b):
    n = ada_w.shape[1]
    tn = 1536
    return pl.pallas_call(
        _ada_kernel,
        out_shape=jax.ShapeDtypeStruct((SUBLANES, n), F32),
        grid=(n // tn,),
        in_specs=[pl.BlockSpec((SUBLANES, D_MODEL), lambda j: (0, 0)),
                  pl.BlockSpec((D_MODEL, tn), lambda j: (0, j)),
                  pl.BlockSpec((1, tn), lambda j: (0, j))],
        out_specs=pl.BlockSpec((SUBLANES, tn), lambda j: (0, j)),
        compiler_params=pltpu.CompilerParams(vmem_limit_bytes=VMEM_LIMIT),
        name="ada",
    )(c_pad, ada_w, ada_b)


def _inproj_kernel(x_ref, sh_ref, sc_ref, g_ref, w_ref, lb_ref, pw_ref, pb_ref, ps_ref,
                   q_ref, lf_ref, k_ref, v_ref, sog_ref, pm_ref, sga_ref, sgb_ref, halo_ref):
    s = pl.program_id(1)
    tm = x_ref.shape[1]
    x = x_ref[0]
    h = x * lax.rsqrt(jnp.mean(x * x, axis=-1, keepdims=True) + EPS) * g_ref[...]
    h = h * (1.0 + sc_ref[0]) + sh_ref[0]
    hb = h.astype(BF16)

    def proj(lo, n):
        return _dot(hb, w_ref[:, lo:lo + n])

    q = proj(COL_Q, HG_WIDTH)
    q_ref[...] = _silu(q) * (HG_DK ** -0.5)
    sig = _sigmoid(proj(COL_F, HG_WIDTH))
    lb = lb_ref[...]
    lf_ref[...] = jnp.log(lb + (1.0 - lb) * sig)
    k_ref[...] = (1.0 - lb) * (1.0 - sig)
    v_ref[...] = proj(COL_I, HG_WIDTH)
    sog_ref[...] = _silu(proj(COL_OG, HG_WIDTH)).astype(BF16)
    sga_ref[...] = _sigmoid(proj(COL_GA, D_MODEL)).astype(BF16)
    sgb_ref[...] = _sigmoid(proj(COL_GB, D_MODEL)).astype(BF16)

    u = proj(COL_U, POOL_WIDTH)
    @pl.when(s == 0)
    def _():
        halo_ref[...] = jnp.zeros_like(halo_ref)

    ext = jnp.concatenate([halo_ref[...], u], axis=0)
    halo_ref[...] = u[tm - POOL_HALO:, :]
    s2 = ext + pltpu.roll(ext, 1, 0)
    s4 = s2 + pltpu.roll(s2, 2, 0)
    s8 = s4 + pltpu.roll(s4, 4, 0)
    s16 = s8 + pltpu.roll(s8, 8, 0)
    pos1 = (s * tm + 1 + lax.broadcasted_iota(jnp.int32, (tm, 1), 0)).astype(F32)
    for g, (w, sw) in enumerate(zip(POOL_WINDOWS, (s2, s4, s8, s16))):
        cols = slice(g * POOL_GROUP, (g + 1) * POOL_GROUP)
        m = sw[POOL_HALO:, cols] / jnp.minimum(pos1, float(w)) - u[:, cols]
        y = _dot(m.astype(BF16), pw_ref[g]) + pb_ref[g]
        pm_ref[:, cols] = (y * ps_ref[:, cols]).astype(BF16)


def _inproj(x, mod3, norm_g, w_in_b, lb, pool_w_b, pool_b, pool_scale, tm):
    B, S, D = x.shape
    T = B * S
    nS = S // tm
    row = lambda b, s: (b * nS + s, 0)
    const2 = lambda b, s: (0, 0)
    const3 = lambda b, s: (0, 0, 0)
    half = lambda dt: jax.ShapeDtypeStruct((T, HG_WIDTH), dt)
    full = lambda dt: jax.ShapeDtypeStruct((T, D), dt)
    return pl.pallas_call(
        _inproj_kernel,
        out_shape=(half(F32), half(F32), half(F32), half(F32), half(BF16), half(BF16), full(BF16), full(BF16)),
        grid=(B, nS),
        in_specs=[pl.BlockSpec((1, tm, D), lambda b, s: (b, s, 0)),
                  pl.BlockSpec((1, 1, D), lambda b, s: (b, 0, 0)),
                  pl.BlockSpec((1, 1, D), lambda b, s: (b, 0, 1)),
                  pl.BlockSpec((1, D), const2),
                  pl.BlockSpec(w_in_b.shape, const2),
                  pl.BlockSpec((1, HG_WIDTH), const2),
                  pl.BlockSpec(pool_w_b.shape, const3),
                  pl.BlockSpec(pool_b.shape, const3),
                  pl.BlockSpec((1, POOL_WIDTH), const2)],
        out_specs=(pl.BlockSpec((tm, HG_WIDTH), row),) * 6 + (pl.BlockSpec((tm, D), row),) * 2,
        scratch_shapes=[pltpu.VMEM((POOL_HALO, POOL_WIDTH), F32)],
        compiler_params=pltpu.CompilerParams(
            dimension_semantics=("arbitrary", "arbitrary"), vmem_limit_bytes=VMEM_LIMIT),
        name="inproj",
    )(x, mod3, mod3, norm_g, w_in_b, lb, pool_w_b, pool_b, pool_scale)


def _hgrn_kernel(q_ref, lf_ref, k_ref, v_ref, sog_ref, gn_ref, o_ref, *st_refs):
    C = HG_CHUNK
    n_chunks = q_ref.shape[0] // C

    @pl.when(pl.program_id(1) == 0)
    def _():
        for st_ref in st_refs:
            st_ref[...] = jnp.zeros_like(st_ref)

    r_i = lax.broadcasted_iota(jnp.int32, (C, C), 0)
    c_i = lax.broadcasted_iota(jnp.int32, (C, C), 1)
    tril = (c_i <= r_i).astype(BF16)
    same_blk = (r_i // HG_BLK) == (c_i // HG_BLK)
    row = lax.broadcasted_iota(jnp.int32, (C, HG_DK), 0)
    row_in_sub = row % HG_SUB
    upper_half = (row % HG_BLK) >= HG_SUB
    row_blk = row // HG_BLK
    n_blk = C // HG_BLK

    def cumsum_rows(x):
        hi = x.astype(BF16)
        r1 = x - hi.astype(F32)
        mid = r1.astype(BF16)
        lo = (r1 - mid.astype(F32)).astype(BF16)
        return _dot(tril, hi) + _dot(tril, mid) + _dot(tril, lo)

    def block_rows(x, size, which):
        pieces = []
        for g in range(C // size):
            src = g * size + which
            pieces.append(jnp.zeros((size, x.shape[1]), F32) if src < 0
                          else jnp.broadcast_to(x[src:src + 1, :], (size, x.shape[1])))
        return jnp.concatenate(pieces, axis=0)

    def chunk(ci, carry):
        rs = pl.ds(pl.multiple_of(ci * C, C), C)
        b_all = cumsum_rows(lf_ref[rs, :])
        for h in range(HG_HEADS):
            cs = slice(h * HG_DK, (h + 1) * HG_DK)
            q = q_ref[rs, cs]
            k = k_ref[rs, cs]
            v = v_ref[rs, cs]
            b = b_all[:, cs]
            vb = v.astype(BF16)

            kt = k * jnp.exp(block_rows(b, HG_BLK, HG_BLK - 1) - b)
            q_parts, k_parts = [], []
            for j in range(n_blk - 1):
                bj = b[HG_BLK * j + HG_BLK - 1:HG_BLK * (j + 1), :]
                after = row >= HG_BLK * (j + 1)
                q_parts.append(q * jnp.exp(jnp.where(after, b - bj, -jnp.inf)))
                k_parts.append(jnp.where(row_blk == j, kt, 0.0))
            qcat = jnp.concatenate(q_parts, axis=1).astype(BF16)
            kcat = jnp.concatenate(k_parts, axis=1).astype(BF16)
            scores = _dot_nt(qcat, kcat)
            b_prev = block_rows(b, HG_SUB, -1)
            b_sub = block_rows(b, HG_SUB, HG_SUB - 1)
            qh = (q * jnp.exp(jnp.where(upper_half, b - b_prev, -jnp.inf))).astype(BF16)
            kh = jnp.where(upper_half, 0.0, k * jnp.exp(b_sub - b)).astype(BF16)
            scores = scores + jnp.where(same_blk, _dot_nt(qh, kh), 0.0)
            o = _dot(scores.astype(BF16), vb)

            o = o + jnp.sum(q * k, axis=-1, keepdims=True) * v
            for d in range(1, HG_SUB):
                kd = pltpu.roll(k, d, 0)
                bd = pltpu.roll(b, d, 0)
                vd = pltpu.roll(v, d, 0)
                e = jnp.exp(jnp.where(row_in_sub >= d, b - bd, -jnp.inf))
                o = o + jnp.sum(q * kd * e, axis=-1, keepdims=True) * vd

            st = st_refs[h][...]
            o = o + _dot_nt((q * jnp.exp(b)).astype(BF16), st.astype(BF16))
            b_end = b[C - 1:C, :]
            k_end = (k * jnp.exp(b_end - b)).astype(BF16)
            st_refs[h][...] = st * jnp.exp(b_end) + _dot_tn(vb, k_end)

            on = o * lax.rsqrt(jnp.mean(o * o, axis=-1, keepdims=True) + EPS) * gn_ref[...]
            o_ref[rs, cs] = (on * sog_ref[rs, cs].astype(F32)).astype(BF16)
        return carry

    lax.fori_loop(0, n_chunks, chunk, 0)


def _hgrn(q, lf, k, v, sog, gn, B, S, tb):
    T = B * S
    nS = S // tb
    row = lambda b, s: (b * nS + s, 0)
    blk = pl.BlockSpec((tb, HG_WIDTH), row)
    return pl.pallas_call(
        _hgrn_kernel,
        out_shape=jax.ShapeDtypeStruct((T, HG_WIDTH), BF16),
        grid=(B, nS),
        in_specs=[blk, blk, blk, blk, blk, pl.BlockSpec((1, HG_DK), lambda b, s: (0, 0))],
        out_specs=blk,
        scratch_shapes=[pltpu.VMEM((HG_DK, HG_DK), F32)] * HG_HEADS,
        compiler_params=pltpu.CompilerParams(
            dimension_semantics=("arbitrary", "arbitrary"), vmem_limit_bytes=VMEM_LIMIT),
        name="hgrn",
    )(q, lf, k, v, sog, gn)


def _split_kernel(w_ref, hi_ref, lo_ref):
    w = w_ref[...]
    hi = w.astype(BF16)
    hi_ref[...] = hi
    lo_ref[...] = (w - hi.astype(F32)).astype(BF16)


def _split_bf16(w):
    out = jax.ShapeDtypeStruct(w.shape, BF16)
    return pl.pallas_call(_split_kernel, out_shape=(out, out), name="split")(w)


def _mix_kernel(x_ref, oa_ref, pm_ref, sga_ref, sgb_ref, g1_ref, sh2_ref, sc2_ref, n2_ref,
                wua_ref, wub_ref, wo_ref, rw_hi_ref, rw_lo_ref, x1_ref, h2_ref, lg_ref):
    tm = x_ref.shape[1]
    ya = _dot(oa_ref[...], wua_ref[...])
    yb = _dot(pm_ref[...], wub_ref[...])
    mix = sga_ref[...].astype(F32) * ya + sgb_ref[...].astype(F32) * yb
    x1 = x_ref[0] + g1_ref[0] * _dot(mix.astype(BF16), wo_ref[...])
    x1_ref[...] = x1
    h2 = x1 * lax.rsqrt(jnp.mean(x1 * x1, axis=-1, keepdims=True) + EPS) * n2_ref[...]
    h2 = h2 * (1.0 + sc2_ref[0]) + sh2_ref[0]
    for j, chunk in enumerate(_row_chunks(h2)):
        h2_ref[pl.ds(j, tm, stride=ROW_TILES), :] = chunk
    h_hi = h2.astype(BF16)
    h_lo = (h2 - h_hi.astype(F32)).astype(BF16)
    rw_hi = rw_hi_ref[...]
    lg_ref[...] = _dot_nt(rw_hi, h_hi) + _dot_nt(rw_hi, h_lo) + _dot_nt(rw_lo_ref[...], h_hi)


def _mix(x, oa, pm, sga, sgb, mod3, norm2_g, wua, wub, wo, rw_hi, rw_lo, tm):
    B, S, D = x.shape
    T = B * S
    nS = S // tm
    row = lambda b, s: (b * nS + s, 0)
    const2 = lambda b, s: (0, 0)
    return pl.pallas_call(
        _mix_kernel,
        out_shape=(jax.ShapeDtypeStruct((T, D), F32),
                   jax.ShapeDtypeStruct((T * ROW_TILES, LANES), F32),
                   jax.ShapeDtypeStruct((N_EXPERTS, T), F32)),
        grid=(B, nS),
        in_specs=[pl.BlockSpec((1, tm, D), lambda b, s: (b, s, 0)),
                  pl.BlockSpec((tm, HG_WIDTH), row),
                  pl.BlockSpec((tm, POOL_WIDTH), row),
                  pl.BlockSpec((tm, D), row),
                  pl.BlockSpec((tm, D), row),
                  pl.BlockSpec((1, 1, D), lambda b, s: (b, 0, 2)),
                  pl.BlockSpec((1, 1, D), lambda b, s: (b, 0, 3)),
                  pl.BlockSpec((1, 1, D), lambda b, s: (b, 0, 4)),
                  pl.BlockSpec((1, D), const2),
                  pl.BlockSpec(wua.shape, const2),
                  pl.BlockSpec(wub.shape, const2),
                  pl.BlockSpec(wo.shape, const2),
                  pl.BlockSpec(rw_hi.shape, const2),
                  pl.BlockSpec(rw_lo.shape, const2)],
        out_specs=(pl.BlockSpec((tm, D), row),
                   pl.BlockSpec((tm * ROW_TILES, LANES), row),
                   pl.BlockSpec((N_EXPERTS, tm), lambda b, s: (0, b * nS + s))),
        compiler_params=pltpu.CompilerParams(
            dimension_semantics=("arbitrary", "arbitrary"), vmem_limit_bytes=VMEM_LIMIT),
        name="mix",
    )(x, oa, pm, sga, sgb, mod3, mod3, mod3, norm2_g, wua, wub, wo, rw_hi, rw_lo)


def _route_kernel(lg_ref, bias_ref, idx_ref, gate_ref, rank_ref, cnt_ref, carry_ref):
    tl = lg_ref.shape[1]
    neg = -jnp.inf

    @pl.when(pl.program_id(0) == 0)
    def _():
        carry_ref[...] = jnp.zeros_like(carry_ref)

    s = _sigmoid(lg_ref[...])
    biased = s + bias_ref[...]
    rowid = lax.broadcasted_iota(jnp.int32, (N_EXPERTS, tl), 0)

    def first_argmax(x, ids, sentinel):
        m = jnp.max(x, axis=0, keepdims=True)
        return jnp.min(jnp.where(x == m, ids, sentinel), axis=0, keepdims=True), m

    gscores = []
    for g in range(N_GROUPS):
        xg = biased[g * GROUP_SIZE:(g + 1) * GROUP_SIZE, :]
        rid = g * GROUP_SIZE + lax.broadcasted_iota(jnp.int32, (GROUP_SIZE, tl), 0)
        first, m1 = first_argmax(xg, rid, N_EXPERTS)
        m2 = jnp.max(jnp.where(rid == first, neg, xg), axis=0, keepdims=True)
        gscores.append(m1 + m2)
    blocks = []
    for g in range(N_GROUPS):
        beaten = jnp.zeros((1, tl), F32)
        for o in range(N_GROUPS):
            if o != g:
                wins = (gscores[o] >= gscores[g]) if o < g else (gscores[o] > gscores[g])
                beaten = beaten + jnp.where(wins, 1.0, 0.0)
        xg = biased[g * GROUP_SIZE:(g + 1) * GROUP_SIZE, :]
        blocks.append(jnp.where(beaten < float(TOPK_GROUPS), xg, neg))
    masked = jnp.concatenate(blocks, axis=0)

    idxs, gates = [], []
    chosen = jnp.zeros((N_EXPERTS, tl), F32)
    for _ in range(TOP_K):
        first, _m = first_argmax(masked, rowid, N_EXPERTS)
        sel = rowid == first
        gates.append(jnp.sum(jnp.where(sel, s, 0.0), axis=0, keepdims=True))
        idxs.append(first)
        chosen = jnp.where(sel, 1.0, chosen)
        masked = jnp.where(sel, neg, masked)
    gate_sum = functools.reduce(lambda a, b: a + b, gates)
    for k in range(TOP_K):
        gate_ref[k:k + 1, :] = gates[k] / gate_sum * ROUTED_SCALE
        idx_ref[k:k + 1, :] = idxs[k]

    lr = lax.broadcasted_iota(jnp.int32, (tl, tl), 0)
    lc = lax.broadcasted_iota(jnp.int32, (tl, tl), 1)
    prefix = (lr <= lc).astype(BF16)
    cnt_incl = _dot(chosen.astype(BF16), prefix)
    carry = carry_ref[...]
    rank_excl = cnt_incl - chosen + carry
    for k in range(TOP_K):
        rank_k = jnp.sum(jnp.where(rowid == idxs[k], rank_excl, 0.0), axis=0, keepdims=True)
        rank_ref[k:k + 1, :] = rank_k.astype(jnp.int32)
    carry = carry + jnp.sum(chosen, axis=1, keepdims=True)
    carry_ref[...] = carry
    cnt_ref[...] = carry.astype(jnp.int32)


def _route(logits_t, bias, tl):
    T = logits_t.shape[1]
    tok = lambda i: (0, i)
    return pl.pallas_call(
        _route_kernel,
        out_shape=(jax.ShapeDtypeStruct((TOP_K, T), jnp.int32),
                   jax.ShapeDtypeStruct((TOP_K, T), F32),
                   jax.ShapeDtypeStruct((TOP_K, T), jnp.int32),
                   jax.ShapeDtypeStruct((N_EXPERTS, 1), jnp.int32)),
        grid=(T // tl,),
        in_specs=[pl.BlockSpec((N_EXPERTS, tl), tok), pl.BlockSpec((N_EXPERTS, 1), lambda i: (0, 0))],
        out_specs=(pl.BlockSpec((TOP_K, tl), tok), pl.BlockSpec((TOP_K, tl), tok),
                   pl.BlockSpec((TOP_K, tl), tok), pl.BlockSpec((N_EXPERTS, 1), lambda i: (0, 0))),
        scratch_shapes=[pltpu.VMEM((N_EXPERTS, 1), F32)],
        compiler_params=pltpu.CompilerParams(
            dimension_semantics=("arbitrary",), vmem_limit_bytes=VMEM_LIMIT),
        name="route",
    )(logits_t, bias)


def _as_rows(ref):
    return ref.reshape(ref.shape[0] // ROW_TILES, ROW_TILES, LANES)


def _wait_rows(rows_ref, n, sem):
    pltpu.make_async_copy(rows_ref.at[pl.ds(0, n)], rows_ref.at[pl.ds(0, n)], sem).wait()


def _scatter_kernel(pos_ref, h2_ref, xs_ref, zero_ref, sem, zsem):
    src = _as_rows(h2_ref)
    dst = _as_rows(xs_ref)
    tt = src.shape[0]

    @pl.when(pl.program_id(0) == 0)
    def _():
        zero_ref[...] = jnp.zeros_like(zero_ref)
        tail = xs_ref.at[pl.ds(xs_ref.shape[0] - zero_ref.shape[0], zero_ref.shape[0])]
        fill = pltpu.make_async_copy(zero_ref, tail, zsem)
        fill.start()
        fill.wait()

    def start(t, c):
        for k in range(TOP_K):
            pltpu.make_async_copy(src.at[t], dst.at[pos_ref[k, t]], sem).start(priority=k % 2)
        return c

    lax.fori_loop(0, tt, start, 0)
    _wait_rows(dst, tt * TOP_K, sem)


def _scatter(pos, h2_tm, tt):
    n_rows = h2_tm.shape[0] // ROW_TILES * TOP_K + EXPERT_TILE
    return pl.pallas_call(
        _scatter_kernel,
        out_shape=jax.ShapeDtypeStruct((n_rows * ROW_TILES, LANES), F32),
        grid=(pos.shape[1] // tt,),
        in_specs=[pl.BlockSpec((TOP_K, tt), lambda i: (0, i), memory_space=pltpu.SMEM),
                  pl.BlockSpec((tt * ROW_TILES, LANES), lambda i: (i, 0))],
        out_specs=pl.BlockSpec(memory_space=pl.ANY),
        scratch_shapes=[pltpu.VMEM((EXPERT_TILE * ROW_TILES, LANES), F32),
                        pltpu.SemaphoreType.DMA, pltpu.SemaphoreType.DMA],
        compiler_params=pltpu.CompilerParams(
            dimension_semantics=("arbitrary",), vmem_limit_bytes=VMEM_LIMIT),
        name="scatter",
    )(pos, h2_tm)


def _experts_kernel(exp_ref, row0_ref, valid_ref, newe_ref, nitems_ref, slot_ref, nexte_ref,
                    xs_hbm, wg_hbm, wu_hbm, wd_hbm, ys_hbm,
                    xbuf_ref, ybuf_ref, sg_ref, su_ref, sd_ref, wgb_ref, wub_ref, wdb_ref, hm_ref,
                    xsem, ysem, wsem):
    i = pl.program_id(0)
    tr = EXPERT_TILE
    tile_rows = tr * ROW_TILES
    n_items = nitems_ref[0]
    part_sizes = tuple(tr >> (b + 1) for b in range(tr.bit_length() - 1))

    def ring(item, first_row=0, n_rows=tr):
        start = ((item % TILE_RING) * tr + first_row) * ROW_TILES
        return pl.ds(pl.multiple_of(start, ROW_TILES), n_rows * ROW_TILES)

    def hbm_rows(first_row, n_rows=tr):
        return pl.ds(pl.multiple_of(first_row * ROW_TILES, ROW_TILES), n_rows * ROW_TILES)

    def x_copy(item):
        return pltpu.make_async_copy(xs_hbm.at[hbm_rows(row0_ref[item])], xbuf_ref.at[ring(item)],
                                     xsem.at[item % TILE_RING])

    def y_copies(item, go):
        v = valid_ref[item]
        sem = ysem.at[item % TILE_RING]

        @pl.when(v == tr)
        def _():
            go(pltpu.make_async_copy(ybuf_ref.at[ring(item)], ys_hbm.at[hbm_rows(row0_ref[item])], sem))

        @pl.when(v != tr)
        def _():
            for size in part_sizes:
                @pl.when((v & size) != 0)
                def _():
                    first = v & ~(2 * size - 1)
                    go(pltpu.make_async_copy(ybuf_ref.at[ring(item, first, size)],
                                             ys_hbm.at[hbm_rows(row0_ref[item] + first, size)], sem))

    def weight_copies(e, slot):
        return (pltpu.make_async_copy(wg_hbm.at[e], sg_ref.at[slot], wsem.at[slot]),
                pltpu.make_async_copy(wu_hbm.at[e], su_ref.at[slot], wsem.at[slot]),
                pltpu.make_async_copy(wd_hbm.at[e], sd_ref.at[slot], wsem.at[slot]))

    a_on = i < n_items
    j = jnp.maximum(i - 1, 0)
    b_on = (i >= 1) & (i - 1 < n_items)
    e = exp_ref[i]

    @pl.when(i == 0)
    def _():
        hm_ref[...] = jnp.zeros_like(hm_ref)
        for i0 in range(TILE_AHEAD):
            @pl.when(i0 < n_items)
            def _():
                x_copy(i0).start()

    @pl.when(a_on)
    def _():
        @pl.when(i + TILE_AHEAD < n_items)
        def _():
            x_copy(i + TILE_AHEAD).start()

        x_copy(i).wait()

    @pl.when(b_on & (j >= TILE_RING))
    def _():
        y_copies(j - TILE_RING, lambda c: c.wait())

    @pl.when(a_on & (newe_ref[i] == 1))
    def _():
        slot = slot_ref[i]
        nxt = nexte_ref[i]

        @pl.when(i == 0)
        def _():
            for c in weight_copies(e, slot):
                c.start()

        @pl.when(nxt >= 0)
        def _():
            for c in weight_copies(nxt, 1 - slot):
                c.start()

        for c in weight_copies(e, slot):
            c.wait()
        wgb_ref[...] = sg_ref[slot].astype(BF16)
        wub_ref[...] = su_ref[slot].astype(BF16)
        wdb_ref[slot] = sd_ref[slot].astype(BF16)

    @pl.when(i <= n_items)
    def _():
        out_row = (j % TILE_RING) * tile_rows
        for c, chunk in enumerate(_row_chunks(_dot(hm_ref[j % 2], wdb_ref[slot_ref[j]]))):
            ybuf_ref[pl.ds(out_row + c, tr, stride=ROW_TILES), :] = chunk

        x = _load_rows(xbuf_ref, tr, first_row=(i % TILE_RING) * tr).astype(BF16)
        hm_ref[i % 2] = (_silu(_dot(x, wgb_ref[...])) * _dot(x, wub_ref[...])).astype(BF16)

    @pl.when(b_on)
    def _():
        y_copies(j, lambda c: c.start())

    @pl.when(b_on & (j == n_items - 1))
    def _():
        for back in range(TILE_RING):
            @pl.when(j - back >= 0)
            def _():
                y_copies(j - back, lambda c: c.wait())


def _experts(meta, xs, wg, wu, wd, n_items_max):
    tile_rows = EXPERT_TILE * ROW_TILES
    out_rows = xs.shape[0] - tile_rows
    hbm = pl.BlockSpec(memory_space=pl.ANY)
    n_slots = 2
    grid_spec = pltpu.PrefetchScalarGridSpec(
        num_scalar_prefetch=len(meta),
        grid=(n_items_max + 1,),
        in_specs=[hbm, hbm, hbm, hbm],
        out_specs=hbm,
        scratch_shapes=[pltpu.VMEM((TILE_RING * tile_rows, LANES), F32),
                        pltpu.VMEM((TILE_RING * tile_rows, LANES), F32),
                        pltpu.VMEM((n_slots, D_MODEL, D_EXPERT), F32),
                        pltpu.VMEM((n_slots, D_MODEL, D_EXPERT), F32),
                        pltpu.VMEM((n_slots, D_EXPERT, D_MODEL), F32),
                        pltpu.VMEM((D_MODEL, D_EXPERT), BF16),
                        pltpu.VMEM((D_MODEL, D_EXPERT), BF16),
                        pltpu.VMEM((n_slots, D_EXPERT, D_MODEL), BF16),
                        pltpu.VMEM((2, EXPERT_TILE, D_EXPERT), BF16),
                        pltpu.SemaphoreType.DMA((TILE_RING,)),
                        pltpu.SemaphoreType.DMA((TILE_RING,)),
                        pltpu.SemaphoreType.DMA((n_slots,))])
    return pl.pallas_call(
        _experts_kernel,
        out_shape=jax.ShapeDtypeStruct((out_rows, LANES), F32),
        grid_spec=grid_spec,
        compiler_params=pltpu.CompilerParams(
            dimension_semantics=("arbitrary",), vmem_limit_bytes=VMEM_LIMIT),
        name="experts",
    )(*meta, xs, wg, wu, wd)


def _combine_kernel(pos_ref, pos_next_ref, ys_ref, h2_ref, x1_ref, gate_ref, g2_ref, swg_ref, swu_ref, swd_ref,
                    fg_ref, out_ref, buf_a, buf_b, sem):
    i = pl.program_id(0)
    tt = x1_ref.shape[0] // 2
    src = _as_rows(ys_ref)

    def gather(p_ref, col0, buf, s):
        dst = _as_rows(buf)
        for t in range(tt):
            for k in range(TOP_K):
                pltpu.make_async_copy(src.at[p_ref[k, col0 + t]], dst.at[k * tt + t],
                                      sem.at[s]).start(priority=k % 2)

    @pl.when(i == 0)
    def _():
        dst = _as_rows(buf_a)

        def start(t, c):
            for k in range(TOP_K):
                pltpu.make_async_copy(src.at[pos_ref[k, t]], dst.at[k * tt + t], sem.at[0]).start(priority=k % 2)
            return c

        lax.fori_loop(0, tt, start, 0)

    def tile(row0, buf, s, prefetch):
        tok = pl.ds(row0, tt)
        h2 = _load_rows(h2_ref, tt, first_row=row0).astype(BF16)
        hm = (_silu(_dot(h2, swg_ref[...])) * _dot(h2, swu_ref[...])).astype(BF16)
        _wait_rows(_as_rows(buf), tt * TOP_K, sem.at[s])
        prefetch()
        gate = gate_ref[tok, :]
        ssq = jnp.zeros((tt, 1), F32)
        for c in range(ROW_TILES):
            cols = slice(c * LANES, (c + 1) * LANES)
            acc = _dot(hm, swd_ref[:, cols])
            for k in range(TOP_K):
                acc = acc + gate[:, k:k + 1] * buf[pl.ds(k * tt * ROW_TILES + c, tt, stride=ROW_TILES), :]
            x2 = x1_ref[tok, cols] + g2_ref[0, :, cols] * acc
            out_ref[tok, cols] = x2
            ssq = ssq + jnp.sum(x2 * x2, axis=-1, keepdims=True)
        out_ref[tok, :] = out_ref[tok, :] * lax.rsqrt(ssq * (1.0 / D_MODEL) + EPS) * fg_ref[...]

    tile(0, buf_a, 0, lambda: gather(pos_ref, tt, buf_b, 1))
    tile(tt, buf_b, 1, lambda: gather(pos_next_ref, 0, buf_a, 0))

    @pl.when(i == pl.num_programs(0) - 1)
    def _():
        _wait_rows(_as_rows(buf_a), tt * TOP_K, sem.at[0])


def _combine(pos, ys, h2_tm, x1, gate_tm, mod3, swg, swu, swd, fg, B, S, tt):
    T, D = x1.shape
    gather_rows = tt * TOP_K * ROW_TILES
    tt = 2 * tt
    nS = S // tt
    const2 = lambda i: (0, 0)
    n_tiles = T // tt
    return pl.pallas_call(
        _combine_kernel,
        out_shape=jax.ShapeDtypeStruct((T, D), F32),
        grid=(n_tiles,),
        in_specs=[pl.BlockSpec((TOP_K, tt), lambda i: (0, i), memory_space=pltpu.SMEM),
                  pl.BlockSpec((TOP_K, tt), lambda i: (0, jnp.minimum(i + 1, n_tiles - 1)),
                               memory_space=pltpu.SMEM),
                  pl.BlockSpec(memory_space=pl.ANY),
                  pl.BlockSpec((tt * ROW_TILES, LANES), lambda i: (i, 0)),
                  pl.BlockSpec((tt, D), lambda i: (i, 0)),
                  pl.BlockSpec((tt, TOP_K), lambda i: (i, 0)),
                  pl.BlockSpec((1, 1, D), lambda i: (i // nS, 0, 5)),
                  pl.BlockSpec(swg.shape, const2),
                  pl.BlockSpec(swu.shape, const2),
                  pl.BlockSpec(swd.shape, const2),
                  pl.BlockSpec((1, D), const2)],
        out_specs=pl.BlockSpec((tt, D), lambda i: (i, 0)),
        scratch_shapes=[pltpu.VMEM((gather_rows, LANES), F32),
                        pltpu.VMEM((gather_rows, LANES), F32),
                        pltpu.SemaphoreType.DMA((2,))],
        compiler_params=pltpu.CompilerParams(
            dimension_semantics=("arbitrary",), vmem_limit_bytes=VMEM_LIMIT),
        name="combine",
    )(pos, pos, ys, h2_tm, x1, gate_tm, mod3, swg, swu, swd, fg)


def _place_kernel(idx_ref, rank_ref, off_ref, pos_ref):
    tl = idx_ref.shape[1]
    rowid = lax.broadcasted_iota(jnp.int32, (N_EXPERTS, tl), 0)
    off = off_ref[...].astype(F32)
    for k in range(TOP_K):
        base = jnp.sum(jnp.where(rowid == idx_ref[k:k + 1, :], off, 0.0), axis=0, keepdims=True)
        pos_ref[k:k + 1, :] = base.astype(jnp.int32) + rank_ref[k:k + 1, :]


def _place(idx, rank, off, tl):
    T = idx.shape[1]
    tok = pl.BlockSpec((TOP_K, tl), lambda i: (0, i))
    return pl.pallas_call(
        _place_kernel,
        out_shape=jax.ShapeDtypeStruct((TOP_K, T), jnp.int32),
        grid=(T // tl,),
        in_specs=[tok, tok, pl.BlockSpec((N_EXPERTS, 1), lambda i: (0, 0))],
        out_specs=tok,
        name="place",
    )(idx, rank, off)


def _plan(counts, n_items_max):
    counts = counts[:, 0]
    off = jnp.cumsum(counts) - counts
    n_e = (counts + EXPERT_TILE - 1) // EXPERT_TILE
    item_end = jnp.cumsum(n_e)
    item_start = item_end - n_e
    n_items = item_end[-1]
    ids = jnp.arange(n_items_max + 1, dtype=jnp.int32)
    ids_c = jnp.minimum(ids, n_items - 1)
    item_e = jnp.minimum(jnp.sum(item_end[None, :] <= ids_c[:, None], axis=1), N_EXPERTS - 1).astype(jnp.int32)
    ids_e = jnp.arange(N_EXPERTS, dtype=jnp.int32)
    of_item = item_e[:, None] == ids_e[None, :]

    def per_item(table):
        return jnp.sum(jnp.where(of_item, table[None, :], 0), axis=1).astype(jnp.int32)

    first_row = (ids_c - per_item(item_start)) * EXPERT_TILE
    item_row0 = per_item(off) + first_row
    item_valid = jnp.minimum(per_item(counts) - first_row, EXPERT_TILE).astype(jnp.int32)
    prev_e = jnp.concatenate([jnp.full((1,), -1, jnp.int32), item_e[:-1]])
    item_newe = (item_e != prev_e).astype(jnp.int32)
    item_slot = ((jnp.cumsum(item_newe) - 1) % 2).astype(jnp.int32)
    later = jnp.where((counts[None, :] > 0) & (ids_e[None, :] > ids_e[:, None]), ids_e[None, :], N_EXPERTS)
    next_e = jnp.min(later, axis=1)
    next_e = jnp.where(next_e < N_EXPERTS, next_e, -1).astype(jnp.int32)
    meta = (item_e, item_row0, item_valid, item_newe, n_items.reshape(1).astype(jnp.int32),
            item_slot, per_item(next_e))
    return off.astype(jnp.int32).reshape(N_EXPERTS, 1), meta


def kernel(x, c, ada_w, ada_b, norm1_g, w_in, hgrn_lb, hgrn_norm_g, pool_w, pool_b, pool_scale, w_up_a, w_up_b, w_out, norm2_g, router_w, router_bias, exp_w_gate, exp_w_up, exp_w_down, shared_w_gate, shared_w_up, shared_w_down, final_norm_g):
    B, S, D = x.shape
    T = B * S
    assert ada_w.shape[0] == 1, "single-layer trunk only: the final norm is fused into the combine step"
    lb_all = jnp.cumsum(jax.nn.softmax(hgrn_lb.astype(F32), axis=0), axis=0)
    c_pad = jnp.zeros((SUBLANES, D), F32).at[:B].set(c)
    n_items_max = T * TOP_K // EXPERT_TILE + N_EXPERTS

    for l in range(1):
        mod = _ada(c_pad, ada_w[l], ada_b[l].reshape(1, -1))
        mod3 = mod[:B].reshape(B, 1, 6 * D)

        q, lf, k, v, sog, pm, sga, sgb = _inproj(
            x, mod3, norm1_g[l].reshape(1, D), w_in[l].astype(BF16), lb_all[l].reshape(1, HG_WIDTH),
            pool_w[l].astype(BF16), pool_b[l].reshape(len(POOL_WINDOWS), 1, POOL_GROUP),
            pool_scale[l].reshape(1, POOL_WIDTH), tm=256)
        oa = _hgrn(q, lf, k, v, sog, hgrn_norm_g[l].reshape(1, HG_DK), B, S, tb=512)

        rw_hi, rw_lo = _split_bf16(router_w[l].T)
        x1, h2_tm, logits_t = _mix(
            x, oa, pm, sga, sgb, mod3, norm2_g[l].reshape(1, D), w_up_a[l].astype(BF16),
            w_up_b[l].astype(BF16), w_out[l].astype(BF16), rw_hi, rw_lo, tm=512)

        idx, gate, rank, counts = _route(logits_t, router_bias[l].reshape(N_EXPERTS, 1), tl=256)
        off, meta = _plan(counts, n_items_max)
        pos = _place(idx, rank, off, tl=512)

        xs = _scatter(pos, h2_tm, tt=256)
        ys = _experts(meta, xs, exp_w_gate[l], exp_w_up[l], exp_w_down[l], n_items_max)

        tt_c = 128
        fg = final_norm_g.reshape(1, D)
        x = _combine(pos, ys, h2_tm, x1, gate.T, mod3,
                     shared_w_gate[l].astype(BF16), shared_w_up[l].astype(BF16),
                     shared_w_down[l].astype(BF16), fg, B, S, tt_c).reshape(B, S, D)
    return x
```

```python
import functools

import jax
import jax.numpy as jnp
from jax import lax
from jax.experimental import pallas as pl
from jax.experimental.pallas import tpu as pltpu

F32 = jnp.float32
BF16 = jnp.bfloat16
HIGHEST = lax.Precision.HIGHEST

D_MODEL = 1024
HG_WIDTH = 512
HG_DK = 128
HG_HEADS = 4
HG_CHUNK = 64
HG_BLK = 16
HG_SUB = 8
POOL_WIDTH = 512
POOL_WINDOWS = (2, 4, 8, 16)
POOL_GROUP = 128
POOL_HALO = 16
N_EXPERTS = 256
TOP_K = 8
N_GROUPS = 8
TOPK_GROUPS = 4
GROUP_SIZE = N_EXPERTS // N_GROUPS
D_EXPERT = 256
ROUTED_SCALE = 2.5
EPS = 1e-6

LANES = 128
SUBLANES = 8
ROW_TILES = D_MODEL // LANES
EXPERT_TILE = 256
TILE_RING = 8
TILE_AHEAD = TILE_RING - 1
VMEM_LIMIT = 56 * 1024 * 1024

COL_Q, COL_F, COL_I, COL_OG, COL_U, COL_GA, COL_GB = 0, 512, 1024, 1536, 2048, 2560, 3584


def _sigmoid(x):
    return 1.0 / (1.0 + jnp.exp(-x))


def _silu(x):
    return x * _sigmoid(x)


def _dot(a, b):
    return jnp.dot(a, b, preferred_element_type=F32)


def _dot_nt(a, b):
    return lax.dot_general(a, b, (((1,), (1,)), ((), ())), preferred_element_type=F32)


def _dot_tn(a, b):
    return lax.dot_general(a, b, (((0,), (0,)), ((), ())), preferred_element_type=F32)


def _row_chunks(x):
    return [x[:, j * LANES:(j + 1) * LANES] for j in range(ROW_TILES)]


def _load_rows(ref, n, first_row=0):
    return jnp.concatenate(
        [ref[pl.ds(first_row * ROW_TILES + j, n, stride=ROW_TILES), :] for j in range(ROW_TILES)], axis=1)


def _ada_kernel(c_ref, w_ref, b_ref, o_ref):
    cond = _silu(c_ref[...])
    o_ref[...] = jnp.dot(cond, w_ref[...], precision=HIGHEST, preferred_element_type=F32) + b_ref[...]


def _ada(c_pad, ada_w, ada_b):
    n = ada_w.shape[1]
    tn = 1536
    return pl.pallas_call(
        _ada_kernel,
        out_shape=jax.ShapeDtypeStruct((SUBLANES, n), F32),
        grid=(n // tn,),
        in_specs=[pl.BlockSpec((SUBLANES, D_MODEL), lambda j: (0, 0)),
                  pl.BlockSpec((D_MODEL, tn), lambda j: (0, j)),
                  pl.BlockSpec((1, tn), lambda j: (0, j))],
        out_specs=pl.BlockSpec((SUBLANES, tn), lambda j: (0, j)),
        compiler_params=pltpu.CompilerParams(vmem_limit_bytes=VMEM_LIMIT),
        name="ada",
    )(c_pad, ada_w, ada_b)


def _inproj_kernel(x_ref, sh_ref, sc_ref, g_ref, w_ref, lb_ref, pw_ref, pb_ref, ps_ref,
                   q_ref, lf_ref, k_ref, v_ref, sog_ref, pm_ref, sga_ref, sgb_ref, halo_ref):
    s = pl.program_id(1)
    tm = x_ref.shape[1]
    x = x_ref[0]
    h = x * lax.rsqrt(jnp.mean(x * x, axis=-1, keepdims=True) + EPS) * g_ref[...]
    h = h * (1.0 + sc_ref[0]) + sh_ref[0]
    hb = h.astype(BF16)

    def proj(lo, n):
        return _dot(hb, w_ref[:, lo:lo + n])

    q = proj(COL_Q, HG_WIDTH)
    q_ref[...] = _silu(q) * (HG_DK ** -0.5)
    sig = _sigmoid(proj(COL_F, HG_WIDTH))
    lb = lb_ref[...]
    lf_ref[...] = jnp.log(lb + (1.0 - lb) * sig)
    k_ref[...] = (1.0 - lb) * (1.0 - sig)
    v_ref[...] = proj(COL_I, HG_WIDTH)
    sog_ref[...] = _silu(proj(COL_OG, HG_WIDTH)).astype(BF16)
    sga_ref[...] = _sigmoid(proj(COL_GA, D_MODEL)).astype(BF16)
    sgb_ref[...] = _sigmoid(proj(COL_GB, D_MODEL)).astype(BF16)

    u = proj(COL_U, POOL_WIDTH)
    @pl.when(s == 0)
    def _():
        halo_ref[...] = jnp.zeros_like(halo_ref)

    ext = jnp.concatenate([halo_ref[...], u], axis=0)
    halo_ref[...] = u[tm - POOL_HALO:, :]
    s2 = ext + pltpu.roll(ext, 1, 0)
    s4 = s2 + pltpu.roll(s2, 2, 0)
    s8 = s4 + pltpu.roll(s4, 4, 0)
    s16 = s8 + pltpu.roll(s8, 8, 0)
    pos1 = (s * tm + 1 + lax.broadcasted_iota(jnp.int32, (tm, 1), 0)).astype(F32)
    for g, (w, sw) in enumerate(zip(POOL_WINDOWS, (s2, s4, s8, s16))):
        cols = slice(g * POOL_GROUP, (g + 1) * POOL_GROUP)
        m = sw[POOL_HALO:, cols] / jnp.minimum(pos1, float(w)) - u[:, cols]
        y = _dot(m.astype(BF16), pw_ref[g]) + pb_ref[g]
        pm_ref[:, cols] = (y * ps_ref[:, cols]).astype(BF16)


def _inproj(x, mod3, norm_g, w_in_b, lb, pool_w_b, pool_b, pool_scale, tm):
    B, S, D = x.shape
    T = B * S
    nS = S // tm
    row = lambda b, s: (b * nS + s, 0)
    const2 = lambda b, s: (0, 0)
    const3 = lambda b, s: (0, 0, 0)
    half = lambda dt: jax.ShapeDtypeStruct((T, HG_WIDTH), dt)
    full = lambda dt: jax.ShapeDtypeStruct((T, D), dt)
    return pl.pallas_call(
        _inproj_kernel,
        out_shape=(half(F32), half(F32), half(F32), half(F32), half(BF16), half(BF16), full(BF16), full(BF16)),
        grid=(B, nS),
        in_specs=[pl.BlockSpec((1, tm, D), lambda b, s: (b, s, 0)),
                  pl.BlockSpec((1, 1, D), lambda b, s: (b, 0, 0)),
                  pl.BlockSpec((1, 1, D), lambda b, s: (b, 0, 1)),
                  pl.BlockSpec((1, D), const2),
                  pl.BlockSpec(w_in_b.shape, const2),
                  pl.BlockSpec((1, HG_WIDTH), const2),
                  pl.BlockSpec(pool_w_b.shape, const3),
                  pl.BlockSpec(pool_b.shape, const3),
                  pl.BlockSpec((1, POOL_WIDTH), const2)],
        out_specs=(pl.BlockSpec((tm, HG_WIDTH), row),) * 6 + (pl.BlockSpec((tm, D), row),) * 2,
        scratch_shapes=[pltpu.VMEM((POOL_HALO, POOL_WIDTH), F32)],
        compiler_params=pltpu.CompilerParams(
            dimension_semantics=("arbitrary", "arbitrary"), vmem_limit_bytes=VMEM_LIMIT),
        name="inproj",
    )(x, mod3, mod3, norm_g, w_in_b, lb, pool_w_b, pool_b, pool_scale)


def _hgrn_kernel(q_ref, lf_ref, k_ref, v_ref, sog_ref, gn_ref, o_ref, *st_refs):
    C = HG_CHUNK
    n_chunks = q_ref.shape[0] // C

    @pl.when(pl.program_id(1) == 0)
    def _():
        for st_ref in st_refs:
            st_ref[...] = jnp.zeros_like(st_ref)

    r_i = lax.broadcasted_iota(jnp.int32, (C, C), 0)
    c_i = lax.broadcasted_iota(jnp.int32, (C, C), 1)
    tril = (c_i <= r_i).astype(BF16)
    same_blk = (r_i // HG_BLK) == (c_i // HG_BLK)
    row = lax.broadcasted_iota(jnp.int32, (C, HG_DK), 0)
    row_in_sub = row % HG_SUB
    upper_half = (row % HG_BLK) >= HG_SUB
    row_blk = row // HG_BLK
    n_blk = C // HG_BLK

    def cumsum_rows(x):
        hi = x.astype(BF16)
        r1 = x - hi.astype(F32)
        mid = r1.astype(BF16)
        lo = (r1 - mid.astype(F32)).astype(BF16)
        return _dot(tril, hi) + _dot(tril, mid) + _dot(tril, lo)

    def block_rows(x, size, which):
        pieces = []
        for g in range(C // size):
            src = g * size + which
            pieces.append(jnp.zeros((size, x.shape[1]), F32) if src < 0
                          else jnp.broadcast_to(x[src:src + 1, :], (size, x.shape[1])))
        return jnp.concatenate(pieces, axis=0)

    def chunk(ci, carry):
        rs = pl.ds(pl.multiple_of(ci * C, C), C)
        b_all = cumsum_rows(lf_ref[rs, :])
        for h in range(HG_HEADS):
            cs = slice(h * HG_DK, (h + 1) * HG_DK)
            q = q_ref[rs, cs]
            k = k_ref[rs, cs]
            v = v_ref[rs, cs]
            b = b_all[:, cs]
            vb = v.astype(BF16)

            kt = k * jnp.exp(block_rows(b, HG_BLK, HG_BLK - 1) - b)
            q_parts, k_parts = [], []
            for j in range(n_blk - 1):
                bj = b[HG_BLK * j + HG_BLK - 1:HG_BLK * (j + 1), :]
                after = row >= HG_BLK * (j + 1)
                q_parts.append(q * jnp.exp(jnp.where(after, b - bj, -jnp.inf)))
                k_parts.append(jnp.where(row_blk == j, kt, 0.0))
            qcat = jnp.concatenate(q_parts, axis=1).astype(BF16)
            kcat = jnp.concatenate(k_parts, axis=1).astype(BF16)
            scores = _dot_nt(qcat, kcat)
            b_prev = block_rows(b, HG_SUB, -1)
            b_sub = block_rows(b, HG_SUB, HG_SUB - 1)
            qh = (q * jnp.exp(jnp.where(upper_half, b - b_prev, -jnp.inf))).astype(BF16)
            kh = jnp.where(upper_half, 0.0, k * jnp.exp(b_sub - b)).astype(BF16)
            scores = scores + jnp.where(same_blk, _dot_nt(qh, kh), 0.0)
            o = _dot(scores.astype(BF16), vb)

            o = o + jnp.sum(q * k, axis=-1, keepdims=True) * v
            for d in range(1, HG_SUB):
                kd = pltpu.roll(k, d, 0)
                bd = pltpu.roll(b, d, 0)
                vd = pltpu.roll(v, d, 0)
                e = jnp.exp(jnp.where(row_in_sub >= d, b - bd, -jnp.inf))
                o = o + jnp.sum(q * kd * e, axis=-1, keepdims=True) * vd

            st = st_refs[h][...]
            o = o + _dot_nt((q * jnp.exp(b)).astype(BF16), st.astype(BF16))
            b_end = b[C - 1:C, :]
            k_end = (k * jnp.exp(b_end - b)).astype(BF16)
            st_refs[h][...] = st * jnp.exp(b_end) + _dot_tn(vb, k_end)

            on = o * lax.rsqrt(jnp.mean(o * o, axis=-1, keepdims=True) + EPS) * gn_ref[...]
            o_ref[rs, cs] = (on * sog_ref[rs, cs].astype(F32)).astype(BF16)
        return carry

    lax.fori_loop(0, n_chunks, chunk, 0)


def _hgrn(q, lf, k, v, sog, gn, B, S, tb):
    T = B * S
    nS = S // tb
    row = lambda b, s: (b * nS + s, 0)
    blk = pl.BlockSpec((tb, HG_WIDTH), row)
    return pl.pallas_call(
        _hgrn_kernel,
        out_shape=jax.ShapeDtypeStruct((T, HG_WIDTH), BF16),
        grid=(B, nS),
        in_specs=[blk, blk, blk, blk, blk, pl.BlockSpec((1, HG_DK), lambda b, s: (0, 0))],
        out_specs=blk,
        scratch_shapes=[pltpu.VMEM((HG_DK, HG_DK), F32)] * HG_HEADS,
        compiler_params=pltpu.CompilerParams(
            dimension_semantics=("arbitrary", "arbitrary"), vmem_limit_bytes=VMEM_LIMIT),
        name="hgrn",
    )(q, lf, k, v, sog, gn)


def _split_kernel(w_ref, hi_ref, lo_ref):
    w = w_ref[...]
    hi = w.astype(BF16)
    hi_ref[...] = hi
    lo_ref[...] = (w - hi.astype(F32)).astype(BF16)


def _split_bf16(w):
    out = jax.ShapeDtypeStruct(w.shape, BF16)
    return pl.pallas_call(_split_kernel, out_shape=(out, out), name="split")(w)


def _mix_kernel(x_ref, oa_ref, pm_ref, sga_ref, sgb_ref, g1_ref, sh2_ref, sc2_ref, n2_ref,
                wua_ref, wub_ref, wo_ref, rw_hi_ref, rw_lo_ref, x1_ref, h2_ref, lg_ref):
    tm = x_ref.shape[1]
    ya = _dot(oa_ref[...], wua_ref[...])
    yb = _dot(pm_ref[...], wub_ref[...])
    mix = sga_ref[...].astype(F32) * ya + sgb_ref[...].astype(F32) * yb
    x1 = x_ref[0] + g1_ref[0] * _dot(mix.astype(BF16), wo_ref[...])
    x1_ref[...] = x1
    h2 = x1 * lax.rsqrt(jnp.mean(x1 * x1, axis=-1, keepdims=True) + EPS) * n2_ref[...]
    h2 = h2 * (1.0 + sc2_ref[0]) + sh2_ref[0]
    for j, chunk in enumerate(_row_chunks(h2)):
        h2_ref[pl.ds(j, tm, stride=ROW_TILES), :] = chunk
    h_hi = h2.astype(BF16)
    h_lo = (h2 - h_hi.astype(F32)).astype(BF16)
    rw_hi = rw_hi_ref[...]
    lg_ref[...] = _dot_nt(rw_hi, h_hi) + _dot_nt(rw_hi, h_lo) + _dot_nt(rw_lo_ref[...], h_hi)


def _mix(x, oa, pm, sga, sgb, mod3, norm2_g, wua, wub, wo, rw_hi, rw_lo, tm):
    B, S, D = x.shape
    T = B * S
    nS = S // tm
    row = lambda b, s: (b * nS + s, 0)
    const2 = lambda b, s: (0, 0)
    return pl.pallas_call(
        _mix_kernel,
        out_shape=(jax.ShapeDtypeStruct((T, D), F32),
                   jax.ShapeDtypeStruct((T * ROW_TILES, LANES), F32),
                   jax.ShapeDtypeStruct((N_EXPERTS, T), F32)),
        grid=(B, nS),
        in_specs=[pl.BlockSpec((1, tm, D), lambda b, s: (b, s, 0)),
                  pl.BlockSpec((tm, HG_WIDTH), row),
                  pl.BlockSpec((tm, POOL_WIDTH), row),
                  pl.BlockSpec((tm, D), row),
                  pl.BlockSpec((tm, D), row),
                  pl.BlockSpec((1, 1, D), lambda b, s: (b, 0, 2)),
                  pl.BlockSpec((1, 1, D), lambda b, s: (b, 0, 3)),
                  pl.BlockSpec((1, 1, D), lambda b, s: (b, 0, 4)),
                  pl.BlockSpec((1, D), const2),
                  pl.BlockSpec(wua.shape, const2),
                  pl.BlockSpec(wub.shape, const2),
                  pl.BlockSpec(wo.shape, const2),
                  pl.BlockSpec(rw_hi.shape, const2),
                  pl.BlockSpec(rw_lo.shape, const2)],
        out_specs=(pl.BlockSpec((tm, D), row),
                   pl.BlockSpec((tm * ROW_TILES, LANES), row),
                   pl.BlockSpec((N_EXPERTS, tm), lambda b, s: (0, b * nS + s))),
        compiler_params=pltpu.CompilerParams(
            dimension_semantics=("arbitrary", "arbitrary"), vmem_limit_bytes=VMEM_LIMIT),
        name="mix",
    )(x, oa, pm, sga, sgb, mod3, mod3, mod3, norm2_g, wua, wub, wo, rw_hi, rw_lo)


def _route_kernel(lg_ref, bias_ref, idx_ref, gate_ref, rank_ref, cnt_ref, carry_ref):
    tl = lg_ref.shape[1]
    neg = -jnp.inf

    @pl.when(pl.program_id(0) == 0)
    def _():
        carry_ref[...] = jnp.zeros_like(carry_ref)

    s = _sigmoid(lg_ref[...])
    biased = s + bias_ref[...]
    rowid = lax.broadcasted_iota(jnp.int32, (N_EXPERTS, tl), 0)

    def first_argmax(x, ids, sentinel):
        m = jnp.max(x, axis=0, keepdims=True)
        return jnp.min(jnp.where(x == m, ids, sentinel), axis=0, keepdims=True), m

    gscores = []
    for g in range(N_GROUPS):
        xg = biased[g * GROUP_SIZE:(g + 1) * GROUP_SIZE, :]
        rid = g * GROUP_SIZE + lax.broadcasted_iota(jnp.int32, (GROUP_SIZE, tl), 0)
        first, m1 = first_argmax(xg, rid, N_EXPERTS)
        m2 = jnp.max(jnp.where(rid == first, neg, xg), axis=0, keepdims=True)
        gscores.append(m1 + m2)
    blocks = []
    for g in range(N_GROUPS):
        beaten = jnp.zeros((1, tl), F32)
        for o in range(N_GROUPS):
            if o != g:
                wins = (gscores[o] >= gscores[g]) if o < g else (gscores[o] > gscores[g])
                beaten = beaten + jnp.where(wins, 1.0, 0.0)
        xg = biased[g * GROUP_SIZE:(g + 1) * GROUP_SIZE, :]
        blocks.append(jnp.where(beaten < float(TOPK_GROUPS), xg, neg))
    masked = jnp.concatenate(blocks, axis=0)

    idxs, gates = [], []
    chosen = jnp.zeros((N_EXPERTS, tl), F32)
    for _ in range(TOP_K):
        first, _m = first_argmax(masked, rowid, N_EXPERTS)
        sel = rowid == first
        gates.append(jnp.sum(jnp.where(sel, s, 0.0), axis=0, keepdims=True))
        idxs.append(first)
        chosen = jnp.where(sel, 1.0, chosen)
        masked = jnp.where(sel, neg, masked)
    gate_sum = functools.reduce(lambda a, b: a + b, gates)
    for k in range(TOP_K):
        gate_ref[k:k + 1, :] = gates[k] / gate_sum * ROUTED_SCALE
        idx_ref[k:k + 1, :] = idxs[k]

    lr = lax.broadcasted_iota(jnp.int32, (tl, tl), 0)
    lc = lax.broadcasted_iota(jnp.int32, (tl, tl), 1)
    prefix = (lr <= lc).astype(BF16)
    cnt_incl = _dot(chosen.astype(BF16), prefix)
    carry = carry_ref[...]
    rank_excl = cnt_incl - chosen + carry
    for k in range(TOP_K):
        rank_k = jnp.sum(jnp.where(rowid == idxs[k], rank_excl, 0.0), axis=0, keepdims=True)
        rank_ref[k:k + 1, :] = rank_k.astype(jnp.int32)
    carry = carry + jnp.sum(chosen, axis=1, keepdims=True)
    carry_ref[...] = carry
    cnt_ref[...] = carry.astype(jnp.int32)


def _route(logits_t, bias, tl):
    T = logits_t.shape[1]
    tok = lambda i: (0, i)
    return pl.pallas_call(
        _route_kernel,
        out_shape=(jax.ShapeDtypeStruct((TOP_K, T), jnp.int32),
                   jax.ShapeDtypeStruct((TOP_K, T), F32),
                   jax.ShapeDtypeStruct((TOP_K, T), jnp.int32),
                   jax.ShapeDtypeStruct((N_EXPERTS, 1), jnp.int32)),
        grid=(T // tl,),
        in_specs=[pl.BlockSpec((N_EXPERTS, tl), tok), pl.BlockSpec((N_EXPERTS, 1), lambda i: (0, 0))],
        out_specs=(pl.BlockSpec((TOP_K, tl), tok), pl.BlockSpec((TOP_K, tl), tok),
                   pl.BlockSpec((TOP_K, tl), tok), pl.BlockSpec((N_EXPERTS, 1), lambda i: (0, 0))),
        scratch_shapes=[pltpu.VMEM((N_EXPERTS, 1), F32)],
        compiler_params=pltpu.CompilerParams(
            dimension_semantics=("arbitrary",), vmem_limit_bytes=VMEM_LIMIT),
        name="route",
    )(logits_t, bias)


def _as_rows(ref):
    return ref.reshape(ref.shape[0] // ROW_TILES, ROW_TILES, LANES)


def _wait_rows(rows_ref, n, sem):
    pltpu.make_async_copy(rows_ref.at[pl.ds(0, n)], rows_ref.at[pl.ds(0, n)], sem).wait()


def _scatter_kernel(pos_ref, h2_ref, xs_ref, zero_ref, sem, zsem):
    src = _as_rows(h2_ref)
    dst = _as_rows(xs_ref)
    tt = src.shape[0]

    @pl.when(pl.program_id(0) == 0)
    def _():
        zero_ref[...] = jnp.zeros_like(zero_ref)
        tail = xs_ref.at[pl.ds(xs_ref.shape[0] - zero_ref.shape[0], zero_ref.shape[0])]
        fill = pltpu.make_async_copy(zero_ref, tail, zsem)
        fill.start()
        fill.wait()

    def start(t, c):
        for k in range(TOP_K):
            pltpu.make_async_copy(src.at[t], dst.at[pos_ref[k, t]], sem).start(priority=k % 2)
        return c

    lax.fori_loop(0, tt, start, 0)
    _wait_rows(dst, tt * TOP_K, sem)


def _scatter(pos, h2_tm, tt):
    n_rows = h2_tm.shape[0] // ROW_TILES * TOP_K + EXPERT_TILE
    return pl.pallas_call(
        _scatter_kernel,
        out_shape=jax.ShapeDtypeStruct((n_rows * ROW_TILES, LANES), F32),
        grid=(pos.shape[1] // tt,),
        in_specs=[pl.BlockSpec((TOP_K, tt), lambda i: (0, i), memory_space=pltpu.SMEM),
                  pl.BlockSpec((tt * ROW_TILES, LANES), lambda i: (i, 0))],
        out_specs=pl.BlockSpec(memory_space=pl.ANY),
        scratch_shapes=[pltpu.VMEM((EXPERT_TILE * ROW_TILES, LANES), F32),
                        pltpu.SemaphoreType.DMA, pltpu.SemaphoreType.DMA],
        compiler_params=pltpu.CompilerParams(
            dimension_semantics=("arbitrary",), vmem_limit_bytes=VMEM_LIMIT),
        name="scatter",
    )(pos, h2_tm)


def _experts_kernel(exp_ref, row0_ref, valid_ref, newe_ref, nitems_ref, slot_ref, nexte_ref,
                    xs_hbm, wg_hbm, wu_hbm, wd_hbm, ys_hbm,
                    xbuf_ref, ybuf_ref, sg_ref, su_ref, sd_ref, wgb_ref, wub_ref, wdb_ref, hm_ref,
                    xsem, ysem, wsem):
    i = pl.program_id(0)
    tr = EXPERT_TILE
    tile_rows = tr * ROW_TILES
    n_items = nitems_ref[0]
    part_sizes = tuple(tr >> (b + 1) for b in range(tr.bit_length() - 1))

    def ring(item, first_row=0, n_rows=tr):
        start = ((item % TILE_RING) * tr + first_row) * ROW_TILES
        return pl.ds(pl.multiple_of(start, ROW_TILES), n_rows * ROW_TILES)

    def hbm_rows(first_row, n_rows=tr):
        return pl.ds(pl.multiple_of(first_row * ROW_TILES, ROW_TILES), n_rows * ROW_TILES)

    def x_copy(item):
        return pltpu.make_async_copy(xs_hbm.at[hbm_rows(row0_ref[item])], xbuf_ref.at[ring(item)],
                                     xsem.at[item % TILE_RING])

    def y_copies(item, go):
        v = valid_ref[item]
        sem = ysem.at[item % TILE_RING]

        @pl.when(v == tr)
        def _():
            go(pltpu.make_async_copy(ybuf_ref.at[ring(item)], ys_hbm.at[hbm_rows(row0_ref[item])], sem))

        @pl.when(v != tr)
        def _():
            for size in part_sizes:
                @pl.when((v & size) != 0)
                def _():
                    first = v & ~(2 * size - 1)
                    go(pltpu.make_async_copy(ybuf_ref.at[ring(item, first, size)],
                                             ys_hbm.at[hbm_rows(row0_ref[item] + first, size)], sem))

    def weight_copies(e, slot):
        return (pltpu.make_async_copy(wg_hbm.at[e], sg_ref.at[slot], wsem.at[slot]),
                pltpu.make_async_copy(wu_hbm.at[e], su_ref.at[slot], wsem.at[slot]),
                pltpu.make_async_copy(wd_hbm.at[e], sd_ref.at[slot], wsem.at[slot]))

    a_on = i < n_items
    j = jnp.maximum(i - 1, 0)
    b_on = (i >= 1) & (i - 1 < n_items)
    e = exp_ref[i]

    @pl.when(i == 0)
    def _():
        hm_ref[...] = jnp.zeros_like(hm_ref)
        for i0 in range(TILE_AHEAD):
            @pl.when(i0 < n_items)
            def _():
                x_copy(i0).start()

    @pl.when(a_on)
    def _():
        @pl.when(i + TILE_AHEAD < n_items)
        def _():
            x_copy(i + TILE_AHEAD).start()

        x_copy(i).wait()

    @pl.when(b_on & (j >= TILE_RING))
    def _():
        y_copies(j - TILE_RING, lambda c: c.wait())

    @pl.when(a_on & (newe_ref[i] == 1))
    def _():
        slot = slot_ref[i]
        nxt = nexte_ref[i]

        @pl.when(i == 0)
        def _():
            for c in weight_copies(e, slot):
                c.start()

        @pl.when(nxt >= 0)
        def _():
            for c in weight_copies(nxt, 1 - slot):
                c.start()

        for c in weight_copies(e, slot):
            c.wait()
        wgb_ref[...] = sg_ref[slot].astype(BF16)
        wub_ref[...] = su_ref[slot].astype(BF16)
        wdb_ref[slot] = sd_ref[slot].astype(BF16)

    @pl.when(i <= n_items)
    def _():
        out_row = (j % TILE_RING) * tile_rows
        for c, chunk in enumerate(_row_chunks(_dot(hm_ref[j % 2], wdb_ref[slot_ref[j]]))):
            ybuf_ref[pl.ds(out_row + c, tr, stride=ROW_TILES), :] = chunk

        x = _load_rows(xbuf_ref, tr, first_row=(i % TILE_RING) * tr).astype(BF16)
        hm_ref[i % 2] = (_silu(_dot(x, wgb_ref[...])) * _dot(x, wub_ref[...])).astype(BF16)

    @pl.when(b_on)
    def _():
        y_copies(j, lambda c: c.start())

    @pl.when(b_on & (j == n_items - 1))
    def _():
        for back in range(TILE_RING):
            @pl.when(j - back >= 0)
            def _():
                y_copies(j - back, lambda c: c.wait())


def _experts(meta, xs, wg, wu, wd, n_items_max):
    tile_rows = EXPERT_TILE * ROW_TILES
    out_rows = xs.shape[0] - tile_rows
    hbm = pl.BlockSpec(memory_space=pl.ANY)
    n_slots = 2
    grid_spec = pltpu.PrefetchScalarGridSpec(
        num_scalar_prefetch=len(meta),
        grid=(n_items_max + 1,),
        in_specs=[hbm, hbm, hbm, hbm],
        out_specs=hbm,
        scratch_shapes=[pltpu.VMEM((TILE_RING * tile_rows, LANES), F32),
                        pltpu.VMEM((TILE_RING * tile_rows, LANES), F32),
                        pltpu.VMEM((n_slots, D_MODEL, D_EXPERT), F32),
                        pltpu.VMEM((n_slots, D_MODEL, D_EXPERT), F32),
                        pltpu.VMEM((n_slots, D_EXPERT, D_MODEL), F32),
                        pltpu.VMEM((D_MODEL, D_EXPERT), BF16),
                        pltpu.VMEM((D_MODEL, D_EXPERT), BF16),
                        pltpu.VMEM((n_slots, D_EXPERT, D_MODEL), BF16),
                        pltpu.VMEM((2, EXPERT_TILE, D_EXPERT), BF16),
                        pltpu.SemaphoreType.DMA((TILE_RING,)),
                        pltpu.SemaphoreType.DMA((TILE_RING,)),
                        pltpu.SemaphoreType.DMA((n_slots,))])
    return pl.pallas_call(
        _experts_kernel,
        out_shape=jax.ShapeDtypeStruct((out_rows, LANES), F32),
        grid_spec=grid_spec,
        compiler_params=pltpu.CompilerParams(
            dimension_semantics=("arbitrary",), vmem_limit_bytes=VMEM_LIMIT),
        name="experts",
    )(*meta, xs, wg, wu, wd)


def _combine_kernel(pos_ref, pos_next_ref, ys_ref, h2_ref, x1_ref, gate_ref, g2_ref, swg_ref, swu_ref, swd_ref,
                    fg_ref, out_ref, buf_a, buf_b, sem):
    i = pl.program_id(0)
    tt = x1_ref.shape[0] // 2
    src = _as_rows(ys_ref)

    def gather(p_ref, col0, buf, s):
        dst = _as_rows(buf)
        for t in range(tt):
            for k in range(TOP_K):
                pltpu.make_async_copy(src.at[p_ref[k, col0 + t]], dst.at[k * tt + t],
                                      sem.at[s]).start(priority=k % 2)

    @pl.when(i == 0)
    def _():
        dst = _as_rows(buf_a)

        def start(t, c):
            for k in range(TOP_K):
                pltpu.make_async_copy(src.at[pos_ref[k, t]], dst.at[k * tt + t], sem.at[0]).start(priority=k % 2)
            return c

        lax.fori_loop(0, tt, start, 0)

    def tile(row0, buf, s, prefetch):
        tok = pl.ds(row0, tt)
        h2 = _load_rows(h2_ref, tt, first_row=row0).astype(BF16)
        hm = (_silu(_dot(h2, swg_ref[...])) * _dot(h2, swu_ref[...])).astype(BF16)
        _wait_rows(_as_rows(buf), tt * TOP_K, sem.at[s])
        prefetch()
        gate = gate_ref[tok, :]
        ssq = jnp.zeros((tt, 1), F32)
        for c in range(ROW_TILES):
            cols = slice(c * LANES, (c + 1) * LANES)
            acc = _dot(hm, swd_ref[:, cols])
            for k in range(TOP_K):
                acc = acc + gate[:, k:k + 1] * buf[pl.ds(k * tt * ROW_TILES + c, tt, stride=ROW_TILES), :]
            x2 = x1_ref[tok, cols] + g2_ref[0, :, cols] * acc
            out_ref[tok, cols] = x2
            ssq = ssq + jnp.sum(x2 * x2, axis=-1, keepdims=True)
        out_ref[tok, :] = out_ref[tok, :] * lax.rsqrt(ssq * (1.0 / D_MODEL) + EPS) * fg_ref[...]

    tile(0, buf_a, 0, lambda: gather(pos_ref, tt, buf_b, 1))
    tile(tt, buf_b, 1, lambda: gather(pos_next_ref, 0, buf_a, 0))

    @pl.when(i == pl.num_programs(0) - 1)
    def _():
        _wait_rows(_as_rows(buf_a), tt * TOP_K, sem.at[0])


def _combine(pos, ys, h2_tm, x1, gate_tm, mod3, swg, swu, swd, fg, B, S, tt):
    T, D = x1.shape
    gather_rows = tt * TOP_K * ROW_TILES
    tt = 2 * tt
    nS = S // tt
    const2 = lambda i: (0, 0)
    n_tiles = T // tt
    return pl.pallas_call(
        _combine_kernel,
        out_shape=jax.ShapeDtypeStruct((T, D), F32),
        grid=(n_tiles,),
        in_specs=[pl.BlockSpec((TOP_K, tt), lambda i: (0, i), memory_space=pltpu.SMEM),
                  pl.BlockSpec((TOP_K, tt), lambda i: (0, jnp.minimum(i + 1, n_tiles - 1)),
                               memory_space=pltpu.SMEM),
                  pl.BlockSpec(memory_space=pl.ANY),
                  pl.BlockSpec((tt * ROW_TILES, LANES), lambda i: (i, 0)),
                  pl.BlockSpec((tt, D), lambda i: (i, 0)),
                  pl.BlockSpec((tt, TOP_K), lambda i: (i, 0)),
                  pl.BlockSpec((1, 1, D), lambda i: (i // nS, 0, 5)),
                  pl.BlockSpec(swg.shape, const2),
                  pl.BlockSpec(swu.shape, const2),
                  pl.BlockSpec(swd.shape, const2),
                  pl.BlockSpec((1, D), const2)],
        out_specs=pl.BlockSpec((tt, D), lambda i: (i, 0)),
        scratch_shapes=[pltpu.VMEM((gather_rows, LANES), F32),
                        pltpu.VMEM((gather_rows, LANES), F32),
                        pltpu.SemaphoreType.DMA((2,))],
        compiler_params=pltpu.CompilerParams(
            dimension_semantics=("arbitrary",), vmem_limit_bytes=VMEM_LIMIT),
        name="combine",
    )(pos, pos, ys, h2_tm, x1, gate_tm, mod3, swg, swu, swd, fg)


def _place_kernel(idx_ref, rank_ref, off_ref, pos_ref):
    tl = idx_ref.shape[1]
    rowid = lax.broadcasted_iota(jnp.int32, (N_EXPERTS, tl), 0)
    off = off_ref[...].astype(F32)
    for k in range(TOP_K):
        base = jnp.sum(jnp.where(rowid == idx_ref[k:k + 1, :], off, 0.0), axis=0, keepdims=True)
        pos_ref[k:k + 1, :] = base.astype(jnp.int32) + rank_ref[k:k + 1, :]


def _place(idx, rank, off, tl):
    T = idx.shape[1]
    tok = pl.BlockSpec((TOP_K, tl), lambda i: (0, i))
    return pl.pallas_call(
        _place_kernel,
        out_shape=jax.ShapeDtypeStruct((TOP_K, T), jnp.int32),
        grid=(T // tl,),
        in_specs=[tok, tok, pl.BlockSpec((N_EXPERTS, 1), lambda i: (0, 0))],
        out_specs=tok,
        name="place",
    )(idx, rank, off)


def _plan(counts, n_items_max):
    counts = counts[:, 0]
    off = jnp.cumsum(counts) - counts
    n_e = (counts + EXPERT_TILE - 1) // EXPERT_TILE
    item_end = jnp.cumsum(n_e)
    item_start = item_end - n_e
    n_items = item_end[-1]
    ids = jnp.arange(n_items_max + 1, dtype=jnp.int32)
    ids_c = jnp.minimum(ids, n_items - 1)
    item_e = jnp.minimum(jnp.sum(item_end[None, :] <= ids_c[:, None], axis=1), N_EXPERTS - 1).astype(jnp.int32)
    ids_e = jnp.arange(N_EXPERTS, dtype=jnp.int32)
    of_item = item_e[:, None] == ids_e[None, :]

    def per_item(table):
        return jnp.sum(jnp.where(of_item, table[None, :], 0), axis=1).astype(jnp.int32)

    first_row = (ids_c - per_item(item_start)) * EXPERT_TILE
    item_row0 = per_item(off) + first_row
    item_valid = jnp.minimum(per_item(counts) - first_row, EXPERT_TILE).astype(jnp.int32)
    prev_e = jnp.concatenate([jnp.full((1,), -1, jnp.int32), item_e[:-1]])
    item_newe = (item_e != prev_e).astype(jnp.int32)
    item_slot = ((jnp.cumsum(item_newe) - 1) % 2).astype(jnp.int32)
    later = jnp.where((counts[None, :] > 0) & (ids_e[None, :] > ids_e[:, None]), ids_e[None, :], N_EXPERTS)
    next_e = jnp.min(later, axis=1)
    next_e = jnp.where(next_e < N_EXPERTS, next_e, -1).astype(jnp.int32)
    meta = (item_e, item_row0, item_valid, item_newe, n_items.reshape(1).astype(jnp.int32),
            item_slot, per_item(next_e))
    return off.astype(jnp.int32).reshape(N_EXPERTS, 1), meta


def kernel(x, c, ada_w, ada_b, norm1_g, w_in, hgrn_lb, hgrn_norm_g, pool_w, pool_b, pool_scale, w_up_a, w_up_b, w_out, norm2_g, router_w, router_bias, exp_w_gate, exp_w_up, exp_w_down, shared_w_gate, shared_w_up, shared_w_down, final_norm_g):
    B, S, D = x.shape
    T = B * S
    assert ada_w.shape[0] == 1, "single-layer trunk only: the final norm is fused into the combine step"
    lb_all = jnp.cumsum(jax.nn.softmax(hgrn_lb.astype(F32), axis=0), axis=0)
    c_pad = jnp.zeros((SUBLANES, D), F32).at[:B].set(c)
    n_items_max = T * TOP_K // EXPERT_TILE + N_EXPERTS

    for l in range(1):
        mod = _ada(c_pad, ada_w[l], ada_b[l].reshape(1, -1))
        mod3 = mod[:B].reshape(B, 1, 6 * D)

        q, lf, k, v, sog, pm, sga, sgb = _inproj(
            x, mod3, norm1_g[l].reshape(1, D), w_in[l].astype(BF16), lb_all[l].reshape(1, HG_WIDTH),
            pool_w[l].astype(BF16), pool_b[l].reshape(len(POOL_WINDOWS), 1, POOL_GROUP),
            pool_scale[l].reshape(1, POOL_WIDTH), tm=256)
        oa = _hgrn(q, lf, k, v, sog, hgrn_norm_g[l].reshape(1, HG_DK), B, S, tb=512)

        rw_hi, rw_lo = _split_bf16(router_w[l].T)
        x1, h2_tm, logits_t = _mix(
            x, oa, pm, sga, sgb, mod3, norm2_g[l].reshape(1, D), w_up_a[l].astype(BF16),
            w_up_b[l].astype(BF16), w_out[l].astype(BF16), rw_hi, rw_lo, tm=512)

        idx, gate, rank, counts = _route(logits_t, router_bias[l].reshape(N_EXPERTS, 1), tl=256)
        off, meta = _plan(counts, n_items_max)
        pos = _place(idx, rank, off, tl=512)

        xs = _scatter(pos, h2_tm, tt=256)
        ys = _experts(meta, xs, exp_w_gate[l], exp_w_up[l], exp_w_down[l], n_items_max)

        tt_c = 128
        fg = final_norm_g.reshape(1, D)
        x = _combine(pos, ys, h2_tm, x1, gate.T, mod3,
                     shared_w_gate[l].astype(BF16), shared_w_up[l].astype(BF16),
                     shared_w_down[l].astype(BF16), fg, B, S, tt_c).reshape(B, S, D)
    return x
```

```python
import functools

import jax
import jax.numpy as jnp
from jax import lax
from jax.experimental import pallas as pl
from jax.experimental.pallas import tpu as pltpu

F32 = jnp.float32
BF16 = jnp.bfloat16
HIGHEST = lax.Precision.HIGHEST

D_MODEL = 1024
HG_WIDTH = 512
HG_DK = 128
HG_HEADS = 4
HG_CHUNK = 64
HG_BLK = 16
HG_SUB = 8
POOL_WIDTH = 512
POOL_WINDOWS = (2, 4, 8, 16)
POOL_GROUP = 128
POOL_HALO = 16
N_EXPERTS = 256
TOP_K = 8
N_GROUPS = 8
TOPK_GROUPS = 4
GROUP_SIZE = N_EXPERTS // N_GROUPS
D_EXPERT = 256
ROUTED_SCALE = 2.5
EPS = 1e-6

LANES = 128
SUBLANES = 8
ROW_TILES = D_MODEL // LANES
EXPERT_TILE = 256
TILE_RING = 8
TILE_AHEAD = TILE_RING - 1
VMEM_LIMIT = 56 * 1024 * 1024

COL_Q, COL_F, COL_I, COL_OG, COL_U, COL_GA, COL_GB = 0, 512, 1024, 1536, 2048, 2560, 3584


def _sigmoid(x):
    return 1.0 / (1.0 + jnp.exp(-x))


def _silu(x):
    return x * _sigmoid(x)


def _dot(a, b):
    return jnp.dot(a, b, preferred_element_type=F32)


def _dot_nt(a, b):
    return lax.dot_general(a, b, (((1,), (1,)), ((), ())), preferred_element_type=F32)


def _dot_tn(a, b):
    return lax.dot_general(a, b, (((0,), (0,)), ((), ())), preferred_element_type=F32)


def _row_chunks(x):
    return [x[:, j * LANES:(j + 1) * LANES] for j in range(ROW_TILES)]


def _load_rows(ref, n, first_row=0):
    return jnp.concatenate(
        [ref[pl.ds(first_row * ROW_TILES + j, n, stride=ROW_TILES), :] for j in range(ROW_TILES)], axis=1)


def _ada_kernel(c_ref, w_ref, b_ref, o_ref):
    cond = _silu(c_ref[...])
    o_ref[...] = jnp.dot(cond, w_ref[...], precision=HIGHEST, preferred_element_type=F32) + b_ref[...]


def _ada(c_pad, ada_w, ada_b):
    n = ada_w.shape[1]
    tn = 1536
    return pl.pallas_call(
        _ada_kernel,
        out_shape=jax.ShapeDtypeStruct((SUBLANES, n), F32),
        grid=(n // tn,),
        in_specs=[pl.BlockSpec((SUBLANES, D_MODEL), lambda j: (0, 0)),
                  pl.BlockSpec((D_MODEL, tn), lambda j: (0, j)),
                  pl.BlockSpec((1, tn), lambda j: (0, j))],
        out_specs=pl.BlockSpec((SUBLANES, tn), lambda j: (0, j)),
        compiler_params=pltpu.CompilerParams(vmem_limit_bytes=VMEM_LIMIT),
        name="ada",
    )(c_pad, ada_w, ada_b)


def _inproj_kernel(x_ref, sh_ref, sc_ref, g_ref, w_ref, lb_ref, pw_ref, pb_ref, ps_ref,
                   q_ref, lf_ref, k_ref, v_ref, sog_ref, pm_ref, sga_ref, sgb_ref, halo_ref):
    s = pl.program_id(1)
    tm = x_ref.shape[1]

    @pl.when(s == 0)
    def _():
        halo_ref[...] = jnp.zeros_like(halo_ref)

    x = x_ref[0]
    h = x * lax.rsqrt(jnp.mean(x * x, axis=-1, keepdims=True) + EPS) * g_ref[...]
    h = h * (1.0 + sc_ref[0]) + sh_ref[0]
    hb = h.astype(BF16)

    def proj(lo, n):
        return _dot(hb, w_ref[:, lo:lo + n])

    q = proj(COL_Q, HG_WIDTH)
    q_ref[...] = _silu(q) * (HG_DK ** -0.5)
    sig = _sigmoid(proj(COL_F, HG_WIDTH))
    lb = lb_ref[...]
    lf_ref[...] = jnp.log(lb + (1.0 - lb) * sig)
    k_ref[...] = (1.0 - lb) * (1.0 - sig)
    v_ref[...] = proj(COL_I, HG_WIDTH)
    sog_ref[...] = _silu(proj(COL_OG, HG_WIDTH)).astype(BF16)
    sga_ref[...] = _sigmoid(proj(COL_GA, D_MODEL)).astype(BF16)
    sgb_ref[...] = _sigmoid(proj(COL_GB, D_MODEL)).astype(BF16)

    u = proj(COL_U, POOL_WIDTH)
    ext = jnp.concatenate([halo_ref[...], u], axis=0)
    halo_ref[...] = u[tm - POOL_HALO:, :]
    s2 = ext + pltpu.roll(ext, 1, 0)
    s4 = s2 + pltpu.roll(s2, 2, 0)
    s8 = s4 + pltpu.roll(s4, 4, 0)
    s16 = s8 + pltpu.roll(s8, 8, 0)
    pos1 = (s * tm + 1 + lax.broadcasted_iota(jnp.int32, (tm, 1), 0)).astype(F32)
    for g, (w, sw) in enumerate(zip(POOL_WINDOWS, (s2, s4, s8, s16))):
        cols = slice(g * POOL_GROUP, (g + 1) * POOL_GROUP)
        m = sw[POOL_HALO:, cols] / jnp.minimum(pos1, float(w)) - u[:, cols]
        y = _dot(m.astype(BF16), pw_ref[g]) + pb_ref[g]
        pm_ref[:, cols] = (y * ps_ref[:, cols]).astype(BF16)


def _inproj(x, mod3, norm_g, w_in_b, lb, pool_w_b, pool_b, pool_scale, tm):
    B, S, D = x.shape
    T = B * S
    nS = S // tm
    row = lambda b, s: (b * nS + s, 0)
    const2 = lambda b, s: (0, 0)
    const3 = lambda b, s: (0, 0, 0)
    half = lambda dt: jax.ShapeDtypeStruct((T, HG_WIDTH), dt)
    full = lambda dt: jax.ShapeDtypeStruct((T, D), dt)
    return pl.pallas_call(
        _inproj_kernel,
        out_shape=(half(F32), half(F32), half(F32), half(F32), half(BF16), half(BF16), full(BF16), full(BF16)),
        grid=(B, nS),
        in_specs=[pl.BlockSpec((1, tm, D), lambda b, s: (b, s, 0)),
                  pl.BlockSpec((1, 1, D), lambda b, s: (b, 0, 0)),
                  pl.BlockSpec((1, 1, D), lambda b, s: (b, 0, 1)),
                  pl.BlockSpec((1, D), const2),
                  pl.BlockSpec(w_in_b.shape, const2),
                  pl.BlockSpec((1, HG_WIDTH), const2),
                  pl.BlockSpec(pool_w_b.shape, const3),
                  pl.BlockSpec(pool_b.shape, const3),
                  pl.BlockSpec((1, POOL_WIDTH), const2)],
        out_specs=(pl.BlockSpec((tm, HG_WIDTH), row),) * 6 + (pl.BlockSpec((tm, D), row),) * 2,
        scratch_shapes=[pltpu.VMEM((POOL_HALO, POOL_WIDTH), F32)],
        compiler_params=pltpu.CompilerParams(
            dimension_semantics=("arbitrary", "arbitrary"), vmem_limit_bytes=VMEM_LIMIT),
        name="inproj",
    )(x, mod3, mod3, norm_g, w_in_b, lb, pool_w_b, pool_b, pool_scale)


def _hgrn_kernel(q_ref, lf_ref, k_ref, v_ref, sog_ref, gn_ref, o_ref, *st_refs):
    C = HG_CHUNK
    n_chunks = q_ref.shape[0] // C

    @pl.when(pl.program_id(1) == 0)
    def _():
        for st_ref in st_refs:
            st_ref[...] = jnp.zeros_like(st_ref)

    r_i = lax.broadcasted_iota(jnp.int32, (C, C), 0)
    c_i = lax.broadcasted_iota(jnp.int32, (C, C), 1)
    tril = (c_i <= r_i).astype(BF16)
    same_blk = (r_i // HG_BLK) == (c_i // HG_BLK)
    row = lax.broadcasted_iota(jnp.int32, (C, HG_DK), 0)
    row_in_sub = row % HG_SUB
    upper_half = (row % HG_BLK) >= HG_SUB
    row_blk = row // HG_BLK
    n_blk = C // HG_BLK

    def cumsum_rows(x):
        hi = x.astype(BF16)
        r1 = x - hi.astype(F32)
        mid = r1.astype(BF16)
        lo = (r1 - mid.astype(F32)).astype(BF16)
        return _dot(tril, hi) + _dot(tril, mid) + _dot(tril, lo)

    def block_rows(x, size, which):
        pieces = []
        for g in range(C // size):
            src = g * size + which
            pieces.append(jnp.zeros((size, x.shape[1]), F32) if src < 0
                          else jnp.broadcast_to(x[src:src + 1, :], (size, x.shape[1])))
        return jnp.concatenate(pieces, axis=0)

    def chunk(ci, carry):
        rs = pl.ds(pl.multiple_of(ci * C, C), C)
        b_all = cumsum_rows(lf_ref[rs, :])
        for h in range(HG_HEADS):
            cs = slice(h * HG_DK, (h + 1) * HG_DK)
            q = q_ref[rs, cs]
            k = k_ref[rs, cs]
            v = v_ref[rs, cs]
            b = b_all[:, cs]
            vb = v.astype(BF16)

            kt = k * jnp.exp(block_rows(b, HG_BLK, HG_BLK - 1) - b)
            q_parts, k_parts = [], []
            for j in range(n_blk - 1):
                bj = b[HG_BLK * j + HG_BLK - 1:HG_BLK * (j + 1), :]
                after = row >= HG_BLK * (j + 1)
                q_parts.append(q * jnp.exp(jnp.where(after, b - bj, -jnp.inf)))
                k_parts.append(jnp.where(row_blk == j, kt, 0.0))
            qcat = jnp.concatenate(q_parts, axis=1).astype(BF16)
            kcat = jnp.concatenate(k_parts, axis=1).astype(BF16)
            scores = _dot_nt(qcat, kcat)
            b_prev = block_rows(b, HG_SUB, -1)
            b_sub = block_rows(b, HG_SUB, HG_SUB - 1)
            qh = (q * jnp.exp(jnp.where(upper_half, b - b_prev, -jnp.inf))).astype(BF16)
            kh = jnp.where(upper_half, 0.0, k * jnp.exp(b_sub - b)).astype(BF16)
            scores = scores + jnp.where(same_blk, _dot_nt(qh, kh), 0.0)
            o = _dot(scores.astype(BF16), vb)

            o = o + jnp.sum(q * k, axis=-1, keepdims=True) * v
            for d in range(1, HG_SUB):
                kd = pltpu.roll(k, d, 0)
                bd = pltpu.roll(b, d, 0)
                vd = pltpu.roll(v, d, 0)
                e = jnp.exp(jnp.where(row_in_sub >= d, b - bd, -jnp.inf))
                o = o + jnp.sum(q * kd * e, axis=-1, keepdims=True) * vd

            st = st_refs[h][...]
            o = o + _dot_nt((q * jnp.exp(b)).astype(BF16), st.astype(BF16))
            b_end = b[C - 1:C, :]
            k_end = (k * jnp.exp(b_end - b)).astype(BF16)
            st_refs[h][...] = st * jnp.exp(b_end) + _dot_tn(vb, k_end)

            on = o * lax.rsqrt(jnp.mean(o * o, axis=-1, keepdims=True) + EPS) * gn_ref[...]
            o_ref[rs, cs] = (on * sog_ref[rs, cs].astype(F32)).astype(BF16)
        return carry

    lax.fori_loop(0, n_chunks, chunk, 0)


def _hgrn(q, lf, k, v, sog, gn, B, S, tb):
    T = B * S
    nS = S // tb
    row = lambda b, s: (b * nS + s, 0)
    blk = pl.BlockSpec((tb, HG_WIDTH), row)
    return pl.pallas_call(
        _hgrn_kernel,
        out_shape=jax.ShapeDtypeStruct((T, HG_WIDTH), BF16),
        grid=(B, nS),
        in_specs=[blk, blk, blk, blk, blk, pl.BlockSpec((1, HG_DK), lambda b, s: (0, 0))],
        out_specs=blk,
        scratch_shapes=[pltpu.VMEM((HG_DK, HG_DK), F32)] * HG_HEADS,
        compiler_params=pltpu.CompilerParams(
            dimension_semantics=("arbitrary", "arbitrary"), vmem_limit_bytes=VMEM_LIMIT),
        name="hgrn",
    )(q, lf, k, v, sog, gn)


def _split_kernel(w_ref, hi_ref, lo_ref):
    w = w_ref[...]
    hi = w.astype(BF16)
    hi_ref[...] = hi
    lo_ref[...] = (w - hi.astype(F32)).astype(BF16)


def _split_bf16(w):
    out = jax.ShapeDtypeStruct(w.shape, BF16)
    return pl.pallas_call(_split_kernel, out_shape=(out, out), name="split")(w)


def _mix_kernel(x_ref, oa_ref, pm_ref, sga_ref, sgb_ref, g1_ref, sh2_ref, sc2_ref, n2_ref,
                wua_ref, wub_ref, wo_ref, rw_hi_ref, rw_lo_ref, x1_ref, h2_ref, lg_ref):
    tm = x_ref.shape[1]
    ya = _dot(oa_ref[...], wua_ref[...])
    yb = _dot(pm_ref[...], wub_ref[...])
    mix = sga_ref[...].astype(F32) * ya + sgb_ref[...].astype(F32) * yb
    x1 = x_ref[0] + g1_ref[0] * _dot(mix.astype(BF16), wo_ref[...])
    x1_ref[...] = x1
    h2 = x1 * lax.rsqrt(jnp.mean(x1 * x1, axis=-1, keepdims=True) + EPS) * n2_ref[...]
    h2 = h2 * (1.0 + sc2_ref[0]) + sh2_ref[0]
    for j, chunk in enumerate(_row_chunks(h2)):
        h2_ref[pl.ds(j, tm, stride=ROW_TILES), :] = chunk
    h_hi = h2.astype(BF16)
    h_lo = (h2 - h_hi.astype(F32)).astype(BF16)
    rw_hi = rw_hi_ref[...]
    lg_ref[...] = _dot_nt(rw_hi, h_hi) + _dot_nt(rw_hi, h_lo) + _dot_nt(rw_lo_ref[...], h_hi)


def _mix(x, oa, pm, sga, sgb, mod3, norm2_g, wua, wub, wo, rw_hi, rw_lo, tm):
    B, S, D = x.shape
    T = B * S
    nS = S // tm
    row = lambda b, s: (b * nS + s, 0)
    const2 = lambda b, s: (0, 0)
    return pl.pallas_call(
        _mix_kernel,
        out_shape=(jax.ShapeDtypeStruct((T, D), F32),
                   jax.ShapeDtypeStruct((T * ROW_TILES, LANES), F32),
                   jax.ShapeDtypeStruct((N_EXPERTS, T), F32)),
        grid=(B, nS),
        in_specs=[pl.BlockSpec((1, tm, D), lambda b, s: (b, s, 0)),
                  pl.BlockSpec((tm, HG_WIDTH), row),
                  pl.BlockSpec((tm, POOL_WIDTH), row),
                  pl.BlockSpec((tm, D), row),
                  pl.BlockSpec((tm, D), row),
                  pl.BlockSpec((1, 1, D), lambda b, s: (b, 0, 2)),
                  pl.BlockSpec((1, 1, D), lambda b, s: (b, 0, 3)),
                  pl.BlockSpec((1, 1, D), lambda b, s: (b, 0, 4)),
                  pl.BlockSpec((1, D), const2),
                  pl.BlockSpec(wua.shape, const2),
                  pl.BlockSpec(wub.shape, const2),
                  pl.BlockSpec(wo.shape, const2),
                  pl.BlockSpec(rw_hi.shape, const2),
                  pl.BlockSpec(rw_lo.shape, const2)],
        out_specs=(pl.BlockSpec((tm, D), row),
                   pl.BlockSpec((tm * ROW_TILES, LANES), row),
                   pl.BlockSpec((N_EXPERTS, tm), lambda b, s: (0, b * nS + s))),
        compiler_params=pltpu.CompilerParams(
            dimension_semantics=("arbitrary", "arbitrary"), vmem_limit_bytes=VMEM_LIMIT),
        name="mix",
    )(x, oa, pm, sga, sgb, mod3, mod3, mod3, norm2_g, wua, wub, wo, rw_hi, rw_lo)


def _route_kernel(lg_ref, bias_ref, idx_ref, gate_ref, rank_ref, cnt_ref, carry_ref):
    tl = lg_ref.shape[1]
    neg = -jnp.inf

    @pl.when(pl.program_id(0) == 0)
    def _():
        carry_ref[...] = jnp.zeros_like(carry_ref)

    s = _sigmoid(lg_ref[...])
    biased = s + bias_ref[...]
    rowid = lax.broadcasted_iota(jnp.int32, (N_EXPERTS, tl), 0)

    def first_argmax(x, ids, sentinel):
        m = jnp.max(x, axis=0, keepdims=True)
        return jnp.min(jnp.where(x == m, ids, sentinel), axis=0, keepdims=True), m

    gscores = []
    for g in range(N_GROUPS):
        xg = biased[g * GROUP_SIZE:(g + 1) * GROUP_SIZE, :]
        rid = g * GROUP_SIZE + lax.broadcasted_iota(jnp.int32, (GROUP_SIZE, tl), 0)
        first, m1 = first_argmax(xg, rid, N_EXPERTS)
        m2 = jnp.max(jnp.where(rid == first, neg, xg), axis=0, keepdims=True)
        gscores.append(m1 + m2)
    blocks = []
    for g in range(N_GROUPS):
        beaten = jnp.zeros((1, tl), F32)
        for o in range(N_GROUPS):
            if o != g:
                wins = (gscores[o] >= gscores[g]) if o < g else (gscores[o] > gscores[g])
                beaten = beaten + jnp.where(wins, 1.0, 0.0)
        xg = biased[g * GROUP_SIZE:(g + 1) * GROUP_SIZE, :]
        blocks.append(jnp.where(beaten < float(TOPK_GROUPS), xg, neg))
    masked = jnp.concatenate(blocks, axis=0)

    idxs, gates = [], []
    chosen = jnp.zeros((N_EXPERTS, tl), F32)
    for _ in range(TOP_K):
        first, _m = first_argmax(masked, rowid, N_EXPERTS)
        sel = rowid == first
        gates.append(jnp.sum(jnp.where(sel, s, 0.0), axis=0, keepdims=True))
        idxs.append(first)
        chosen = jnp.where(sel, 1.0, chosen)
        masked = jnp.where(sel, neg, masked)
    gate_sum = functools.reduce(lambda a, b: a + b, gates)
    for k in range(TOP_K):
        gate_ref[k:k + 1, :] = gates[k] / gate_sum * ROUTED_SCALE
        idx_ref[k:k + 1, :] = idxs[k]

    lr = lax.broadcasted_iota(jnp.int32, (tl, tl), 0)
    lc = lax.broadcasted_iota(jnp.int32, (tl, tl), 1)
    prefix = (lr <= lc).astype(BF16)
    cnt_incl = _dot(chosen.astype(BF16), prefix)
    carry = carry_ref[...]
    rank_excl = cnt_incl - chosen + carry
    for k in range(TOP_K):
        rank_k = jnp.sum(jnp.where(rowid == idxs[k], rank_excl, 0.0), axis=0, keepdims=True)
        rank_ref[k:k + 1, :] = rank_k.astype(jnp.int32)
    carry = carry + jnp.sum(chosen, axis=1, keepdims=True)
    carry_ref[...] = carry
    cnt_ref[...] = carry.astype(jnp.int32)


def _route(logits_t, bias, tl):
    T = logits_t.shape[1]
    tok = lambda i: (0, i)
    return pl.pallas_call(
        _route_kernel,
        out_shape=(jax.ShapeDtypeStruct((TOP_K, T), jnp.int32),
                   jax.ShapeDtypeStruct((TOP_K, T), F32),
                   jax.ShapeDtypeStruct((TOP_K, T), jnp.int32),
                   jax.ShapeDtypeStruct((N_EXPERTS, 1), jnp.int32)),
        grid=(T // tl,),
        in_specs=[pl.BlockSpec((N_EXPERTS, tl), tok), pl.BlockSpec((N_EXPERTS, 1), lambda i: (0, 0))],
        out_specs=(pl.BlockSpec((TOP_K, tl), tok), pl.BlockSpec((TOP_K, tl), tok),
                   pl.BlockSpec((TOP_K, tl), tok), pl.BlockSpec((N_EXPERTS, 1), lambda i: (0, 0))),
        scratch_shapes=[pltpu.VMEM((N_EXPERTS, 1), F32)],
        compiler_params=pltpu.CompilerParams(
            dimension_semantics=("arbitrary",), vmem_limit_bytes=VMEM_LIMIT),
        name="route",
    )(logits_t, bias)


def _as_rows(ref):
    return ref.reshape(ref.shape[0] // ROW_TILES, ROW_TILES, LANES)


def _wait_rows(rows_ref, n, sem):
    pltpu.make_async_copy(rows_ref.at[pl.ds(0, n)], rows_ref.at[pl.ds(0, n)], sem).wait()


def _scatter_kernel(pos_ref, h2_ref, xs_ref, zero_ref, sem, zsem):
    src = _as_rows(h2_ref)
    dst = _as_rows(xs_ref)
    tt = src.shape[0]

    @pl.when(pl.program_id(0) == 0)
    def _():
        zero_ref[...] = jnp.zeros_like(zero_ref)
        tail = xs_ref.at[pl.ds(xs_ref.shape[0] - zero_ref.shape[0], zero_ref.shape[0])]
        fill = pltpu.make_async_copy(zero_ref, tail, zsem)
        fill.start()
        fill.wait()

    def start(t, c):
        for k in range(TOP_K):
            pltpu.make_async_copy(src.at[t], dst.at[pos_ref[k, t]], sem).start(priority=k % 2)
        return c

    lax.fori_loop(0, tt, start, 0)
    _wait_rows(dst, tt * TOP_K, sem)


def _scatter(pos, h2_tm, tt):
    n_rows = h2_tm.shape[0] // ROW_TILES * TOP_K + EXPERT_TILE
    return pl.pallas_call(
        _scatter_kernel,
        out_shape=jax.ShapeDtypeStruct((n_rows * ROW_TILES, LANES), F32),
        grid=(pos.shape[1] // tt,),
        in_specs=[pl.BlockSpec((TOP_K, tt), lambda i: (0, i), memory_space=pltpu.SMEM),
                  pl.BlockSpec((tt * ROW_TILES, LANES), lambda i: (i, 0))],
        out_specs=pl.BlockSpec(memory_space=pl.ANY),
        scratch_shapes=[pltpu.VMEM((EXPERT_TILE * ROW_TILES, LANES), F32),
                        pltpu.SemaphoreType.DMA, pltpu.SemaphoreType.DMA],
        compiler_params=pltpu.CompilerParams(
            dimension_semantics=("arbitrary",), vmem_limit_bytes=VMEM_LIMIT),
        name="scatter",
    )(pos, h2_tm)


def _experts_kernel(exp_ref, row0_ref, valid_ref, newe_ref, nitems_ref, slot_ref, nexte_ref,
                    xs_hbm, wg_hbm, wu_hbm, wd_hbm, ys_hbm,
                    xbuf_ref, ybuf_ref, sg_ref, su_ref, sd_ref, wgb_ref, wub_ref, wdb_ref, hm_ref,
                    xsem, ysem, wsem):
    i = pl.program_id(0)
    tr = EXPERT_TILE
    tile_rows = tr * ROW_TILES
    n_items = nitems_ref[0]
    part_sizes = tuple(tr >> (b + 1) for b in range(tr.bit_length() - 1))

    def ring(item, first_row=0, n_rows=tr):
        start = ((item % TILE_RING) * tr + first_row) * ROW_TILES
        return pl.ds(pl.multiple_of(start, ROW_TILES), n_rows * ROW_TILES)

    def hbm_rows(first_row, n_rows=tr):
        return pl.ds(pl.multiple_of(first_row * ROW_TILES, ROW_TILES), n_rows * ROW_TILES)

    def x_copy(item):
        return pltpu.make_async_copy(xs_hbm.at[hbm_rows(row0_ref[item])], xbuf_ref.at[ring(item)],
                                     xsem.at[item % TILE_RING])

    def y_copies(item, go):
        v = valid_ref[item]
        sem = ysem.at[item % TILE_RING]

        @pl.when(v == tr)
        def _():
            go(pltpu.make_async_copy(ybuf_ref.at[ring(item)], ys_hbm.at[hbm_rows(row0_ref[item])], sem))

        @pl.when(v != tr)
        def _():
            for size in part_sizes:
                @pl.when((v & size) != 0)
                def _():
                    first = v & ~(2 * size - 1)
                    go(pltpu.make_async_copy(ybuf_ref.at[ring(item, first, size)],
                                             ys_hbm.at[hbm_rows(row0_ref[item] + first, size)], sem))

    def weight_copies(e, slot):
        return (pltpu.make_async_copy(wg_hbm.at[e], sg_ref.at[slot], wsem.at[slot]),
                pltpu.make_async_copy(wu_hbm.at[e], su_ref.at[slot], wsem.at[slot]),
                pltpu.make_async_copy(wd_hbm.at[e], sd_ref.at[slot], wsem.at[slot]))

    a_on = i < n_items
    j = jnp.maximum(i - 1, 0)
    b_on = (i >= 1) & (i - 1 < n_items)
    e = exp_ref[i]

    @pl.when(i == 0)
    def _():
        hm_ref[...] = jnp.zeros_like(hm_ref)
        for i0 in range(TILE_AHEAD):
            @pl.when(i0 < n_items)
            def _():
                x_copy(i0).start()

    @pl.when(a_on)
    def _():
        @pl.when(i + TILE_AHEAD < n_items)
        def _():
            x_copy(i + TILE_AHEAD).start()

        x_copy(i).wait()

    @pl.when(b_on & (j >= TILE_RING))
    def _():
        y_copies(j - TILE_RING, lambda c: c.wait())

    @pl.when(a_on & (newe_ref[i] == 1))
    def _():
        slot = slot_ref[i]
        nxt = nexte_ref[i]

        @pl.when(i == 0)
        def _():
            for c in weight_copies(e, slot):
                c.start()

        @pl.when(nxt >= 0)
        def _():
            for c in weight_copies(nxt, 1 - slot):
                c.start()

        for c in weight_copies(e, slot):
            c.wait()
        wgb_ref[...] = sg_ref[slot].astype(BF16)
        wub_ref[...] = su_ref[slot].astype(BF16)
        wdb_ref[slot] = sd_ref[slot].astype(BF16)

    @pl.when(i <= n_items)
    def _():
        out_row = (j % TILE_RING) * tile_rows
        for c, chunk in enumerate(_row_chunks(_dot(hm_ref[j % 2], wdb_ref[slot_ref[j]]))):
            ybuf_ref[pl.ds(out_row + c, tr, stride=ROW_TILES), :] = chunk

        x = _load_rows(xbuf_ref, tr, first_row=(i % TILE_RING) * tr).astype(BF16)
        hm_ref[i % 2] = (_silu(_dot(x, wgb_ref[...])) * _dot(x, wub_ref[...])).astype(BF16)

    @pl.when(b_on)
    def _():
        y_copies(j, lambda c: c.start())

    @pl.when(b_on & (j == n_items - 1))
    def _():
        for back in range(TILE_RING):
            @pl.when(j - back >= 0)
            def _():
                y_copies(j - back, lambda c: c.wait())


def _experts(meta, xs, wg, wu, wd, n_items_max):
    tile_rows = EXPERT_TILE * ROW_TILES
    out_rows = xs.shape[0] - tile_rows
    hbm = pl.BlockSpec(memory_space=pl.ANY)
    n_slots = 2
    grid_spec = pltpu.PrefetchScalarGridSpec(
        num_scalar_prefetch=len(meta),
        grid=(n_items_max + 1,),
        in_specs=[hbm, hbm, hbm, hbm],
        out_specs=hbm,
        scratch_shapes=[pltpu.VMEM((TILE_RING * tile_rows, LANES), F32),
                        pltpu.VMEM((TILE_RING * tile_rows, LANES), F32),
                        pltpu.VMEM((n_slots, D_MODEL, D_EXPERT), F32),
                        pltpu.VMEM((n_slots, D_MODEL, D_EXPERT), F32),
                        pltpu.VMEM((n_slots, D_EXPERT, D_MODEL), F32),
                        pltpu.VMEM((D_MODEL, D_EXPERT), BF16),
                        pltpu.VMEM((D_MODEL, D_EXPERT), BF16),
                        pltpu.VMEM((n_slots, D_EXPERT, D_MODEL), BF16),
                        pltpu.VMEM((2, EXPERT_TILE, D_EXPERT), BF16),
                        pltpu.SemaphoreType.DMA((TILE_RING,)),
                        pltpu.SemaphoreType.DMA((TILE_RING,)),
                        pltpu.SemaphoreType.DMA((n_slots,))])
    return pl.pallas_call(
        _experts_kernel,
        out_shape=jax.ShapeDtypeStruct((out_rows, LANES), F32),
        grid_spec=grid_spec,
        compiler_params=pltpu.CompilerParams(
            dimension_semantics=("arbitrary",), vmem_limit_bytes=VMEM_LIMIT),
        name="experts",
    )(*meta, xs, wg, wu, wd)


def _combine_kernel(pos_ref, pos_next_ref, ys_ref, h2_ref, x1_ref, gate_ref, g2_ref, swg_ref, swu_ref, swd_ref,
                    fg_ref, out_ref, buf_a, buf_b, sem):
    i = pl.program_id(0)
    tt = x1_ref.shape[0] // 2
    src = _as_rows(ys_ref)

    def gather(p_ref, col0, buf, s):
        dst = _as_rows(buf)
        for t in range(tt):
            for k in range(TOP_K):
                pltpu.make_async_copy(src.at[p_ref[k, col0 + t]], dst.at[k * tt + t],
                                      sem.at[s]).start(priority=k % 2)

    @pl.when(i == 0)
    def _():
        dst = _as_rows(buf_a)

        def start(t, c):
            for k in range(TOP_K):
                pltpu.make_async_copy(src.at[pos_ref[k, t]], dst.at[k * tt + t], sem.at[0]).start(priority=k % 2)
            return c

        lax.fori_loop(0, tt, start, 0)

    def tile(row0, buf, s, prefetch):
        tok = pl.ds(row0, tt)
        _wait_rows(_as_rows(buf), tt * TOP_K, sem.at[s])
        prefetch()
        h2 = _load_rows(h2_ref, tt, first_row=row0).astype(BF16)
        hm = (_silu(_dot(h2, swg_ref[...])) * _dot(h2, swu_ref[...])).astype(BF16)
        gate = gate_ref[tok, :]
        ssq = jnp.zeros((tt, 1), F32)
        for c in range(ROW_TILES):
            cols = slice(c * LANES, (c + 1) * LANES)
            acc = _dot(hm, swd_ref[:, cols])
            for k in range(TOP_K):
                acc = acc + gate[:, k:k + 1] * buf[pl.ds(k * tt * ROW_TILES + c, tt, stride=ROW_TILES), :]
            x2 = x1_ref[tok, cols] + g2_ref[0, :, cols] * acc
            out_ref[tok, cols] = x2
            ssq = ssq + jnp.sum(x2 * x2, axis=-1, keepdims=True)
        out_ref[tok, :] = out_ref[tok, :] * lax.rsqrt(ssq * (1.0 / D_MODEL) + EPS) * fg_ref[...]

    tile(0, buf_a, 0, lambda: gather(pos_ref, tt, buf_b, 1))
    tile(tt, buf_b, 1, lambda: gather(pos_next_ref, 0, buf_a, 0))

    @pl.when(i == pl.num_programs(0) - 1)
    def _():
        _wait_rows(_as_rows(buf_a), tt * TOP_K, sem.at[0])


def _combine(pos, ys, h2_tm, x1, gate_tm, mod3, swg, swu, swd, fg, B, S, tt):
    T, D = x1.shape
    gather_rows = tt * TOP_K * ROW_TILES
    tt = 2 * tt
    nS = S // tt
    const2 = lambda i: (0, 0)
    n_tiles = T // tt
    return pl.pallas_call(
        _combine_kernel,
        out_shape=jax.ShapeDtypeStruct((T, D), F32),
        grid=(n_tiles,),
        in_specs=[pl.BlockSpec((TOP_K, tt), lambda i: (0, i), memory_space=pltpu.SMEM),
                  pl.BlockSpec((TOP_K, tt), lambda i: (0, jnp.minimum(i + 1, n_tiles - 1)),
                               memory_space=pltpu.SMEM),
                  pl.BlockSpec(memory_space=pl.ANY),
                  pl.BlockSpec((tt * ROW_TILES, LANES), lambda i: (i, 0)),
                  pl.BlockSpec((tt, D), lambda i: (i, 0)),
                  pl.BlockSpec((tt, TOP_K), lambda i: (i, 0)),
                  pl.BlockSpec((1, 1, D), lambda i: (i // nS, 0, 5)),
                  pl.BlockSpec(swg.shape, const2),
                  pl.BlockSpec(swu.shape, const2),
                  pl.BlockSpec(swd.shape, const2),
                  pl.BlockSpec((1, D), const2)],
        out_specs=pl.BlockSpec((tt, D), lambda i: (i, 0)),
        scratch_shapes=[pltpu.VMEM((gather_rows, LANES), F32),
                        pltpu.VMEM((gather_rows, LANES), F32),
                        pltpu.SemaphoreType.DMA((2,))],
        compiler_params=pltpu.CompilerParams(
            dimension_semantics=("arbitrary",), vmem_limit_bytes=VMEM_LIMIT),
        name="combine",
    )(pos, pos, ys, h2_tm, x1, gate_tm, mod3, swg, swu, swd, fg)


def _place_kernel(idx_ref, rank_ref, off_ref, pos_ref):
    tl = idx_ref.shape[1]
    rowid = lax.broadcasted_iota(jnp.int32, (N_EXPERTS, tl), 0)
    off = off_ref[...].astype(F32)
    for k in range(TOP_K):
        base = jnp.sum(jnp.where(rowid == idx_ref[k:k + 1, :], off, 0.0), axis=0, keepdims=True)
        pos_ref[k:k + 1, :] = base.astype(jnp.int32) + rank_ref[k:k + 1, :]


def _place(idx, rank, off, tl):
    T = idx.shape[1]
    tok = pl.BlockSpec((TOP_K, tl), lambda i: (0, i))
    return pl.pallas_call(
        _place_kernel,
        out_shape=jax.ShapeDtypeStruct((TOP_K, T), jnp.int32),
        grid=(T // tl,),
        in_specs=[tok, tok, pl.BlockSpec((N_EXPERTS, 1), lambda i: (0, 0))],
        out_specs=tok,
        name="place",
    )(idx, rank, off)


def _plan(counts, n_items_max):
    counts = counts[:, 0]
    off = jnp.cumsum(counts) - counts
    n_e = (counts + EXPERT_TILE - 1) // EXPERT_TILE
    item_end = jnp.cumsum(n_e)
    item_start = item_end - n_e
    n_items = item_end[-1]
    ids = jnp.arange(n_items_max + 1, dtype=jnp.int32)
    ids_c = jnp.minimum(ids, n_items - 1)
    item_e = jnp.minimum(jnp.sum(item_end[None, :] <= ids_c[:, None], axis=1), N_EXPERTS - 1).astype(jnp.int32)
    ids_e = jnp.arange(N_EXPERTS, dtype=jnp.int32)
    of_item = item_e[:, None] == ids_e[None, :]

    def per_item(table):
        return jnp.sum(jnp.where(of_item, table[None, :], 0), axis=1).astype(jnp.int32)

    first_row = (ids_c - per_item(item_start)) * EXPERT_TILE
    item_row0 = per_item(off) + first_row
    item_valid = jnp.minimum(per_item(counts) - first_row, EXPERT_TILE).astype(jnp.int32)
    prev_e = jnp.concatenate([jnp.full((1,), -1, jnp.int32), item_e[:-1]])
    item_newe = (item_e != prev_e).astype(jnp.int32)
    item_slot = ((jnp.cumsum(item_newe) - 1) % 2).astype(jnp.int32)
    later = jnp.where((counts[None, :] > 0) & (ids_e[None, :] > ids_e[:, None]), ids_e[None, :], N_EXPERTS)
    next_e = jnp.min(later, axis=1)
    next_e = jnp.where(next_e < N_EXPERTS, next_e, -1).astype(jnp.int32)
    meta = (item_e, item_row0, item_valid, item_newe, n_items.reshape(1).astype(jnp.int32),
            item_slot, per_item(next_e))
    return off.astype(jnp.int32).reshape(N_EXPERTS, 1), meta


def kernel(x, c, ada_w, ada_b, norm1_g, w_in, hgrn_lb, hgrn_norm_g, pool_w, pool_b, pool_scale, w_up_a, w_up_b, w_out, norm2_g, router_w, router_bias, exp_w_gate, exp_w_up, exp_w_down, shared_w_gate, shared_w_up, shared_w_down, final_norm_g):
    B, S, D = x.shape
    T = B * S
    assert ada_w.shape[0] == 1, "single-layer trunk only: the final norm is fused into the combine step"
    lb_all = jnp.cumsum(jax.nn.softmax(hgrn_lb.astype(F32), axis=0), axis=0)
    c_pad = jnp.zeros((SUBLANES, D), F32).at[:B].set(c)
    n_items_max = T * TOP_K // EXPERT_TILE + N_EXPERTS

    for l in range(1):
        mod = _ada(c_pad, ada_w[l], ada_b[l].reshape(1, -1))
        mod3 = mod[:B].reshape(B, 1, 6 * D)

        q, lf, k, v, sog, pm, sga, sgb = _inproj(
            x, mod3, norm1_g[l].reshape(1, D), w_in[l].astype(BF16), lb_all[l].reshape(1, HG_WIDTH),
            pool_w[l].astype(BF16), pool_b[l].reshape(len(POOL_WINDOWS), 1, POOL_GROUP),
            pool_scale[l].reshape(1, POOL_WIDTH), tm=256)
        oa = _hgrn(q, lf, k, v, sog, hgrn_norm_g[l].reshape(1, HG_DK), B, S, tb=512)

        rw_hi, rw_lo = _split_bf16(router_w[l].T)
        x1, h2_tm, logits_t = _mix(
            x, oa, pm, sga, sgb, mod3, norm2_g[l].reshape(1, D), w_up_a[l].astype(BF16),
            w_up_b[l].astype(BF16), w_out[l].astype(BF16), rw_hi, rw_lo, tm=512)

        idx, gate, rank, counts = _route(logits_t, router_bias[l].reshape(N_EXPERTS, 1), tl=256)
        off, meta = _plan(counts, n_items_max)
        pos = _place(idx, rank, off, tl=512)

        xs = _scatter(pos, h2_tm, tt=256)
        ys = _experts(meta, xs, exp_w_gate[l], exp_w_up[l], exp_w_down[l], n_items_max)

        tt_c = 128
        fg = final_norm_g.reshape(1, D)
        x = _combine(pos, ys, h2_tm, x1, gate.T, mod3,
                     shared_w_gate[l].astype(BF16), shared_w_up[l].astype(BF16),
                     shared_w_down[l].astype(BF16), fg, B, S, tt_c).reshape(B, S, D)
    return x
```

```python
import functools

import jax
import jax.numpy as jnp
from jax import lax
from jax.experimental import pallas as pl
from jax.experimental.pallas import tpu as pltpu

F32 = jnp.float32
BF16 = jnp.bfloat16
HIGHEST = lax.Precision.HIGHEST

D_MODEL = 1024
HG_WIDTH = 512
HG_DK = 128
HG_HEADS = 4
HG_CHUNK = 64
HG_BLK = 16
HG_SUB = 8
POOL_WIDTH = 512
POOL_WINDOWS = (2, 4, 8, 16)
POOL_GROUP = 128
POOL_HALO = 16
N_EXPERTS = 256
TOP_K = 8
N_GROUPS = 8
TOPK_GROUPS = 4
GROUP_SIZE = N_EXPERTS // N_GROUPS
D_EXPERT = 256
ROUTED_SCALE = 2.5
EPS = 1e-6

LANES = 128
SUBLANES = 8
ROW_TILES = D_MODEL // LANES
EXPERT_TILE = 256
TILE_RING = 8
TILE_AHEAD = TILE_RING - 1
VMEM_LIMIT = 56 * 1024 * 1024

COL_Q, COL_F, COL_I, COL_OG, COL_U, COL_GA, COL_GB = 0, 512, 1024, 1536, 2048, 2560, 3584


def _sigmoid(x):
    return 1.0 / (1.0 + jnp.exp(-x))


def _silu(x):
    return x * _sigmoid(x)


def _dot(a, b):
    return jnp.dot(a, b, preferred_element_type=F32)


def _dot_nt(a, b):
    return lax.dot_general(a, b, (((1,), (1,)), ((), ())), preferred_element_type=F32)


def _dot_tn(a, b):
    return lax.dot_general(a, b, (((0,), (0,)), ((), ())), preferred_element_type=F32)


def _row_chunks(x):
    return [x[:, j * LANES:(j + 1) * LANES] for j in range(ROW_TILES)]


def _rows_to_matrix(rows, n):
    chunks = pltpu.einshape("tcl->ctl", rows.reshape(n, ROW_TILES, LANES))
    return jnp.concatenate([chunks[j] for j in range(ROW_TILES)], axis=1)


def _load_rows(ref, n, first_row=0):
    return jnp.concatenate(
        [ref[pl.ds(first_row * ROW_TILES + j, n, stride=ROW_TILES), :] for j in range(ROW_TILES)], axis=1)


def _ada_kernel(c_ref, w_ref, b_ref, o_ref):
    cond = _silu(c_ref[...])
    o_ref[...] = jnp.dot(cond, w_ref[...], precision=HIGHEST, preferred_element_type=F32) + b_ref[...]


def _ada(c_pad, ada_w, ada_b):
    n = ada_w.shape[1]
    tn = 1536
    return pl.pallas_call(
        _ada_kernel,
        out_shape=jax.ShapeDtypeStruct((SUBLANES, n), F32),
        grid=(n // tn,),
        in_specs=[pl.BlockSpec((SUBLANES, D_MODEL), lambda j: (0, 0)),
                  pl.BlockSpec((D_MODEL, tn), lambda j: (0, j)),
                  pl.BlockSpec((1, tn), lambda j: (0, j))],
        out_specs=pl.BlockSpec((SUBLANES, tn), lambda j: (0, j)),
        compiler_params=pltpu.CompilerParams(vmem_limit_bytes=VMEM_LIMIT),
        name="ada",
    )(c_pad, ada_w, ada_b)


def _inproj_kernel(x_ref, sh_ref, sc_ref, g_ref, w_ref, lb_ref, pw_ref, pb_ref, ps_ref,
                   q_ref, lf_ref, k_ref, v_ref, sog_ref, pm_ref, sga_ref, sgb_ref, halo_ref):
    s = pl.program_id(1)
    tm = x_ref.shape[1]

    @pl.when(s == 0)
    def _():
        halo_ref[...] = jnp.zeros_like(halo_ref)

    x = x_ref[0]
    h = x * lax.rsqrt(jnp.mean(x * x, axis=-1, keepdims=True) + EPS) * g_ref[...]
    h = h * (1.0 + sc_ref[0]) + sh_ref[0]
    hb = h.astype(BF16)

    def proj(lo, n):
        return _dot(hb, w_ref[:, lo:lo + n])

    q = proj(COL_Q, HG_WIDTH)
    q_ref[...] = _silu(q) * (HG_DK ** -0.5)
    sig = _sigmoid(proj(COL_F, HG_WIDTH))
    lb = lb_ref[...]
    lf_ref[...] = jnp.log(lb + (1.0 - lb) * sig)
    k_ref[...] = (1.0 - lb) * (1.0 - sig)
    v_ref[...] = proj(COL_I, HG_WIDTH)
    sog_ref[...] = _silu(proj(COL_OG, HG_WIDTH)).astype(BF16)
    sga_ref[...] = _sigmoid(proj(COL_GA, D_MODEL)).astype(BF16)
    sgb_ref[...] = _sigmoid(proj(COL_GB, D_MODEL)).astype(BF16)

    u = proj(COL_U, POOL_WIDTH)
    ext = jnp.concatenate([halo_ref[...], u], axis=0)
    halo_ref[...] = u[tm - POOL_HALO:, :]
    s2 = ext + pltpu.roll(ext, 1, 0)
    s4 = s2 + pltpu.roll(s2, 2, 0)
    s8 = s4 + pltpu.roll(s4, 4, 0)
    s16 = s8 + pltpu.roll(s8, 8, 0)
    pos1 = (s * tm + 1 + lax.broadcasted_iota(jnp.int32, (tm, 1), 0)).astype(F32)
    for g, (w, sw) in enumerate(zip(POOL_WINDOWS, (s2, s4, s8, s16))):
        cols = slice(g * POOL_GROUP, (g + 1) * POOL_GROUP)
        m = sw[POOL_HALO:, cols] / jnp.minimum(pos1, float(w)) - u[:, cols]
        y = _dot(m.astype(BF16), pw_ref[g]) + pb_ref[g]
        pm_ref[:, cols] = (y * ps_ref[:, cols]).astype(BF16)


def _inproj(x, mod3, norm_g, w_in_b, lb, pool_w_b, pool_b, pool_scale, tm):
    B, S, D = x.shape
    T = B * S
    nS = S // tm
    row = lambda b, s: (b * nS + s, 0)
    const2 = lambda b, s: (0, 0)
    const3 = lambda b, s: (0, 0, 0)
    half = lambda dt: jax.ShapeDtypeStruct((T, HG_WIDTH), dt)
    full = lambda dt: jax.ShapeDtypeStruct((T, D), dt)
    return pl.pallas_call(
        _inproj_kernel,
        out_shape=(half(F32), half(F32), half(F32), half(F32), half(BF16), half(BF16), full(BF16), full(BF16)),
        grid=(B, nS),
        in_specs=[pl.BlockSpec((1, tm, D), lambda b, s: (b, s, 0)),
                  pl.BlockSpec((1, 1, D), lambda b, s: (b, 0, 0)),
                  pl.BlockSpec((1, 1, D), lambda b, s: (b, 0, 1)),
                  pl.BlockSpec((1, D), const2),
                  pl.BlockSpec(w_in_b.shape, const2),
                  pl.BlockSpec((1, HG_WIDTH), const2),
                  pl.BlockSpec(pool_w_b.shape, const3),
                  pl.BlockSpec(pool_b.shape, const3),
                  pl.BlockSpec((1, POOL_WIDTH), const2)],
        out_specs=(pl.BlockSpec((tm, HG_WIDTH), row),) * 6 + (pl.BlockSpec((tm, D), row),) * 2,
        scratch_shapes=[pltpu.VMEM((POOL_HALO, POOL_WIDTH), F32)],
        compiler_params=pltpu.CompilerParams(
            dimension_semantics=("arbitrary", "arbitrary"), vmem_limit_bytes=VMEM_LIMIT),
        name="inproj",
    )(x, mod3, mod3, norm_g, w_in_b, lb, pool_w_b, pool_b, pool_scale)


def _hgrn_kernel(q_ref, lf_ref, k_ref, v_ref, sog_ref, gn_ref, o_ref, *st_refs):
    C = HG_CHUNK
    n_chunks = q_ref.shape[0] // C

    @pl.when(pl.program_id(1) == 0)
    def _():
        for st_ref in st_refs:
            st_ref[...] = jnp.zeros_like(st_ref)

    r_i = lax.broadcasted_iota(jnp.int32, (C, C), 0)
    c_i = lax.broadcasted_iota(jnp.int32, (C, C), 1)
    tril = (c_i <= r_i).astype(BF16)
    same_blk = (r_i // HG_BLK) == (c_i // HG_BLK)
    row = lax.broadcasted_iota(jnp.int32, (C, HG_DK), 0)
    row_in_sub = row % HG_SUB
    upper_half = (row % HG_BLK) >= HG_SUB
    row_blk = row // HG_BLK
    n_blk = C // HG_BLK

    def cumsum_rows(x):
        hi = x.astype(BF16)
        r1 = x - hi.astype(F32)
        mid = r1.astype(BF16)
        lo = (r1 - mid.astype(F32)).astype(BF16)
        return _dot(tril, hi) + _dot(tril, mid) + _dot(tril, lo)

    def block_rows(x, size, which):
        pieces = []
        for g in range(C // size):
            src = g * size + which
            pieces.append(jnp.zeros((size, x.shape[1]), F32) if src < 0
                          else jnp.broadcast_to(x[src:src + 1, :], (size, x.shape[1])))
        return jnp.concatenate(pieces, axis=0)

    def chunk(ci, carry):
        rs = pl.ds(pl.multiple_of(ci * C, C), C)
        b_all = cumsum_rows(lf_ref[rs, :])
        for h in range(HG_HEADS):
            cs = slice(h * HG_DK, (h + 1) * HG_DK)
            q = q_ref[rs, cs]
            k = k_ref[rs, cs]
            v = v_ref[rs, cs]
            b = b_all[:, cs]
            vb = v.astype(BF16)

            kt = k * jnp.exp(block_rows(b, HG_BLK, HG_BLK - 1) - b)
            q_parts, k_parts = [], []
            for j in range(n_blk - 1):
                bj = b[HG_BLK * j + HG_BLK - 1:HG_BLK * (j + 1), :]
                after = row >= HG_BLK * (j + 1)
                q_parts.append(q * jnp.exp(jnp.where(after, b - bj, -jnp.inf)))
                k_parts.append(jnp.where(row_blk == j, kt, 0.0))
            qcat = jnp.concatenate(q_parts, axis=1).astype(BF16)
            kcat = jnp.concatenate(k_parts, axis=1).astype(BF16)
            scores = _dot_nt(qcat, kcat)
            b_prev = block_rows(b, HG_SUB, -1)
            b_sub = block_rows(b, HG_SUB, HG_SUB - 1)
            qh = (q * jnp.exp(jnp.where(upper_half, b - b_prev, -jnp.inf))).astype(BF16)
            kh = jnp.where(upper_half, 0.0, k * jnp.exp(b_sub - b)).astype(BF16)
            scores = scores + jnp.where(same_blk, _dot_nt(qh, kh), 0.0)
            o = _dot(scores.astype(BF16), vb)

            o = o + jnp.sum(q * k, axis=-1, keepdims=True) * v
            for d in range(1, HG_SUB):
                kd = pltpu.roll(k, d, 0)
                bd = pltpu.roll(b, d, 0)
                vd = pltpu.roll(v, d, 0)
                e = jnp.exp(jnp.where(row_in_sub >= d, b - bd, -jnp.inf))
                o = o + jnp.sum(q * kd * e, axis=-1, keepdims=True) * vd

            st = st_refs[h][...]
            o = o + _dot_nt((q * jnp.exp(b)).astype(BF16), st.astype(BF16))
            b_end = b[C - 1:C, :]
            k_end = (k * jnp.exp(b_end - b)).astype(BF16)
            st_refs[h][...] = st * jnp.exp(b_end) + _dot_tn(vb, k_end)

            on = o * lax.rsqrt(jnp.mean(o * o, axis=-1, keepdims=True) + EPS) * gn_ref[...]
            o_ref[rs, cs] = (on * sog_ref[rs, cs].astype(F32)).astype(BF16)
        return carry

    lax.fori_loop(0, n_chunks, chunk, 0)


def _hgrn(q, lf, k, v, sog, gn, B, S, tb):
    T = B * S
    nS = S // tb
    row = lambda b, s: (b * nS + s, 0)
    blk = pl.BlockSpec((tb, HG_WIDTH), row)
    return pl.pallas_call(
        _hgrn_kernel,
        out_shape=jax.ShapeDtypeStruct((T, HG_WIDTH), BF16),
        grid=(B, nS),
        in_specs=[blk, blk, blk, blk, blk, pl.BlockSpec((1, HG_DK), lambda b, s: (0, 0))],
        out_specs=blk,
        scratch_shapes=[pltpu.VMEM((HG_DK, HG_DK), F32)] * HG_HEADS,
        compiler_params=pltpu.CompilerParams(
            dimension_semantics=("arbitrary", "arbitrary"), vmem_limit_bytes=VMEM_LIMIT),
        name="hgrn",
    )(q, lf, k, v, sog, gn)


def _split_kernel(w_ref, hi_ref, lo_ref):
    w = w_ref[...]
    hi = w.astype(BF16)
    hi_ref[...] = hi
    lo_ref[...] = (w - hi.astype(F32)).astype(BF16)


def _split_bf16(w):
    out = jax.ShapeDtypeStruct(w.shape, BF16)
    return pl.pallas_call(_split_kernel, out_shape=(out, out), name="split")(w)


def _mix_kernel(x_ref, oa_ref, pm_ref, sga_ref, sgb_ref, g1_ref, sh2_ref, sc2_ref, n2_ref,
                wua_ref, wub_ref, wo_ref, rw_hi_ref, rw_lo_ref, x1_ref, h2_ref, lg_ref):
    tm = x_ref.shape[1]
    ya = _dot(oa_ref[...], wua_ref[...])
    yb = _dot(pm_ref[...], wub_ref[...])
    mix = sga_ref[...].astype(F32) * ya + sgb_ref[...].astype(F32) * yb
    x1 = x_ref[0] + g1_ref[0] * _dot(mix.astype(BF16), wo_ref[...])
    x1_ref[...] = x1
    h2 = x1 * lax.rsqrt(jnp.mean(x1 * x1, axis=-1, keepdims=True) + EPS) * n2_ref[...]
    h2 = h2 * (1.0 + sc2_ref[0]) + sh2_ref[0]
    for j, chunk in enumerate(_row_chunks(h2)):
        h2_ref[pl.ds(j, tm, stride=ROW_TILES), :] = chunk
    h_hi = h2.astype(BF16)
    h_lo = (h2 - h_hi.astype(F32)).astype(BF16)
    rw_hi = rw_hi_ref[...]
    lg_ref[...] = _dot_nt(rw_hi, h_hi) + _dot_nt(rw_hi, h_lo) + _dot_nt(rw_lo_ref[...], h_hi)


def _mix(x, oa, pm, sga, sgb, mod3, norm2_g, wua, wub, wo, rw_hi, rw_lo, tm):
    B, S, D = x.shape
    T = B * S
    nS = S // tm
    row = lambda b, s: (b * nS + s, 0)
    const2 = lambda b, s: (0, 0)
    return pl.pallas_call(
        _mix_kernel,
        out_shape=(jax.ShapeDtypeStruct((T, D), F32),
                   jax.ShapeDtypeStruct((T * ROW_TILES, LANES), F32),
                   jax.ShapeDtypeStruct((N_EXPERTS, T), F32)),
        grid=(B, nS),
        in_specs=[pl.BlockSpec((1, tm, D), lambda b, s: (b, s, 0)),
                  pl.BlockSpec((tm, HG_WIDTH), row),
                  pl.BlockSpec((tm, POOL_WIDTH), row),
                  pl.BlockSpec((tm, D), row),
                  pl.BlockSpec((tm, D), row),
                  pl.BlockSpec((1, 1, D), lambda b, s: (b, 0, 2)),
                  pl.BlockSpec((1, 1, D), lambda b, s: (b, 0, 3)),
                  pl.BlockSpec((1, 1, D), lambda b, s: (b, 0, 4)),
                  pl.BlockSpec((1, D), const2),
                  pl.BlockSpec(wua.shape, const2),
                  pl.BlockSpec(wub.shape, const2),
                  pl.BlockSpec(wo.shape, const2),
                  pl.BlockSpec(rw_hi.shape, const2),
                  pl.BlockSpec(rw_lo.shape, const2)],
        out_specs=(pl.BlockSpec((tm, D), row),
                   pl.BlockSpec((tm * ROW_TILES, LANES), row),
                   pl.BlockSpec((N_EXPERTS, tm), lambda b, s: (0, b * nS + s))),
        compiler_params=pltpu.CompilerParams(
            dimension_semantics=("arbitrary", "arbitrary"), vmem_limit_bytes=VMEM_LIMIT),
        name="mix",
    )(x, oa, pm, sga, sgb, mod3, mod3, mod3, norm2_g, wua, wub, wo, rw_hi, rw_lo)


def _route_kernel(lg_ref, bias_ref, idx_ref, gate_ref, rank_ref, cnt_ref, carry_ref):
    tl = lg_ref.shape[1]
    neg = -jnp.inf

    @pl.when(pl.program_id(0) == 0)
    def _():
        carry_ref[...] = jnp.zeros_like(carry_ref)

    s = _sigmoid(lg_ref[...])
    biased = s + bias_ref[...]
    rowid = lax.broadcasted_iota(jnp.int32, (N_EXPERTS, tl), 0)

    def first_argmax(x, ids, sentinel):
        m = jnp.max(x, axis=0, keepdims=True)
        return jnp.min(jnp.where(x == m, ids, sentinel), axis=0, keepdims=True), m

    gscores = []
    for g in range(N_GROUPS):
        xg = biased[g * GROUP_SIZE:(g + 1) * GROUP_SIZE, :]
        rid = g * GROUP_SIZE + lax.broadcasted_iota(jnp.int32, (GROUP_SIZE, tl), 0)
        first, m1 = first_argmax(xg, rid, N_EXPERTS)
        m2 = jnp.max(jnp.where(rid == first, neg, xg), axis=0, keepdims=True)
        gscores.append(m1 + m2)
    blocks = []
    for g in range(N_GROUPS):
        beaten = jnp.zeros((1, tl), F32)
        for o in range(N_GROUPS):
            if o != g:
                wins = (gscores[o] >= gscores[g]) if o < g else (gscores[o] > gscores[g])
                beaten = beaten + jnp.where(wins, 1.0, 0.0)
        xg = biased[g * GROUP_SIZE:(g + 1) * GROUP_SIZE, :]
        blocks.append(jnp.where(beaten < float(TOPK_GROUPS), xg, neg))
    masked = jnp.concatenate(blocks, axis=0)

    idxs, gates = [], []
    chosen = jnp.zeros((N_EXPERTS, tl), F32)
    for _ in range(TOP_K):
        first, _m = first_argmax(masked, rowid, N_EXPERTS)
        sel = rowid == first
        gates.append(jnp.sum(jnp.where(sel, s, 0.0), axis=0, keepdims=True))
        idxs.append(first)
        chosen = jnp.where(sel, 1.0, chosen)
        masked = jnp.where(sel, neg, masked)
    gate_sum = functools.reduce(lambda a, b: a + b, gates)
    for k in range(TOP_K):
        gate_ref[k:k + 1, :] = gates[k] / gate_sum * ROUTED_SCALE
        idx_ref[k:k + 1, :] = idxs[k]

    lr = lax.broadcasted_iota(jnp.int32, (tl, tl), 0)
    lc = lax.broadcasted_iota(jnp.int32, (tl, tl), 1)
    prefix = (lr <= lc).astype(BF16)
    cnt_incl = _dot(chosen.astype(BF16), prefix)
    carry = carry_ref[...]
    rank_excl = cnt_incl - chosen + carry
    for k in range(TOP_K):
        rank_k = jnp.sum(jnp.where(rowid == idxs[k], rank_excl, 0.0), axis=0, keepdims=True)
        rank_ref[k:k + 1, :] = rank_k.astype(jnp.int32)
    carry = carry + jnp.sum(chosen, axis=1, keepdims=True)
    carry_ref[...] = carry
    cnt_ref[...] = carry.astype(jnp.int32)


def _route(logits_t, bias, tl):
    T = logits_t.shape[1]
    tok = lambda i: (0, i)
    return pl.pallas_call(
        _route_kernel,
        out_shape=(jax.ShapeDtypeStruct((TOP_K, T), jnp.int32),
                   jax.ShapeDtypeStruct((TOP_K, T), F32),
                   jax.ShapeDtypeStruct((TOP_K, T), jnp.int32),
                   jax.ShapeDtypeStruct((N_EXPERTS, 1), jnp.int32)),
        grid=(T // tl,),
        in_specs=[pl.BlockSpec((N_EXPERTS, tl), tok), pl.BlockSpec((N_EXPERTS, 1), lambda i: (0, 0))],
        out_specs=(pl.BlockSpec((TOP_K, tl), tok), pl.BlockSpec((TOP_K, tl), tok),
                   pl.BlockSpec((TOP_K, tl), tok), pl.BlockSpec((N_EXPERTS, 1), lambda i: (0, 0))),
        scratch_shapes=[pltpu.VMEM((N_EXPERTS, 1), F32)],
        compiler_params=pltpu.CompilerParams(
            dimension_semantics=("arbitrary",), vmem_limit_bytes=VMEM_LIMIT),
        name="route",
    )(logits_t, bias)


def _as_rows(ref):
    return ref.reshape(ref.shape[0] // ROW_TILES, ROW_TILES, LANES)


def _wait_rows(rows_ref, n, sem):
    pltpu.make_async_copy(rows_ref.at[pl.ds(0, n)], rows_ref.at[pl.ds(0, n)], sem).wait()


def _scatter_kernel(pos_ref, h2_ref, xs_ref, zero_ref, sem, zsem):
    src = _as_rows(h2_ref)
    dst = _as_rows(xs_ref)
    tt = src.shape[0]

    @pl.when(pl.program_id(0) == 0)
    def _():
        zero_ref[...] = jnp.zeros_like(zero_ref)
        tail = xs_ref.at[pl.ds(xs_ref.shape[0] - zero_ref.shape[0], zero_ref.shape[0])]
        fill = pltpu.make_async_copy(zero_ref, tail, zsem)
        fill.start()
        fill.wait()

    def start(t, c):
        for k in range(TOP_K):
            pltpu.make_async_copy(src.at[t], dst.at[pos_ref[k, t]], sem).start(priority=k % 2)
        return c

    lax.fori_loop(0, tt, start, 0)
    _wait_rows(dst, tt * TOP_K, sem)


def _scatter(pos, h2_tm, tt):
    n_rows = h2_tm.shape[0] // ROW_TILES * TOP_K + EXPERT_TILE
    return pl.pallas_call(
        _scatter_kernel,
        out_shape=jax.ShapeDtypeStruct((n_rows * ROW_TILES, LANES), F32),
        grid=(pos.shape[1] // tt,),
        in_specs=[pl.BlockSpec((TOP_K, tt), lambda i: (0, i), memory_space=pltpu.SMEM),
                  pl.BlockSpec((tt * ROW_TILES, LANES), lambda i: (i, 0))],
        out_specs=pl.BlockSpec(memory_space=pl.ANY),
        scratch_shapes=[pltpu.VMEM((EXPERT_TILE * ROW_TILES, LANES), F32),
                        pltpu.SemaphoreType.DMA, pltpu.SemaphoreType.DMA],
        compiler_params=pltpu.CompilerParams(
            dimension_semantics=("arbitrary",), vmem_limit_bytes=VMEM_LIMIT),
        name="scatter",
    )(pos, h2_tm)


def _experts_kernel(exp_ref, row0_ref, valid_ref, newe_ref, nitems_ref, slot_ref, nexte_ref,
                    xs_hbm, wg_hbm, wu_hbm, wd_hbm, ys_hbm,
                    xbuf_ref, ybuf_ref, sg_ref, su_ref, sd_ref, wgb_ref, wub_ref, wdb_ref, hm_ref,
                    xsem, ysem, wsem):
    i = pl.program_id(0)
    tr = EXPERT_TILE
    tile_rows = tr * ROW_TILES
    n_items = nitems_ref[0]
    part_sizes = tuple(tr >> (b + 1) for b in range(tr.bit_length() - 1))

    def ring(item, first_row=0, n_rows=tr):
        start = ((item % TILE_RING) * tr + first_row) * ROW_TILES
        return pl.ds(pl.multiple_of(start, ROW_TILES), n_rows * ROW_TILES)

    def hbm_rows(first_row, n_rows=tr):
        return pl.ds(pl.multiple_of(first_row * ROW_TILES, ROW_TILES), n_rows * ROW_TILES)

    def x_copy(item):
        return pltpu.make_async_copy(xs_hbm.at[hbm_rows(row0_ref[item])], xbuf_ref.at[ring(item)],
                                     xsem.at[item % TILE_RING])

    def y_copies(item, go):
        v = valid_ref[item]
        sem = ysem.at[item % TILE_RING]

        @pl.when(v == tr)
        def _():
            go(pltpu.make_async_copy(ybuf_ref.at[ring(item)], ys_hbm.at[hbm_rows(row0_ref[item])], sem))

        @pl.when(v != tr)
        def _():
            for size in part_sizes:
                @pl.when((v & size) != 0)
                def _():
                    first = v & ~(2 * size - 1)
                    go(pltpu.make_async_copy(ybuf_ref.at[ring(item, first, size)],
                                             ys_hbm.at[hbm_rows(row0_ref[item] + first, size)], sem))

    def weight_copies(e, slot):
        return (pltpu.make_async_copy(wg_hbm.at[e], sg_ref.at[slot], wsem.at[slot]),
                pltpu.make_async_copy(wu_hbm.at[e], su_ref.at[slot], wsem.at[slot]),
                pltpu.make_async_copy(wd_hbm.at[e], sd_ref.at[slot], wsem.at[slot]))

    a_on = i < n_items
    j = jnp.maximum(i - 1, 0)
    b_on = (i >= 1) & (i - 1 < n_items)
    e = exp_ref[i]

    @pl.when(i == 0)
    def _():
        hm_ref[...] = jnp.zeros_like(hm_ref)
        for i0 in range(TILE_AHEAD):
            @pl.when(i0 < n_items)
            def _():
                x_copy(i0).start()

    @pl.when(a_on)
    def _():
        @pl.when(i + TILE_AHEAD < n_items)
        def _():
            x_copy(i + TILE_AHEAD).start()

        x_copy(i).wait()

    @pl.when(b_on & (j >= TILE_RING))
    def _():
        y_copies(j - TILE_RING, lambda c: c.wait())

    @pl.when(a_on & (newe_ref[i] == 1))
    def _():
        slot = slot_ref[i]
        nxt = nexte_ref[i]

        @pl.when(i == 0)
        def _():
            for c in weight_copies(e, slot):
                c.start()

        @pl.when(nxt >= 0)
        def _():
            for c in weight_copies(nxt, 1 - slot):
                c.start()

        for c in weight_copies(e, slot):
            c.wait()
        wgb_ref[...] = sg_ref[slot].astype(BF16)
        wub_ref[...] = su_ref[slot].astype(BF16)
        wdb_ref[slot] = sd_ref[slot].astype(BF16)

    @pl.when(i <= n_items)
    def _():
        out_row = (j % TILE_RING) * tile_rows
        for c, chunk in enumerate(_row_chunks(_dot(hm_ref[j % 2], wdb_ref[slot_ref[j]]))):
            ybuf_ref[pl.ds(out_row + c, tr, stride=ROW_TILES), :] = chunk

        x = _rows_to_matrix(xbuf_ref[ring(i), :], tr).astype(BF16)
        hm_ref[i % 2] = (_silu(_dot(x, wgb_ref[...])) * _dot(x, wub_ref[...])).astype(BF16)

    @pl.when(b_on)
    def _():
        y_copies(j, lambda c: c.start())

    @pl.when(b_on & (j == n_items - 1))
    def _():
        for back in range(TILE_RING):
            @pl.when(j - back >= 0)
            def _():
                y_copies(j - back, lambda c: c.wait())


def _experts(meta, xs, wg, wu, wd, n_items_max):
    tile_rows = EXPERT_TILE * ROW_TILES
    out_rows = xs.shape[0] - tile_rows
    hbm = pl.BlockSpec(memory_space=pl.ANY)
    n_slots = 2
    grid_spec = pltpu.PrefetchScalarGridSpec(
        num_scalar_prefetch=len(meta),
        grid=(n_items_max + 1,),
        in_specs=[hbm, hbm, hbm, hbm],
        out_specs=hbm,
        scratch_shapes=[pltpu.VMEM((TILE_RING * tile_rows, LANES), F32),
                        pltpu.VMEM((TILE_RING * tile_rows, LANES), F32),
                        pltpu.VMEM((n_slots, D_MODEL, D_EXPERT), F32),
                        pltpu.VMEM((n_slots, D_MODEL, D_EXPERT), F32),
                        pltpu.VMEM((n_slots, D_EXPERT, D_MODEL), F32),
                        pltpu.VMEM((D_MODEL, D_EXPERT), BF16),
                        pltpu.VMEM((D_MODEL, D_EXPERT), BF16),
                        pltpu.VMEM((n_slots, D_EXPERT, D_MODEL), BF16),
                        pltpu.VMEM((2, EXPERT_TILE, D_EXPERT), BF16),
                        pltpu.SemaphoreType.DMA((TILE_RING,)),
                        pltpu.SemaphoreType.DMA((TILE_RING,)),
                        pltpu.SemaphoreType.DMA((n_slots,))])
    return pl.pallas_call(
        _experts_kernel,
        out_shape=jax.ShapeDtypeStruct((out_rows, LANES), F32),
        grid_spec=grid_spec,
        compiler_params=pltpu.CompilerParams(
            dimension_semantics=("arbitrary",), vmem_limit_bytes=VMEM_LIMIT),
        name="experts",
    )(*meta, xs, wg, wu, wd)


def _combine_kernel(pos_ref, pos_next_ref, ys_ref, h2_ref, x1_ref, gate_ref, g2_ref, swg_ref, swu_ref, swd_ref,
                    fg_ref, out_ref, buf_a, buf_b, sem):
    i = pl.program_id(0)
    tt = x1_ref.shape[0] // 2
    src = _as_rows(ys_ref)

    def gather(p_ref, col0, buf, s):
        dst = _as_rows(buf)
        for t in range(tt):
            for k in range(TOP_K):
                pltpu.make_async_copy(src.at[p_ref[k, col0 + t]], dst.at[k * tt + t],
                                      sem.at[s]).start(priority=k % 2)

    @pl.when(i == 0)
    def _():
        dst = _as_rows(buf_a)

        def start(t, c):
            for k in range(TOP_K):
                pltpu.make_async_copy(src.at[pos_ref[k, t]], dst.at[k * tt + t], sem.at[0]).start(priority=k % 2)
            return c

        lax.fori_loop(0, tt, start, 0)

    def tile(row0, buf, s, prefetch):
        tok = pl.ds(row0, tt)
        _wait_rows(_as_rows(buf), tt * TOP_K, sem.at[s])
        prefetch()
        h2 = _load_rows(h2_ref, tt, first_row=row0).astype(BF16)
        hm = (_silu(_dot(h2, swg_ref[...])) * _dot(h2, swu_ref[...])).astype(BF16)
        gate = gate_ref[tok, :]
        ssq = jnp.zeros((tt, 1), F32)
        for c in range(ROW_TILES):
            cols = slice(c * LANES, (c + 1) * LANES)
            acc = _dot(hm, swd_ref[:, cols])
            for k in range(TOP_K):
                acc = acc + gate[:, k:k + 1] * buf[pl.ds(k * tt * ROW_TILES + c, tt, stride=ROW_TILES), :]
            x2 = x1_ref[tok, cols] + g2_ref[0, :, cols] * acc
            out_ref[tok, cols] = x2
            ssq = ssq + jnp.sum(x2 * x2, axis=-1, keepdims=True)
        out_ref[tok, :] = out_ref[tok, :] * lax.rsqrt(ssq * (1.0 / D_MODEL) + EPS) * fg_ref[...]

    tile(0, buf_a, 0, lambda: gather(pos_ref, tt, buf_b, 1))
    tile(tt, buf_b, 1, lambda: gather(pos_next_ref, 0, buf_a, 0))

    @pl.when(i == pl.num_programs(0) - 1)
    def _():
        _wait_rows(_as_rows(buf_a), tt * TOP_K, sem.at[0])


def _combine(pos, ys, h2_tm, x1, gate_tm, mod3, swg, swu, swd, fg, B, S, tt):
    T, D = x1.shape
    gather_rows = tt * TOP_K * ROW_TILES
    tt = 2 * tt
    nS = S // tt
    const2 = lambda i: (0, 0)
    n_tiles = T // tt
    return pl.pallas_call(
        _combine_kernel,
        out_shape=jax.ShapeDtypeStruct((T, D), F32),
        grid=(n_tiles,),
        in_specs=[pl.BlockSpec((TOP_K, tt), lambda i: (0, i), memory_space=pltpu.SMEM),
                  pl.BlockSpec((TOP_K, tt), lambda i: (0, jnp.minimum(i + 1, n_tiles - 1)),
                               memory_space=pltpu.SMEM),
                  pl.BlockSpec(memory_space=pl.ANY),
                  pl.BlockSpec((tt * ROW_TILES, LANES), lambda i: (i, 0)),
                  pl.BlockSpec((tt, D), lambda i: (i, 0)),
                  pl.BlockSpec((tt, TOP_K), lambda i: (i, 0)),
                  pl.BlockSpec((1, 1, D), lambda i: (i // nS, 0, 5)),
                  pl.BlockSpec(swg.shape, const2),
                  pl.BlockSpec(swu.shape, const2),
                  pl.BlockSpec(swd.shape, const2),
                  pl.BlockSpec((1, D), const2)],
        out_specs=pl.BlockSpec((tt, D), lambda i: (i, 0)),
        scratch_shapes=[pltpu.VMEM((gather_rows, LANES), F32),
                        pltpu.VMEM((gather_rows, LANES), F32),
                        pltpu.SemaphoreType.DMA((2,))],
        compiler_params=pltpu.CompilerParams(
            dimension_semantics=("arbitrary",), vmem_limit_bytes=VMEM_LIMIT),
        name="combine",
    )(pos, pos, ys, h2_tm, x1, gate_tm, mod3, swg, swu, swd, fg)


def _place_kernel(idx_ref, rank_ref, off_ref, pos_ref):
    tl = idx_ref.shape[1]
    rowid = lax.broadcasted_iota(jnp.int32, (N_EXPERTS, tl), 0)
    off = off_ref[...].astype(F32)
    for k in range(TOP_K):
        base = jnp.sum(jnp.where(rowid == idx_ref[k:k + 1, :], off, 0.0), axis=0, keepdims=True)
        pos_ref[k:k + 1, :] = base.astype(jnp.int32) + rank_ref[k:k + 1, :]


def _place(idx, rank, off, tl):
    T = idx.shape[1]
    tok = pl.BlockSpec((TOP_K, tl), lambda i: (0, i))
    return pl.pallas_call(
        _place_kernel,
        out_shape=jax.ShapeDtypeStruct((TOP_K, T), jnp.int32),
        grid=(T // tl,),
        in_specs=[tok, tok, pl.BlockSpec((N_EXPERTS, 1), lambda i: (0, 0))],
        out_specs=tok,
        name="place",
    )(idx, rank, off)


def _plan(counts, n_items_max):
    counts = counts[:, 0]
    off = jnp.cumsum(counts) - counts
    n_e = (counts + EXPERT_TILE - 1) // EXPERT_TILE
    item_end = jnp.cumsum(n_e)
    item_start = item_end - n_e
    n_items = item_end[-1]
    ids = jnp.arange(n_items_max + 1, dtype=jnp.int32)
    ids_c = jnp.minimum(ids, n_items - 1)
    item_e = jnp.minimum(jnp.sum(item_end[None, :] <= ids_c[:, None], axis=1), N_EXPERTS - 1).astype(jnp.int32)
    ids_e = jnp.arange(N_EXPERTS, dtype=jnp.int32)
    of_item = item_e[:, None] == ids_e[None, :]

    def per_item(table):
        return jnp.sum(jnp.where(of_item, table[None, :], 0), axis=1).astype(jnp.int32)

    first_row = (ids_c - per_item(item_start)) * EXPERT_TILE
    item_row0 = per_item(off) + first_row
    item_valid = jnp.minimum(per_item(counts) - first_row, EXPERT_TILE).astype(jnp.int32)
    prev_e = jnp.concatenate([jnp.full((1,), -1, jnp.int32), item_e[:-1]])
    item_newe = (item_e != prev_e).astype(jnp.int32)
    item_slot = ((jnp.cumsum(item_newe) - 1) % 2).astype(jnp.int32)
    later = jnp.where((counts[None, :] > 0) & (ids_e[None, :] > ids_e[:, None]), ids_e[None, :], N_EXPERTS)
    next_e = jnp.min(later, axis=1)
    next_e = jnp.where(next_e < N_EXPERTS, next_e, -1).astype(jnp.int32)
    meta = (item_e, item_row0, item_valid, item_newe, n_items.reshape(1).astype(jnp.int32),
            item_slot, per_item(next_e))
    return off.astype(jnp.int32).reshape(N_EXPERTS, 1), meta


def kernel(x, c, ada_w, ada_b, norm1_g, w_in, hgrn_lb, hgrn_norm_g, pool_w, pool_b, pool_scale, w_up_a, w_up_b, w_out, norm2_g, router_w, router_bias, exp_w_gate, exp_w_up, exp_w_down, shared_w_gate, shared_w_up, shared_w_down, final_norm_g):
    B, S, D = x.shape
    T = B * S
    assert ada_w.shape[0] == 1, "single-layer trunk only: the final norm is fused into the combine step"
    lb_all = jnp.cumsum(jax.nn.softmax(hgrn_lb.astype(F32), axis=0), axis=0)
    c_pad = jnp.zeros((SUBLANES, D), F32).at[:B].set(c)
    n_items_max = T * TOP_K // EXPERT_TILE + N_EXPERTS

    for l in range(1):
        mod = _ada(c_pad, ada_w[l], ada_b[l].reshape(1, -1))
        mod3 = mod[:B].reshape(B, 1, 6 * D)

        q, lf, k, v, sog, pm, sga, sgb = _inproj(
            x, mod3, norm1_g[l].reshape(1, D), w_in[l].astype(BF16), lb_all[l].reshape(1, HG_WIDTH),
            pool_w[l].astype(BF16), pool_b[l].reshape(len(POOL_WINDOWS), 1, POOL_GROUP),
            pool_scale[l].reshape(1, POOL_WIDTH), tm=256)
        oa = _hgrn(q, lf, k, v, sog, hgrn_norm_g[l].reshape(1, HG_DK), B, S, tb=512)

        rw_hi, rw_lo = _split_bf16(router_w[l].T)
        x1, h2_tm, logits_t = _mix(
            x, oa, pm, sga, sgb, mod3, norm2_g[l].reshape(1, D), w_up_a[l].astype(BF16),
            w_up_b[l].astype(BF16), w_out[l].astype(BF16), rw_hi, rw_lo, tm=512)

        idx, gate, rank, counts = _route(logits_t, router_bias[l].reshape(N_EXPERTS, 1), tl=256)
        off, meta = _plan(counts, n_items_max)
        pos = _place(idx, rank, off, tl=512)

        xs = _scatter(pos, h2_tm, tt=256)
        ys = _experts(meta, xs, exp_w_gate[l], exp_w_up[l], exp_w_down[l], n_items_max)

        tt_c = 128
        fg = final_norm_g.reshape(1, D)
        x = _combine(pos, ys, h2_tm, x1, gate.T, mod3,
                     shared_w_gate[l].astype(BF16), shared_w_up[l].astype(BF16),
                     shared_w_down[l].astype(BF16), fg, B, S, tt_c).reshape(B, S, D)
    return x
```

```python
import functools

import jax
import jax.numpy as jnp
from jax import lax
from jax.experimental import pallas as pl
from jax.experimental.pallas import tpu as pltpu

F32 = jnp.float32
BF16 = jnp.bfloat16
HIGHEST = lax.Precision.HIGHEST

D_MODEL = 1024
HG_WIDTH = 512
HG_DK = 128
HG_HEADS = 4
HG_CHUNK = 64
HG_BLK = 16
HG_SUB = 8
POOL_WIDTH = 512
POOL_WINDOWS = (2, 4, 8, 16)
POOL_GROUP = 128
POOL_HALO = 16
N_EXPERTS = 256
TOP_K = 8
N_GROUPS = 8
TOPK_GROUPS = 4
GROUP_SIZE = N_EXPERTS // N_GROUPS
D_EXPERT = 256
ROUTED_SCALE = 2.5
EPS = 1e-6

LANES = 128
SUBLANES = 8
ROW_TILES = D_MODEL // LANES
EXPERT_TILE = 256
TILE_RING = 8
TILE_AHEAD = TILE_RING - 1
VMEM_LIMIT = 56 * 1024 * 1024

COL_Q, COL_F, COL_I, COL_OG, COL_U, COL_GA, COL_GB = 0, 512, 1024, 1536, 2048, 2560, 3584


def _sigmoid(x):
    return 1.0 / (1.0 + jnp.exp(-x))


def _silu(x):
    return x * _sigmoid(x)


def _dot(a, b):
    return jnp.dot(a, b, preferred_element_type=F32)


def _dot_nt(a, b):
    return lax.dot_general(a, b, (((1,), (1,)), ((), ())), preferred_element_type=F32)


def _dot_tn(a, b):
    return lax.dot_general(a, b, (((0,), (0,)), ((), ())), preferred_element_type=F32)


def _row_chunks(x):
    return [x[:, j * LANES:(j + 1) * LANES] for j in range(ROW_TILES)]


def _rows_to_matrix(rows, n):
    groups = n // SUBLANES
    chunks = jnp.swapaxes(rows.reshape(groups, SUBLANES, ROW_TILES, LANES), 1, 2)
    return jnp.concatenate([chunks[:, j].reshape(n, LANES) for j in range(ROW_TILES)], axis=1)


def _load_rows(ref, n, first_row=0):
    return jnp.concatenate(
        [ref[pl.ds(first_row * ROW_TILES + j, n, stride=ROW_TILES), :] for j in range(ROW_TILES)], axis=1)


def _ada_kernel(c_ref, w_ref, b_ref, o_ref):
    cond = _silu(c_ref[...])
    o_ref[...] = jnp.dot(cond, w_ref[...], precision=HIGHEST, preferred_element_type=F32) + b_ref[...]


def _ada(c_pad, ada_w, ada_b):
    n = ada_w.shape[1]
    tn = 1536
    return pl.pallas_call(
        _ada_kernel,
        out_shape=jax.ShapeDtypeStruct((SUBLANES, n), F32),
        grid=(n // tn,),
        in_specs=[pl.BlockSpec((SUBLANES, D_MODEL), lambda j: (0, 0)),
                  pl.BlockSpec((D_MODEL, tn), lambda j: (0, j)),
                  pl.BlockSpec((1, tn), lambda j: (0, j))],
        out_specs=pl.BlockSpec((SUBLANES, tn), lambda j: (0, j)),
        compiler_params=pltpu.CompilerParams(vmem_limit_bytes=VMEM_LIMIT),
        name="ada",
    )(c_pad, ada_w, ada_b)


def _inproj_kernel(x_ref, sh_ref, sc_ref, g_ref, w_ref, lb_ref, pw_ref, pb_ref, ps_ref,
                   q_ref, lf_ref, k_ref, v_ref, sog_ref, pm_ref, sga_ref, sgb_ref, halo_ref):
    s = pl.program_id(1)
    tm = x_ref.shape[1]

    @pl.when(s == 0)
    def _():
        halo_ref[...] = jnp.zeros_like(halo_ref)

    x = x_ref[0]
    h = x * lax.rsqrt(jnp.mean(x * x, axis=-1, keepdims=True) + EPS) * g_ref[...]
    h = h * (1.0 + sc_ref[0]) + sh_ref[0]
    hb = h.astype(BF16)

    def proj(lo, n):
        return _dot(hb, w_ref[:, lo:lo + n])

    q = proj(COL_Q, HG_WIDTH)
    q_ref[...] = _silu(q) * (HG_DK ** -0.5)
    sig = _sigmoid(proj(COL_F, HG_WIDTH))
    lb = lb_ref[...]
    lf_ref[...] = jnp.log(lb + (1.0 - lb) * sig)
    k_ref[...] = (1.0 - lb) * (1.0 - sig)
    v_ref[...] = proj(COL_I, HG_WIDTH)
    sog_ref[...] = _silu(proj(COL_OG, HG_WIDTH)).astype(BF16)
    sga_ref[...] = _sigmoid(proj(COL_GA, D_MODEL)).astype(BF16)
    sgb_ref[...] = _sigmoid(proj(COL_GB, D_MODEL)).astype(BF16)

    u = proj(COL_U, POOL_WIDTH)
    ext = jnp.concatenate([halo_ref[...], u], axis=0)
    halo_ref[...] = u[tm - POOL_HALO:, :]
    s2 = ext + pltpu.roll(ext, 1, 0)
    s4 = s2 + pltpu.roll(s2, 2, 0)
    s8 = s4 + pltpu.roll(s4, 4, 0)
    s16 = s8 + pltpu.roll(s8, 8, 0)
    pos1 = (s * tm + 1 + lax.broadcasted_iota(jnp.int32, (tm, 1), 0)).astype(F32)
    for g, (w, sw) in enumerate(zip(POOL_WINDOWS, (s2, s4, s8, s16))):
        cols = slice(g * POOL_GROUP, (g + 1) * POOL_GROUP)
        m = sw[POOL_HALO:, cols] / jnp.minimum(pos1, float(w)) - u[:, cols]
        y = _dot(m.astype(BF16), pw_ref[g]) + pb_ref[g]
        pm_ref[:, cols] = (y * ps_ref[:, cols]).astype(BF16)


def _inproj(x, mod3, norm_g, w_in_b, lb, pool_w_b, pool_b, pool_scale, tm):
    B, S, D = x.shape
    T = B * S
    nS = S // tm
    row = lambda b, s: (b * nS + s, 0)
    const2 = lambda b, s: (0, 0)
    const3 = lambda b, s: (0, 0, 0)
    half = lambda dt: jax.ShapeDtypeStruct((T, HG_WIDTH), dt)
    full = lambda dt: jax.ShapeDtypeStruct((T, D), dt)
    return pl.pallas_call(
        _inproj_kernel,
        out_shape=(half(F32), half(F32), half(F32), half(F32), half(BF16), half(BF16), full(BF16), full(BF16)),
        grid=(B, nS),
        in_specs=[pl.BlockSpec((1, tm, D), lambda b, s: (b, s, 0)),
                  pl.BlockSpec((1, 1, D), lambda b, s: (b, 0, 0)),
                  pl.BlockSpec((1, 1, D), lambda b, s: (b, 0, 1)),
                  pl.BlockSpec((1, D), const2),
                  pl.BlockSpec(w_in_b.shape, const2),
                  pl.BlockSpec((1, HG_WIDTH), const2),
                  pl.BlockSpec(pool_w_b.shape, const3),
                  pl.BlockSpec(pool_b.shape, const3),
                  pl.BlockSpec((1, POOL_WIDTH), const2)],
        out_specs=(pl.BlockSpec((tm, HG_WIDTH), row),) * 6 + (pl.BlockSpec((tm, D), row),) * 2,
        scratch_shapes=[pltpu.VMEM((POOL_HALO, POOL_WIDTH), F32)],
        compiler_params=pltpu.CompilerParams(
            dimension_semantics=("arbitrary", "arbitrary"), vmem_limit_bytes=VMEM_LIMIT),
        name="inproj",
    )(x, mod3, mod3, norm_g, w_in_b, lb, pool_w_b, pool_b, pool_scale)


def _hgrn_kernel(q_ref, lf_ref, k_ref, v_ref, sog_ref, gn_ref, o_ref, *st_refs):
    C = HG_CHUNK
    n_chunks = q_ref.shape[0] // C

    @pl.when(pl.program_id(1) == 0)
    def _():
        for st_ref in st_refs:
            st_ref[...] = jnp.zeros_like(st_ref)

    r_i = lax.broadcasted_iota(jnp.int32, (C, C), 0)
    c_i = lax.broadcasted_iota(jnp.int32, (C, C), 1)
    tril = (c_i <= r_i).astype(BF16)
    same_blk = (r_i // HG_BLK) == (c_i // HG_BLK)
    row = lax.broadcasted_iota(jnp.int32, (C, HG_DK), 0)
    row_in_sub = row % HG_SUB
    upper_half = (row % HG_BLK) >= HG_SUB
    row_blk = row // HG_BLK
    n_blk = C // HG_BLK

    def cumsum_rows(x):
        hi = x.astype(BF16)
        r1 = x - hi.astype(F32)
        mid = r1.astype(BF16)
        lo = (r1 - mid.astype(F32)).astype(BF16)
        return _dot(tril, hi) + _dot(tril, mid) + _dot(tril, lo)

    def block_rows(x, size, which):
        pieces = []
        for g in range(C // size):
            src = g * size + which
            pieces.append(jnp.zeros((size, x.shape[1]), F32) if src < 0
                          else jnp.broadcast_to(x[src:src + 1, :], (size, x.shape[1])))
        return jnp.concatenate(pieces, axis=0)

    def chunk(ci, carry):
        rs = pl.ds(pl.multiple_of(ci * C, C), C)
        b_all = cumsum_rows(lf_ref[rs, :])
        for h in range(HG_HEADS):
            cs = slice(h * HG_DK, (h + 1) * HG_DK)
            q = q_ref[rs, cs]
            k = k_ref[rs, cs]
            v = v_ref[rs, cs]
            b = b_all[:, cs]
            vb = v.astype(BF16)

            kt = k * jnp.exp(block_rows(b, HG_BLK, HG_BLK - 1) - b)
            q_parts, k_parts = [], []
            for j in range(n_blk - 1):
                bj = b[HG_BLK * j + HG_BLK - 1:HG_BLK * (j + 1), :]
                after = row >= HG_BLK * (j + 1)
                q_parts.append(q * jnp.exp(jnp.where(after, b - bj, -jnp.inf)))
                k_parts.append(jnp.where(row_blk == j, kt, 0.0))
            qcat = jnp.concatenate(q_parts, axis=1).astype(BF16)
            kcat = jnp.concatenate(k_parts, axis=1).astype(BF16)
            scores = _dot_nt(qcat, kcat)
            b_prev = block_rows(b, HG_SUB, -1)
            b_sub = block_rows(b, HG_SUB, HG_SUB - 1)
            qh = (q * jnp.exp(jnp.where(upper_half, b - b_prev, -jnp.inf))).astype(BF16)
            kh = jnp.where(upper_half, 0.0, k * jnp.exp(b_sub - b)).astype(BF16)
            scores = scores + jnp.where(same_blk, _dot_nt(qh, kh), 0.0)
            o = _dot(scores.astype(BF16), vb)

            o = o + jnp.sum(q * k, axis=-1, keepdims=True) * v
            for d in range(1, HG_SUB):
                kd = pltpu.roll(k, d, 0)
                bd = pltpu.roll(b, d, 0)
                vd = pltpu.roll(v, d, 0)
                e = jnp.exp(jnp.where(row_in_sub >= d, b - bd, -jnp.inf))
                o = o + jnp.sum(q * kd * e, axis=-1, keepdims=True) * vd

            st = st_refs[h][...]
            o = o + _dot_nt((q * jnp.exp(b)).astype(BF16), st.astype(BF16))
            b_end = b[C - 1:C, :]
            k_end = (k * jnp.exp(b_end - b)).astype(BF16)
            st_refs[h][...] = st * jnp.exp(b_end) + _dot_tn(vb, k_end)

            on = o * lax.rsqrt(jnp.mean(o * o, axis=-1, keepdims=True) + EPS) * gn_ref[...]
            o_ref[rs, cs] = (on * sog_ref[rs, cs].astype(F32)).astype(BF16)
        return carry

    lax.fori_loop(0, n_chunks, chunk, 0)


def _hgrn(q, lf, k, v, sog, gn, B, S, tb):
    T = B * S
    nS = S // tb
    row = lambda b, s: (b * nS + s, 0)
    blk = pl.BlockSpec((tb, HG_WIDTH), row)
    return pl.pallas_call(
        _hgrn_kernel,
        out_shape=jax.ShapeDtypeStruct((T, HG_WIDTH), BF16),
        grid=(B, nS),
        in_specs=[blk, blk, blk, blk, blk, pl.BlockSpec((1, HG_DK), lambda b, s: (0, 0))],
        out_specs=blk,
        scratch_shapes=[pltpu.VMEM((HG_DK, HG_DK), F32)] * HG_HEADS,
        compiler_params=pltpu.CompilerParams(
            dimension_semantics=("arbitrary", "arbitrary"), vmem_limit_bytes=VMEM_LIMIT),
        name="hgrn",
    )(q, lf, k, v, sog, gn)


def _split_kernel(w_ref, hi_ref, lo_ref):
    w = w_ref[...]
    hi = w.astype(BF16)
    hi_ref[...] = hi
    lo_ref[...] = (w - hi.astype(F32)).astype(BF16)


def _split_bf16(w):
    out = jax.ShapeDtypeStruct(w.shape, BF16)
    return pl.pallas_call(_split_kernel, out_shape=(out, out), name="split")(w)


def _mix_kernel(x_ref, oa_ref, pm_ref, sga_ref, sgb_ref, g1_ref, sh2_ref, sc2_ref, n2_ref,
                wua_ref, wub_ref, wo_ref, rw_hi_ref, rw_lo_ref, x1_ref, h2_ref, lg_ref):
    tm = x_ref.shape[1]
    ya = _dot(oa_ref[...], wua_ref[...])
    yb = _dot(pm_ref[...], wub_ref[...])
    mix = sga_ref[...].astype(F32) * ya + sgb_ref[...].astype(F32) * yb
    x1 = x_ref[0] + g1_ref[0] * _dot(mix.astype(BF16), wo_ref[...])
    x1_ref[...] = x1
    h2 = x1 * lax.rsqrt(jnp.mean(x1 * x1, axis=-1, keepdims=True) + EPS) * n2_ref[...]
    h2 = h2 * (1.0 + sc2_ref[0]) + sh2_ref[0]
    for j, chunk in enumerate(_row_chunks(h2)):
        h2_ref[pl.ds(j, tm, stride=ROW_TILES), :] = chunk
    h_hi = h2.astype(BF16)
    h_lo = (h2 - h_hi.astype(F32)).astype(BF16)
    rw_hi = rw_hi_ref[...]
    lg_ref[...] = _dot_nt(rw_hi, h_hi) + _dot_nt(rw_hi, h_lo) + _dot_nt(rw_lo_ref[...], h_hi)


def _mix(x, oa, pm, sga, sgb, mod3, norm2_g, wua, wub, wo, rw_hi, rw_lo, tm):
    B, S, D = x.shape
    T = B * S
    nS = S // tm
    row = lambda b, s: (b * nS + s, 0)
    const2 = lambda b, s: (0, 0)
    return pl.pallas_call(
        _mix_kernel,
        out_shape=(jax.ShapeDtypeStruct((T, D), F32),
                   jax.ShapeDtypeStruct((T * ROW_TILES, LANES), F32),
                   jax.ShapeDtypeStruct((N_EXPERTS, T), F32)),
        grid=(B, nS),
        in_specs=[pl.BlockSpec((1, tm, D), lambda b, s: (b, s, 0)),
                  pl.BlockSpec((tm, HG_WIDTH), row),
                  pl.BlockSpec((tm, POOL_WIDTH), row),
                  pl.BlockSpec((tm, D), row),
                  pl.BlockSpec((tm, D), row),
                  pl.BlockSpec((1, 1, D), lambda b, s: (b, 0, 2)),
                  pl.BlockSpec((1, 1, D), lambda b, s: (b, 0, 3)),
                  pl.BlockSpec((1, 1, D), lambda b, s: (b, 0, 4)),
                  pl.BlockSpec((1, D), const2),
                  pl.BlockSpec(wua.shape, const2),
                  pl.BlockSpec(wub.shape, const2),
                  pl.BlockSpec(wo.shape, const2),
                  pl.BlockSpec(rw_hi.shape, const2),
                  pl.BlockSpec(rw_lo.shape, const2)],
        out_specs=(pl.BlockSpec((tm, D), row),
                   pl.BlockSpec((tm * ROW_TILES, LANES), row),
                   pl.BlockSpec((N_EXPERTS, tm), lambda b, s: (0, b * nS + s))),
        compiler_params=pltpu.CompilerParams(
            dimension_semantics=("arbitrary", "arbitrary"), vmem_limit_bytes=VMEM_LIMIT),
        name="mix",
    )(x, oa, pm, sga, sgb, mod3, mod3, mod3, norm2_g, wua, wub, wo, rw_hi, rw_lo)


def _route_kernel(lg_ref, bias_ref, idx_ref, gate_ref, rank_ref, cnt_ref, carry_ref):
    tl = lg_ref.shape[1]
    neg = -jnp.inf

    @pl.when(pl.program_id(0) == 0)
    def _():
        carry_ref[...] = jnp.zeros_like(carry_ref)

    s = _sigmoid(lg_ref[...])
    biased = s + bias_ref[...]
    rowid = lax.broadcasted_iota(jnp.int32, (N_EXPERTS, tl), 0)

    def first_argmax(x, ids, sentinel):
        m = jnp.max(x, axis=0, keepdims=True)
        return jnp.min(jnp.where(x == m, ids, sentinel), axis=0, keepdims=True), m

    gscores = []
    for g in range(N_GROUPS):
        xg = biased[g * GROUP_SIZE:(g + 1) * GROUP_SIZE, :]
        rid = g * GROUP_SIZE + lax.broadcasted_iota(jnp.int32, (GROUP_SIZE, tl), 0)
        first, m1 = first_argmax(xg, rid, N_EXPERTS)
        m2 = jnp.max(jnp.where(rid == first, neg, xg), axis=0, keepdims=True)
        gscores.append(m1 + m2)
    blocks = []
    for g in range(N_GROUPS):
        beaten = jnp.zeros((1, tl), F32)
        for o in range(N_GROUPS):
            if o != g:
                wins = (gscores[o] >= gscores[g]) if o < g else (gscores[o] > gscores[g])
                beaten = beaten + jnp.where(wins, 1.0, 0.0)
        xg = biased[g * GROUP_SIZE:(g + 1) * GROUP_SIZE, :]
        blocks.append(jnp.where(beaten < float(TOPK_GROUPS), xg, neg))
    masked = jnp.concatenate(blocks, axis=0)

    idxs, gates = [], []
    chosen = jnp.zeros((N_EXPERTS, tl), F32)
    for _ in range(TOP_K):
        first, _m = first_argmax(masked, rowid, N_EXPERTS)
        sel = rowid == first
        gates.append(jnp.sum(jnp.where(sel, s, 0.0), axis=0, keepdims=True))
        idxs.append(first)
        chosen = jnp.where(sel, 1.0, chosen)
        masked = jnp.where(sel, neg, masked)
    gate_sum = functools.reduce(lambda a, b: a + b, gates)
    for k in range(TOP_K):
        gate_ref[k:k + 1, :] = gates[k] / gate_sum * ROUTED_SCALE
        idx_ref[k:k + 1, :] = idxs[k]

    lr = lax.broadcasted_iota(jnp.int32, (tl, tl), 0)
    lc = lax.broadcasted_iota(jnp.int32, (tl, tl), 1)
    prefix = (lr <= lc).astype(BF16)
    cnt_incl = _dot(chosen.astype(BF16), prefix)
    carry = carry_ref[...]
    rank_excl = cnt_incl - chosen + carry
    for k in range(TOP_K):
        rank_k = jnp.sum(jnp.where(rowid == idxs[k], rank_excl, 0.0), axis=0, keepdims=True)
        rank_ref[k:k + 1, :] = rank_k.astype(jnp.int32)
    carry = carry + jnp.sum(chosen, axis=1, keepdims=True)
    carry_ref[...] = carry
    cnt_ref[...] = carry.astype(jnp.int32)


def _route(logits_t, bias, tl):
    T = logits_t.shape[1]
    tok = lambda i: (0, i)
    return pl.pallas_call(
        _route_kernel,
        out_shape=(jax.ShapeDtypeStruct((TOP_K, T), jnp.int32),
                   jax.ShapeDtypeStruct((TOP_K, T), F32),
                   jax.ShapeDtypeStruct((TOP_K, T), jnp.int32),
                   jax.ShapeDtypeStruct((N_EXPERTS, 1), jnp.int32)),
        grid=(T // tl,),
        in_specs=[pl.BlockSpec((N_EXPERTS, tl), tok), pl.BlockSpec((N_EXPERTS, 1), lambda i: (0, 0))],
        out_specs=(pl.BlockSpec((TOP_K, tl), tok), pl.BlockSpec((TOP_K, tl), tok),
                   pl.BlockSpec((TOP_K, tl), tok), pl.BlockSpec((N_EXPERTS, 1), lambda i: (0, 0))),
        scratch_shapes=[pltpu.VMEM((N_EXPERTS, 1), F32)],
        compiler_params=pltpu.CompilerParams(
            dimension_semantics=("arbitrary",), vmem_limit_bytes=VMEM_LIMIT),
        name="route",
    )(logits_t, bias)


def _as_rows(ref):
    return ref.reshape(ref.shape[0] // ROW_TILES, ROW_TILES, LANES)


def _wait_rows(rows_ref, n, sem):
    pltpu.make_async_copy(rows_ref.at[pl.ds(0, n)], rows_ref.at[pl.ds(0, n)], sem).wait()


def _scatter_kernel(pos_ref, h2_ref, xs_ref, zero_ref, sem, zsem):
    src = _as_rows(h2_ref)
    dst = _as_rows(xs_ref)
    tt = src.shape[0]

    @pl.when(pl.program_id(0) == 0)
    def _():
        zero_ref[...] = jnp.zeros_like(zero_ref)
        tail = xs_ref.at[pl.ds(xs_ref.shape[0] - zero_ref.shape[0], zero_ref.shape[0])]
        fill = pltpu.make_async_copy(zero_ref, tail, zsem)
        fill.start()
        fill.wait()

    def start(t, c):
        for k in range(TOP_K):
            pltpu.make_async_copy(src.at[t], dst.at[pos_ref[k, t]], sem).start(priority=k % 2)
        return c

    lax.fori_loop(0, tt, start, 0)
    _wait_rows(dst, tt * TOP_K, sem)


def _scatter(pos, h2_tm, tt):
    n_rows = h2_tm.shape[0] // ROW_TILES * TOP_K + EXPERT_TILE
    return pl.pallas_call(
        _scatter_kernel,
        out_shape=jax.ShapeDtypeStruct((n_rows * ROW_TILES, LANES), F32),
        grid=(pos.shape[1] // tt,),
        in_specs=[pl.BlockSpec((TOP_K, tt), lambda i: (0, i), memory_space=pltpu.SMEM),
                  pl.BlockSpec((tt * ROW_TILES, LANES), lambda i: (i, 0))],
        out_specs=pl.BlockSpec(memory_space=pl.ANY),
        scratch_shapes=[pltpu.VMEM((EXPERT_TILE * ROW_TILES, LANES), F32),
                        pltpu.SemaphoreType.DMA, pltpu.SemaphoreType.DMA],
        compiler_params=pltpu.CompilerParams(
            dimension_semantics=("arbitrary",), vmem_limit_bytes=VMEM_LIMIT),
        name="scatter",
    )(pos, h2_tm)


def _experts_kernel(exp_ref, row0_ref, valid_ref, newe_ref, nitems_ref, slot_ref, nexte_ref,
                    xs_hbm, wg_hbm, wu_hbm, wd_hbm, ys_hbm,
                    xbuf_ref, ybuf_ref, sg_ref, su_ref, sd_ref, wgb_ref, wub_ref, wdb_ref, hm_ref,
                    xsem, ysem, wsem):
    i = pl.program_id(0)
    tr = EXPERT_TILE
    tile_rows = tr * ROW_TILES
    n_items = nitems_ref[0]
    part_sizes = tuple(tr >> (b + 1) for b in range(tr.bit_length() - 1))

    def ring(item, first_row=0, n_rows=tr):
        start = ((item % TILE_RING) * tr + first_row) * ROW_TILES
        return pl.ds(pl.multiple_of(start, ROW_TILES), n_rows * ROW_TILES)

    def hbm_rows(first_row, n_rows=tr):
        return pl.ds(pl.multiple_of(first_row * ROW_TILES, ROW_TILES), n_rows * ROW_TILES)

    def x_copy(item):
        return pltpu.make_async_copy(xs_hbm.at[hbm_rows(row0_ref[item])], xbuf_ref.at[ring(item)],
                                     xsem.at[item % TILE_RING])

    def y_copies(item, go):
        v = valid_ref[item]
        sem = ysem.at[item % TILE_RING]

        @pl.when(v == tr)
        def _():
            go(pltpu.make_async_copy(ybuf_ref.at[ring(item)], ys_hbm.at[hbm_rows(row0_ref[item])], sem))

        @pl.when(v != tr)
        def _():
            for size in part_sizes:
                @pl.when((v & size) != 0)
                def _():
                    first = v & ~(2 * size - 1)
                    go(pltpu.make_async_copy(ybuf_ref.at[ring(item, first, size)],
                                             ys_hbm.at[hbm_rows(row0_ref[item] + first, size)], sem))

    def weight_copies(e, slot):
        return (pltpu.make_async_copy(wg_hbm.at[e], sg_ref.at[slot], wsem.at[slot]),
                pltpu.make_async_copy(wu_hbm.at[e], su_ref.at[slot], wsem.at[slot]),
                pltpu.make_async_copy(wd_hbm.at[e], sd_ref.at[slot], wsem.at[slot]))

    a_on = i < n_items
    j = jnp.maximum(i - 1, 0)
    b_on = (i >= 1) & (i - 1 < n_items)
    e = exp_ref[i]

    @pl.when(i == 0)
    def _():
        hm_ref[...] = jnp.zeros_like(hm_ref)
        for i0 in range(TILE_AHEAD):
            @pl.when(i0 < n_items)
            def _():
                x_copy(i0).start()

    @pl.when(a_on)
    def _():
        @pl.when(i + TILE_AHEAD < n_items)
        def _():
            x_copy(i + TILE_AHEAD).start()

        x_copy(i).wait()

    @pl.when(b_on & (j >= TILE_RING))
    def _():
        y_copies(j - TILE_RING, lambda c: c.wait())

    @pl.when(a_on & (newe_ref[i] == 1))
    def _():
        slot = slot_ref[i]
        nxt = nexte_ref[i]

        @pl.when(i == 0)
        def _():
            for c in weight_copies(e, slot):
                c.start()

        @pl.when(nxt >= 0)
        def _():
            for c in weight_copies(nxt, 1 - slot):
                c.start()

        for c in weight_copies(e, slot):
            c.wait()
        wgb_ref[...] = sg_ref[slot].astype(BF16)
        wub_ref[...] = su_ref[slot].astype(BF16)
        wdb_ref[slot] = sd_ref[slot].astype(BF16)

    @pl.when(i <= n_items)
    def _():
        out_row = (j % TILE_RING) * tile_rows
        for c, chunk in enumerate(_row_chunks(_dot(hm_ref[j % 2], wdb_ref[slot_ref[j]]))):
            ybuf_ref[pl.ds(out_row + c, tr, stride=ROW_TILES), :] = chunk

        x = _rows_to_matrix(xbuf_ref[ring(i), :], tr).astype(BF16)
        hm_ref[i % 2] = (_silu(_dot(x, wgb_ref[...])) * _dot(x, wub_ref[...])).astype(BF16)

    @pl.when(b_on)
    def _():
        y_copies(j, lambda c: c.start())

    @pl.when(b_on & (j == n_items - 1))
    def _():
        for back in range(TILE_RING):
            @pl.when(j - back >= 0)
            def _():
                y_copies(j - back, lambda c: c.wait())


def _experts(meta, xs, wg, wu, wd, n_items_max):
    tile_rows = EXPERT_TILE * ROW_TILES
    out_rows = xs.shape[0] - tile_rows
    hbm = pl.BlockSpec(memory_space=pl.ANY)
    n_slots = 2
    grid_spec = pltpu.PrefetchScalarGridSpec(
        num_scalar_prefetch=len(meta),
        grid=(n_items_max + 1,),
        in_specs=[hbm, hbm, hbm, hbm],
        out_specs=hbm,
        scratch_shapes=[pltpu.VMEM((TILE_RING * tile_rows, LANES), F32),
                        pltpu.VMEM((TILE_RING * tile_rows, LANES), F32),
                        pltpu.VMEM((n_slots, D_MODEL, D_EXPERT), F32),
                        pltpu.VMEM((n_slots, D_MODEL, D_EXPERT), F32),
                        pltpu.VMEM((n_slots, D_EXPERT, D_MODEL), F32),
                        pltpu.VMEM((D_MODEL, D_EXPERT), BF16),
                        pltpu.VMEM((D_MODEL, D_EXPERT), BF16),
                        pltpu.VMEM((n_slots, D_EXPERT, D_MODEL), BF16),
                        pltpu.VMEM((2, EXPERT_TILE, D_EXPERT), BF16),
                        pltpu.SemaphoreType.DMA((TILE_RING,)),
                        pltpu.SemaphoreType.DMA((TILE_RING,)),
                        pltpu.SemaphoreType.DMA((n_slots,))])
    return pl.pallas_call(
        _experts_kernel,
        out_shape=jax.ShapeDtypeStruct((out_rows, LANES), F32),
        grid_spec=grid_spec,
        compiler_params=pltpu.CompilerParams(
            dimension_semantics=("arbitrary",), vmem_limit_bytes=VMEM_LIMIT),
        name="experts",
    )(*meta, xs, wg, wu, wd)


def _combine_kernel(pos_ref, pos_next_ref, ys_ref, h2_ref, x1_ref, gate_ref, g2_ref, swg_ref, swu_ref, swd_ref,
                    fg_ref, out_ref, buf_a, buf_b, sem):
    i = pl.program_id(0)
    tt = x1_ref.shape[0] // 2
    src = _as_rows(ys_ref)

    def gather(p_ref, col0, buf, s):
        dst = _as_rows(buf)
        for t in range(tt):
            for k in range(TOP_K):
                pltpu.make_async_copy(src.at[p_ref[k, col0 + t]], dst.at[k * tt + t],
                                      sem.at[s]).start(priority=k % 2)

    @pl.when(i == 0)
    def _():
        dst = _as_rows(buf_a)

        def start(t, c):
            for k in range(TOP_K):
                pltpu.make_async_copy(src.at[pos_ref[k, t]], dst.at[k * tt + t], sem.at[0]).start(priority=k % 2)
            return c

        lax.fori_loop(0, tt, start, 0)

    def tile(row0, buf, s, prefetch):
        tok = pl.ds(row0, tt)
        _wait_rows(_as_rows(buf), tt * TOP_K, sem.at[s])
        prefetch()
        h2 = _load_rows(h2_ref, tt, first_row=row0).astype(BF16)
        hm = (_silu(_dot(h2, swg_ref[...])) * _dot(h2, swu_ref[...])).astype(BF16)
        gate = gate_ref[tok, :]
        ssq = jnp.zeros((tt, 1), F32)
        for c in range(ROW_TILES):
            cols = slice(c * LANES, (c + 1) * LANES)
            acc = _dot(hm, swd_ref[:, cols])
            for k in range(TOP_K):
                acc = acc + gate[:, k:k + 1] * buf[pl.ds(k * tt * ROW_TILES + c, tt, stride=ROW_TILES), :]
            x2 = x1_ref[tok, cols] + g2_ref[0, :, cols] * acc
            out_ref[tok, cols] = x2
            ssq = ssq + jnp.sum(x2 * x2, axis=-1, keepdims=True)
        out_ref[tok, :] = out_ref[tok, :] * lax.rsqrt(ssq * (1.0 / D_MODEL) + EPS) * fg_ref[...]

    tile(0, buf_a, 0, lambda: gather(pos_ref, tt, buf_b, 1))
    tile(tt, buf_b, 1, lambda: gather(pos_next_ref, 0, buf_a, 0))

    @pl.when(i == pl.num_programs(0) - 1)
    def _():
        _wait_rows(_as_rows(buf_a), tt * TOP_K, sem.at[0])


def _combine(pos, ys, h2_tm, x1, gate_tm, mod3, swg, swu, swd, fg, B, S, tt):
    T, D = x1.shape
    gather_rows = tt * TOP_K * ROW_TILES
    tt = 2 * tt
    nS = S // tt
    const2 = lambda i: (0, 0)
    n_tiles = T // tt
    return pl.pallas_call(
        _combine_kernel,
        out_shape=jax.ShapeDtypeStruct((T, D), F32),
        grid=(n_tiles,),
        in_specs=[pl.BlockSpec((TOP_K, tt), lambda i: (0, i), memory_space=pltpu.SMEM),
                  pl.BlockSpec((TOP_K, tt), lambda i: (0, jnp.minimum(i + 1, n_tiles - 1)),
                               memory_space=pltpu.SMEM),
                  pl.BlockSpec(memory_space=pl.ANY),
                  pl.BlockSpec((tt * ROW_TILES, LANES), lambda i: (i, 0)),
                  pl.BlockSpec((tt, D), lambda i: (i, 0)),
                  pl.BlockSpec((tt, TOP_K), lambda i: (i, 0)),
                  pl.BlockSpec((1, 1, D), lambda i: (i // nS, 0, 5)),
                  pl.BlockSpec(swg.shape, const2),
                  pl.BlockSpec(swu.shape, const2),
                  pl.BlockSpec(swd.shape, const2),
                  pl.BlockSpec((1, D), const2)],
        out_specs=pl.BlockSpec((tt, D), lambda i: (i, 0)),
        scratch_shapes=[pltpu.VMEM((gather_rows, LANES), F32),
                        pltpu.VMEM((gather_rows, LANES), F32),
                        pltpu.SemaphoreType.DMA((2,))],
        compiler_params=pltpu.CompilerParams(
            dimension_semantics=("arbitrary",), vmem_limit_bytes=VMEM_LIMIT),
        name="combine",
    )(pos, pos, ys, h2_tm, x1, gate_tm, mod3, swg, swu, swd, fg)


def _place_kernel(idx_ref, rank_ref, off_ref, pos_ref):
    tl = idx_ref.shape[1]
    rowid = lax.broadcasted_iota(jnp.int32, (N_EXPERTS, tl), 0)
    off = off_ref[...].astype(F32)
    for k in range(TOP_K):
        base = jnp.sum(jnp.where(rowid == idx_ref[k:k + 1, :], off, 0.0), axis=0, keepdims=True)
        pos_ref[k:k + 1, :] = base.astype(jnp.int32) + rank_ref[k:k + 1, :]


def _place(idx, rank, off, tl):
    T = idx.shape[1]
    tok = pl.BlockSpec((TOP_K, tl), lambda i: (0, i))
    return pl.pallas_call(
        _place_kernel,
        out_shape=jax.ShapeDtypeStruct((TOP_K, T), jnp.int32),
        grid=(T // tl,),
        in_specs=[tok, tok, pl.BlockSpec((N_EXPERTS, 1), lambda i: (0, 0))],
        out_specs=tok,
        name="place",
    )(idx, rank, off)


def _plan(counts, n_items_max):
    counts = counts[:, 0]
    off = jnp.cumsum(counts) - counts
    n_e = (counts + EXPERT_TILE - 1) // EXPERT_TILE
    item_end = jnp.cumsum(n_e)
    item_start = item_end - n_e
    n_items = item_end[-1]
    ids = jnp.arange(n_items_max + 1, dtype=jnp.int32)
    ids_c = jnp.minimum(ids, n_items - 1)
    item_e = jnp.minimum(jnp.sum(item_end[None, :] <= ids_c[:, None], axis=1), N_EXPERTS - 1).astype(jnp.int32)
    ids_e = jnp.arange(N_EXPERTS, dtype=jnp.int32)
    of_item = item_e[:, None] == ids_e[None, :]

    def per_item(table):
        return jnp.sum(jnp.where(of_item, table[None, :], 0), axis=1).astype(jnp.int32)

    first_row = (ids_c - per_item(item_start)) * EXPERT_TILE
    item_row0 = per_item(off) + first_row
    item_valid = jnp.minimum(per_item(counts) - first_row, EXPERT_TILE).astype(jnp.int32)
    prev_e = jnp.concatenate([jnp.full((1,), -1, jnp.int32), item_e[:-1]])
    item_newe = (item_e != prev_e).astype(jnp.int32)
    item_slot = ((jnp.cumsum(item_newe) - 1) % 2).astype(jnp.int32)
    later = jnp.where((counts[None, :] > 0) & (ids_e[None, :] > ids_e[:, None]), ids_e[None, :], N_EXPERTS)
    next_e = jnp.min(later, axis=1)
    next_e = jnp.where(next_e < N_EXPERTS, next_e, -1).astype(jnp.int32)
    meta = (item_e, item_row0, item_valid, item_newe, n_items.reshape(1).astype(jnp.int32),
            item_slot, per_item(next_e))
    return off.astype(jnp.int32).reshape(N_EXPERTS, 1), meta


def kernel(x, c, ada_w, ada_b, norm1_g, w_in, hgrn_lb, hgrn_norm_g, pool_w, pool_b, pool_scale, w_up_a, w_up_b, w_out, norm2_g, router_w, router_bias, exp_w_gate, exp_w_up, exp_w_down, shared_w_gate, shared_w_up, shared_w_down, final_norm_g):
    B, S, D = x.shape
    T = B * S
    assert ada_w.shape[0] == 1, "single-layer trunk only: the final norm is fused into the combine step"
    lb_all = jnp.cumsum(jax.nn.softmax(hgrn_lb.astype(F32), axis=0), axis=0)
    c_pad = jnp.zeros((SUBLANES, D), F32).at[:B].set(c)
    n_items_max = T * TOP_K // EXPERT_TILE + N_EXPERTS

    for l in range(1):
        mod = _ada(c_pad, ada_w[l], ada_b[l].reshape(1, -1))
        mod3 = mod[:B].reshape(B, 1, 6 * D)

        q, lf, k, v, sog, pm, sga, sgb = _inproj(
            x, mod3, norm1_g[l].reshape(1, D), w_in[l].astype(BF16), lb_all[l].reshape(1, HG_WIDTH),
            pool_w[l].astype(BF16), pool_b[l].reshape(len(POOL_WINDOWS), 1, POOL_GROUP),
            pool_scale[l].reshape(1, POOL_WIDTH), tm=256)
        oa = _hgrn(q, lf, k, v, sog, hgrn_norm_g[l].reshape(1, HG_DK), B, S, tb=512)

        rw_hi, rw_lo = _split_bf16(router_w[l].T)
        x1, h2_tm, logits_t = _mix(
            x, oa, pm, sga, sgb, mod3, norm2_g[l].reshape(1, D), w_up_a[l].astype(BF16),
            w_up_b[l].astype(BF16), w_out[l].astype(BF16), rw_hi, rw_lo, tm=512)

        idx, gate, rank, counts = _route(logits_t, router_bias[l].reshape(N_EXPERTS, 1), tl=256)
        off, meta = _plan(counts, n_items_max)
        pos = _place(idx, rank, off, tl=512)

        xs = _scatter(pos, h2_tm, tt=256)
        ys = _experts(meta, xs, exp_w_gate[l], exp_w_up[l], exp_w_down[l], n_items_max)

        tt_c = 128
        fg = final_norm_g.reshape(1, D)
        x = _combine(pos, ys, h2_tm, x1, gate.T, mod3,
                     shared_w_gate[l].astype(BF16), shared_w_up[l].astype(BF16),
                     shared_w_down[l].astype(BF16), fg, B, S, tt_c).reshape(B, S, D)
    return x
```

```python
import functools

import jax
import jax.numpy as jnp
from jax import lax
from jax.experimental import pallas as pl
from jax.experimental.pallas import tpu as pltpu

F32 = jnp.float32
BF16 = jnp.bfloat16
HIGHEST = lax.Precision.HIGHEST

D_MODEL = 1024
HG_WIDTH = 512
HG_DK = 128
HG_HEADS = 4
HG_CHUNK = 64
HG_BLK = 16
HG_SUB = 8
HG_UNROLL = 8
POOL_WIDTH = 512
POOL_WINDOWS = (2, 4, 8, 16)
POOL_GROUP = 128
POOL_HALO = 16
N_EXPERTS = 256
TOP_K = 8
N_GROUPS = 8
TOPK_GROUPS = 4
GROUP_SIZE = N_EXPERTS // N_GROUPS
D_EXPERT = 256
ROUTED_SCALE = 2.5
EPS = 1e-6

LANES = 128
SUBLANES = 8
ROW_TILES = D_MODEL // LANES
EXPERT_TILE = 256
TILE_RING = 8
TILE_AHEAD = TILE_RING - 1
VMEM_LIMIT = 56 * 1024 * 1024

COL_Q, COL_F, COL_I, COL_OG, COL_U, COL_GA, COL_GB = 0, 512, 1024, 1536, 2048, 2560, 3584


def _sigmoid(x):
    return 1.0 / (1.0 + jnp.exp(-x))


def _silu(x):
    return x * _sigmoid(x)


def _dot(a, b):
    return jnp.dot(a, b, preferred_element_type=F32)


def _dot_nt(a, b):
    return lax.dot_general(a, b, (((1,), (1,)), ((), ())), preferred_element_type=F32)


def _dot_tn(a, b):
    return lax.dot_general(a, b, (((0,), (0,)), ((), ())), preferred_element_type=F32)


def _row_chunks(x):
    return [x[:, j * LANES:(j + 1) * LANES] for j in range(ROW_TILES)]


def _rows_to_matrix(rows, n):
    groups = n // SUBLANES
    chunks = jnp.swapaxes(rows.reshape(groups, SUBLANES, ROW_TILES, LANES), 1, 2)
    return jnp.concatenate([chunks[:, j].reshape(n, LANES) for j in range(ROW_TILES)], axis=1)


def _load_rows(ref, n, first_row=0):
    return jnp.concatenate(
        [ref[pl.ds(first_row * ROW_TILES + j, n, stride=ROW_TILES), :] for j in range(ROW_TILES)], axis=1)


def _ada_kernel(c_ref, w_ref, b_ref, o_ref):
    cond = _silu(c_ref[...])
    o_ref[...] = jnp.dot(cond, w_ref[...], precision=HIGHEST, preferred_element_type=F32) + b_ref[...]


def _ada(c_pad, ada_w, ada_b):
    n = ada_w.shape[1]
    tn = 1536
    return pl.pallas_call(
        _ada_kernel,
        out_shape=jax.ShapeDtypeStruct((SUBLANES, n), F32),
        grid=(n // tn,),
        in_specs=[pl.BlockSpec((SUBLANES, D_MODEL), lambda j: (0, 0)),
                  pl.BlockSpec((D_MODEL, tn), lambda j: (0, j)),
                  pl.BlockSpec((1, tn), lambda j: (0, j))],
        out_specs=pl.BlockSpec((SUBLANES, tn), lambda j: (0, j)),
        compiler_params=pltpu.CompilerParams(vmem_limit_bytes=VMEM_LIMIT),
        name="ada",
    )(c_pad, ada_w, ada_b)


def _inproj_kernel(x_ref, sh_ref, sc_ref, g_ref, w_ref, lb_ref, pw_ref, pb_ref, ps_ref,
                   q_ref, lf_ref, k_ref, v_ref, sog_ref, pm_ref, sga_ref, sgb_ref, halo_ref):
    s = pl.program_id(1)
    tm = x_ref.shape[1]

    @pl.when(s == 0)
    def _():
        halo_ref[...] = jnp.zeros_like(halo_ref)

    x = x_ref[0]
    h = x * lax.rsqrt(jnp.mean(x * x, axis=-1, keepdims=True) + EPS) * g_ref[...]
    h = h * (1.0 + sc_ref[0]) + sh_ref[0]
    hb = h.astype(BF16)

    def proj(lo, n):
        return _dot(hb, w_ref[:, lo:lo + n])

    q = proj(COL_Q, HG_WIDTH)
    q_ref[...] = _silu(q) * (HG_DK ** -0.5)
    sig = _sigmoid(proj(COL_F, HG_WIDTH))
    lb = lb_ref[...]
    lf_ref[...] = jnp.log(lb + (1.0 - lb) * sig)
    k_ref[...] = (1.0 - lb) * (1.0 - sig)
    v_ref[...] = proj(COL_I, HG_WIDTH)
    sog_ref[...] = _silu(proj(COL_OG, HG_WIDTH)).astype(BF16)
    sga_ref[...] = _sigmoid(proj(COL_GA, D_MODEL)).astype(BF16)
    sgb_ref[...] = _sigmoid(proj(COL_GB, D_MODEL)).astype(BF16)

    u = proj(COL_U, POOL_WIDTH)
    ext = jnp.concatenate([halo_ref[...], u], axis=0)
    halo_ref[...] = u[tm - POOL_HALO:, :]
    s2 = ext + pltpu.roll(ext, 1, 0)
    s4 = s2 + pltpu.roll(s2, 2, 0)
    s8 = s4 + pltpu.roll(s4, 4, 0)
    s16 = s8 + pltpu.roll(s8, 8, 0)
    pos1 = (s * tm + 1 + lax.broadcasted_iota(jnp.int32, (tm, 1), 0)).astype(F32)
    for g, (w, sw) in enumerate(zip(POOL_WINDOWS, (s2, s4, s8, s16))):
        cols = slice(g * POOL_GROUP, (g + 1) * POOL_GROUP)
        m = sw[POOL_HALO:, cols] / jnp.minimum(pos1, float(w)) - u[:, cols]
        y = _dot(m.astype(BF16), pw_ref[g]) + pb_ref[g]
        pm_ref[:, cols] = (y * ps_ref[:, cols]).astype(BF16)


def _inproj(x, mod3, norm_g, w_in_b, lb, pool_w_b, pool_b, pool_scale, tm):
    B, S, D = x.shape
    T = B * S
    nS = S // tm
    row = lambda b, s: (b * nS + s, 0)
    const2 = lambda b, s: (0, 0)
    const3 = lambda b, s: (0, 0, 0)
    half = lambda dt: jax.ShapeDtypeStruct((T, HG_WIDTH), dt)
    full = lambda dt: jax.ShapeDtypeStruct((T, D), dt)
    return pl.pallas_call(
        _inproj_kernel,
        out_shape=(half(F32), half(F32), half(F32), half(F32), half(BF16), half(BF16), full(BF16), full(BF16)),
        grid=(B, nS),
        in_specs=[pl.BlockSpec((1, tm, D), lambda b, s: (b, s, 0)),
                  pl.BlockSpec((1, 1, D), lambda b, s: (b, 0, 0)),
                  pl.BlockSpec((1, 1, D), lambda b, s: (b, 0, 1)),
                  pl.BlockSpec((1, D), const2),
                  pl.BlockSpec(w_in_b.shape, const2),
                  pl.BlockSpec((1, HG_WIDTH), const2),
                  pl.BlockSpec(pool_w_b.shape, const3),
                  pl.BlockSpec(pool_b.shape, const3),
                  pl.BlockSpec((1, POOL_WIDTH), const2)],
        out_specs=(pl.BlockSpec((tm, HG_WIDTH), row),) * 6 + (pl.BlockSpec((tm, D), row),) * 2,
        scratch_shapes=[pltpu.VMEM((POOL_HALO, POOL_WIDTH), F32)],
        compiler_params=pltpu.CompilerParams(
            dimension_semantics=("arbitrary", "arbitrary"), vmem_limit_bytes=VMEM_LIMIT),
        name="inproj",
    )(x, mod3, mod3, norm_g, w_in_b, lb, pool_w_b, pool_b, pool_scale)


def _hgrn_kernel(q_ref, lf_ref, k_ref, v_ref, sog_ref, gn_ref, o_ref, *st_refs):
    C = HG_CHUNK
    n_chunks = q_ref.shape[0] // C

    @pl.when(pl.program_id(1) == 0)
    def _():
        for st_ref in st_refs:
            st_ref[...] = jnp.zeros_like(st_ref)

    r_i = lax.broadcasted_iota(jnp.int32, (C, C), 0)
    c_i = lax.broadcasted_iota(jnp.int32, (C, C), 1)
    tril = (c_i <= r_i).astype(BF16)
    same_blk = (r_i // HG_BLK) == (c_i // HG_BLK)
    row = lax.broadcasted_iota(jnp.int32, (C, HG_DK), 0)
    row_in_sub = row % HG_SUB
    upper_half = (row % HG_BLK) >= HG_SUB
    row_blk = row // HG_BLK
    n_blk = C // HG_BLK

    def cumsum_rows(x):
        hi = x.astype(BF16)
        r1 = x - hi.astype(F32)
        mid = r1.astype(BF16)
        lo = (r1 - mid.astype(F32)).astype(BF16)
        return _dot(tril, hi) + _dot(tril, mid) + _dot(tril, lo)

    def block_rows(x, size, which):
        pieces = []
        for g in range(C // size):
            src = g * size + which
            pieces.append(jnp.zeros((size, x.shape[1]), F32) if src < 0
                          else jnp.broadcast_to(x[src:src + 1, :], (size, x.shape[1])))
        return jnp.concatenate(pieces, axis=0)

    def chunk(ci, carry):
        rs = pl.ds(pl.multiple_of(ci * C, C), C)
        b_all = cumsum_rows(lf_ref[rs, :])
        for h in range(HG_HEADS):
            cs = slice(h * HG_DK, (h + 1) * HG_DK)
            q = q_ref[rs, cs]
            k = k_ref[rs, cs]
            v = v_ref[rs, cs]
            b = b_all[:, cs]
            vb = v.astype(BF16)

            kt = k * jnp.exp(block_rows(b, HG_BLK, HG_BLK - 1) - b)
            q_parts, k_parts = [], []
            for j in range(n_blk - 1):
                bj = b[HG_BLK * j + HG_BLK - 1:HG_BLK * (j + 1), :]
                after = row >= HG_BLK * (j + 1)
                q_parts.append(q * jnp.exp(jnp.where(after, b - bj, -jnp.inf)))
                k_parts.append(jnp.where(row_blk == j, kt, 0.0))
            qcat = jnp.concatenate(q_parts, axis=1).astype(BF16)
            kcat = jnp.concatenate(k_parts, axis=1).astype(BF16)
            scores = _dot_nt(qcat, kcat)
            b_prev = block_rows(b, HG_SUB, -1)
            b_sub = block_rows(b, HG_SUB, HG_SUB - 1)
            qh = (q * jnp.exp(jnp.where(upper_half, b - b_prev, -jnp.inf))).astype(BF16)
            kh = jnp.where(upper_half, 0.0, k * jnp.exp(b_sub - b)).astype(BF16)
            scores = scores + jnp.where(same_blk, _dot_nt(qh, kh), 0.0)
            o = _dot(scores.astype(BF16), vb)

            o = o + jnp.sum(q * k, axis=-1, keepdims=True) * v
            for d in range(1, HG_SUB):
                kd = pltpu.roll(k, d, 0)
                bd = pltpu.roll(b, d, 0)
                vd = pltpu.roll(v, d, 0)
                e = jnp.exp(jnp.where(row_in_sub >= d, b - bd, -jnp.inf))
                o = o + jnp.sum(q * kd * e, axis=-1, keepdims=True) * vd

            st = st_refs[h][...]
            o = o + _dot_nt((q * jnp.exp(b)).astype(BF16), st.astype(BF16))
            b_end = b[C - 1:C, :]
            k_end = (k * jnp.exp(b_end - b)).astype(BF16)
            st_refs[h][...] = st * jnp.exp(b_end) + _dot_tn(vb, k_end)

            on = o * lax.rsqrt(jnp.mean(o * o, axis=-1, keepdims=True) + EPS) * gn_ref[...]
            o_ref[rs, cs] = (on * sog_ref[rs, cs].astype(F32)).astype(BF16)
        return carry

    def chunk_group(cg, carry):
        for u in range(HG_UNROLL):
            carry = chunk(HG_UNROLL * cg + u, carry)
        return carry

    lax.fori_loop(0, n_chunks // HG_UNROLL, chunk_group, 0)


def _hgrn(q, lf, k, v, sog, gn, B, S, tb):
    T = B * S
    nS = S // tb
    row = lambda b, s: (b * nS + s, 0)
    blk = pl.BlockSpec((tb, HG_WIDTH), row)
    return pl.pallas_call(
        _hgrn_kernel,
        out_shape=jax.ShapeDtypeStruct((T, HG_WIDTH), BF16),
        grid=(B, nS),
        in_specs=[blk, blk, blk, blk, blk, pl.BlockSpec((1, HG_DK), lambda b, s: (0, 0))],
        out_specs=blk,
        scratch_shapes=[pltpu.VMEM((HG_DK, HG_DK), F32)] * HG_HEADS,
        compiler_params=pltpu.CompilerParams(
            dimension_semantics=("arbitrary", "arbitrary"), vmem_limit_bytes=VMEM_LIMIT),
        name="hgrn",
    )(q, lf, k, v, sog, gn)


def _split_kernel(w_ref, hi_ref, lo_ref):
    w = w_ref[...]
    hi = w.astype(BF16)
    hi_ref[...] = hi
    lo_ref[...] = (w - hi.astype(F32)).astype(BF16)


def _split_bf16(w):
    out = jax.ShapeDtypeStruct(w.shape, BF16)
    return pl.pallas_call(_split_kernel, out_shape=(out, out), name="split")(w)


def _mix_kernel(x_ref, oa_ref, pm_ref, sga_ref, sgb_ref, g1_ref, sh2_ref, sc2_ref, n2_ref,
                wua_ref, wub_ref, wo_ref, rw_hi_ref, rw_lo_ref, x1_ref, h2_ref, lg_ref):
    tm = x_ref.shape[1]
    ya = _dot(oa_ref[...], wua_ref[...])
    yb = _dot(pm_ref[...], wub_ref[...])
    mix = sga_ref[...].astype(F32) * ya + sgb_ref[...].astype(F32) * yb
    x1 = x_ref[0] + g1_ref[0] * _dot(mix.astype(BF16), wo_ref[...])
    x1_ref[...] = x1
    h2 = x1 * lax.rsqrt(jnp.mean(x1 * x1, axis=-1, keepdims=True) + EPS) * n2_ref[...]
    h2 = h2 * (1.0 + sc2_ref[0]) + sh2_ref[0]
    for j, chunk in enumerate(_row_chunks(h2)):
        h2_ref[pl.ds(j, tm, stride=ROW_TILES), :] = chunk
    h_hi = h2.astype(BF16)
    h_lo = (h2 - h_hi.astype(F32)).astype(BF16)
    rw_hi = rw_hi_ref[...]
    lg_ref[...] = _dot_nt(rw_hi, h_hi) + _dot_nt(rw_hi, h_lo) + _dot_nt(rw_lo_ref[...], h_hi)


def _mix(x, oa, pm, sga, sgb, mod3, norm2_g, wua, wub, wo, rw_hi, rw_lo, tm):
    B, S, D = x.shape
    T = B * S
    nS = S // tm
    row = lambda b, s: (b * nS + s, 0)
    const2 = lambda b, s: (0, 0)
    return pl.pallas_call(
        _mix_kernel,
        out_shape=(jax.ShapeDtypeStruct((T, D), F32),
                   jax.ShapeDtypeStruct((T * ROW_TILES, LANES), F32),
                   jax.ShapeDtypeStruct((N_EXPERTS, T), F32)),
        grid=(B, nS),
        in_specs=[pl.BlockSpec((1, tm, D), lambda b, s: (b, s, 0)),
                  pl.BlockSpec((tm, HG_WIDTH), row),
                  pl.BlockSpec((tm, POOL_WIDTH), row),
                  pl.BlockSpec((tm, D), row),
                  pl.BlockSpec((tm, D), row),
                  pl.BlockSpec((1, 1, D), lambda b, s: (b, 0, 2)),
                  pl.BlockSpec((1, 1, D), lambda b, s: (b, 0, 3)),
                  pl.BlockSpec((1, 1, D), lambda b, s: (b, 0, 4)),
                  pl.BlockSpec((1, D), const2),
                  pl.BlockSpec(wua.shape, const2),
                  pl.BlockSpec(wub.shape, const2),
                  pl.BlockSpec(wo.shape, const2),
                  pl.BlockSpec(rw_hi.shape, const2),
                  pl.BlockSpec(rw_lo.shape, const2)],
        out_specs=(pl.BlockSpec((tm, D), row),
                   pl.BlockSpec((tm * ROW_TILES, LANES), row),
                   pl.BlockSpec((N_EXPERTS, tm), lambda b, s: (0, b * nS + s))),
        compiler_params=pltpu.CompilerParams(
            dimension_semantics=("arbitrary", "arbitrary"), vmem_limit_bytes=VMEM_LIMIT),
        name="mix",
    )(x, oa, pm, sga, sgb, mod3, mod3, mod3, norm2_g, wua, wub, wo, rw_hi, rw_lo)


def _route_kernel(lg_ref, bias_ref, idx_ref, gate_ref, rank_ref, cnt_ref, carry_ref):
    tl = lg_ref.shape[1]
    neg = -jnp.inf

    @pl.when(pl.program_id(0) == 0)
    def _():
        carry_ref[...] = jnp.zeros_like(carry_ref)

    s = _sigmoid(lg_ref[...])
    biased = s + bias_ref[...]
    rowid = lax.broadcasted_iota(jnp.int32, (N_EXPERTS, tl), 0)

    def first_argmax(x, ids, sentinel):
        m = jnp.max(x, axis=0, keepdims=True)
        return jnp.min(jnp.where(x == m, ids, sentinel), axis=0, keepdims=True), m

    gscores = []
    for g in range(N_GROUPS):
        xg = biased[g * GROUP_SIZE:(g + 1) * GROUP_SIZE, :]
        rid = g * GROUP_SIZE + lax.broadcasted_iota(jnp.int32, (GROUP_SIZE, tl), 0)
        first, m1 = first_argmax(xg, rid, N_EXPERTS)
        m2 = jnp.max(jnp.where(rid == first, neg, xg), axis=0, keepdims=True)
        gscores.append(m1 + m2)
    blocks = []
    for g in range(N_GROUPS):
        beaten = jnp.zeros((1, tl), F32)
        for o in range(N_GROUPS):
            if o != g:
                wins = (gscores[o] >= gscores[g]) if o < g else (gscores[o] > gscores[g])
                beaten = beaten + jnp.where(wins, 1.0, 0.0)
        xg = biased[g * GROUP_SIZE:(g + 1) * GROUP_SIZE, :]
        blocks.append(jnp.where(beaten < float(TOPK_GROUPS), xg, neg))
    masked = jnp.concatenate(blocks, axis=0)

    idxs, gates = [], []
    chosen = jnp.zeros((N_EXPERTS, tl), F32)
    for _ in range(TOP_K):
        first, _m = first_argmax(masked, rowid, N_EXPERTS)
        sel = rowid == first
        gates.append(jnp.sum(jnp.where(sel, s, 0.0), axis=0, keepdims=True))
        idxs.append(first)
        chosen = jnp.where(sel, 1.0, chosen)
        masked = jnp.where(sel, neg, masked)
    gate_sum = functools.reduce(lambda a, b: a + b, gates)
    for k in range(TOP_K):
        gate_ref[k:k + 1, :] = gates[k] / gate_sum * ROUTED_SCALE
        idx_ref[k:k + 1, :] = idxs[k]

    lr = lax.broadcasted_iota(jnp.int32, (tl, tl), 0)
    lc = lax.broadcasted_iota(jnp.int32, (tl, tl), 1)
    prefix = (lr <= lc).astype(BF16)
    cnt_incl = _dot(chosen.astype(BF16), prefix)
    carry = carry_ref[...]
    rank_excl = cnt_incl - chosen + carry
    for k in range(TOP_K):
        rank_k = jnp.sum(jnp.where(rowid == idxs[k], rank_excl, 0.0), axis=0, keepdims=True)
        rank_ref[k:k + 1, :] = rank_k.astype(jnp.int32)
    carry = carry + jnp.sum(chosen, axis=1, keepdims=True)
    carry_ref[...] = carry
    cnt_ref[...] = carry.astype(jnp.int32)


def _route(logits_t, bias, tl):
    T = logits_t.shape[1]
    tok = lambda i: (0, i)
    return pl.pallas_call(
        _route_kernel,
        out_shape=(jax.ShapeDtypeStruct((TOP_K, T), jnp.int32),
                   jax.ShapeDtypeStruct((TOP_K, T), F32),
                   jax.ShapeDtypeStruct((TOP_K, T), jnp.int32),
                   jax.ShapeDtypeStruct((N_EXPERTS, 1), jnp.int32)),
        grid=(T // tl,),
        in_specs=[pl.BlockSpec((N_EXPERTS, tl), tok), pl.BlockSpec((N_EXPERTS, 1), lambda i: (0, 0))],
        out_specs=(pl.BlockSpec((TOP_K, tl), tok), pl.BlockSpec((TOP_K, tl), tok),
                   pl.BlockSpec((TOP_K, tl), tok), pl.BlockSpec((N_EXPERTS, 1), lambda i: (0, 0))),
        scratch_shapes=[pltpu.VMEM((N_EXPERTS, 1), F32)],
        compiler_params=pltpu.CompilerParams(
            dimension_semantics=("arbitrary",), vmem_limit_bytes=VMEM_LIMIT),
        name="route",
    )(logits_t, bias)


def _as_rows(ref):
    return ref.reshape(ref.shape[0] // ROW_TILES, ROW_TILES, LANES)


def _wait_rows(rows_ref, n, sem):
    pltpu.make_async_copy(rows_ref.at[pl.ds(0, n)], rows_ref.at[pl.ds(0, n)], sem).wait()


def _scatter_kernel(pos_ref, h2_ref, xs_ref, zero_ref, sem, zsem):
    src = _as_rows(h2_ref)
    dst = _as_rows(xs_ref)
    tt = src.shape[0]

    @pl.when(pl.program_id(0) == 0)
    def _():
        zero_ref[...] = jnp.zeros_like(zero_ref)
        tail = xs_ref.at[pl.ds(xs_ref.shape[0] - zero_ref.shape[0], zero_ref.shape[0])]
        fill = pltpu.make_async_copy(zero_ref, tail, zsem)
        fill.start()
        fill.wait()

    def start(t, c):
        for k in range(TOP_K):
            pltpu.make_async_copy(src.at[t], dst.at[pos_ref[k, t]], sem).start(priority=k % 2)
        return c

    lax.fori_loop(0, tt, start, 0)
    _wait_rows(dst, tt * TOP_K, sem)


def _scatter(pos, h2_tm, tt):
    n_rows = h2_tm.shape[0] // ROW_TILES * TOP_K + EXPERT_TILE
    return pl.pallas_call(
        _scatter_kernel,
        out_shape=jax.ShapeDtypeStruct((n_rows * ROW_TILES, LANES), F32),
        grid=(pos.shape[1] // tt,),
        in_specs=[pl.BlockSpec((TOP_K, tt), lambda i: (0, i), memory_space=pltpu.SMEM),
                  pl.BlockSpec((tt * ROW_TILES, LANES), lambda i: (i, 0))],
        out_specs=pl.BlockSpec(memory_space=pl.ANY),
        scratch_shapes=[pltpu.VMEM((EXPERT_TILE * ROW_TILES, LANES), F32),
                        pltpu.SemaphoreType.DMA, pltpu.SemaphoreType.DMA],
        compiler_params=pltpu.CompilerParams(
            dimension_semantics=("arbitrary",), vmem_limit_bytes=VMEM_LIMIT),
        name="scatter",
    )(pos, h2_tm)


def _experts_kernel(exp_ref, row0_ref, valid_ref, newe_ref, nitems_ref, slot_ref, nexte_ref,
                    xs_hbm, wg_hbm, wu_hbm, wd_hbm, ys_hbm,
                    xbuf_ref, ybuf_ref, sg_ref, su_ref, sd_ref, wgb_ref, wub_ref, wdb_ref, hm_ref,
                    xsem, ysem, wsem):
    i = pl.program_id(0)
    tr = EXPERT_TILE
    tile_rows = tr * ROW_TILES
    n_items = nitems_ref[0]
    part_sizes = tuple(tr >> (b + 1) for b in range(tr.bit_length() - 1))

    def ring(item, first_row=0, n_rows=tr):
        start = ((item % TILE_RING) * tr + first_row) * ROW_TILES
        return pl.ds(pl.multiple_of(start, ROW_TILES), n_rows * ROW_TILES)

    def hbm_rows(first_row, n_rows=tr):
        return pl.ds(pl.multiple_of(first_row * ROW_TILES, ROW_TILES), n_rows * ROW_TILES)

    def x_copy(item):
        return pltpu.make_async_copy(xs_hbm.at[hbm_rows(row0_ref[item])], xbuf_ref.at[ring(item)],
                                     xsem.at[item % TILE_RING])

    def y_copies(item, go):
        v = valid_ref[item]
        sem = ysem.at[item % TILE_RING]

        @pl.when(v == tr)
        def _():
            go(pltpu.make_async_copy(ybuf_ref.at[ring(item)], ys_hbm.at[hbm_rows(row0_ref[item])], sem))

        @pl.when(v != tr)
        def _():
            for size in part_sizes:
                @pl.when((v & size) != 0)
                def _():
                    first = v & ~(2 * size - 1)
                    go(pltpu.make_async_copy(ybuf_ref.at[ring(item, first, size)],
                                             ys_hbm.at[hbm_rows(row0_ref[item] + first, size)], sem))

    def weight_copies(e, slot):
        return (pltpu.make_async_copy(wg_hbm.at[e], sg_ref.at[slot], wsem.at[slot]),
                pltpu.make_async_copy(wu_hbm.at[e], su_ref.at[slot], wsem.at[slot]),
                pltpu.make_async_copy(wd_hbm.at[e], sd_ref.at[slot], wsem.at[slot]))

    a_on = i < n_items
    j = jnp.maximum(i - 1, 0)
    b_on = (i >= 1) & (i - 1 < n_items)
    e = exp_ref[i]

    @pl.when(i == 0)
    def _():
        hm_ref[...] = jnp.zeros_like(hm_ref)
        for i0 in range(TILE_AHEAD):
            @pl.when(i0 < n_items)
            def _():
                x_copy(i0).start()

    @pl.when(a_on)
    def _():
        @pl.when(i + TILE_AHEAD < n_items)
        def _():
            x_copy(i + TILE_AHEAD).start()

        x_copy(i).wait()

    @pl.when(b_on & (j >= TILE_RING))
    def _():
        y_copies(j - TILE_RING, lambda c: c.wait())

    @pl.when(a_on & (newe_ref[i] == 1))
    def _():
        slot = slot_ref[i]
        nxt = nexte_ref[i]

        @pl.when(i == 0)
        def _():
            for c in weight_copies(e, slot):
                c.start()

        @pl.when(nxt >= 0)
        def _():
            for c in weight_copies(nxt, 1 - slot):
                c.start()

        for c in weight_copies(e, slot):
            c.wait()
        wgb_ref[...] = sg_ref[slot].astype(BF16)
        wub_ref[...] = su_ref[slot].astype(BF16)
        wdb_ref[slot] = sd_ref[slot].astype(BF16)

    @pl.when(i <= n_items)
    def _():
        out_row = (j % TILE_RING) * tile_rows
        for c, chunk in enumerate(_row_chunks(_dot(hm_ref[j % 2], wdb_ref[slot_ref[j]]))):
            ybuf_ref[pl.ds(out_row + c, tr, stride=ROW_TILES), :] = chunk

        x = _rows_to_matrix(xbuf_ref[ring(i), :], tr).astype(BF16)
        hm_ref[i % 2] = (_silu(_dot(x, wgb_ref[...])) * _dot(x, wub_ref[...])).astype(BF16)

    @pl.when(b_on)
    def _():
        y_copies(j, lambda c: c.start())

    @pl.when(b_on & (j == n_items - 1))
    def _():
        for back in range(TILE_RING):
            @pl.when(j - back >= 0)
            def _():
                y_copies(j - back, lambda c: c.wait())


def _experts(meta, xs, wg, wu, wd, n_items_max):
    tile_rows = EXPERT_TILE * ROW_TILES
    out_rows = xs.shape[0] - tile_rows
    hbm = pl.BlockSpec(memory_space=pl.ANY)
    n_slots = 2
    grid_spec = pltpu.PrefetchScalarGridSpec(
        num_scalar_prefetch=len(meta),
        grid=(n_items_max + 1,),
        in_specs=[hbm, hbm, hbm, hbm],
        out_specs=hbm,
        scratch_shapes=[pltpu.VMEM((TILE_RING * tile_rows, LANES), F32),
                        pltpu.VMEM((TILE_RING * tile_rows, LANES), F32),
                        pltpu.VMEM((n_slots, D_MODEL, D_EXPERT), F32),
                        pltpu.VMEM((n_slots, D_MODEL, D_EXPERT), F32),
                        pltpu.VMEM((n_slots, D_EXPERT, D_MODEL), F32),
                        pltpu.VMEM((D_MODEL, D_EXPERT), BF16),
                        pltpu.VMEM((D_MODEL, D_EXPERT), BF16),
                        pltpu.VMEM((n_slots, D_EXPERT, D_MODEL), BF16),
                        pltpu.VMEM((2, EXPERT_TILE, D_EXPERT), BF16),
                        pltpu.SemaphoreType.DMA((TILE_RING,)),
                        pltpu.SemaphoreType.DMA((TILE_RING,)),
                        pltpu.SemaphoreType.DMA((n_slots,))])
    return pl.pallas_call(
        _experts_kernel,
        out_shape=jax.ShapeDtypeStruct((out_rows, LANES), F32),
        grid_spec=grid_spec,
        compiler_params=pltpu.CompilerParams(
            dimension_semantics=("arbitrary",), vmem_limit_bytes=VMEM_LIMIT),
        name="experts",
    )(*meta, xs, wg, wu, wd)


def _combine_kernel(pos_ref, pos_next_ref, ys_ref, h2_ref, x1_ref, gate_ref, g2_ref, swg_ref, swu_ref, swd_ref,
                    fg_ref, out_ref, buf_a, buf_b, sem):
    i = pl.program_id(0)
    tt = x1_ref.shape[0] // 2
    src = _as_rows(ys_ref)

    def gather(p_ref, col0, buf, s):
        dst = _as_rows(buf)
        for t in range(tt):
            for k in range(TOP_K):
                pltpu.make_async_copy(src.at[p_ref[k, col0 + t]], dst.at[k * tt + t],
                                      sem.at[s]).start(priority=k % 2)

    @pl.when(i == 0)
    def _():
        dst = _as_rows(buf_a)

        def start(t, c):
            for k in range(TOP_K):
                pltpu.make_async_copy(src.at[pos_ref[k, t]], dst.at[k * tt + t], sem.at[0]).start(priority=k % 2)
            return c

        lax.fori_loop(0, tt, start, 0)

    def tile(row0, buf, s, prefetch):
        tok = pl.ds(row0, tt)
        _wait_rows(_as_rows(buf), tt * TOP_K, sem.at[s])
        prefetch()
        h2 = _load_rows(h2_ref, tt, first_row=row0).astype(BF16)
        hm = (_silu(_dot(h2, swg_ref[...])) * _dot(h2, swu_ref[...])).astype(BF16)
        gate = gate_ref[tok, :]
        ssq = jnp.zeros((tt, 1), F32)
        for c in range(ROW_TILES):
            cols = slice(c * LANES, (c + 1) * LANES)
            acc = _dot(hm, swd_ref[:, cols])
            for k in range(TOP_K):
                acc = acc + gate[:, k:k + 1] * buf[pl.ds(k * tt * ROW_TILES + c, tt, stride=ROW_TILES), :]
            x2 = x1_ref[tok, cols] + g2_ref[0, :, cols] * acc
            out_ref[tok, cols] = x2
            ssq = ssq + jnp.sum(x2 * x2, axis=-1, keepdims=True)
        out_ref[tok, :] = out_ref[tok, :] * lax.rsqrt(ssq * (1.0 / D_MODEL) + EPS) * fg_ref[...]

    tile(0, buf_a, 0, lambda: gather(pos_ref, tt, buf_b, 1))
    tile(tt, buf_b, 1, lambda: gather(pos_next_ref, 0, buf_a, 0))

    @pl.when(i == pl.num_programs(0) - 1)
    def _():
        _wait_rows(_as_rows(buf_a), tt * TOP_K, sem.at[0])


def _combine(pos, ys, h2_tm, x1, gate_tm, mod3, swg, swu, swd, fg, B, S, tt):
    T, D = x1.shape
    gather_rows = tt * TOP_K * ROW_TILES
    tt = 2 * tt
    nS = S // tt
    const2 = lambda i: (0, 0)
    n_tiles = T // tt
    return pl.pallas_call(
        _combine_kernel,
        out_shape=jax.ShapeDtypeStruct((T, D), F32),
        grid=(n_tiles,),
        in_specs=[pl.BlockSpec((TOP_K, tt), lambda i: (0, i), memory_space=pltpu.SMEM),
                  pl.BlockSpec((TOP_K, tt), lambda i: (0, jnp.minimum(i + 1, n_tiles - 1)),
                               memory_space=pltpu.SMEM),
                  pl.BlockSpec(memory_space=pl.ANY),
                  pl.BlockSpec((tt * ROW_TILES, LANES), lambda i: (i, 0)),
                  pl.BlockSpec((tt, D), lambda i: (i, 0)),
                  pl.BlockSpec((tt, TOP_K), lambda i: (i, 0)),
                  pl.BlockSpec((1, 1, D), lambda i: (i // nS, 0, 5)),
                  pl.BlockSpec(swg.shape, const2),
                  pl.BlockSpec(swu.shape, const2),
                  pl.BlockSpec(swd.shape, const2),
                  pl.BlockSpec((1, D), const2)],
        out_specs=pl.BlockSpec((tt, D), lambda i: (i, 0)),
        scratch_shapes=[pltpu.VMEM((gather_rows, LANES), F32),
                        pltpu.VMEM((gather_rows, LANES), F32),
                        pltpu.SemaphoreType.DMA((2,))],
        compiler_params=pltpu.CompilerParams(
            dimension_semantics=("arbitrary",), vmem_limit_bytes=VMEM_LIMIT),
        name="combine",
    )(pos, pos, ys, h2_tm, x1, gate_tm, mod3, swg, swu, swd, fg)


def _place_kernel(idx_ref, rank_ref, off_ref, pos_ref):
    tl = idx_ref.shape[1]
    rowid = lax.broadcasted_iota(jnp.int32, (N_EXPERTS, tl), 0)
    off = off_ref[...].astype(F32)
    for k in range(TOP_K):
        base = jnp.sum(jnp.where(rowid == idx_ref[k:k + 1, :], off, 0.0), axis=0, keepdims=True)
        pos_ref[k:k + 1, :] = base.astype(jnp.int32) + rank_ref[k:k + 1, :]


def _place(idx, rank, off, tl):
    T = idx.shape[1]
    tok = pl.BlockSpec((TOP_K, tl), lambda i: (0, i))
    return pl.pallas_call(
        _place_kernel,
        out_shape=jax.ShapeDtypeStruct((TOP_K, T), jnp.int32),
        grid=(T // tl,),
        in_specs=[tok, tok, pl.BlockSpec((N_EXPERTS, 1), lambda i: (0, 0))],
        out_specs=tok,
        name="place",
    )(idx, rank, off)


def _plan(counts, n_items_max):
    counts = counts[:, 0]
    off = jnp.cumsum(counts) - counts
    n_e = (counts + EXPERT_TILE - 1) // EXPERT_TILE
    item_end = jnp.cumsum(n_e)
    item_start = item_end - n_e
    n_items = item_end[-1]
    ids = jnp.arange(n_items_max + 1, dtype=jnp.int32)
    ids_c = jnp.minimum(ids, n_items - 1)
    item_e = jnp.minimum(jnp.sum(item_end[None, :] <= ids_c[:, None], axis=1), N_EXPERTS - 1).astype(jnp.int32)
    ids_e = jnp.arange(N_EXPERTS, dtype=jnp.int32)
    of_item = item_e[:, None] == ids_e[None, :]

    def per_item(table):
        return jnp.sum(jnp.where(of_item, table[None, :], 0), axis=1).astype(jnp.int32)

    first_row = (ids_c - per_item(item_start)) * EXPERT_TILE
    item_row0 = per_item(off) + first_row
    item_valid = jnp.minimum(per_item(counts) - first_row, EXPERT_TILE).astype(jnp.int32)
    prev_e = jnp.concatenate([jnp.full((1,), -1, jnp.int32), item_e[:-1]])
    item_newe = (item_e != prev_e).astype(jnp.int32)
    item_slot = ((jnp.cumsum(item_newe) - 1) % 2).astype(jnp.int32)
    later = jnp.where((counts[None, :] > 0) & (ids_e[None, :] > ids_e[:, None]), ids_e[None, :], N_EXPERTS)
    next_e = jnp.min(later, axis=1)
    next_e = jnp.where(next_e < N_EXPERTS, next_e, -1).astype(jnp.int32)
    meta = (item_e, item_row0, item_valid, item_newe, n_items.reshape(1).astype(jnp.int32),
            item_slot, per_item(next_e))
    return off.astype(jnp.int32).reshape(N_EXPERTS, 1), meta


def kernel(x, c, ada_w, ada_b, norm1_g, w_in, hgrn_lb, hgrn_norm_g, pool_w, pool_b, pool_scale, w_up_a, w_up_b, w_out, norm2_g, router_w, router_bias, exp_w_gate, exp_w_up, exp_w_down, shared_w_gate, shared_w_up, shared_w_down, final_norm_g):
    B, S, D = x.shape
    T = B * S
    assert ada_w.shape[0] == 1, "single-layer trunk only: the final norm is fused into the combine step"
    lb_all = jnp.cumsum(jax.nn.softmax(hgrn_lb.astype(F32), axis=0), axis=0)
    c_pad = jnp.zeros((SUBLANES, D), F32).at[:B].set(c)
    n_items_max = T * TOP_K // EXPERT_TILE + N_EXPERTS

    for l in range(1):
        mod = _ada(c_pad, ada_w[l], ada_b[l].reshape(1, -1))
        mod3 = mod[:B].reshape(B, 1, 6 * D)

        q, lf, k, v, sog, pm, sga, sgb = _inproj(
            x, mod3, norm1_g[l].reshape(1, D), w_in[l].astype(BF16), lb_all[l].reshape(1, HG_WIDTH),
            pool_w[l].astype(BF16), pool_b[l].reshape(len(POOL_WINDOWS), 1, POOL_GROUP),
            pool_scale[l].reshape(1, POOL_WIDTH), tm=256)
        oa = _hgrn(q, lf, k, v, sog, hgrn_norm_g[l].reshape(1, HG_DK), B, S, tb=512)

        rw_hi, rw_lo = _split_bf16(router_w[l].T)
        x1, h2_tm, logits_t = _mix(
            x, oa, pm, sga, sgb, mod3, norm2_g[l].reshape(1, D), w_up_a[l].astype(BF16),
            w_up_b[l].astype(BF16), w_out[l].astype(BF16), rw_hi, rw_lo, tm=512)

        idx, gate, rank, counts = _route(logits_t, router_bias[l].reshape(N_EXPERTS, 1), tl=256)
        off, meta = _plan(counts, n_items_max)
        pos = _place(idx, rank, off, tl=512)

        xs = _scatter(pos, h2_tm, tt=256)
        ys = _experts(meta, xs, exp_w_gate[l], exp_w_up[l], exp_w_down[l], n_items_max)

        tt_c = 128
        fg = final_norm_g.reshape(1, D)
        x = _combine(pos, ys, h2_tm, x1, gate.T, mod3,
                     shared_w_gate[l].astype(BF16), shared_w_up[l].astype(BF16),
                     shared_w_down[l].astype(BF16), fg, B, S, tt_c).reshape(B, S, D)
    return x
```

```python
import functools

import jax
import jax.numpy as jnp
from jax import lax
from jax.experimental import pallas as pl
from jax.experimental.pallas import tpu as pltpu

F32 = jnp.float32
BF16 = jnp.bfloat16
HIGHEST = lax.Precision.HIGHEST

D_MODEL = 1024
HG_WIDTH = 512
HG_DK = 128
HG_HEADS = 4
HG_CHUNK = 64
HG_BLK = 16
HG_SUB = 8
HG_UNROLL = 8
POOL_WIDTH = 512
POOL_WINDOWS = (2, 4, 8, 16)
POOL_GROUP = 128
POOL_HALO = 16
N_EXPERTS = 256
TOP_K = 8
N_GROUPS = 8
TOPK_GROUPS = 4
GROUP_SIZE = N_EXPERTS // N_GROUPS
D_EXPERT = 256
ROUTED_SCALE = 2.5
EPS = 1e-6

LANES = 128
SUBLANES = 8
ROW_TILES = D_MODEL // LANES
EXPERT_TILE = 256
TILE_RING = 8
TILE_AHEAD = TILE_RING - 1
VMEM_LIMIT = 56 * 1024 * 1024

COL_Q, COL_F, COL_I, COL_OG, COL_U, COL_GA, COL_GB = 0, 512, 1024, 1536, 2048, 2560, 3584


def _sigmoid(x):
    return 1.0 / (1.0 + jnp.exp(-x))


def _silu(x):
    return x * _sigmoid(x)


def _dot(a, b):
    return jnp.dot(a, b, preferred_element_type=F32)


def _dot_nt(a, b):
    return lax.dot_general(a, b, (((1,), (1,)), ((), ())), preferred_element_type=F32)


def _dot_tn(a, b):
    return lax.dot_general(a, b, (((0,), (0,)), ((), ())), preferred_element_type=F32)


def _row_chunks(x):
    return [x[:, j * LANES:(j + 1) * LANES] for j in range(ROW_TILES)]


def _rows_to_matrix(rows, n):
    groups = n // SUBLANES
    chunks = jnp.swapaxes(rows.reshape(groups, SUBLANES, ROW_TILES, LANES), 1, 2)
    return jnp.concatenate([chunks[:, j].reshape(n, LANES) for j in range(ROW_TILES)], axis=1)


def _load_rows(ref, n, first_row=0):
    return jnp.concatenate(
        [ref[pl.ds(first_row * ROW_TILES + j, n, stride=ROW_TILES), :] for j in range(ROW_TILES)], axis=1)


def _ada_kernel(c_ref, w_ref, b_ref, o_ref):
    cond = _silu(c_ref[...])
    o_ref[...] = jnp.dot(cond, w_ref[...], precision=HIGHEST, preferred_element_type=F32) + b_ref[...]


def _ada(c_pad, ada_w, ada_b):
    n = ada_w.shape[1]
    tn = 1536
    return pl.pallas_call(
        _ada_kernel,
        out_shape=jax.ShapeDtypeStruct((SUBLANES, n), F32),
        grid=(n // tn,),
        in_specs=[pl.BlockSpec((SUBLANES, D_MODEL), lambda j: (0, 0)),
                  pl.BlockSpec((D_MODEL, tn), lambda j: (0, j)),
                  pl.BlockSpec((1, tn), lambda j: (0, j))],
        out_specs=pl.BlockSpec((SUBLANES, tn), lambda j: (0, j)),
        compiler_params=pltpu.CompilerParams(vmem_limit_bytes=VMEM_LIMIT),
        name="ada",
    )(c_pad, ada_w, ada_b)


def _inproj_kernel(x_ref, sh_ref, sc_ref, g_ref, w_ref, lb_ref, pw_ref, pb_ref, ps_ref,
                   q_ref, lf_ref, k_ref, v_ref, sog_ref, pm_ref, sga_ref, sgb_ref, halo_ref):
    s = pl.program_id(1)
    tm = x_ref.shape[1]

    @pl.when(s == 0)
    def _():
        halo_ref[...] = jnp.zeros_like(halo_ref)

    x = x_ref[0]
    h = x * lax.rsqrt(jnp.mean(x * x, axis=-1, keepdims=True) + EPS) * g_ref[...]
    h = h * (1.0 + sc_ref[0]) + sh_ref[0]
    hb = h.astype(BF16)

    proj_all = _dot(hb, w_ref[...])

    def proj(lo, n):
        return proj_all[:, lo:lo + n]

    q = proj(COL_Q, HG_WIDTH)
    q_ref[...] = _silu(q) * (HG_DK ** -0.5)
    sig = _sigmoid(proj(COL_F, HG_WIDTH))
    lb = lb_ref[...]
    lf_ref[...] = jnp.log(lb + (1.0 - lb) * sig)
    k_ref[...] = (1.0 - lb) * (1.0 - sig)
    v_ref[...] = proj(COL_I, HG_WIDTH)
    sog_ref[...] = _silu(proj(COL_OG, HG_WIDTH)).astype(BF16)
    sga_ref[...] = _sigmoid(proj(COL_GA, D_MODEL)).astype(BF16)
    sgb_ref[...] = _sigmoid(proj(COL_GB, D_MODEL)).astype(BF16)

    u = proj(COL_U, POOL_WIDTH)
    ext = jnp.concatenate([halo_ref[...], u], axis=0)
    halo_ref[...] = u[tm - POOL_HALO:, :]
    s2 = ext + pltpu.roll(ext, 1, 0)
    s4 = s2 + pltpu.roll(s2, 2, 0)
    s8 = s4 + pltpu.roll(s4, 4, 0)
    s16 = s8 + pltpu.roll(s8, 8, 0)
    pos1 = (s * tm + 1 + lax.broadcasted_iota(jnp.int32, (tm, 1), 0)).astype(F32)
    for g, (w, sw) in enumerate(zip(POOL_WINDOWS, (s2, s4, s8, s16))):
        cols = slice(g * POOL_GROUP, (g + 1) * POOL_GROUP)
        m = sw[POOL_HALO:, cols] / jnp.minimum(pos1, float(w)) - u[:, cols]
        y = _dot(m.astype(BF16), pw_ref[g]) + pb_ref[g]
        pm_ref[:, cols] = (y * ps_ref[:, cols]).astype(BF16)


def _inproj(x, mod3, norm_g, w_in_b, lb, pool_w_b, pool_b, pool_scale, tm):
    B, S, D = x.shape
    T = B * S
    nS = S // tm
    row = lambda b, s: (b * nS + s, 0)
    const2 = lambda b, s: (0, 0)
    const3 = lambda b, s: (0, 0, 0)
    half = lambda dt: jax.ShapeDtypeStruct((T, HG_WIDTH), dt)
    full = lambda dt: jax.ShapeDtypeStruct((T, D), dt)
    return pl.pallas_call(
        _inproj_kernel,
        out_shape=(half(F32), half(F32), half(F32), half(F32), half(BF16), half(BF16), full(BF16), full(BF16)),
        grid=(B, nS),
        in_specs=[pl.BlockSpec((1, tm, D), lambda b, s: (b, s, 0)),
                  pl.BlockSpec((1, 1, D), lambda b, s: (b, 0, 0)),
                  pl.BlockSpec((1, 1, D), lambda b, s: (b, 0, 1)),
                  pl.BlockSpec((1, D), const2),
                  pl.BlockSpec(w_in_b.shape, const2),
                  pl.BlockSpec((1, HG_WIDTH), const2),
                  pl.BlockSpec(pool_w_b.shape, const3),
                  pl.BlockSpec(pool_b.shape, const3),
                  pl.BlockSpec((1, POOL_WIDTH), const2)],
        out_specs=(pl.BlockSpec((tm, HG_WIDTH), row),) * 6 + (pl.BlockSpec((tm, D), row),) * 2,
        scratch_shapes=[pltpu.VMEM((POOL_HALO, POOL_WIDTH), F32)],
        compiler_params=pltpu.CompilerParams(
            dimension_semantics=("arbitrary", "arbitrary"), vmem_limit_bytes=VMEM_LIMIT),
        name="inproj",
    )(x, mod3, mod3, norm_g, w_in_b, lb, pool_w_b, pool_b, pool_scale)


def _hgrn_kernel(q_ref, lf_ref, k_ref, v_ref, sog_ref, gn_ref, o_ref, *st_refs):
    C = HG_CHUNK
    n_chunks = q_ref.shape[0] // C

    @pl.when(pl.program_id(1) == 0)
    def _():
        for st_ref in st_refs:
            st_ref[...] = jnp.zeros_like(st_ref)

    r_i = lax.broadcasted_iota(jnp.int32, (C, C), 0)
    c_i = lax.broadcasted_iota(jnp.int32, (C, C), 1)
    tril = (c_i <= r_i).astype(BF16)
    same_blk = (r_i // HG_BLK) == (c_i // HG_BLK)
    row = lax.broadcasted_iota(jnp.int32, (C, HG_DK), 0)
    row_in_sub = row % HG_SUB
    upper_half = (row % HG_BLK) >= HG_SUB
    row_blk = row // HG_BLK
    n_blk = C // HG_BLK

    def cumsum_rows(x):
        hi = x.astype(BF16)
        r1 = x - hi.astype(F32)
        mid = r1.astype(BF16)
        lo = (r1 - mid.astype(F32)).astype(BF16)
        return _dot(tril, hi) + _dot(tril, mid) + _dot(tril, lo)

    def block_rows(x, size, which):
        pieces = []
        for g in range(C // size):
            src = g * size + which
            pieces.append(jnp.zeros((size, x.shape[1]), F32) if src < 0
                          else jnp.broadcast_to(x[src:src + 1, :], (size, x.shape[1])))
        return jnp.concatenate(pieces, axis=0)

    def chunk(ci, carry):
        rs = pl.ds(pl.multiple_of(ci * C, C), C)
        b_all = cumsum_rows(lf_ref[rs, :])
        for h in range(HG_HEADS):
            cs = slice(h * HG_DK, (h + 1) * HG_DK)
            q = q_ref[rs, cs]
            k = k_ref[rs, cs]
            v = v_ref[rs, cs]
            b = b_all[:, cs]
            vb = v.astype(BF16)

            kt = k * jnp.exp(block_rows(b, HG_BLK, HG_BLK - 1) - b)
            q_parts, k_parts = [], []
            for j in range(n_blk - 1):
                bj = b[HG_BLK * j + HG_BLK - 1:HG_BLK * (j + 1), :]
                after = row >= HG_BLK * (j + 1)
                q_parts.append(q * jnp.exp(jnp.where(after, b - bj, -jnp.inf)))
                k_parts.append(jnp.where(row_blk == j, kt, 0.0))
            qcat = jnp.concatenate(q_parts, axis=1).astype(BF16)
            kcat = jnp.concatenate(k_parts, axis=1).astype(BF16)
            scores = _dot_nt(qcat, kcat)
            b_prev = block_rows(b, HG_SUB, -1)
            b_sub = block_rows(b, HG_SUB, HG_SUB - 1)
            qh = (q * jnp.exp(jnp.where(upper_half, b - b_prev, -jnp.inf))).astype(BF16)
            kh = jnp.where(upper_half, 0.0, k * jnp.exp(b_sub - b)).astype(BF16)
            scores = scores + jnp.where(same_blk, _dot_nt(qh, kh), 0.0)
            o = _dot(scores.astype(BF16), vb)

            o = o + jnp.sum(q * k, axis=-1, keepdims=True) * v
            for d in range(1, HG_SUB):
                kd = pltpu.roll(k, d, 0)
                bd = pltpu.roll(b, d, 0)
                vd = pltpu.roll(v, d, 0)
                e = jnp.exp(jnp.where(row_in_sub >= d, b - bd, -jnp.inf))
                o = o + jnp.sum(q * kd * e, axis=-1, keepdims=True) * vd

            st = st_refs[h][...]
            o = o + _dot_nt((q * jnp.exp(b)).astype(BF16), st.astype(BF16))
            b_end = b[C - 1:C, :]
            k_end = (k * jnp.exp(b_end - b)).astype(BF16)
            st_refs[h][...] = st * jnp.exp(b_end) + _dot_tn(vb, k_end)

            on = o * lax.rsqrt(jnp.mean(o * o, axis=-1, keepdims=True) + EPS) * gn_ref[...]
            o_ref[rs, cs] = (on * sog_ref[rs, cs].astype(F32)).astype(BF16)
        return carry

    def chunk_group(cg, carry):
        for u in range(HG_UNROLL):
            carry = chunk(HG_UNROLL * cg + u, carry)
        return carry

    lax.fori_loop(0, n_chunks // HG_UNROLL, chunk_group, 0)


def _hgrn(q, lf, k, v, sog, gn, B, S, tb):
    T = B * S
    nS = S // tb
    row = lambda b, s: (b * nS + s, 0)
    blk = pl.BlockSpec((tb, HG_WIDTH), row)
    return pl.pallas_call(
        _hgrn_kernel,
        out_shape=jax.ShapeDtypeStruct((T, HG_WIDTH), BF16),
        grid=(B, nS),
        in_specs=[blk, blk, blk, blk, blk, pl.BlockSpec((1, HG_DK), lambda b, s: (0, 0))],
        out_specs=blk,
        scratch_shapes=[pltpu.VMEM((HG_DK, HG_DK), F32)] * HG_HEADS,
        compiler_params=pltpu.CompilerParams(
            dimension_semantics=("arbitrary", "arbitrary"), vmem_limit_bytes=VMEM_LIMIT),
        name="hgrn",
    )(q, lf, k, v, sog, gn)


def _split_kernel(w_ref, hi_ref, lo_ref):
    w = w_ref[...]
    hi = w.astype(BF16)
    hi_ref[...] = hi
    lo_ref[...] = (w - hi.astype(F32)).astype(BF16)


def _split_bf16(w):
    out = jax.ShapeDtypeStruct(w.shape, BF16)
    return pl.pallas_call(_split_kernel, out_shape=(out, out), name="split")(w)


def _mix_kernel(x_ref, oa_ref, pm_ref, sga_ref, sgb_ref, g1_ref, sh2_ref, sc2_ref, n2_ref,
                wua_ref, wub_ref, wo_ref, rw_hi_ref, rw_lo_ref, x1_ref, h2_ref, lg_ref):
    tm = x_ref.shape[1]
    ya = _dot(oa_ref[...], wua_ref[...])
    yb = _dot(pm_ref[...], wub_ref[...])
    mix = sga_ref[...].astype(F32) * ya + sgb_ref[...].astype(F32) * yb
    x1 = x_ref[0] + g1_ref[0] * _dot(mix.astype(BF16), wo_ref[...])
    x1_ref[...] = x1
    h2 = x1 * lax.rsqrt(jnp.mean(x1 * x1, axis=-1, keepdims=True) + EPS) * n2_ref[...]
    h2 = h2 * (1.0 + sc2_ref[0]) + sh2_ref[0]
    for j, chunk in enumerate(_row_chunks(h2)):
        h2_ref[pl.ds(j, tm, stride=ROW_TILES), :] = chunk
    h_hi = h2.astype(BF16)
    h_lo = (h2 - h_hi.astype(F32)).astype(BF16)
    rw_hi = rw_hi_ref[...]
    lg_ref[...] = _dot_nt(rw_hi, h_hi) + _dot_nt(rw_hi, h_lo) + _dot_nt(rw_lo_ref[...], h_hi)


def _mix(x, oa, pm, sga, sgb, mod3, norm2_g, wua, wub, wo, rw_hi, rw_lo, tm):
    B, S, D = x.shape
    T = B * S
    nS = S // tm
    row = lambda b, s: (b * nS + s, 0)
    const2 = lambda b, s: (0, 0)
    return pl.pallas_call(
        _mix_kernel,
        out_shape=(jax.ShapeDtypeStruct((T, D), F32),
                   jax.ShapeDtypeStruct((T * ROW_TILES, LANES), F32),
                   jax.ShapeDtypeStruct((N_EXPERTS, T), F32)),
        grid=(B, nS),
        in_specs=[pl.BlockSpec((1, tm, D), lambda b, s: (b, s, 0)),
                  pl.BlockSpec((tm, HG_WIDTH), row),
                  pl.BlockSpec((tm, POOL_WIDTH), row),
                  pl.BlockSpec((tm, D), row),
                  pl.BlockSpec((tm, D), row),
                  pl.BlockSpec((1, 1, D), lambda b, s: (b, 0, 2)),
                  pl.BlockSpec((1, 1, D), lambda b, s: (b, 0, 3)),
                  pl.BlockSpec((1, 1, D), lambda b, s: (b, 0, 4)),
                  pl.BlockSpec((1, D), const2),
                  pl.BlockSpec(wua.shape, const2),
                  pl.BlockSpec(wub.shape, const2),
                  pl.BlockSpec(wo.shape, const2),
                  pl.BlockSpec(rw_hi.shape, const2),
                  pl.BlockSpec(rw_lo.shape, const2)],
        out_specs=(pl.BlockSpec((tm, D), row),
                   pl.BlockSpec((tm * ROW_TILES, LANES), row),
                   pl.BlockSpec((N_EXPERTS, tm), lambda b, s: (0, b * nS + s))),
        compiler_params=pltpu.CompilerParams(
            dimension_semantics=("arbitrary", "arbitrary"), vmem_limit_bytes=VMEM_LIMIT),
        name="mix",
    )(x, oa, pm, sga, sgb, mod3, mod3, mod3, norm2_g, wua, wub, wo, rw_hi, rw_lo)


def _route_kernel(lg_ref, bias_ref, idx_ref, gate_ref, rank_ref, cnt_ref, carry_ref):
    tl = lg_ref.shape[1]
    neg = -jnp.inf

    @pl.when(pl.program_id(0) == 0)
    def _():
        carry_ref[...] = jnp.zeros_like(carry_ref)

    s = _sigmoid(lg_ref[...])
    biased = s + bias_ref[...]
    rowid = lax.broadcasted_iota(jnp.int32, (N_EXPERTS, tl), 0)

    def first_argmax(x, ids, sentinel):
        m = jnp.max(x, axis=0, keepdims=True)
        return jnp.min(jnp.where(x == m, ids, sentinel), axis=0, keepdims=True), m

    gscores = []
    for g in range(N_GROUPS):
        xg = biased[g * GROUP_SIZE:(g + 1) * GROUP_SIZE, :]
        rid = g * GROUP_SIZE + lax.broadcasted_iota(jnp.int32, (GROUP_SIZE, tl), 0)
        first, m1 = first_argmax(xg, rid, N_EXPERTS)
        m2 = jnp.max(jnp.where(rid == first, neg, xg), axis=0, keepdims=True)
        gscores.append(m1 + m2)
    blocks = []
    for g in range(N_GROUPS):
        beaten = jnp.zeros((1, tl), F32)
        for o in range(N_GROUPS):
            if o != g:
                wins = (gscores[o] >= gscores[g]) if o < g else (gscores[o] > gscores[g])
                beaten = beaten + jnp.where(wins, 1.0, 0.0)
        xg = biased[g * GROUP_SIZE:(g + 1) * GROUP_SIZE, :]
        blocks.append(jnp.where(beaten < float(TOPK_GROUPS), xg, neg))
    masked = jnp.concatenate(blocks, axis=0)

    idxs, gates = [], []
    chosen = jnp.zeros((N_EXPERTS, tl), F32)
    for _ in range(TOP_K):
        first, _m = first_argmax(masked, rowid, N_EXPERTS)
        sel = rowid == first
        gates.append(jnp.sum(jnp.where(sel, s, 0.0), axis=0, keepdims=True))
        idxs.append(first)
        chosen = jnp.where(sel, 1.0, chosen)
        masked = jnp.where(sel, neg, masked)
    gate_sum = functools.reduce(lambda a, b: a + b, gates)
    for k in range(TOP_K):
        gate_ref[k:k + 1, :] = gates[k] / gate_sum * ROUTED_SCALE
        idx_ref[k:k + 1, :] = idxs[k]

    lr = lax.broadcasted_iota(jnp.int32, (tl, tl), 0)
    lc = lax.broadcasted_iota(jnp.int32, (tl, tl), 1)
    prefix = (lr <= lc).astype(BF16)
    cnt_incl = _dot(chosen.astype(BF16), prefix)
    carry = carry_ref[...]
    rank_excl = cnt_incl - chosen + carry
    for k in range(TOP_K):
        rank_k = jnp.sum(jnp.where(rowid == idxs[k], rank_excl, 0.0), axis=0, keepdims=True)
        rank_ref[k:k + 1, :] = rank_k.astype(jnp.int32)
    carry = carry + jnp.sum(chosen, axis=1, keepdims=True)
    carry_ref[...] = carry
    cnt_ref[...] = carry.astype(jnp.int32)


def _route(logits_t, bias, tl):
    T = logits_t.shape[1]
    tok = lambda i: (0, i)
    return pl.pallas_call(
        _route_kernel,
        out_shape=(jax.ShapeDtypeStruct((TOP_K, T), jnp.int32),
                   jax.ShapeDtypeStruct((TOP_K, T), F32),
                   jax.ShapeDtypeStruct((TOP_K, T), jnp.int32),
                   jax.ShapeDtypeStruct((N_EXPERTS, 1), jnp.int32)),
        grid=(T // tl,),
        in_specs=[pl.BlockSpec((N_EXPERTS, tl), tok), pl.BlockSpec((N_EXPERTS, 1), lambda i: (0, 0))],
        out_specs=(pl.BlockSpec((TOP_K, tl), tok), pl.BlockSpec((TOP_K, tl), tok),
                   pl.BlockSpec((TOP_K, tl), tok), pl.BlockSpec((N_EXPERTS, 1), lambda i: (0, 0))),
        scratch_shapes=[pltpu.VMEM((N_EXPERTS, 1), F32)],
        compiler_params=pltpu.CompilerParams(
            dimension_semantics=("arbitrary",), vmem_limit_bytes=VMEM_LIMIT),
        name="route",
    )(logits_t, bias)


def _as_rows(ref):
    return ref.reshape(ref.shape[0] // ROW_TILES, ROW_TILES, LANES)


def _wait_rows(rows_ref, n, sem):
    pltpu.make_async_copy(rows_ref.at[pl.ds(0, n)], rows_ref.at[pl.ds(0, n)], sem).wait()


def _scatter_kernel(pos_ref, h2_ref, xs_ref, zero_ref, sem, zsem):
    src = _as_rows(h2_ref)
    dst = _as_rows(xs_ref)
    tt = src.shape[0]

    @pl.when(pl.program_id(0) == 0)
    def _():
        zero_ref[...] = jnp.zeros_like(zero_ref)
        tail = xs_ref.at[pl.ds(xs_ref.shape[0] - zero_ref.shape[0], zero_ref.shape[0])]
        fill = pltpu.make_async_copy(zero_ref, tail, zsem)
        fill.start()
        fill.wait()

    def start(t, c):
        for k in range(TOP_K):
            pltpu.make_async_copy(src.at[t], dst.at[pos_ref[k, t]], sem).start(priority=k % 2)
        return c

    lax.fori_loop(0, tt, start, 0)
    _wait_rows(dst, tt * TOP_K, sem)


def _scatter(pos, h2_tm, tt):
    n_rows = h2_tm.shape[0] // ROW_TILES * TOP_K + EXPERT_TILE
    return pl.pallas_call(
        _scatter_kernel,
        out_shape=jax.ShapeDtypeStruct((n_rows * ROW_TILES, LANES), F32),
        grid=(pos.shape[1] // tt,),
        in_specs=[pl.BlockSpec((TOP_K, tt), lambda i: (0, i), memory_space=pltpu.SMEM),
                  pl.BlockSpec((tt * ROW_TILES, LANES), lambda i: (i, 0))],
        out_specs=pl.BlockSpec(memory_space=pl.ANY),
        scratch_shapes=[pltpu.VMEM((EXPERT_TILE * ROW_TILES, LANES), F32),
                        pltpu.SemaphoreType.DMA, pltpu.SemaphoreType.DMA],
        compiler_params=pltpu.CompilerParams(
            dimension_semantics=("arbitrary",), vmem_limit_bytes=VMEM_LIMIT),
        name="scatter",
    )(pos, h2_tm)


def _experts_kernel(exp_ref, row0_ref, valid_ref, newe_ref, nitems_ref, slot_ref, nexte_ref,
                    xs_hbm, wg_hbm, wu_hbm, wd_hbm, ys_hbm,
                    xbuf_ref, ybuf_ref, sg_ref, su_ref, sd_ref, wgb_ref, wub_ref, wdb_ref, hm_ref,
                    xsem, ysem, wsem):
    i = pl.program_id(0)
    tr = EXPERT_TILE
    tile_rows = tr * ROW_TILES
    n_items = nitems_ref[0]
    part_sizes = tuple(tr >> (b + 1) for b in range(tr.bit_length() - 1))

    def ring(item, first_row=0, n_rows=tr):
        start = ((item % TILE_RING) * tr + first_row) * ROW_TILES
        return pl.ds(pl.multiple_of(start, ROW_TILES), n_rows * ROW_TILES)

    def hbm_rows(first_row, n_rows=tr):
        return pl.ds(pl.multiple_of(first_row * ROW_TILES, ROW_TILES), n_rows * ROW_TILES)

    def x_copy(item):
        return pltpu.make_async_copy(xs_hbm.at[hbm_rows(row0_ref[item])], xbuf_ref.at[ring(item)],
                                     xsem.at[item % TILE_RING])

    def y_copies(item, go):
        v = valid_ref[item]
        sem = ysem.at[item % TILE_RING]

        @pl.when(v == tr)
        def _():
            go(pltpu.make_async_copy(ybuf_ref.at[ring(item)], ys_hbm.at[hbm_rows(row0_ref[item])], sem))

        @pl.when(v != tr)
        def _():
            for size in part_sizes:
                @pl.when((v & size) != 0)
                def _():
                    first = v & ~(2 * size - 1)
                    go(pltpu.make_async_copy(ybuf_ref.at[ring(item, first, size)],
                                             ys_hbm.at[hbm_rows(row0_ref[item] + first, size)], sem))

    def weight_copies(e, slot):
        return (pltpu.make_async_copy(wg_hbm.at[e], sg_ref.at[slot], wsem.at[slot]),
                pltpu.make_async_copy(wu_hbm.at[e], su_ref.at[slot], wsem.at[slot]),
                pltpu.make_async_copy(wd_hbm.at[e], sd_ref.at[slot], wsem.at[slot]))

    a_on = i < n_items
    j = jnp.maximum(i - 1, 0)
    b_on = (i >= 1) & (i - 1 < n_items)
    e = exp_ref[i]

    @pl.when(i == 0)
    def _():
        hm_ref[...] = jnp.zeros_like(hm_ref)
        for i0 in range(TILE_AHEAD):
            @pl.when(i0 < n_items)
            def _():
                x_copy(i0).start()

    @pl.when(a_on)
    def _():
        @pl.when(i + TILE_AHEAD < n_items)
        def _():
            x_copy(i + TILE_AHEAD).start()

        x_copy(i).wait()

    @pl.when(b_on & (j >= TILE_RING))
    def _():
        y_copies(j - TILE_RING, lambda c: c.wait())

    @pl.when(a_on & (newe_ref[i] == 1))
    def _():
        slot = slot_ref[i]
        nxt = nexte_ref[i]

        @pl.when(i == 0)
        def _():
            for c in weight_copies(e, slot):
                c.start()

        @pl.when(nxt >= 0)
        def _():
            for c in weight_copies(nxt, 1 - slot):
                c.start()

        for c in weight_copies(e, slot):
            c.wait()
        wgb_ref[...] = sg_ref[slot].astype(BF16)
        wub_ref[...] = su_ref[slot].astype(BF16)
        wdb_ref[slot] = sd_ref[slot].astype(BF16)

    @pl.when(i <= n_items)
    def _():
        out_row = (j % TILE_RING) * tile_rows
        for c, chunk in enumerate(_row_chunks(_dot(hm_ref[j % 2], wdb_ref[slot_ref[j]]))):
            ybuf_ref[pl.ds(out_row + c, tr, stride=ROW_TILES), :] = chunk

        x = _rows_to_matrix(xbuf_ref[ring(i), :], tr).astype(BF16)
        hm_ref[i % 2] = (_silu(_dot(x, wgb_ref[...])) * _dot(x, wub_ref[...])).astype(BF16)

    @pl.when(b_on)
    def _():
        y_copies(j, lambda c: c.start())

    @pl.when(b_on & (j == n_items - 1))
    def _():
        for back in range(TILE_RING):
            @pl.when(j - back >= 0)
            def _():
                y_copies(j - back, lambda c: c.wait())


def _experts(meta, xs, wg, wu, wd, n_items_max):
    tile_rows = EXPERT_TILE * ROW_TILES
    out_rows = xs.shape[0] - tile_rows
    hbm = pl.BlockSpec(memory_space=pl.ANY)
    n_slots = 2
    grid_spec = pltpu.PrefetchScalarGridSpec(
        num_scalar_prefetch=len(meta),
        grid=(n_items_max + 1,),
        in_specs=[hbm, hbm, hbm, hbm],
        out_specs=hbm,
        scratch_shapes=[pltpu.VMEM((TILE_RING * tile_rows, LANES), F32),
                        pltpu.VMEM((TILE_RING * tile_rows, LANES), F32),
                        pltpu.VMEM((n_slots, D_MODEL, D_EXPERT), F32),
                        pltpu.VMEM((n_slots, D_MODEL, D_EXPERT), F32),
                        pltpu.VMEM((n_slots, D_EXPERT, D_MODEL), F32),
                        pltpu.VMEM((D_MODEL, D_EXPERT), BF16),
                        pltpu.VMEM((D_MODEL, D_EXPERT), BF16),
                        pltpu.VMEM((n_slots, D_EXPERT, D_MODEL), BF16),
                        pltpu.VMEM((2, EXPERT_TILE, D_EXPERT), BF16),
                        pltpu.SemaphoreType.DMA((TILE_RING,)),
                        pltpu.SemaphoreType.DMA((TILE_RING,)),
                        pltpu.SemaphoreType.DMA((n_slots,))])
    return pl.pallas_call(
        _experts_kernel,
        out_shape=jax.ShapeDtypeStruct((out_rows, LANES), F32),
        grid_spec=grid_spec,
        compiler_params=pltpu.CompilerParams(
            dimension_semantics=("arbitrary",), vmem_limit_bytes=VMEM_LIMIT),
        name="experts",
    )(*meta, xs, wg, wu, wd)


def _combine_kernel(pos_ref, pos_next_ref, ys_ref, h2_ref, x1_ref, gate_ref, g2_ref, swg_ref, swu_ref, swd_ref,
                    fg_ref, out_ref, buf_a, buf_b, sem):
    i = pl.program_id(0)
    tt = x1_ref.shape[0] // 2
    src = _as_rows(ys_ref)

    def gather(p_ref, col0, buf, s):
        dst = _as_rows(buf)
        for t in range(tt):
            for k in range(TOP_K):
                pltpu.make_async_copy(src.at[p_ref[k, col0 + t]], dst.at[k * tt + t],
                                      sem.at[s]).start(priority=k % 2)

    @pl.when(i == 0)
    def _():
        dst = _as_rows(buf_a)

        def start(t, c):
            for k in range(TOP_K):
                pltpu.make_async_copy(src.at[pos_ref[k, t]], dst.at[k * tt + t], sem.at[0]).start(priority=k % 2)
            return c

        lax.fori_loop(0, tt, start, 0)

    def tile(row0, buf, s, prefetch):
        tok = pl.ds(row0, tt)
        _wait_rows(_as_rows(buf), tt * TOP_K, sem.at[s])
        prefetch()
        h2 = _load_rows(h2_ref, tt, first_row=row0).astype(BF16)
        hm = (_silu(_dot(h2, swg_ref[...])) * _dot(h2, swu_ref[...])).astype(BF16)
        gate = gate_ref[tok, :]
        ssq = jnp.zeros((tt, 1), F32)
        for c in range(ROW_TILES):
            cols = slice(c * LANES, (c + 1) * LANES)
            acc = _dot(hm, swd_ref[:, cols])
            for k in range(TOP_K):
                acc = acc + gate[:, k:k + 1] * buf[pl.ds(k * tt * ROW_TILES + c, tt, stride=ROW_TILES), :]
            x2 = x1_ref[tok, cols] + g2_ref[0, :, cols] * acc
            out_ref[tok, cols] = x2
            ssq = ssq + jnp.sum(x2 * x2, axis=-1, keepdims=True)
        out_ref[tok, :] = out_ref[tok, :] * lax.rsqrt(ssq * (1.0 / D_MODEL) + EPS) * fg_ref[...]

    tile(0, buf_a, 0, lambda: gather(pos_ref, tt, buf_b, 1))
    tile(tt, buf_b, 1, lambda: gather(pos_next_ref, 0, buf_a, 0))

    @pl.when(i == pl.num_programs(0) - 1)
    def _():
        _wait_rows(_as_rows(buf_a), tt * TOP_K, sem.at[0])


def _combine(pos, ys, h2_tm, x1, gate_tm, mod3, swg, swu, swd, fg, B, S, tt):
    T, D = x1.shape
    gather_rows = tt * TOP_K * ROW_TILES
    tt = 2 * tt
    nS = S // tt
    const2 = lambda i: (0, 0)
    n_tiles = T // tt
    return pl.pallas_call(
        _combine_kernel,
        out_shape=jax.ShapeDtypeStruct((T, D), F32),
        grid=(n_tiles,),
        in_specs=[pl.BlockSpec((TOP_K, tt), lambda i: (0, i), memory_space=pltpu.SMEM),
                  pl.BlockSpec((TOP_K, tt), lambda i: (0, jnp.minimum(i + 1, n_tiles - 1)),
                               memory_space=pltpu.SMEM),
                  pl.BlockSpec(memory_space=pl.ANY),
                  pl.BlockSpec((tt * ROW_TILES, LANES), lambda i: (i, 0)),
                  pl.BlockSpec((tt, D), lambda i: (i, 0)),
                  pl.BlockSpec((tt, TOP_K), lambda i: (i, 0)),
                  pl.BlockSpec((1, 1, D), lambda i: (i // nS, 0, 5)),
                  pl.BlockSpec(swg.shape, const2),
                  pl.BlockSpec(swu.shape, const2),
                  pl.BlockSpec(swd.shape, const2),
                  pl.BlockSpec((1, D), const2)],
        out_specs=pl.BlockSpec((tt, D), lambda i: (i, 0)),
        scratch_shapes=[pltpu.VMEM((gather_rows, LANES), F32),
                        pltpu.VMEM((gather_rows, LANES), F32),
                        pltpu.SemaphoreType.DMA((2,))],
        compiler_params=pltpu.CompilerParams(
            dimension_semantics=("arbitrary",), vmem_limit_bytes=VMEM_LIMIT),
        name="combine",
    )(pos, pos, ys, h2_tm, x1, gate_tm, mod3, swg, swu, swd, fg)


def _place_kernel(idx_ref, rank_ref, off_ref, pos_ref):
    tl = idx_ref.shape[1]
    rowid = lax.broadcasted_iota(jnp.int32, (N_EXPERTS, tl), 0)
    off = off_ref[...].astype(F32)
    for k in range(TOP_K):
        base = jnp.sum(jnp.where(rowid == idx_ref[k:k + 1, :], off, 0.0), axis=0, keepdims=True)
        pos_ref[k:k + 1, :] = base.astype(jnp.int32) + rank_ref[k:k + 1, :]


def _place(idx, rank, off, tl):
    T = idx.shape[1]
    tok = pl.BlockSpec((TOP_K, tl), lambda i: (0, i))
    return pl.pallas_call(
        _place_kernel,
        out_shape=jax.ShapeDtypeStruct((TOP_K, T), jnp.int32),
        grid=(T // tl,),
        in_specs=[tok, tok, pl.BlockSpec((N_EXPERTS, 1), lambda i: (0, 0))],
        out_specs=tok,
        name="place",
    )(idx, rank, off)


def _plan(counts, n_items_max):
    counts = counts[:, 0]
    off = jnp.cumsum(counts) - counts
    n_e = (counts + EXPERT_TILE - 1) // EXPERT_TILE
    item_end = jnp.cumsum(n_e)
    item_start = item_end - n_e
    n_items = item_end[-1]
    ids = jnp.arange(n_items_max + 1, dtype=jnp.int32)
    ids_c = jnp.minimum(ids, n_items - 1)
    item_e = jnp.minimum(jnp.sum(item_end[None, :] <= ids_c[:, None], axis=1), N_EXPERTS - 1).astype(jnp.int32)
    ids_e = jnp.arange(N_EXPERTS, dtype=jnp.int32)
    of_item = item_e[:, None] == ids_e[None, :]

    def per_item(table):
        return jnp.sum(jnp.where(of_item, table[None, :], 0), axis=1).astype(jnp.int32)

    first_row = (ids_c - per_item(item_start)) * EXPERT_TILE
    item_row0 = per_item(off) + first_row
    item_valid = jnp.minimum(per_item(counts) - first_row, EXPERT_TILE).astype(jnp.int32)
    prev_e = jnp.concatenate([jnp.full((1,), -1, jnp.int32), item_e[:-1]])
    item_newe = (item_e != prev_e).astype(jnp.int32)
    item_slot = ((jnp.cumsum(item_newe) - 1) % 2).astype(jnp.int32)
    later = jnp.where((counts[None, :] > 0) & (ids_e[None, :] > ids_e[:, None]), ids_e[None, :], N_EXPERTS)
    next_e = jnp.min(later, axis=1)
    next_e = jnp.where(next_e < N_EXPERTS, next_e, -1).astype(jnp.int32)
    meta = (item_e, item_row0, item_valid, item_newe, n_items.reshape(1).astype(jnp.int32),
            item_slot, per_item(next_e))
    return off.astype(jnp.int32).reshape(N_EXPERTS, 1), meta


def kernel(x, c, ada_w, ada_b, norm1_g, w_in, hgrn_lb, hgrn_norm_g, pool_w, pool_b, pool_scale, w_up_a, w_up_b, w_out, norm2_g, router_w, router_bias, exp_w_gate, exp_w_up, exp_w_down, shared_w_gate, shared_w_up, shared_w_down, final_norm_g):
    B, S, D = x.shape
    T = B * S
    assert ada_w.shape[0] == 1, "single-layer trunk only: the final norm is fused into the combine step"
    lb_all = jnp.cumsum(jax.nn.softmax(hgrn_lb.astype(F32), axis=0), axis=0)
    c_pad = jnp.zeros((SUBLANES, D), F32).at[:B].set(c)
    n_items_max = T * TOP_K // EXPERT_TILE + N_EXPERTS

    for l in range(1):
        mod = _ada(c_pad, ada_w[l], ada_b[l].reshape(1, -1))
        mod3 = mod[:B].reshape(B, 1, 6 * D)

        q, lf, k, v, sog, pm, sga, sgb = _inproj(
            x, mod3, norm1_g[l].reshape(1, D), w_in[l].astype(BF16), lb_all[l].reshape(1, HG_WIDTH),
            pool_w[l].astype(BF16), pool_b[l].reshape(len(POOL_WINDOWS), 1, POOL_GROUP),
            pool_scale[l].reshape(1, POOL_WIDTH), tm=256)
        oa = _hgrn(q, lf, k, v, sog, hgrn_norm_g[l].reshape(1, HG_DK), B, S, tb=512)

        rw_hi, rw_lo = _split_bf16(router_w[l].T)
        x1, h2_tm, logits_t = _mix(
            x, oa, pm, sga, sgb, mod3, norm2_g[l].reshape(1, D), w_up_a[l].astype(BF16),
            w_up_b[l].astype(BF16), w_out[l].astype(BF16), rw_hi, rw_lo, tm=512)

        idx, gate, rank, counts = _route(logits_t, router_bias[l].reshape(N_EXPERTS, 1), tl=256)
        off, meta = _plan(counts, n_items_max)
        pos = _place(idx, rank, off, tl=512)

        xs = _scatter(pos, h2_tm, tt=256)
        ys = _experts(meta, xs, exp_w_gate[l], exp_w_up[l], exp_w_down[l], n_items_max)

        tt_c = 128
        fg = final_norm_g.reshape(1, D)
        x = _combine(pos, ys, h2_tm, x1, gate.T, mod3,
                     shared_w_gate[l].astype(BF16), shared_w_up[l].astype(BF16),
                     shared_w_down[l].astype(BF16), fg, B, S, tt_c).reshape(B, S, D)
    return x
```
